```python
import math
import jax, jax.numpy as jnp
from jax import lax
import numpy as np

D_MODEL = 2048
BATCH = 16
SEQ = 256
DEPTH = 2
DEC_BATCH = 4
DEC_SEQ = 1024
PAST_LEN = 512

GRID_W = 64
N_MIXERS = 4
BRANCH = D_MODEL // N_MIXERS
N_GROUPS = 4
GROUP_W = BRANCH // N_GROUPS
CHUNK = 128
POOL_WINDOWS = (2, 4, 8, 16)
N_HEADS_C = 4
HEAD_DIM_C = BRANCH // N_HEADS_C
QK_HALF = HEAD_DIM_C // 2
ROPE_AXIS_DIM = QK_HALF // 2
ROPE_BASE = 10000.0
Q_BLOCK = 128
SHORT_CONV = 3
HY_BANDS = 16
HY_EMB = 1 + 2 * HY_BANDS
HY_HIDDEN = 64
HY_ORDER = 2
HY_FAST_DECAY = 0.3
HY_SLOW_DECAY = 1.5
HY_TARGET = 1e-2
N_IN_PIECES = 13
D_IN = N_IN_PIECES * BRANCH
D_MIX = N_MIXERS * BRANCH
LN_EPS = 1e-6
F32 = jnp.float32

kernel_name = "hybrid_prefix_diffusion_step"


def _layer_norm(x):
    x32 = x.astype(F32)
    mu = jnp.mean(x32, axis=-1, keepdims=True)
    xc = x32 - mu
    var = jnp.mean(xc * xc, axis=-1, keepdims=True)
    return (xc * lax.rsqrt(var + LN_EPS)).astype(x.dtype)


def _chunk_gmlp(u, v, w_s, b_s):
    b, L, _ = v.shape
    vc = _layer_norm(v).reshape(b, L // CHUNK, CHUNK, N_GROUPS, GROUP_W)
    mixed = jnp.einsum("gij,bnjgc->bnigc", w_s, vc) + jnp.swapaxes(b_s, 0, 1)[:, :, None]
    return u * mixed.reshape(b, L, BRANCH)


def _multiscale_pool(p, w_pool, pool_scale):
    b, L, _ = p.shape
    pg = p.reshape(b, L, N_GROUPS, GROUP_W)
    csum = jnp.pad(jnp.cumsum(pg.astype(F32), axis=1), ((0, 0), (1, 0), (0, 0), (0, 0)))
    t = jnp.arange(L)
    means = []
    for g, w in enumerate(POOL_WINDOWS):
        lo = jnp.clip(t - w // 2, 0, L)
        hi = jnp.clip(t + w // 2, 0, L)
        win = csum[:, hi, g] - csum[:, lo, g]
        means.append(win / (hi - lo).astype(F32)[None, :, None])
    pooled = jnp.stack(means, axis=2).astype(p.dtype)
    y = jnp.einsum("blgc,gcd->blgd", pooled - pg, w_pool)
    return y.reshape(b, L, BRANCH) * pool_scale


def _axial_angles(L):
    n_rows = L // GRID_W
    row = jnp.repeat(jnp.arange(n_rows), GRID_W).astype(F32)
    col = jnp.tile(jnp.arange(GRID_W), n_rows).astype(F32)
    inv = ROPE_BASE ** (-jnp.arange(0, ROPE_AXIS_DIM, 2, dtype=F32) / ROPE_AXIS_DIM)
    return row[:, None] * inv, col[:, None] * inv


def _rotate(x, ang):
    half = x.shape[-1] // 2
    cos = jnp.cos(ang)[None, :, None, None, :].astype(x.dtype)
    sin = jnp.sin(ang)[None, :, None, None, :].astype(x.dtype)
    x1, x2 = x[..., :half], x[..., half:]
    return jnp.concatenate([x1 * cos - x2 * sin, x2 * cos + x1 * sin], axis=-1)


def _axial_rope(x, ang_row, ang_col):
    return jnp.concatenate([_rotate(x[..., :ROPE_AXIS_DIM], ang_row),
                            _rotate(x[..., ROPE_AXIS_DIM:], ang_col)], axis=-1)


def _diff_attention(q, k, v, lam, lam_init, subln_w):
    bsz, lq = q.shape[0], q.shape[1]
    n_blk = lq // Q_BLOCK
    qb = jnp.moveaxis(q.reshape(bsz, n_blk, Q_BLOCK, N_HEADS_C, 2, QK_HALF), 1, 0)

    def block(q_blk):
        s = jnp.einsum("bqhmd,bkhmd->bhmqk", q_blk, k, preferred_element_type=F32) * (QK_HALF ** -0.5)
        p = jax.nn.softmax(s, axis=-1)
        w = (p[:, :, 0] - lam * p[:, :, 1]).astype(v.dtype)
        return jnp.einsum("bhqk,bkhd->bqhd", w, v)

    o = jnp.moveaxis(lax.map(block, qb), 0, 1).reshape(bsz, lq, N_HEADS_C, HEAD_DIM_C)
    o32 = o.astype(F32)
    o32 = o32 * lax.rsqrt(jnp.mean(o32 * o32, axis=-1, keepdims=True) + 1e-5)
    o = o32.astype(v.dtype) * subln_w * (1.0 - lam_init)
    return o.reshape(bsz, lq, BRANCH)


def _short_conv(x, w, b):
    L = x.shape[1]
    xp = jnp.pad(x, ((0, 0), (1, 1), (0, 0)))
    return xp[:, 0:L] * w[0] + xp[:, 1:L + 1] * w[1] + xp[:, 2:L + 2] * w[2] + b


def _hyena_filters(L, w1, b1, w2, b2, freq, w3):
    t_idx = jnp.arange(L, dtype=F32)
    t_norm = jnp.linspace(0.0, 1.0, L, dtype=F32)
    bands = jnp.linspace(1e-4, HY_BANDS - 1, HY_BANDS, dtype=F32)
    ang = (2.0 * math.pi * t_idx / L)[:, None] * bands[None, :]
    feats = jnp.concatenate([t_norm[:, None], jnp.cos(ang), jnp.sin(ang)], axis=-1)
    fr = freq.astype(F32)
    h = jnp.sin(fr * (feats @ w1.astype(F32) + b1.astype(F32)))
    h = jnp.sin(fr * (h @ w2.astype(F32) + b2.astype(F32)))
    h = (h @ w3.astype(F32)).reshape(L, HY_ORDER, 2, BRANCH)
    deltas = jnp.abs(jnp.linspace(math.log(HY_TARGET) / HY_FAST_DECAY,
                                  math.log(HY_TARGET) / HY_SLOW_DECAY, BRANCH, dtype=F32))
    h = h * jnp.exp(-t_norm[:, None] * deltas[None, :])[:, None, None, :]
    fwd, bwd = h[:, :, 0], h[:, :, 1]
    k = jnp.concatenate([fwd, jnp.zeros_like(fwd[:1]), bwd[1:][::-1]], axis=0)
    k = k / jnp.sum(jnp.abs(k), axis=0, keepdims=True)
    return jnp.fft.rfft(k, axis=0)


def _long_conv(z, k_f, bias):
    L = z.shape[1]
    z32 = z.astype(F32)
    y = jnp.fft.irfft(jnp.fft.rfft(z32, n=2 * L, axis=1) * k_f[None], n=2 * L, axis=1)[:, :L]
    return (y + z32 * bias.astype(F32)).astype(z.dtype)


def _hyena(x1, x2, hv, conv_w, conv_b, k_f, hy_bias):
    xc = _short_conv(jnp.concatenate([x1, x2, hv], axis=-1), conv_w, conv_b)
    g1, g2, z = jnp.split(xc, 3, axis=-1)
    z = g1 * _long_conv(z, k_f[:, 0], hy_bias[0])
    return g2 * _long_conv(z, k_f[:, 1], hy_bias[1])


def _trunk_layer(x, cond, ctx_k, ctx_v, layer_idx, lp):
    bsz, L, _ = x.shape
    alpha = (2.0 * DEPTH) ** 0.25
    mod = jax.nn.silu(cond) @ lp["w_mod"] + lp["b_mod"]
    shift, scale, gate = jnp.split(mod[:, None, :], 3, axis=-1)
    h = _layer_norm(x) * (1.0 + scale) + shift
    (a_u, a_v, a_g, b_x, b_g, c_q, c_k, c_v, c_g,
     d_x1, d_x2, d_v, d_g) = jnp.split(h @ lp["w_in"], N_IN_PIECES, axis=-1)

    y_a = jax.nn.silu(a_g) * _chunk_gmlp(a_u, a_v, lp["gmlp_w"], lp["gmlp_b"])
    y_b = jax.nn.silu(b_g) * _multiscale_pool(b_x, lp["pool_w"], lp["pool_scale"])

    q = c_q.reshape(bsz, L, N_HEADS_C, 2, QK_HALF)
    k = c_k.reshape(bsz, L, N_HEADS_C, 2, QK_HALF)
    v = c_v.reshape(bsz, L, N_HEADS_C, HEAD_DIM_C)
    if ctx_k is None:
        q_att, keys, vals = q, k, v
    else:
        ang_r, ang_c = _axial_angles(L)
        q_att = _axial_rope(q, ang_r, ang_c)
        keys = jnp.concatenate([ctx_k, _axial_rope(k, ang_r, ang_c)], axis=1)
        vals = jnp.concatenate([ctx_v, v], axis=1)
    lam_qk = lp["lambda_qk"].astype(F32)
    lam_init = 0.8 - 0.6 * math.exp(-0.3 * layer_idx)
    lam = jnp.exp(jnp.sum(lam_qk[0] * lam_qk[1])) - jnp.exp(jnp.sum(lam_qk[2] * lam_qk[3])) + lam_init
    y_c = jax.nn.silu(c_g) * _diff_attention(q_att, keys, vals, lam, lam_init, lp["subln_w"])

    k_f = _hyena_filters(L, lp["filt_w1"], lp["filt_b1"], lp["filt_w2"], lp["filt_b2"],
                         lp["filt_freq"], lp["filt_w3"])
    y_d = jax.nn.silu(d_g) * _hyena(d_x1, d_x2, d_v, lp["conv_w"], lp["conv_b"], k_f, lp["hyena_bias"])

    y = jnp.concatenate([y_a, y_b, y_c, y_d], axis=-1) @ lp["w_out"] + lp["b_out"]
    x = _layer_norm(alpha * x + gate * y) * lp["ln_g"] + lp["ln_b"]
    return x, k, v


def setup_inputs(seed: int = 0) -> dict:
    key = jax.random.key(seed)
    ks = jax.random.split(key, 28)

    def nrm(i, shape, s):
        return jax.random.normal(ks[i], shape, F32) * s

    beta = (8.0 * DEPTH) ** -0.25
    return {
        "x_prompt": nrm(0, (BATCH, SEQ, D_MODEL), 1.0),
        "x_sample": nrm(1, (DEC_BATCH, DEC_SEQ, D_MODEL), 1.0),
        "cache_k": nrm(2, (DEC_BATCH, DEPTH, PAST_LEN, N_HEADS_C, 2, QK_HALF), 1.0),
        "cache_v": nrm(3, (DEC_BATCH, DEPTH, PAST_LEN, N_HEADS_C, HEAD_DIM_C), 1.0),
        "c": nrm(4, (DEC_BATCH, D_MODEL), 1.0),
        "c_ctx": nrm(5, (D_MODEL,), 1.0),
        "w_mod": nrm(6, (DEPTH, D_MODEL, 3 * D_MODEL), 0.5 * D_MODEL ** -0.5),
        "b_mod": nrm(7, (DEPTH, 3 * D_MODEL), 0.01),
        "w_in": nrm(8, (DEPTH, D_MODEL, D_IN), D_MODEL ** -0.5),
        "gmlp_w": nrm(9, (DEPTH, N_GROUPS, CHUNK, CHUNK), CHUNK ** -0.5),
        "gmlp_b": 1.0 + nrm(10, (DEPTH, N_GROUPS, CHUNK), 0.01),
        "pool_w": nrm(11, (DEPTH, N_GROUPS, GROUP_W, GROUP_W), GROUP_W ** -0.5),
        "pool_scale": 1.0 + nrm(12, (DEPTH, BRANCH), 0.02),
        "lambda_qk": nrm(13, (DEPTH, 4, QK_HALF), 0.1),
        "subln_w": 1.0 + nrm(14, (DEPTH, HEAD_DIM_C), 0.02),
        "conv_w": nrm(15, (DEPTH, SHORT_CONV, 3 * BRANCH), SHORT_CONV ** -0.5),
        "conv_b": nrm(16, (DEPTH, 3 * BRANCH), 0.01),
        "filt_w1": nrm(17, (DEPTH, HY_EMB, HY_HIDDEN), HY_EMB ** -0.5),
        "filt_b1": nrm(18, (DEPTH, HY_HIDDEN), 0.01),
        "filt_w2": nrm(19, (DEPTH, HY_HIDDEN, HY_HIDDEN), HY_HIDDEN ** -0.5),
        "filt_b2": nrm(20, (DEPTH, HY_HIDDEN), 0.01),
        "filt_freq": 1.0 + nrm(21, (DEPTH, HY_HIDDEN), 0.02),
        "filt_w3": nrm(22, (DEPTH, HY_HIDDEN, HY_ORDER * 2 * BRANCH), HY_HIDDEN ** -0.5),
        "hyena_bias": nrm(23, (DEPTH, HY_ORDER, BRANCH), 0.1),
        "w_out": nrm(24, (DEPTH, D_MIX, D_MODEL), beta * D_MIX ** -0.5),
        "b_out": nrm(25, (DEPTH, D_MODEL), 0.01),
        "ln_g": 1.0 + nrm(26, (DEPTH, D_MODEL), 0.02),
        "ln_b": nrm(27, (DEPTH, D_MODEL), 0.01),
    }


def reference(x_prompt, x_sample, cache_k, cache_v, c, c_ctx, w_mod, b_mod, w_in, gmlp_w, gmlp_b,
              pool_w, pool_scale, lambda_qk, subln_w, conv_w, conv_b, filt_w1, filt_b1, filt_w2,
              filt_b2, filt_freq, filt_w3, hyena_bias, w_out, b_out, ln_g, ln_b):
    xp = x_prompt
    xs = x_sample
    cond_ctx = c_ctx[None, :]
    new_k = []
    new_v = []
    for l in range(DEPTH):
        lp = {
            "w_mod": w_mod[l], "b_mod": b_mod[l], "w_in": w_in[l],
            "gmlp_w": gmlp_w[l], "gmlp_b": gmlp_b[l],
            "pool_w": pool_w[l], "pool_scale": pool_scale[l],
            "lambda_qk": lambda_qk[l], "subln_w": subln_w[l],
            "conv_w": conv_w[l], "conv_b": conv_b[l],
            "filt_w1": filt_w1[l], "filt_b1": filt_b1[l], "filt_w2": filt_w2[l],
            "filt_b2": filt_b2[l], "filt_freq": filt_freq[l], "filt_w3": filt_w3[l],
            "hyena_bias": hyena_bias[l],
            "w_out": w_out[l], "b_out": b_out[l], "ln_g": ln_g[l], "ln_b": ln_b[l],
        }
        xp, k_l, v_l = _trunk_layer(xp, cond_ctx, None, None, l, lp)
        new_k.append(k_l)
        new_v.append(v_l)
        xs, _, _ = _trunk_layer(xs, c, cache_k[:, l], cache_v[:, l], l, lp)
    new_cache_k = jnp.stack(new_k, axis=1)
    new_cache_v = jnp.stack(new_v, axis=1)
    return (xp, xs, new_cache_k, new_cache_v)
```

```python
import functools
import math

import numpy as np
import jax
import jax.numpy as jnp
from jax import lax
from jax.experimental import pallas as pl
from jax.experimental.pallas import tpu as pltpu

F32 = jnp.float32
BF16 = jnp.bfloat16

D_MODEL = 2048
BATCH = 16
SEQ = 256
DEPTH = 2
DEC_BATCH = 4
DEC_SEQ = 1024
PAST_LEN = 512
GRID_W = 64
BRANCH = 512
N_GROUPS = 4
GROUP_W = 128
CHUNK = 128
POOL_WINDOWS = (2, 4, 8, 16)
N_HEADS_C = 4
HEAD_DIM_C = 128
QK_HALF = 64
ROPE_AXIS_DIM = 32
ROPE_BASE = 10000.0
HY_BANDS = 16
HY_EMB = 33
HY_HIDDEN = 64
HY_ORDER = 2
HY_FAST_DECAY = 0.3
HY_SLOW_DECAY = 1.5
HY_TARGET = 1e-2
N_IN_PIECES = 13
D_IN = N_IN_PIECES * BRANCH
LN_EPS = 1e-6

N_CTX = BATCH * SEQ
N_LAT = DEC_BATCH * DEC_SEQ
N_TOK = N_CTX + N_LAT
N_COND = 8
LANES = 128
HY_PAD = LANES
HY_CT = 256
DFT_PASSES = 1
VMEM_LIMIT = 56 * 1024 * 1024

(C_AU, C_AV, C_AG, C_BX, C_BG, C_Q, C_K, C_V, C_CG, C_DX1, C_DX2, C_DV, C_DG) = range(13)


def _silu(x):
    return x * jax.nn.sigmoid(x)


def _bdot(a, b):
    return jnp.dot(a.astype(BF16), b.astype(BF16), preferred_element_type=F32)


def _split(x, n_terms):
    hi = x.astype(BF16)
    if n_terms == 1:
        return (hi,)
    return (hi, (x - hi.astype(F32)).astype(BF16))


def _sdot(a_terms, b_terms):
    acc = jnp.dot(a_terms[0], b_terms[0], preferred_element_type=F32)
    if len(a_terms) > 1:
        acc = acc + jnp.dot(a_terms[1], b_terms[0], preferred_element_type=F32)
    if len(b_terms) > 1:
        acc = acc + jnp.dot(a_terms[0], b_terms[1], preferred_element_type=F32)
    return acc


def _layer_norm(x):
    mu = jnp.mean(x, axis=-1, keepdims=True)
    xc = x - mu
    var = jnp.mean(xc * xc, axis=-1, keepdims=True)
    return xc * lax.rsqrt(var + LN_EPS)


def _cond_row(tile, rows_per_tile):
    n_ctx_tiles = N_CTX // rows_per_tile
    tiles_per_batch = DEC_SEQ // rows_per_tile
    return jnp.where(tile < n_ctx_tiles, 0, 1 + (tile - n_ctx_tiles) // tiles_per_batch)


def _params(*semantics):
    return pltpu.CompilerParams(dimension_semantics=semantics, vmem_limit_bytes=VMEM_LIMIT)


MOD_TN = 512


def _mod_kernel(c_ref, w_ref, b_ref, o_ref):
    o_ref[...] = _bdot(_silu(c_ref[...]), w_ref[...]) + b_ref[...]


def _modulation(cond, w_mod, b_mod):
    n = 3 * D_MODEL
    return pl.pallas_call(
        _mod_kernel,
        grid=(DEPTH, n // MOD_TN),
        in_specs=[
            pl.BlockSpec((N_COND, D_MODEL), lambda l, j: (0, 0)),
            pl.BlockSpec((None, D_MODEL, MOD_TN), lambda l, j: (l, 0, j)),
            pl.BlockSpec((None, 1, MOD_TN), lambda l, j: (l, 0, j)),
        ],
        out_specs=pl.BlockSpec((None, N_COND, MOD_TN), lambda l, j: (l, 0, j)),
        out_shape=jax.ShapeDtypeStruct((DEPTH, N_COND, n), F32),
        compiler_params=_params("arbitrary", "arbitrary"),
        name="modulation",
    )(cond, w_mod, b_mod.reshape(DEPTH, 1, n))


IN_TM = 1024
IN_TN = 512
IN_LN_ROWS = 256


def _inproj_kernel(x_ref, scale_ref, shift_ref, w_ref, o_ref, h_ref):
    @pl.when(pl.program_id(1) == 0)
    def _():
        def body(r, carry):
            rows = pl.ds(pl.multiple_of(r * IN_LN_ROWS, IN_LN_ROWS), IN_LN_ROWS)
            h = _layer_norm(x_ref[rows, :]) * (1.0 + scale_ref[...]) + shift_ref[...]
            h_ref[rows, :] = h.astype(BF16)
            return carry
        lax.fori_loop(0, IN_TM // IN_LN_ROWS, body, 0)

    o_ref[...] = jnp.dot(h_ref[...], w_ref[...], preferred_element_type=F32)


def _in_projection(x, mod4, w_in_bf, layer):
    row = lambda i: _cond_row(i, IN_TM)
    return pl.pallas_call(
        _inproj_kernel,
        grid=(N_TOK // IN_TM, D_IN // IN_TN),
        in_specs=[
            pl.BlockSpec((IN_TM, D_MODEL), lambda i, j: (i, 0)),
            pl.BlockSpec((None, None, 1, D_MODEL), lambda i, j: (layer, row(i), 0, 1)),
            pl.BlockSpec((None, None, 1, D_MODEL), lambda i, j: (layer, row(i), 0, 0)),
            pl.BlockSpec((None, D_MODEL, IN_TN), lambda i, j: (layer, 0, j)),
        ],
        out_specs=pl.BlockSpec((IN_TM, IN_TN), lambda i, j: (i, j)),
        out_shape=jax.ShapeDtypeStruct((N_TOK, D_IN), F32),
        scratch_shapes=[pltpu.VMEM((IN_TM, D_MODEL), BF16)],
        compiler_params=_params("arbitrary", "arbitrary"),
        name="in_projection",
    )(x, mod4, mod4, w_in_bf)


GM_TM = 512


def _gmlp_kernel(u_ref, v_ref, g_ref, w_ref, b_ref, o_ref):
    vn = _layer_norm(v_ref[...]).astype(BF16)
    ws = [w_ref[g].astype(BF16) for g in range(N_GROUPS)]
    for n in range(GM_TM // CHUNK):
        rows = slice(n * CHUNK, (n + 1) * CHUNK)
        for g in range(N_GROUPS):
            cols = slice(g * GROUP_W, (g + 1) * GROUP_W)
            mixed = jnp.dot(ws[g], vn[rows, cols], preferred_element_type=F32) + b_ref[g]
            o_ref[rows, cols] = _silu(g_ref[rows, cols]) * u_ref[rows, cols] * mixed


def _mixer_gmlp(p, gmlp_w, gmlp_b_rows):
    blk = lambda c: pl.BlockSpec((GM_TM, BRANCH), lambda i: (i, c))
    return pl.pallas_call(
        _gmlp_kernel,
        grid=(N_TOK // GM_TM,),
        in_specs=[
            blk(C_AU), blk(C_AV), blk(C_AG),
            pl.BlockSpec((N_GROUPS, CHUNK, CHUNK), lambda i: (0, 0, 0)),
            pl.BlockSpec((N_GROUPS, CHUNK, GROUP_W), lambda i: (0, 0, 0)),
        ],
        out_specs=pl.BlockSpec((GM_TM, BRANCH), lambda i: (i, 0)),
        out_shape=jax.ShapeDtypeStruct((N_TOK, BRANCH), F32),
        compiler_params=_params("arbitrary"),
        name="mixer_gmlp",
    )(p, p, p, gmlp_w, gmlp_b_rows)


POOL_TM = 1024


def _pool_kernel(seq_len, x_ref, g_ref, w_ref, s_ref, o_ref):
    pos = lax.broadcasted_iota(jnp.int32, (POOL_TM, GROUP_W), 0) & (seq_len - 1)
    for g, win in enumerate(POOL_WINDOWS):
        cols = slice(g * GROUP_W, (g + 1) * GROUP_W)
        p = x_ref[:, cols]
        acc = p
        for d in range(-(win // 2), win // 2):
            if d == 0:
                continue
            shifted = pltpu.roll(p, (-d) % POOL_TM, axis=0)
            valid = (pos + d >= 0) & (pos + d < seq_len)
            acc = acc + jnp.where(valid, shifted, 0.0)
        count = jnp.minimum(pos + win // 2, seq_len) - jnp.maximum(pos - win // 2, 0)
        pooled = acc / count.astype(F32)
        y = _bdot(pooled - p, w_ref[g])
        o_ref[:, cols] = _silu(g_ref[:, cols]) * (y * s_ref[:, cols])


def _mixer_pool(p, pool_w, pool_scale_row, seq_len, row0, n_rows):
    t0 = row0 // POOL_TM
    blk = lambda c: pl.BlockSpec((POOL_TM, BRANCH), lambda i: (t0 + i, c))
    return pl.pallas_call(
        functools.partial(_pool_kernel, seq_len),
        grid=(n_rows // POOL_TM,),
        in_specs=[
            blk(C_BX), blk(C_BG),
            pl.BlockSpec((N_GROUPS, GROUP_W, GROUP_W), lambda i: (0, 0, 0)),
            pl.BlockSpec((1, BRANCH), lambda i: (0, 0)),
        ],
        out_specs=pl.BlockSpec((POOL_TM, BRANCH), lambda i: (i, 0)),
        out_shape=jax.ShapeDtypeStruct((n_rows, BRANCH), F32),
        compiler_params=_params("arbitrary"),
        name=f"mixer_pool_{seq_len}",
    )(p, p, pool_w, pool_scale_row)


ATT_TQ = 256
NT_DIMS = (((1,), (1,)), ((), ()))


def _lambda(lam_ref, lam_init):
    lq = lam_ref[...]
    a = jnp.sum(lq[0:1] * lq[1:2], axis=-1, keepdims=True)
    b = jnp.sum(lq[2:3] * lq[3:4], axis=-1, keepdims=True)
    return jnp.exp(a) - jnp.exp(b) + lam_init


def _map_masks():
    lane = lax.broadcasted_iota(jnp.int32, (1, HEAD_DIM_C), 1)
    m0 = (lane < QK_HALF).astype(F32)
    return m0, 1.0 - m0


def _softmax_parts(parts):
    m = functools.reduce(jnp.maximum, [jnp.max(s, axis=-1, keepdims=True) for s in parts])
    es = [jnp.exp(s - m) for s in parts]
    inv = 1.0 / functools.reduce(jnp.add, [jnp.sum(e, axis=-1, keepdims=True) for e in es])
    return [e * inv for e in es]


def _diff_head(q, keys, vals, lam, lam_init, subln, gate):
    m0, m1 = _map_masks()
    scale = QK_HALF ** -0.5
    q0 = (q * m0).astype(BF16)
    q1 = (q * m1).astype(BF16)
    s0 = [lax.dot_general(q0, k, NT_DIMS, preferred_element_type=F32) * scale for k in keys]
    s1 = [lax.dot_general(q1, k, NT_DIMS, preferred_element_type=F32) * scale for k in keys]
    p0 = _softmax_parts(s0)
    p1 = _softmax_parts(s1)
    o = None
    for a, b, v in zip(p0, p1, vals):
        w = (a - lam * b).astype(BF16)
        t = jnp.dot(w, v, preferred_element_type=F32)
        o = t if o is None else o + t
    o = o * lax.rsqrt(jnp.mean(o * o, axis=-1, keepdims=True) + 1e-5)
    o = o * subln * (1.0 - lam_init)
    return _silu(gate) * o


def _attn_ctx_kernel(lam_init, q_ref, k_ref, v_ref, g_ref, lam_ref, sw_ref, o_ref):
    lam = _lambda(lam_ref, lam_init)
    for h in range(N_HEADS_C):
        cols = slice(h * HEAD_DIM_C, (h + 1) * HEAD_DIM_C)
        o_ref[:, cols] = _diff_head(
            q_ref[:, cols], [k_ref[:, cols].astype(BF16)], [v_ref[:, cols].astype(BF16)],
            lam, lam_init, sw_ref[...], g_ref[:, cols])


def _mixer_attn_ctx(p, lambda_qk, subln_row, layer, lam_init):
    blk = lambda c: pl.BlockSpec((SEQ, BRANCH), lambda b: (b, c))
    return pl.pallas_call(
        functools.partial(_attn_ctx_kernel, lam_init),
        grid=(BATCH,),
        in_specs=[
            blk(C_Q), blk(C_K), blk(C_V), blk(C_CG),
            pl.BlockSpec((None, 4, QK_HALF), lambda b: (layer, 0, 0)),
            pl.BlockSpec((None, 1, HEAD_DIM_C), lambda b: (layer, 0, 0)),
        ],
        out_specs=pl.BlockSpec((SEQ, BRANCH), lambda b: (b, 0)),
        out_shape=jax.ShapeDtypeStruct((N_CTX, BRANCH), F32),
        compiler_params=_params("arbitrary"),
        name="mixer_attn_ctx",
    )(p, p, p, p, lambda_qk, subln_row)


def _rope(x, cos, sin_signed):
    lane = lax.broadcasted_iota(jnp.int32, x.shape, 1)
    first_half = (lane & (ROPE_AXIS_DIM - 1)) < (ROPE_AXIS_DIM // 2)
    half = ROPE_AXIS_DIM // 2
    partner = jnp.where(first_half,
                        pltpu.roll(x, x.shape[1] - half, axis=1),
                        pltpu.roll(x, half, axis=1))
    return x * cos + partner * sin_signed


def _attn_lat_kernel(lam_init, q_ref, k_ref, v_ref, g_ref, ck_ref, cv_ref, cosq_ref, sinq_ref,
                     cosk_ref, sink_ref, lam_ref, sw_ref, o_ref, kr_ref, vb_ref):
    @pl.when(pl.program_id(1) == 0)
    def _():
        for h in range(N_HEADS_C):
            cols = slice(h * HEAD_DIM_C, (h + 1) * HEAD_DIM_C)
            kr_ref[:, cols] = _rope(k_ref[:, cols], cosk_ref[...], sink_ref[...]).astype(BF16)
        vb_ref[...] = v_ref[...].astype(BF16)

    lam = _lambda(lam_ref, lam_init)
    for h in range(N_HEADS_C):
        cols = slice(h * HEAD_DIM_C, (h + 1) * HEAD_DIM_C)
        q = _rope(q_ref[:, cols], cosq_ref[...], sinq_ref[...])
        o_ref[:, cols] = _diff_head(
            q, [ck_ref[:, cols].astype(BF16), kr_ref[:, cols]],
            [cv_ref[:, cols].astype(BF16), vb_ref[:, cols]],
            lam, lam_init, sw_ref[...], g_ref[:, cols])


def _rope_tables():
    pos = np.arange(DEC_SEQ)
    row = (pos // GRID_W).astype(np.float64)
    col = (pos % GRID_W).astype(np.float64)
    half = ROPE_AXIS_DIM // 2
    inv = ROPE_BASE ** (-np.arange(0, ROPE_AXIS_DIM, 2, dtype=np.float64) / ROPE_AXIS_DIM)
    lane = np.arange(HEAD_DIM_C)
    axis_is_col = (lane // ROPE_AXIS_DIM) % 2 == 1
    idx = lane % ROPE_AXIS_DIM
    ang = np.where(axis_is_col[None, :], col[:, None], row[:, None]) * inv[idx % half][None, :]
    sign = np.where(idx < half, -1.0, 1.0)[None, :]
    return (jnp.asarray(np.cos(ang), F32), jnp.asarray(np.sin(ang) * sign, F32))


def _mixer_attn_lat(p, cache_k4, cache_v4, lambda_qk, subln_row, layer, lam_init):
    cos_t, sin_t = _rope_tables()
    q_tiles = DEC_SEQ // ATT_TQ
    q0 = N_CTX // ATT_TQ
    b0 = N_CTX // DEC_SEQ
    qblk = lambda c: pl.BlockSpec((ATT_TQ, BRANCH), lambda b, i: (q0 + b * q_tiles + i, c))
    kblk = lambda c: pl.BlockSpec((DEC_SEQ, BRANCH), lambda b, i: (b0 + b, c))
    cblk = pl.BlockSpec((None, None, PAST_LEN, BRANCH), lambda b, i: (b, layer, 0, 0))
    return pl.pallas_call(
        functools.partial(_attn_lat_kernel, lam_init),
        grid=(DEC_BATCH, q_tiles),
        in_specs=[
            qblk(C_Q), kblk(C_K), kblk(C_V), qblk(C_CG), cblk, cblk,
            pl.BlockSpec((ATT_TQ, HEAD_DIM_C), lambda b, i: (i, 0)),
            pl.BlockSpec((ATT_TQ, HEAD_DIM_C), lambda b, i: (i, 0)),
            pl.BlockSpec((DEC_SEQ, HEAD_DIM_C), lambda b, i: (0, 0)),
            pl.BlockSpec((DEC_SEQ, HEAD_DIM_C), lambda b, i: (0, 0)),
            pl.BlockSpec((None, 4, QK_HALF), lambda b, i: (layer, 0, 0)),
            pl.BlockSpec((None, 1, HEAD_DIM_C), lambda b, i: (layer, 0, 0)),
        ],
        out_specs=pl.BlockSpec((ATT_TQ, BRANCH), lambda b, i: (b * q_tiles + i, 0)),
        out_shape=jax.ShapeDtypeStruct((N_LAT, BRANCH), F32),
        scratch_shapes=[pltpu.VMEM((DEC_SEQ, BRANCH), BF16), pltpu.VMEM((DEC_SEQ, BRANCH), BF16)],
        compiler_params=_params("arbitrary", "arbitrary"),
        name="mixer_attn_lat",
    )(p, p, p, p, cache_k4, cache_v4, cos_t, sin_t, cos_t, sin_t, lambda_qk, subln_row)


def _dft_matrices(seq_len):
    n = 2 * seq_len
    f = np.arange(seq_len, dtype=np.float64)[:, None]
    s = np.arange(seq_len, dtype=np.float64)[None, :]
    theta = 2.0 * np.pi * f * s / n
    alt = np.where(np.arange(seq_len) % 2 == 0, 1.0, -1.0)
    ac = np.cos(theta)
    as_ = -np.sin(theta)
    as_[0, :] = alt
    bc = 2.0 * np.cos(theta.T) / n
    bc[:, 0] = 1.0 / n
    bs = -2.0 * np.sin(theta.T) / n
    bs[:, 0] = alt / n
    return tuple(jnp.asarray(m, F32) for m in (ac, as_, bc, bs))


def _filter_features(seq_len):
    t_idx = np.arange(seq_len, dtype=np.float64)
    t_norm = np.linspace(0.0, 1.0, seq_len)
    bands = np.linspace(1e-4, HY_BANDS - 1, HY_BANDS)
    ang = (2.0 * math.pi * t_idx / seq_len)[:, None] * bands[None, :]
    feats = np.concatenate([t_norm[:, None], np.cos(ang), np.sin(ang)], axis=-1)
    feats = np.pad(feats, ((0, 0), (0, HY_PAD - HY_EMB)))
    deltas = np.abs(np.linspace(math.log(HY_TARGET) / HY_FAST_DECAY,
                                math.log(HY_TARGET) / HY_SLOW_DECAY, BRANCH))
    return (jnp.asarray(feats, F32), jnp.asarray(t_norm[:, None], F32),
            jnp.asarray(deltas[None, :], F32))


def _filter_kernel(seq_len, feats_ref, tn_ref, dl_ref, w1_ref, b1_ref, w2_ref, b2_ref, fr_ref,
                   w3f_ref, w3b_ref, ac_ref, as_ref, kr_ref, ki_ref):
    sp = lambda x: _split(x, 2)
    fr = fr_ref[...]
    h = jnp.sin(fr * (_sdot(sp(feats_ref[...]), sp(w1_ref[...])) + b1_ref[...]))
    h = jnp.sin(fr * (_sdot(sp(h), sp(w2_ref[...])) + b2_ref[...]))
    h = sp(h)
    decay = jnp.exp(-tn_ref[...] * dl_ref[...])
    row = lax.broadcasted_iota(jnp.int32, (seq_len, HY_CT), 0)
    fwd = _sdot(h, sp(w3f_ref[...])) * decay
    bwd = jnp.where(row == 0, 0.0, _sdot(h, sp(w3b_ref[...])) * decay)
    norm = (jnp.sum(jnp.abs(fwd), axis=0, keepdims=True)
            + jnp.sum(jnp.abs(bwd), axis=0, keepdims=True))
    fwd = fwd / norm
    bwd = bwd / norm
    even = fwd + bwd
    alt = jnp.where((row & 1) == 0, 1.0, -1.0)
    nyquist = jnp.sum(alt * even, axis=0, keepdims=True)
    kr_ref[...] = _sdot(sp(ac_ref[...]), sp(even))
    ki = _sdot(sp(as_ref[...]), sp(fwd - bwd))
    ki_ref[...] = jnp.where(row == 0, nyquist, ki)


def _hyena_spectrum(seq_len, layer, filt, mats):
    feats, t_norm, deltas = _filter_features(seq_len)
    w1, b1, w2, b2, freq, w3 = filt
    ac, as_ = mats[0], mats[1]
    n_ct = BRANCH // HY_CT
    full = lambda a: pl.BlockSpec(a.shape, lambda o, c: (0,) * a.ndim)
    lyr = lambda r, n: pl.BlockSpec((None, r, n), lambda o, c: (layer, 0, 0))
    out = pl.BlockSpec((None, seq_len, HY_CT), lambda o, c: (o, 0, c))
    return pl.pallas_call(
        functools.partial(_filter_kernel, seq_len),
        grid=(HY_ORDER, n_ct),
        in_specs=[
            full(feats), full(t_norm),
            pl.BlockSpec((1, HY_CT), lambda o, c: (0, c)),
            lyr(HY_PAD, HY_PAD), lyr(1, HY_PAD), lyr(HY_PAD, HY_PAD), lyr(1, HY_PAD), lyr(1, HY_PAD),
            pl.BlockSpec((None, HY_PAD, HY_CT), lambda o, c: (layer, 0, o * 2 * n_ct + c)),
            pl.BlockSpec((None, HY_PAD, HY_CT), lambda o, c: (layer, 0, (o * 2 + 1) * n_ct + c)),
            full(ac), full(as_),
        ],
        out_specs=[out, out],
        out_shape=[jax.ShapeDtypeStruct((HY_ORDER, seq_len, BRANCH), F32)] * 2,
        compiler_params=_params("arbitrary", "arbitrary"),
        name=f"hyena_spectrum_{seq_len}",
    )(feats, t_norm, deltas, w1, b1, w2, b2, freq, w3, w3, ac, as_)


def _hyena_kernel(seq_len, n_terms, x1_ref, x2_ref, hv_ref, g_ref, cw1_ref, cw2_ref, cw3_ref,
                  cb1_ref, cb2_ref, cb3_ref, kr_ref, ki_ref, hb_ref, *rest):
    mats, o_ref = rest[:-1], rest[-1]
    ac, as_, bc, bs = (tuple(r[...] for r in mats[i * n_terms:(i + 1) * n_terms]) for i in range(4))
    row = lax.broadcasted_iota(jnp.int32, (seq_len, HY_CT), 0)
    first, last = row == 0, row == seq_len - 1

    def short_conv(x_ref, w_ref, b_ref):
        x, w = x_ref[...], w_ref[...]
        prev = jnp.where(first, 0.0, pltpu.roll(x, 1, axis=0))
        nxt = jnp.where(last, 0.0, pltpu.roll(x, seq_len - 1, axis=0))
        return prev * w[0:1] + x * w[1:2] + nxt * w[2:3] + b_ref[...]

    def long_conv(z, order):
        zt = _split(z, n_terms)
        zr = _sdot(ac, zt)
        zi = _sdot(as_, zt)
        kr, kp = kr_ref[order], ki_ref[order]
        ki = jnp.where(first, 0.0, kp)
        kn = jnp.where(first, kp, kr)
        yr = zr * kr - zi * ki
        yi = zr * ki + zi * kn
        y = _sdot(bc, _split(yr, n_terms)) + _sdot(bs, _split(yi, n_terms))
        return y + z * hb_ref[order:order + 1, :]

    g1 = short_conv(x1_ref, cw1_ref, cb1_ref)
    g2 = short_conv(x2_ref, cw2_ref, cb2_ref)
    z = short_conv(hv_ref, cw3_ref, cb3_ref)
    z = g1 * long_conv(z, 0)
    o_ref[...] = _silu(g_ref[...]) * (g2 * long_conv(z, 1))


def _mixer_hyena(p, conv_w, conv_b3, hyena_bias, spectrum, mats, layer, seq_len, row0, n_batch):
    kr, ki = spectrum
    n_ct = BRANCH // HY_CT
    b0 = row0 // seq_len
    mat_terms = [t for m in mats for t in _split(m, DFT_PASSES)]
    blk = lambda c: pl.BlockSpec((seq_len, HY_CT), lambda ct, b: (b0 + b, c * n_ct + ct))
    cw = lambda s: pl.BlockSpec((None, 3, HY_CT), lambda ct, b: (layer, 0, s * n_ct + ct))
    cb = lambda s: pl.BlockSpec((None, 1, HY_CT), lambda ct, b: (layer, 0, s * n_ct + ct))
    spec = pl.BlockSpec((HY_ORDER, seq_len, HY_CT), lambda ct, b: (0, 0, ct))
    const = pl.BlockSpec((seq_len, seq_len), lambda ct, b: (0, 0), pipeline_mode=pl.Buffered(1))
    return pl.pallas_call(
        functools.partial(_hyena_kernel, seq_len, DFT_PASSES),
        grid=(n_ct, n_batch),
        in_specs=[
            blk(C_DX1), blk(C_DX2), blk(C_DV), blk(C_DG),
            cw(0), cw(1), cw(2), cb(0), cb(1), cb(2),
            spec, spec,
            pl.BlockSpec((None, HY_ORDER, HY_CT), lambda ct, b: (layer, 0, ct)),
        ] + [const] * len(mat_terms),
        out_specs=pl.BlockSpec((seq_len, HY_CT), lambda ct, b: (b, ct)),
        out_shape=jax.ShapeDtypeStruct((n_batch * seq_len, BRANCH), F32),
        compiler_params=_params("arbitrary", "arbitrary"),
        name=f"mixer_hyena_{seq_len}",
    )(p, p, p, p, conv_w, conv_w, conv_w, conv_b3, conv_b3, conv_b3, kr, ki, hyena_bias, *mat_terms)


OUT_TM = 512


def _outproj_kernel(alpha, ya_ref, yb_ref, yc_ref, yd_ref, x_ref, gate_ref, w_ref, b_ref,
                    lng_ref, lnb_ref, o_ref):
    y = b_ref[...]
    for m, r in enumerate((ya_ref, yb_ref, yc_ref, yd_ref)):
        y = y + jnp.dot(r[...].astype(BF16), w_ref[m * BRANCH:(m + 1) * BRANCH, :],
                        preferred_element_type=F32)
    r = alpha * x_ref[...] + gate_ref[...] * y
    o_ref[...] = _layer_norm(r) * lng_ref[...] + lnb_ref[...]


def _out_projection(ya, yb, yc, yd, x, mod4, w_out_bf, b_out3, ln_g3, ln_b3, layer):
    alpha = (2.0 * DEPTH) ** 0.25
    row = lambda i: _cond_row(i, OUT_TM)
    mix = pl.BlockSpec((OUT_TM, BRANCH), lambda i: (i, 0))
    vec = pl.BlockSpec((None, 1, D_MODEL), lambda i: (layer, 0, 0))
    return pl.pallas_call(
        functools.partial(_outproj_kernel, alpha),
        grid=(N_TOK // OUT_TM,),
        in_specs=[
            mix, mix, mix, mix,
            pl.BlockSpec((OUT_TM, D_MODEL), lambda i: (i, 0)),
            pl.BlockSpec((None, None, 1, D_MODEL), lambda i: (layer, row(i), 0, 2)),
            pl.BlockSpec((None, D_MODEL, D_MODEL), lambda i: (layer, 0, 0)),
            vec, vec, vec,
        ],
        out_specs=pl.BlockSpec((OUT_TM, D_MODEL), lambda i: (i, 0)),
        out_shape=jax.ShapeDtypeStruct((N_TOK, D_MODEL), F32),
        compiler_params=_params("arbitrary"),
        name="out_projection",
    )(ya, yb, yc, yd, x, mod4, w_out_bf, b_out3, ln_g3, ln_b3)


def kernel(x_prompt, x_sample, cache_k, cache_v, c, c_ctx, w_mod, b_mod, w_in, gmlp_w, gmlp_b,
           pool_w, pool_scale, lambda_qk, subln_w, conv_w, conv_b, filt_w1, filt_b1, filt_w2,
           filt_b2, filt_freq, filt_w3, hyena_bias, w_out, b_out, ln_g, ln_b):
    x = jnp.concatenate([x_prompt.reshape(N_CTX, D_MODEL), x_sample.reshape(N_LAT, D_MODEL)], axis=0)
    cond = jnp.concatenate(
        [c_ctx[None, :], c, jnp.zeros((N_COND - 1 - DEC_BATCH, D_MODEL), F32)], axis=0)
    mod4 = _modulation(cond, w_mod, b_mod).reshape(DEPTH, N_COND, 1, 3 * D_MODEL)

    w_in_bf = w_in.astype(BF16)
    w_out_bf = w_out.astype(BF16)
    cache_k4 = cache_k.reshape(DEC_BATCH, DEPTH, PAST_LEN, BRANCH)
    cache_v4 = cache_v.reshape(DEC_BATCH, DEPTH, PAST_LEN, BRANCH)
    gmlp_b_rows = jnp.broadcast_to(gmlp_b[..., None], (DEPTH, N_GROUPS, CHUNK, GROUP_W))
    subln_row = subln_w.reshape(DEPTH, 1, HEAD_DIM_C)
    conv_b3 = conv_b.reshape(DEPTH, 1, 3 * BRANCH)
    b_out3 = b_out.reshape(DEPTH, 1, D_MODEL)
    ln_g3 = ln_g.reshape(DEPTH, 1, D_MODEL)
    ln_b3 = ln_b.reshape(DEPTH, 1, D_MODEL)

    pad_h = HY_PAD - HY_HIDDEN
    filt = (
        jnp.pad(filt_w1, ((0, 0), (0, HY_PAD - HY_EMB), (0, pad_h))),
        jnp.pad(filt_b1, ((0, 0), (0, pad_h))).reshape(DEPTH, 1, HY_PAD),
        jnp.pad(filt_w2, ((0, 0), (0, pad_h), (0, pad_h))),
        jnp.pad(filt_b2, ((0, 0), (0, pad_h))).reshape(DEPTH, 1, HY_PAD),
        jnp.pad(filt_freq, ((0, 0), (0, pad_h))).reshape(DEPTH, 1, HY_PAD),
        jnp.pad(filt_w3, ((0, 0), (0, pad_h), (0, 0))),
    )
    mats_ctx = _dft_matrices(SEQ)
    mats_lat = _dft_matrices(DEC_SEQ)

    new_k, new_v = [], []
    for layer in range(DEPTH):
        lam_init = 0.8 - 0.6 * math.exp(-0.3 * layer)
        p = _in_projection(x, mod4, w_in_bf, layer)
        new_k.append(p[:N_CTX, C_K * BRANCH:(C_K + 1) * BRANCH]
                     .reshape(BATCH, SEQ, N_HEADS_C, 2, QK_HALF))
        new_v.append(p[:N_CTX, C_V * BRANCH:(C_V + 1) * BRANCH]
                     .reshape(BATCH, SEQ, N_HEADS_C, HEAD_DIM_C))

        ya = _mixer_gmlp(p, gmlp_w[layer], gmlp_b_rows[layer])
        ps_row = pool_scale[layer].reshape(1, BRANCH)
        yb = jnp.concatenate([
            _mixer_pool(p, pool_w[layer], ps_row, SEQ, 0, N_CTX),
            _mixer_pool(p, pool_w[layer], ps_row, DEC_SEQ, N_CTX, N_LAT)], axis=0)
        yc = jnp.concatenate([
            _mixer_attn_ctx(p, lambda_qk, subln_row, layer, lam_init),
            _mixer_attn_lat(p, cache_k4, cache_v4, lambda_qk, subln_row, layer, lam_init)], axis=0)
        yd = jnp.concatenate([
            _mixer_hyena(p, conv_w, conv_b3, hyena_bias,
                         _hyena_spectrum(SEQ, layer, filt, mats_ctx), mats_ctx,
                         layer, SEQ, 0, BATCH),
            _mixer_hyena(p, conv_w, conv_b3, hyena_bias,
                         _hyena_spectrum(DEC_SEQ, layer, filt, mats_lat), mats_lat,
                         layer, DEC_SEQ, N_CTX, DEC_BATCH)], axis=0)
        x = _out_projection(ya, yb, yc, yd, x, mod4, w_out_bf, b_out3, ln_g3, ln_b3, layer)

    y_prompt = x[:N_CTX].reshape(BATCH, SEQ, D_MODEL)
    y_sample = x[N_CTX:].reshape(DEC_BATCH, DEC_SEQ, D_MODEL)
    return (y_prompt, y_sample, jnp.stack(new_k, axis=1), jnp.stack(new_v, axis=1))
```

```python
import functools
import math

import numpy as np
import jax
import jax.numpy as jnp
from jax import lax
from jax.experimental import pallas as pl
from jax.experimental.pallas import tpu as pltpu

F32 = jnp.float32
BF16 = jnp.bfloat16

D_MODEL = 2048
BATCH = 16
SEQ = 256
DEPTH = 2
DEC_BATCH = 4
DEC_SEQ = 1024
PAST_LEN = 512
GRID_W = 64
BRANCH = 512
N_GROUPS = 4
GROUP_W = 128
CHUNK = 128
POOL_WINDOWS = (2, 4, 8, 16)
N_HEADS_C = 4
HEAD_DIM_C = 128
QK_HALF = 64
ROPE_AXIS_DIM = 32
ROPE_BASE = 10000.0
HY_BANDS = 16
HY_EMB = 33
HY_HIDDEN = 64
HY_ORDER = 2
HY_FAST_DECAY = 0.3
HY_SLOW_DECAY = 1.5
HY_TARGET = 1e-2
N_IN_PIECES = 13
D_IN = N_IN_PIECES * BRANCH
LN_EPS = 1e-6

N_CTX = BATCH * SEQ
N_LAT = DEC_BATCH * DEC_SEQ
N_TOK = N_CTX + N_LAT
N_COND = 8
LANES = 128
HY_PAD = LANES
HY_CT = 256
DFT_PASSES = 1
VMEM_LIMIT = 56 * 1024 * 1024

(C_AU, C_AV, C_AG, C_BX, C_BG, C_Q, C_K, C_V, C_CG, C_DX1, C_DX2, C_DV, C_DG) = range(13)
D_MIX = 4 * BRANCH
(M_GMLP, M_POOL, M_ATTN, M_HYENA) = range(4)


def _silu(x):
    return x * jax.nn.sigmoid(x)


def _bdot(a, b):
    return jnp.dot(a.astype(BF16), b.astype(BF16), preferred_element_type=F32)


def _split(x, n_terms):
    hi = x.astype(BF16)
    if n_terms == 1:
        return (hi,)
    return (hi, (x - hi.astype(F32)).astype(BF16))


def _sdot(a_terms, b_terms):
    acc = jnp.dot(a_terms[0], b_terms[0], preferred_element_type=F32)
    if len(a_terms) > 1:
        acc = acc + jnp.dot(a_terms[1], b_terms[0], preferred_element_type=F32)
    if len(b_terms) > 1:
        acc = acc + jnp.dot(a_terms[0], b_terms[1], preferred_element_type=F32)
    return acc


def _layer_norm(x):
    mu = jnp.mean(x, axis=-1, keepdims=True)
    xc = x - mu
    var = jnp.mean(xc * xc, axis=-1, keepdims=True)
    return xc * lax.rsqrt(var + LN_EPS)


def _cond_row(tile, rows_per_tile):
    n_ctx_tiles = N_CTX // rows_per_tile
    tiles_per_batch = DEC_SEQ // rows_per_tile
    return jnp.where(tile < n_ctx_tiles, 0, 1 + (tile - n_ctx_tiles) // tiles_per_batch)


def _params(*semantics):
    return pltpu.CompilerParams(dimension_semantics=semantics, vmem_limit_bytes=VMEM_LIMIT)


MOD_TN = 512


def _mod_kernel(c_ref, w_ref, b_ref, o_ref):
    o_ref[...] = _bdot(_silu(c_ref[...]), w_ref[...]) + b_ref[...]


def _modulation(cond, w_mod, b_mod):
    n = 3 * D_MODEL
    return pl.pallas_call(
        _mod_kernel,
        grid=(DEPTH, n // MOD_TN),
        in_specs=[
            pl.BlockSpec((N_COND, D_MODEL), lambda l, j: (0, 0)),
            pl.BlockSpec((None, D_MODEL, MOD_TN), lambda l, j: (l, 0, j)),
            pl.BlockSpec((None, 1, MOD_TN), lambda l, j: (l, 0, j)),
        ],
        out_specs=pl.BlockSpec((None, N_COND, MOD_TN), lambda l, j: (l, 0, j)),
        out_shape=jax.ShapeDtypeStruct((DEPTH, N_COND, n), F32),
        compiler_params=_params("arbitrary", "arbitrary"),
        name="modulation",
    )(cond, w_mod, b_mod.reshape(DEPTH, 1, n))


IN_TM = 1024
IN_TN = 512
IN_LN_ROWS = 256


def _inproj_kernel(x_ref, scale_ref, shift_ref, w_ref, o_ref, h_ref):
    @pl.when(pl.program_id(1) == 0)
    def _():
        def body(r, carry):
            rows = pl.ds(pl.multiple_of(r * IN_LN_ROWS, IN_LN_ROWS), IN_LN_ROWS)
            h = _layer_norm(x_ref[rows, :]) * (1.0 + scale_ref[...]) + shift_ref[...]
            h_ref[rows, :] = h.astype(BF16)
            return carry
        lax.fori_loop(0, IN_TM // IN_LN_ROWS, body, 0)

    o_ref[...] = jnp.dot(h_ref[...], w_ref[...], preferred_element_type=F32)


def _in_projection(x, mod4, w_in_bf, layer):
    row = lambda i: _cond_row(i, IN_TM)
    return pl.pallas_call(
        _inproj_kernel,
        grid=(N_TOK // IN_TM, D_IN // IN_TN),
        in_specs=[
            pl.BlockSpec((IN_TM, D_MODEL), lambda i, j: (i, 0)),
            pl.BlockSpec((None, None, 1, D_MODEL), lambda i, j: (layer, row(i), 0, 1)),
            pl.BlockSpec((None, None, 1, D_MODEL), lambda i, j: (layer, row(i), 0, 0)),
            pl.BlockSpec((None, D_MODEL, IN_TN), lambda i, j: (layer, 0, j)),
        ],
        out_specs=pl.BlockSpec((IN_TM, IN_TN), lambda i, j: (i, j)),
        out_shape=jax.ShapeDtypeStruct((N_TOK, D_IN), F32),
        scratch_shapes=[pltpu.VMEM((IN_TM, D_MODEL), BF16)],
        compiler_params=_params("arbitrary", "arbitrary"),
        name="in_projection",
    )(x, mod4, mod4, w_in_bf)


GM_TM = 512


def _gmlp_kernel(u_ref, v_ref, g_ref, w_ref, b_ref, o_ref):
    vn = _layer_norm(v_ref[...]).astype(BF16)
    ws = [w_ref[g].astype(BF16) for g in range(N_GROUPS)]
    for n in range(GM_TM // CHUNK):
        rows = slice(n * CHUNK, (n + 1) * CHUNK)
        for g in range(N_GROUPS):
            cols = slice(g * GROUP_W, (g + 1) * GROUP_W)
            mixed = jnp.dot(ws[g], vn[rows, cols], preferred_element_type=F32) + b_ref[g]
            o_ref[rows, cols] = (_silu(g_ref[rows, cols]) * u_ref[rows, cols] * mixed).astype(BF16)


def _mixer_gmlp(p, gmlp_w, gmlp_b_rows):
    blk = lambda c: pl.BlockSpec((GM_TM, BRANCH), lambda i: (i, c))
    return pl.pallas_call(
        _gmlp_kernel,
        grid=(N_TOK // GM_TM,),
        in_specs=[
            blk(C_AU), blk(C_AV), blk(C_AG),
            pl.BlockSpec((N_GROUPS, CHUNK, CHUNK), lambda i: (0, 0, 0)),
            pl.BlockSpec((N_GROUPS, CHUNK, GROUP_W), lambda i: (0, 0, 0)),
        ],
        out_specs=pl.BlockSpec((GM_TM, BRANCH), lambda i: (i, M_GMLP)),
        out_shape=jax.ShapeDtypeStruct((N_TOK, D_MIX), BF16),
        compiler_params=_params("arbitrary"),
        name="mixer_gmlp",
    )(p, p, p, gmlp_w, gmlp_b_rows)


POOL_TM = 1024


def _pool_kernel(x_ref, g_ref, w_ref, s_ref, ymix_ref, o_ref):
    del ymix_ref
    seq_len = jnp.where(pl.program_id(0) < N_CTX // POOL_TM, SEQ, DEC_SEQ)
    pos = lax.broadcasted_iota(jnp.int32, (POOL_TM, GROUP_W), 0) & (seq_len - 1)
    for g, win in enumerate(POOL_WINDOWS):
        cols = slice(g * GROUP_W, (g + 1) * GROUP_W)
        p = x_ref[:, cols]
        acc = p
        for d in range(-(win // 2), win // 2):
            if d == 0:
                continue
            shifted = pltpu.roll(p, (-d) % POOL_TM, axis=0)
            valid = (pos + d >= 0) & (pos + d < seq_len)
            acc = acc + jnp.where(valid, shifted, 0.0)
        count = jnp.minimum(pos + win // 2, seq_len) - jnp.maximum(pos - win // 2, 0)
        pooled = acc / count.astype(F32)
        y = _bdot(pooled - p, w_ref[g])
        o_ref[:, cols] = (_silu(g_ref[:, cols]) * (y * s_ref[:, cols])).astype(BF16)


def _mixer_pool(p, pool_w, pool_scale_row, ymix):
    blk = lambda c: pl.BlockSpec((POOL_TM, BRANCH), lambda i: (i, c))
    return pl.pallas_call(
        _pool_kernel,
        grid=(N_TOK // POOL_TM,),
        in_specs=[
            blk(C_BX), blk(C_BG),
            pl.BlockSpec((N_GROUPS, GROUP_W, GROUP_W), lambda i: (0, 0, 0)),
            pl.BlockSpec((1, BRANCH), lambda i: (0, 0)),
            pl.BlockSpec(memory_space=pl.ANY),
        ],
        out_specs=pl.BlockSpec((POOL_TM, BRANCH), lambda i: (i, M_POOL)),
        out_shape=jax.ShapeDtypeStruct((N_TOK, D_MIX), BF16),
        input_output_aliases={4: 0},
        compiler_params=_params("arbitrary"),
        name="mixer_pool",
    )(p, p, pool_w, pool_scale_row, ymix)


ATT_TQ = 256
NT_DIMS = (((1,), (1,)), ((), ()))


def _lambda(lam_ref, lam_init):
    lq = lam_ref[...]
    a = jnp.sum(lq[0:1] * lq[1:2], axis=-1, keepdims=True)
    b = jnp.sum(lq[2:3] * lq[3:4], axis=-1, keepdims=True)
    return jnp.exp(a) - jnp.exp(b) + lam_init


def _map_masks():
    lane = lax.broadcasted_iota(jnp.int32, (1, HEAD_DIM_C), 1)
    m0 = (lane < QK_HALF).astype(F32)
    return m0, 1.0 - m0


def _softmax_parts(parts):
    m = functools.reduce(jnp.maximum, [jnp.max(s, axis=-1, keepdims=True) for s in parts])
    es = [jnp.exp(s - m) for s in parts]
    inv = 1.0 / functools.reduce(jnp.add, [jnp.sum(e, axis=-1, keepdims=True) for e in es])
    return [e * inv for e in es]


def _diff_head(q, keys, vals, lam, lam_init, subln, gate):
    m0, m1 = _map_masks()
    scale = QK_HALF ** -0.5
    q0 = (q * m0).astype(BF16)
    q1 = (q * m1).astype(BF16)
    s0 = [lax.dot_general(q0, k, NT_DIMS, preferred_element_type=F32) * scale for k in keys]
    s1 = [lax.dot_general(q1, k, NT_DIMS, preferred_element_type=F32) * scale for k in keys]
    p0 = _softmax_parts(s0)
    p1 = _softmax_parts(s1)
    o = None
    for a, b, v in zip(p0, p1, vals):
        w = (a - lam * b).astype(BF16)
        t = jnp.dot(w, v, preferred_element_type=F32)
        o = t if o is None else o + t
    o = o * lax.rsqrt(jnp.mean(o * o, axis=-1, keepdims=True) + 1e-5)
    o = o * subln * (1.0 - lam_init)
    return _silu(gate) * o


def _attn_ctx_kernel(lam_init, q_ref, k_ref, v_ref, g_ref, lam_ref, sw_ref, *rest):
    o_ref, ko_ref, vo_ref = rest[-3:]
    ko_ref[...] = k_ref[...]
    vo_ref[...] = v_ref[...]
    lam = _lambda(lam_ref, lam_init)
    for h in range(N_HEADS_C):
        cols = slice(h * HEAD_DIM_C, (h + 1) * HEAD_DIM_C)
        o_ref[:, cols] = _diff_head(
            q_ref[:, cols], [k_ref[:, cols].astype(BF16)], [v_ref[:, cols].astype(BF16)],
            lam, lam_init, sw_ref[...], g_ref[:, cols]).astype(BF16)


def _mixer_attn_ctx(p, lambda_qk, subln_row, layer, lam_init, ymix, caches):
    blk = lambda c: pl.BlockSpec((SEQ, BRANCH), lambda b: (b, c))
    any_spec = pl.BlockSpec(memory_space=pl.ANY)
    cache_spec = pl.BlockSpec((None, None, SEQ, BRANCH), lambda b: (b, layer, 0, 0))
    cache_shape = jax.ShapeDtypeStruct((BATCH, DEPTH, SEQ, BRANCH), F32)
    in_specs = [
        blk(C_Q), blk(C_K), blk(C_V), blk(C_CG),
        pl.BlockSpec((None, 4, QK_HALF), lambda b: (layer, 0, 0)),
        pl.BlockSpec((None, 1, HEAD_DIM_C), lambda b: (layer, 0, 0)),
        any_spec,
    ]
    args = [p, p, p, p, lambda_qk, subln_row, ymix]
    aliases = {6: 0}
    if caches is not None:
        in_specs += [any_spec, any_spec]
        args += list(caches)
        aliases.update({7: 1, 8: 2})
    return pl.pallas_call(
        functools.partial(_attn_ctx_kernel, lam_init),
        grid=(BATCH,),
        in_specs=in_specs,
        out_specs=[pl.BlockSpec((SEQ, BRANCH), lambda b: (b, M_ATTN)), cache_spec, cache_spec],
        out_shape=[jax.ShapeDtypeStruct((N_TOK, D_MIX), BF16), cache_shape, cache_shape],
        input_output_aliases=aliases,
        compiler_params=_params("arbitrary"),
        name="mixer_attn_ctx",
    )(*args)


def _rope(x, cos, sin_signed):
    lane = lax.broadcasted_iota(jnp.int32, x.shape, 1)
    first_half = (lane & (ROPE_AXIS_DIM - 1)) < (ROPE_AXIS_DIM // 2)
    half = ROPE_AXIS_DIM // 2
    partner = jnp.where(first_half,
                        pltpu.roll(x, x.shape[1] - half, axis=1),
                        pltpu.roll(x, half, axis=1))
    return x * cos + partner * sin_signed


def _attn_lat_kernel(lam_init, q_ref, k_ref, v_ref, g_ref, ck_ref, cv_ref, cosq_ref, sinq_ref,
                     cosk_ref, sink_ref, lam_ref, sw_ref, ymix_ref, o_ref, kr_ref, vb_ref):
    del ymix_ref
    @pl.when(pl.program_id(1) == 0)
    def _():
        for h in range(N_HEADS_C):
            cols = slice(h * HEAD_DIM_C, (h + 1) * HEAD_DIM_C)
            kr_ref[:, cols] = _rope(k_ref[:, cols], cosk_ref[...], sink_ref[...]).astype(BF16)
        vb_ref[...] = v_ref[...].astype(BF16)

    lam = _lambda(lam_ref, lam_init)
    for h in range(N_HEADS_C):
        cols = slice(h * HEAD_DIM_C, (h + 1) * HEAD_DIM_C)
        q = _rope(q_ref[:, cols], cosq_ref[...], sinq_ref[...])
        o_ref[:, cols] = _diff_head(
            q, [ck_ref[:, cols].astype(BF16), kr_ref[:, cols]],
            [cv_ref[:, cols].astype(BF16), vb_ref[:, cols]],
            lam, lam_init, sw_ref[...], g_ref[:, cols]).astype(BF16)


def _rope_tables():
    pos = np.arange(DEC_SEQ)
    row = (pos // GRID_W).astype(np.float64)
    col = (pos % GRID_W).astype(np.float64)
    half = ROPE_AXIS_DIM // 2
    inv = ROPE_BASE ** (-np.arange(0, ROPE_AXIS_DIM, 2, dtype=np.float64) / ROPE_AXIS_DIM)
    lane = np.arange(HEAD_DIM_C)
    axis_is_col = (lane // ROPE_AXIS_DIM) % 2 == 1
    idx = lane % ROPE_AXIS_DIM
    ang = np.where(axis_is_col[None, :], col[:, None], row[:, None]) * inv[idx % half][None, :]
    sign = np.where(idx < half, -1.0, 1.0)[None, :]
    return (jnp.asarray(np.cos(ang), F32), jnp.asarray(np.sin(ang) * sign, F32))


def _mixer_attn_lat(p, cache_k4, cache_v4, lambda_qk, subln_row, layer, lam_init, ymix):
    cos_t, sin_t = _rope_tables()
    q_tiles = DEC_SEQ // ATT_TQ
    q0 = N_CTX // ATT_TQ
    b0 = N_CTX // DEC_SEQ
    qblk = lambda c: pl.BlockSpec((ATT_TQ, BRANCH), lambda b, i: (q0 + b * q_tiles + i, c))
    kblk = lambda c: pl.BlockSpec((DEC_SEQ, BRANCH), lambda b, i: (b0 + b, c))
    cblk = pl.BlockSpec((None, None, PAST_LEN, BRANCH), lambda b, i: (b, layer, 0, 0))
    return pl.pallas_call(
        functools.partial(_attn_lat_kernel, lam_init),
        grid=(DEC_BATCH, q_tiles),
        in_specs=[
            qblk(C_Q), kblk(C_K), kblk(C_V), qblk(C_CG), cblk, cblk,
            pl.BlockSpec((ATT_TQ, HEAD_DIM_C), lambda b, i: (i, 0)),
            pl.BlockSpec((ATT_TQ, HEAD_DIM_C), lambda b, i: (i, 0)),
            pl.BlockSpec((DEC_SEQ, HEAD_DIM_C), lambda b, i: (0, 0)),
            pl.BlockSpec((DEC_SEQ, HEAD_DIM_C), lambda b, i: (0, 0)),
            pl.BlockSpec((None, 4, QK_HALF), lambda b, i: (layer, 0, 0)),
            pl.BlockSpec((None, 1, HEAD_DIM_C), lambda b, i: (layer, 0, 0)),
            pl.BlockSpec(memory_space=pl.ANY),
        ],
        out_specs=pl.BlockSpec((ATT_TQ, BRANCH), lambda b, i: (q0 + b * q_tiles + i, M_ATTN)),
        out_shape=jax.ShapeDtypeStruct((N_TOK, D_MIX), BF16),
        input_output_aliases={12: 0},
        scratch_shapes=[pltpu.VMEM((DEC_SEQ, BRANCH), BF16), pltpu.VMEM((DEC_SEQ, BRANCH), BF16)],
        compiler_params=_params("arbitrary", "arbitrary"),
        name="mixer_attn_lat",
    )(p, p, p, p, cache_k4, cache_v4, cos_t, sin_t, cos_t, sin_t, lambda_qk, subln_row, ymix)


def _dft_matrices(seq_len):
    n = 2 * seq_len
    f = np.arange(seq_len, dtype=np.float64)[:, None]
    s = np.arange(seq_len, dtype=np.float64)[None, :]
    theta = 2.0 * np.pi * f * s / n
    alt = np.where(np.arange(seq_len) % 2 == 0, 1.0, -1.0)
    ac = np.cos(theta)
    as_ = -np.sin(theta)
    as_[0, :] = alt
    bc = 2.0 * np.cos(theta.T) / n
    bc[:, 0] = 1.0 / n
    bs = -2.0 * np.sin(theta.T) / n
    bs[:, 0] = alt / n
    return tuple(jnp.asarray(m, F32) for m in (ac, as_, bc, bs))


def _filter_features(seq_len):
    t_idx = np.arange(seq_len, dtype=np.float64)
    t_norm = np.linspace(0.0, 1.0, seq_len)
    bands = np.linspace(1e-4, HY_BANDS - 1, HY_BANDS)
    ang = (2.0 * math.pi * t_idx / seq_len)[:, None] * bands[None, :]
    feats = np.concatenate([t_norm[:, None], np.cos(ang), np.sin(ang)], axis=-1)
    feats = np.pad(feats, ((0, 0), (0, HY_PAD - HY_EMB)))
    deltas = np.abs(np.linspace(math.log(HY_TARGET) / HY_FAST_DECAY,
                                math.log(HY_TARGET) / HY_SLOW_DECAY, BRANCH))
    return (jnp.asarray(feats, F32), jnp.asarray(t_norm[:, None], F32),
            jnp.asarray(deltas[None, :], F32))


def _filter_kernel(seq_len, feats_ref, tn_ref, dl_ref, w1_ref, b1_ref, w2_ref, b2_ref, fr_ref,
                   w3f_ref, w3b_ref, ac_ref, as_ref, kr_ref, ki_ref):
    sp = lambda x: _split(x, 2)
    fr = fr_ref[...]
    h = jnp.sin(fr * (_sdot(sp(feats_ref[...]), sp(w1_ref[...])) + b1_ref[...]))
    h = jnp.sin(fr * (_sdot(sp(h), sp(w2_ref[...])) + b2_ref[...]))
    h = sp(h)
    decay = jnp.exp(-tn_ref[...] * dl_ref[...])
    row = lax.broadcasted_iota(jnp.int32, (seq_len, HY_CT), 0)
    fwd = _sdot(h, sp(w3f_ref[...])) * decay
    bwd = jnp.where(row == 0, 0.0, _sdot(h, sp(w3b_ref[...])) * decay)
    norm = (jnp.sum(jnp.abs(fwd), axis=0, keepdims=True)
            + jnp.sum(jnp.abs(bwd), axis=0, keepdims=True))
    fwd = fwd / norm
    bwd = bwd / norm
    even = fwd + bwd
    alt = jnp.where((row & 1) == 0, 1.0, -1.0)
    nyquist = jnp.sum(alt * even, axis=0, keepdims=True)
    kr_ref[...] = _sdot(sp(ac_ref[...]), sp(even))
    ki = _sdot(sp(as_ref[...]), sp(fwd - bwd))
    ki_ref[...] = jnp.where(row == 0, nyquist, ki)


def _hyena_spectrum(seq_len, layer, filt, mats):
    feats, t_norm, deltas = _filter_features(seq_len)
    w1, b1, w2, b2, freq, w3 = filt
    ac, as_ = mats[0], mats[1]
    n_ct = BRANCH // HY_CT
    full = lambda a: pl.BlockSpec(a.shape, lambda o, c: (0,) * a.ndim)
    lyr = lambda r, n: pl.BlockSpec((None, r, n), lambda o, c: (layer, 0, 0))
    out = pl.BlockSpec((None, seq_len, HY_CT), lambda o, c: (o, 0, c))
    return pl.pallas_call(
        functools.partial(_filter_kernel, seq_len),
        grid=(HY_ORDER, n_ct),
        in_specs=[
            full(feats), full(t_norm),
            pl.BlockSpec((1, HY_CT), lambda o, c: (0, c)),
            lyr(HY_PAD, HY_PAD), lyr(1, HY_PAD), lyr(HY_PAD, HY_PAD), lyr(1, HY_PAD), lyr(1, HY_PAD),
            pl.BlockSpec((None, HY_PAD, HY_CT), lambda o, c: (layer, 0, o * 2 * n_ct + c)),
            pl.BlockSpec((None, HY_PAD, HY_CT), lambda o, c: (layer, 0, (o * 2 + 1) * n_ct + c)),
            full(ac), full(as_),
        ],
        out_specs=[out, out],
        out_shape=[jax.ShapeDtypeStruct((HY_ORDER, seq_len, BRANCH), F32)] * 2,
        compiler_params=_params("arbitrary", "arbitrary"),
        name=f"hyena_spectrum_{seq_len}",
    )(feats, t_norm, deltas, w1, b1, w2, b2, freq, w3, w3, ac, as_)


def _hyena_kernel(seq_len, n_terms, x1_ref, x2_ref, hv_ref, g_ref, cw1_ref, cw2_ref, cw3_ref,
                  cb1_ref, cb2_ref, cb3_ref, kr_ref, ki_ref, hb_ref, *rest):
    mats, o_ref = rest[:-2], rest[-1]
    ac, as_, bc, bs = (tuple(r[...] for r in mats[i * n_terms:(i + 1) * n_terms]) for i in range(4))
    row = lax.broadcasted_iota(jnp.int32, (seq_len, HY_CT), 0)
    first, last = row == 0, row == seq_len - 1

    def short_conv(x_ref, w_ref, b_ref):
        x, w = x_ref[...], w_ref[...]
        prev = jnp.where(first, 0.0, pltpu.roll(x, 1, axis=0))
        nxt = jnp.where(last, 0.0, pltpu.roll(x, seq_len - 1, axis=0))
        return prev * w[0:1] + x * w[1:2] + nxt * w[2:3] + b_ref[...]

    def long_conv(z, order):
        zt = _split(z, n_terms)
        zr = _sdot(ac, zt)
        zi = _sdot(as_, zt)
        kr, kp = kr_ref[order], ki_ref[order]
        ki = jnp.where(first, 0.0, kp)
        kn = jnp.where(first, kp, kr)
        yr = zr * kr - zi * ki
        yi = zr * ki + zi * kn
        y = _sdot(bc, _split(yr, n_terms)) + _sdot(bs, _split(yi, n_terms))
        return y + z * hb_ref[order:order + 1, :]

    g1 = short_conv(x1_ref, cw1_ref, cb1_ref)
    g2 = short_conv(x2_ref, cw2_ref, cb2_ref)
    z = short_conv(hv_ref, cw3_ref, cb3_ref)
    z = g1 * long_conv(z, 0)
    o_ref[...] = (_silu(g_ref[...]) * (g2 * long_conv(z, 1))).astype(BF16)


def _mixer_hyena(p, conv_w, conv_b3, hyena_bias, spectrum, mats, layer, seq_len, row0, n_batch,
                 ymix):
    kr, ki = spectrum
    n_ct = BRANCH // HY_CT
    b0 = row0 // seq_len
    mat_terms = [t for m in mats for t in _split(m, DFT_PASSES)]
    blk = lambda c: pl.BlockSpec((seq_len, HY_CT), lambda ct, b: (b0 + b, c * n_ct + ct))
    cw = lambda s: pl.BlockSpec((None, 3, HY_CT), lambda ct, b: (layer, 0, s * n_ct + ct))
    cb = lambda s: pl.BlockSpec((None, 1, HY_CT), lambda ct, b: (layer, 0, s * n_ct + ct))
    spec = pl.BlockSpec((HY_ORDER, seq_len, HY_CT), lambda ct, b: (0, 0, ct))
    const = pl.BlockSpec((seq_len, seq_len), lambda ct, b: (0, 0), pipeline_mode=pl.Buffered(1))
    return pl.pallas_call(
        functools.partial(_hyena_kernel, seq_len, DFT_PASSES),
        grid=(n_ct, n_batch),
        in_specs=[
            blk(C_DX1), blk(C_DX2), blk(C_DV), blk(C_DG),
            cw(0), cw(1), cw(2), cb(0), cb(1), cb(2),
            spec, spec,
            pl.BlockSpec((None, HY_ORDER, HY_CT), lambda ct, b: (layer, 0, ct)),
        ] + [const] * len(mat_terms) + [pl.BlockSpec(memory_space=pl.ANY)],
        out_specs=pl.BlockSpec((seq_len, HY_CT), lambda ct, b: (b0 + b, M_HYENA * n_ct + ct)),
        out_shape=jax.ShapeDtypeStruct((N_TOK, D_MIX), BF16),
        input_output_aliases={13 + len(mat_terms): 0},
        compiler_params=_params("arbitrary", "arbitrary"),
        name=f"mixer_hyena_{seq_len}",
    )(p, p, p, p, conv_w, conv_w, conv_w, conv_b3, conv_b3, conv_b3, kr, ki, hyena_bias, *mat_terms,
      ymix)


OUT_TM = 512


def _outproj_kernel(alpha, ymix_ref, x_ref, gate_ref, w_ref, b_ref, lng_ref, lnb_ref, o_ref):
    y = jnp.dot(ymix_ref[...], w_ref[...], preferred_element_type=F32) + b_ref[...]
    r = alpha * x_ref[...] + gate_ref[...] * y
    o_ref[...] = _layer_norm(r) * lng_ref[...] + lnb_ref[...]


def _out_projection(ymix, x, mod4, w_out_bf, b_out3, ln_g3, ln_b3, layer):
    alpha = (2.0 * DEPTH) ** 0.25
    row = lambda i: _cond_row(i, OUT_TM)
    vec = pl.BlockSpec((None, 1, D_MODEL), lambda i: (layer, 0, 0))
    return pl.pallas_call(
        functools.partial(_outproj_kernel, alpha),
        grid=(N_TOK // OUT_TM,),
        in_specs=[
            pl.BlockSpec((OUT_TM, D_MIX), lambda i: (i, 0)),
            pl.BlockSpec((OUT_TM, D_MODEL), lambda i: (i, 0)),
            pl.BlockSpec((None, None, 1, D_MODEL), lambda i: (layer, row(i), 0, 2)),
            pl.BlockSpec((None, D_MODEL, D_MODEL), lambda i: (layer, 0, 0)),
            vec, vec, vec,
        ],
        out_specs=pl.BlockSpec((OUT_TM, D_MODEL), lambda i: (i, 0)),
        out_shape=jax.ShapeDtypeStruct((N_TOK, D_MODEL), F32),
        compiler_params=_params("arbitrary"),
        name="out_projection",
    )(ymix, x, mod4, w_out_bf, b_out3, ln_g3, ln_b3)


def kernel(x_prompt, x_sample, cache_k, cache_v, c, c_ctx, w_mod, b_mod, w_in, gmlp_w, gmlp_b,
           pool_w, pool_scale, lambda_qk, subln_w, conv_w, conv_b, filt_w1, filt_b1, filt_w2,
           filt_b2, filt_freq, filt_w3, hyena_bias, w_out, b_out, ln_g, ln_b):
    x = jnp.concatenate([x_prompt.reshape(N_CTX, D_MODEL), x_sample.reshape(N_LAT, D_MODEL)], axis=0)
    cond = jnp.concatenate(
        [c_ctx[None, :], c, jnp.zeros((N_COND - 1 - DEC_BATCH, D_MODEL), F32)], axis=0)
    mod4 = _modulation(cond, w_mod, b_mod).reshape(DEPTH, N_COND, 1, 3 * D_MODEL)

    w_in_bf = w_in.astype(BF16)
    w_out_bf = w_out.astype(BF16)
    cache_k4 = cache_k.reshape(DEC_BATCH, DEPTH, PAST_LEN, BRANCH)
    cache_v4 = cache_v.reshape(DEC_BATCH, DEPTH, PAST_LEN, BRANCH)
    gmlp_b_rows = jnp.broadcast_to(gmlp_b[..., None], (DEPTH, N_GROUPS, CHUNK, GROUP_W))
    subln_row = subln_w.reshape(DEPTH, 1, HEAD_DIM_C)
    conv_b3 = conv_b.reshape(DEPTH, 1, 3 * BRANCH)
    b_out3 = b_out.reshape(DEPTH, 1, D_MODEL)
    ln_g3 = ln_g.reshape(DEPTH, 1, D_MODEL)
    ln_b3 = ln_b.reshape(DEPTH, 1, D_MODEL)

    pad_h = HY_PAD - HY_HIDDEN
    filt = (
        jnp.pad(filt_w1, ((0, 0), (0, HY_PAD - HY_EMB), (0, pad_h))),
        jnp.pad(filt_b1, ((0, 0), (0, pad_h))).reshape(DEPTH, 1, HY_PAD),
        jnp.pad(filt_w2, ((0, 0), (0, pad_h), (0, pad_h))),
        jnp.pad(filt_b2, ((0, 0), (0, pad_h))).reshape(DEPTH, 1, HY_PAD),
        jnp.pad(filt_freq, ((0, 0), (0, pad_h))).reshape(DEPTH, 1, HY_PAD),
        jnp.pad(filt_w3, ((0, 0), (0, pad_h), (0, 0))),
    )
    mats_ctx = _dft_matrices(SEQ)
    mats_lat = _dft_matrices(DEC_SEQ)

    caches = None
    for layer in range(DEPTH):
        lam_init = 0.8 - 0.6 * math.exp(-0.3 * layer)
        p = _in_projection(x, mod4, w_in_bf, layer)
        ymix = _mixer_gmlp(p, gmlp_w[layer], gmlp_b_rows[layer])
        ymix = _mixer_pool(p, pool_w[layer], pool_scale[layer].reshape(1, BRANCH), ymix)
        ymix, new_k, new_v = _mixer_attn_ctx(p, lambda_qk, subln_row, layer, lam_init, ymix, caches)
        caches = (new_k, new_v)
        ymix = _mixer_attn_lat(p, cache_k4, cache_v4, lambda_qk, subln_row, layer, lam_init, ymix)
        ymix = _mixer_hyena(p, conv_w, conv_b3, hyena_bias,
                            _hyena_spectrum(SEQ, layer, filt, mats_ctx), mats_ctx,
                            layer, SEQ, 0, BATCH, ymix)
        ymix = _mixer_hyena(p, conv_w, conv_b3, hyena_bias,
                            _hyena_spectrum(DEC_SEQ, layer, filt, mats_lat), mats_lat,
                            layer, DEC_SEQ, N_CTX, DEC_BATCH, ymix)
        x = _out_projection(ymix, x, mod4, w_out_bf, b_out3, ln_g3, ln_b3, layer)

    y_prompt = x[:N_CTX].reshape(BATCH, SEQ, D_MODEL)
    y_sample = x[N_CTX:].reshape(DEC_BATCH, DEC_SEQ, D_MODEL)
    new_k, new_v = caches
    return (y_prompt, y_sample,
            new_k.reshape(BATCH, DEPTH, SEQ, N_HEADS_C, 2, QK_HALF),
            new_v.reshape(BATCH, DEPTH, SEQ, N_HEADS_C, HEAD_DIM_C))
```

```python
import functools
import math

import numpy as np
import jax
import jax.numpy as jnp
from jax import lax
from jax.experimental import pallas as pl
from jax.experimental.pallas import tpu as pltpu

F32 = jnp.float32
BF16 = jnp.bfloat16

D_MODEL = 2048
BATCH = 16
SEQ = 256
DEPTH = 2
DEC_BATCH = 4
DEC_SEQ = 1024
PAST_LEN = 512
GRID_W = 64
BRANCH = 512
N_GROUPS = 4
GROUP_W = 128
CHUNK = 128
POOL_WINDOWS = (2, 4, 8, 16)
N_HEADS_C = 4
HEAD_DIM_C = 128
QK_HALF = 64
ROPE_AXIS_DIM = 32
ROPE_BASE = 10000.0
HY_BANDS = 16
HY_EMB = 33
HY_HIDDEN = 64
HY_ORDER = 2
HY_FAST_DECAY = 0.3
HY_SLOW_DECAY = 1.5
HY_TARGET = 1e-2
N_IN_PIECES = 13
D_IN = N_IN_PIECES * BRANCH
LN_EPS = 1e-6

N_CTX = BATCH * SEQ
N_LAT = DEC_BATCH * DEC_SEQ
N_TOK = N_CTX + N_LAT
N_COND = 8
LANES = 128
HY_PAD = LANES
HY_CT = 256
VMEM_LIMIT = 56 * 1024 * 1024

(C_AU, C_AV, C_AG, C_BX, C_BG, C_Q, C_K, C_V, C_CG, C_DX1, C_DX2, C_DV, C_DG) = range(13)
D_MIX = 4 * BRANCH
(M_GMLP, M_POOL, M_ATTN, M_HYENA) = range(4)


def _silu(x):
    return x * jax.nn.sigmoid(x)


def _bdot(a, b):
    return jnp.dot(a.astype(BF16), b.astype(BF16), preferred_element_type=F32)


def _split(x, n_terms):
    hi = x.astype(BF16)
    if n_terms == 1:
        return (hi,)
    return (hi, (x - hi.astype(F32)).astype(BF16))


def _sdot(a_terms, b_terms):
    acc = jnp.dot(a_terms[0], b_terms[0], preferred_element_type=F32)
    if len(a_terms) > 1:
        acc = acc + jnp.dot(a_terms[1], b_terms[0], preferred_element_type=F32)
    if len(b_terms) > 1:
        acc = acc + jnp.dot(a_terms[0], b_terms[1], preferred_element_type=F32)
    return acc


def _layer_norm(x):
    mu = jnp.mean(x, axis=-1, keepdims=True)
    xc = x - mu
    var = jnp.mean(xc * xc, axis=-1, keepdims=True)
    return xc * lax.rsqrt(var + LN_EPS)


def _cond_row(tile, rows_per_tile):
    n_ctx_tiles = N_CTX // rows_per_tile
    tiles_per_batch = DEC_SEQ // rows_per_tile
    return jnp.where(tile < n_ctx_tiles, 0, 1 + (tile - n_ctx_tiles) // tiles_per_batch)


def _params(*semantics):
    return pltpu.CompilerParams(dimension_semantics=semantics, vmem_limit_bytes=VMEM_LIMIT)


MOD_TN = 512


def _mod_kernel(c_ref, w_ref, b_ref, o_ref):
    o_ref[...] = _bdot(_silu(c_ref[...]), w_ref[...]) + b_ref[...]


def _modulation(cond, w_mod, b_mod):
    n = 3 * D_MODEL
    return pl.pallas_call(
        _mod_kernel,
        grid=(DEPTH, n // MOD_TN),
        in_specs=[
            pl.BlockSpec((N_COND, D_MODEL), lambda l, j: (0, 0)),
            pl.BlockSpec((None, D_MODEL, MOD_TN), lambda l, j: (l, 0, j)),
            pl.BlockSpec((None, 1, MOD_TN), lambda l, j: (l, 0, j)),
        ],
        out_specs=pl.BlockSpec((None, N_COND, MOD_TN), lambda l, j: (l, 0, j)),
        out_shape=jax.ShapeDtypeStruct((DEPTH, N_COND, n), F32),
        compiler_params=_params("arbitrary", "arbitrary"),
        name="modulation",
    )(cond, w_mod, b_mod.reshape(DEPTH, 1, n))


IN_TM = 1024
IN_TN = 512
IN_LN_ROWS = 256


def _token_specs(xs, tm):
    n_ctx_tiles = N_CTX // tm
    if len(xs) == 1:
        maps = [lambda i, *_: (i, 0)]
    else:
        maps = [lambda i, *_: (jnp.minimum(i, n_ctx_tiles - 1), 0),
                lambda i, *_: (jnp.maximum(i - n_ctx_tiles, 0), 0)]
    return [pl.BlockSpec((tm, D_MODEL), m) for m in maps]


def _inproj_kernel(n_x, *refs):
    x_refs = refs[:n_x]
    scale_ref, shift_ref, w_ref, o_ref, h_ref = refs[n_x:]

    def fill_h(x_ref):
        def body(r, carry):
            rows = pl.ds(pl.multiple_of(r * IN_LN_ROWS, IN_LN_ROWS), IN_LN_ROWS)
            h = _layer_norm(x_ref[rows, :]) * (1.0 + scale_ref[...]) + shift_ref[...]
            h_ref[rows, :] = h.astype(BF16)
            return carry
        lax.fori_loop(0, IN_TM // IN_LN_ROWS, body, 0)

    first = pl.program_id(1) == 0
    if n_x == 1:
        pl.when(first)(lambda: fill_h(x_refs[0]))
    else:
        is_ctx = pl.program_id(0) < N_CTX // IN_TM
        pl.when(jnp.logical_and(first, is_ctx))(lambda: fill_h(x_refs[0]))
        pl.when(jnp.logical_and(first, jnp.logical_not(is_ctx)))(lambda: fill_h(x_refs[1]))

    o_ref[...] = jnp.dot(h_ref[...], w_ref[...], preferred_element_type=F32)


def _in_projection(xs, mod4, w_in_bf, layer):
    row = lambda i: _cond_row(i, IN_TM)
    return pl.pallas_call(
        functools.partial(_inproj_kernel, len(xs)),
        grid=(N_TOK // IN_TM, D_IN // IN_TN),
        in_specs=_token_specs(xs, IN_TM) + [
            pl.BlockSpec((None, None, 1, D_MODEL), lambda i, j: (layer, row(i), 0, 1)),
            pl.BlockSpec((None, None, 1, D_MODEL), lambda i, j: (layer, row(i), 0, 0)),
            pl.BlockSpec((None, D_MODEL, IN_TN), lambda i, j: (layer, 0, j)),
        ],
        out_specs=pl.BlockSpec((IN_TM, IN_TN), lambda i, j: (i, j)),
        out_shape=jax.ShapeDtypeStruct((N_TOK, D_IN), F32),
        scratch_shapes=[pltpu.VMEM((IN_TM, D_MODEL), BF16)],
        compiler_params=_params("arbitrary", "arbitrary"),
        name="in_projection",
    )(*xs, mod4, mod4, w_in_bf)


GM_TM = 512


def _gmlp_kernel(u_ref, v_ref, g_ref, w_ref, b_ref, o_ref):
    vn = _layer_norm(v_ref[...]).astype(BF16)
    ws = [w_ref[g].astype(BF16) for g in range(N_GROUPS)]
    for n in range(GM_TM // CHUNK):
        rows = slice(n * CHUNK, (n + 1) * CHUNK)
        for g in range(N_GROUPS):
            cols = slice(g * GROUP_W, (g + 1) * GROUP_W)
            mixed = jnp.dot(ws[g], vn[rows, cols], preferred_element_type=F32) + b_ref[g]
            o_ref[rows, cols] = (_silu(g_ref[rows, cols]) * u_ref[rows, cols] * mixed).astype(BF16)


def _mixer_gmlp(p, gmlp_w, gmlp_b_rows):
    blk = lambda c: pl.BlockSpec((GM_TM, BRANCH), lambda i: (i, c))
    return pl.pallas_call(
        _gmlp_kernel,
        grid=(N_TOK // GM_TM,),
        in_specs=[
            blk(C_AU), blk(C_AV), blk(C_AG),
            pl.BlockSpec((N_GROUPS, CHUNK, CHUNK), lambda i: (0, 0, 0)),
            pl.BlockSpec((N_GROUPS, CHUNK, GROUP_W), lambda i: (0, 0, 0)),
        ],
        out_specs=pl.BlockSpec((GM_TM, BRANCH), lambda i: (i, M_GMLP)),
        out_shape=jax.ShapeDtypeStruct((N_TOK, D_MIX), BF16),
        compiler_params=_params("arbitrary"),
        name="mixer_gmlp",
    )(p, p, p, gmlp_w, gmlp_b_rows)


POOL_TM = 1024


def _pool_kernel(x_ref, g_ref, w_ref, s_ref, ymix_ref, o_ref):
    del ymix_ref
    seq_len = jnp.where(pl.program_id(0) < N_CTX // POOL_TM, SEQ, DEC_SEQ)
    pos = lax.broadcasted_iota(jnp.int32, (POOL_TM, GROUP_W), 0) & (seq_len - 1)
    for g, win in enumerate(POOL_WINDOWS):
        cols = slice(g * GROUP_W, (g + 1) * GROUP_W)
        p = x_ref[:, cols]
        acc = p
        for d in range(-(win // 2), win // 2):
            if d == 0:
                continue
            shifted = pltpu.roll(p, (-d) % POOL_TM, axis=0)
            valid = (pos + d >= 0) & (pos + d < seq_len)
            acc = acc + jnp.where(valid, shifted, 0.0)
        count = jnp.minimum(pos + win // 2, seq_len) - jnp.maximum(pos - win // 2, 0)
        pooled = acc / count.astype(F32)
        y = _bdot(pooled - p, w_ref[g])
        o_ref[:, cols] = (_silu(g_ref[:, cols]) * (y * s_ref[:, cols])).astype(BF16)


def _mixer_pool(p, pool_w, pool_scale_row, ymix):
    blk = lambda c: pl.BlockSpec((POOL_TM, BRANCH), lambda i: (i, c))
    return pl.pallas_call(
        _pool_kernel,
        grid=(N_TOK // POOL_TM,),
        in_specs=[
            blk(C_BX), blk(C_BG),
            pl.BlockSpec((N_GROUPS, GROUP_W, GROUP_W), lambda i: (0, 0, 0)),
            pl.BlockSpec((1, BRANCH), lambda i: (0, 0)),
            pl.BlockSpec(memory_space=pl.ANY),
        ],
        out_specs=pl.BlockSpec((POOL_TM, BRANCH), lambda i: (i, M_POOL)),
        out_shape=jax.ShapeDtypeStruct((N_TOK, D_MIX), BF16),
        input_output_aliases={4: 0},
        compiler_params=_params("arbitrary"),
        name="mixer_pool",
    )(p, p, pool_w, pool_scale_row, ymix)


ATT_TQ = 256
NT_DIMS = (((1,), (1,)), ((), ()))


def _lambda(lam_ref, lam_init):
    lq = lam_ref[...]
    a = jnp.sum(lq[0:1] * lq[1:2], axis=-1, keepdims=True)
    b = jnp.sum(lq[2:3] * lq[3:4], axis=-1, keepdims=True)
    return jnp.exp(a) - jnp.exp(b) + lam_init


def _map_masks():
    lane = lax.broadcasted_iota(jnp.int32, (1, HEAD_DIM_C), 1)
    m0 = (lane < QK_HALF).astype(F32)
    return m0, 1.0 - m0


def _softmax_times_v(q, keys, vals):
    s = [lax.dot_general(q, k, NT_DIMS, preferred_element_type=F32) for k in keys]
    m = functools.reduce(jnp.maximum, [jnp.max(x, axis=-1, keepdims=True) for x in s])
    es = [jnp.exp(x - m) for x in s]
    denom = functools.reduce(jnp.add, [jnp.sum(e, axis=-1, keepdims=True) for e in es])
    acc = functools.reduce(
        jnp.add, [jnp.dot(e.astype(BF16), v, preferred_element_type=F32) for e, v in zip(es, vals)])
    return acc * (1.0 / denom)


def _diff_head(q, keys, vals, lam, lam_init, subln, gate):
    m0, m1 = _map_masks()
    qs = q * (QK_HALF ** -0.5)
    o0 = _softmax_times_v((qs * m0).astype(BF16), keys, vals)
    o1 = _softmax_times_v((qs * m1).astype(BF16), keys, vals)
    o = o0 - lam * o1
    o = o * lax.rsqrt(jnp.mean(o * o, axis=-1, keepdims=True) + 1e-5)
    o = o * subln * (1.0 - lam_init)
    return _silu(gate) * o


def _attn_ctx_kernel(lam_init, q_ref, k_ref, v_ref, g_ref, lam_ref, sw_ref, *rest):
    o_ref, ko_ref, vo_ref = rest[-3:]
    ko_ref[...] = k_ref[...]
    vo_ref[...] = v_ref[...]
    lam = _lambda(lam_ref, lam_init)
    for h in range(N_HEADS_C):
        cols = slice(h * HEAD_DIM_C, (h + 1) * HEAD_DIM_C)
        o_ref[:, cols] = _diff_head(
            q_ref[:, cols], [k_ref[:, cols].astype(BF16)], [v_ref[:, cols].astype(BF16)],
            lam, lam_init, sw_ref[...], g_ref[:, cols]).astype(BF16)


def _mixer_attn_ctx(p, lambda_qk, subln_row, layer, lam_init, ymix, caches):
    blk = lambda c: pl.BlockSpec((SEQ, BRANCH), lambda b: (b, c))
    any_spec = pl.BlockSpec(memory_space=pl.ANY)
    cache_spec = pl.BlockSpec((None, None, SEQ, BRANCH), lambda b: (b, layer, 0, 0))
    cache_shape = jax.ShapeDtypeStruct((BATCH, DEPTH, SEQ, BRANCH), F32)
    in_specs = [
        blk(C_Q), blk(C_K), blk(C_V), blk(C_CG),
        pl.BlockSpec((None, 4, QK_HALF), lambda b: (layer, 0, 0)),
        pl.BlockSpec((None, 1, HEAD_DIM_C), lambda b: (layer, 0, 0)),
        any_spec,
    ]
    args = [p, p, p, p, lambda_qk, subln_row, ymix]
    aliases = {6: 0}
    if caches is not None:
        in_specs += [any_spec, any_spec]
        args += list(caches)
        aliases.update({7: 1, 8: 2})
    return pl.pallas_call(
        functools.partial(_attn_ctx_kernel, lam_init),
        grid=(BATCH,),
        in_specs=in_specs,
        out_specs=[pl.BlockSpec((SEQ, BRANCH), lambda b: (b, M_ATTN)), cache_spec, cache_spec],
        out_shape=[jax.ShapeDtypeStruct((N_TOK, D_MIX), BF16), cache_shape, cache_shape],
        input_output_aliases=aliases,
        compiler_params=_params("arbitrary"),
        name="mixer_attn_ctx",
    )(*args)


def _rope(x, cos, sin_signed):
    lane = lax.broadcasted_iota(jnp.int32, x.shape, 1)
    first_half = (lane & (ROPE_AXIS_DIM - 1)) < (ROPE_AXIS_DIM // 2)
    half = ROPE_AXIS_DIM // 2
    partner = jnp.where(first_half,
                        pltpu.roll(x, x.shape[1] - half, axis=1),
                        pltpu.roll(x, half, axis=1))
    return x * cos + partner * sin_signed


def _attn_lat_kernel(lam_init, q_ref, k_ref, v_ref, g_ref, ck_ref, cv_ref, cosq_ref, sinq_ref,
                     cosk_ref, sink_ref, lam_ref, sw_ref, ymix_ref, o_ref, kr_ref, vb_ref):
    del ymix_ref
    @pl.when(pl.program_id(1) == 0)
    def _():
        for h in range(N_HEADS_C):
            cols = slice(h * HEAD_DIM_C, (h + 1) * HEAD_DIM_C)
            kr_ref[:, cols] = _rope(k_ref[:, cols], cosk_ref[...], sink_ref[...]).astype(BF16)
        vb_ref[...] = v_ref[...].astype(BF16)

    lam = _lambda(lam_ref, lam_init)
    for h in range(N_HEADS_C):
        cols = slice(h * HEAD_DIM_C, (h + 1) * HEAD_DIM_C)
        q = _rope(q_ref[:, cols], cosq_ref[...], sinq_ref[...])
        o_ref[:, cols] = _diff_head(
            q, [ck_ref[:, cols].astype(BF16), kr_ref[:, cols]],
            [cv_ref[:, cols].astype(BF16), vb_ref[:, cols]],
            lam, lam_init, sw_ref[...], g_ref[:, cols]).astype(BF16)


def _rope_tables():
    pos = np.arange(DEC_SEQ)
    row = (pos // GRID_W).astype(np.float64)
    col = (pos % GRID_W).astype(np.float64)
    half = ROPE_AXIS_DIM // 2
    inv = ROPE_BASE ** (-np.arange(0, ROPE_AXIS_DIM, 2, dtype=np.float64) / ROPE_AXIS_DIM)
    lane = np.arange(HEAD_DIM_C)
    axis_is_col = (lane // ROPE_AXIS_DIM) % 2 == 1
    idx = lane % ROPE_AXIS_DIM
    ang = np.where(axis_is_col[None, :], col[:, None], row[:, None]) * inv[idx % half][None, :]
    sign = np.where(idx < half, -1.0, 1.0)[None, :]
    return (jnp.asarray(np.cos(ang), F32), jnp.asarray(np.sin(ang) * sign, F32))


def _mixer_attn_lat(p, cache_k4, cache_v4, lambda_qk, subln_row, layer, lam_init, ymix):
    cos_t, sin_t = _rope_tables()
    q_tiles = DEC_SEQ // ATT_TQ
    q0 = N_CTX // ATT_TQ
    b0 = N_CTX // DEC_SEQ
    qblk = lambda c: pl.BlockSpec((ATT_TQ, BRANCH), lambda b, i: (q0 + b * q_tiles + i, c))
    kblk = lambda c: pl.BlockSpec((DEC_SEQ, BRANCH), lambda b, i: (b0 + b, c))
    cblk = pl.BlockSpec((None, None, PAST_LEN, BRANCH), lambda b, i: (b, layer, 0, 0))
    return pl.pallas_call(
        functools.partial(_attn_lat_kernel, lam_init),
        grid=(DEC_BATCH, q_tiles),
        in_specs=[
            qblk(C_Q), kblk(C_K), kblk(C_V), qblk(C_CG), cblk, cblk,
            pl.BlockSpec((ATT_TQ, HEAD_DIM_C), lambda b, i: (i, 0)),
            pl.BlockSpec((ATT_TQ, HEAD_DIM_C), lambda b, i: (i, 0)),
            pl.BlockSpec((DEC_SEQ, HEAD_DIM_C), lambda b, i: (0, 0)),
            pl.BlockSpec((DEC_SEQ, HEAD_DIM_C), lambda b, i: (0, 0)),
            pl.BlockSpec((None, 4, QK_HALF), lambda b, i: (layer, 0, 0)),
            pl.BlockSpec((None, 1, HEAD_DIM_C), lambda b, i: (layer, 0, 0)),
            pl.BlockSpec(memory_space=pl.ANY),
        ],
        out_specs=pl.BlockSpec((ATT_TQ, BRANCH), lambda b, i: (q0 + b * q_tiles + i, M_ATTN)),
        out_shape=jax.ShapeDtypeStruct((N_TOK, D_MIX), BF16),
        input_output_aliases={12: 0},
        scratch_shapes=[pltpu.VMEM((DEC_SEQ, BRANCH), BF16), pltpu.VMEM((DEC_SEQ, BRANCH), BF16)],
        compiler_params=_params("arbitrary", "arbitrary"),
        name="mixer_attn_lat",
    )(p, p, p, p, cache_k4, cache_v4, cos_t, sin_t, cos_t, sin_t, lambda_qk, subln_row, ymix)


HY_ROWS = 1024


def _dft_matrices(seq_len):
    n = 2 * seq_len
    f = np.arange(seq_len, dtype=np.float64)[:, None]
    s = np.arange(seq_len, dtype=np.float64)[None, :]
    theta = 2.0 * np.pi * f * s / n
    alt = np.where(np.arange(seq_len) % 2 == 0, 1.0, -1.0)
    ac = np.cos(theta)
    as_ = -np.sin(theta)
    as_[0, :] = alt
    bc = 2.0 * np.cos(theta.T) / n
    bc[:, 0] = 1.0 / n
    bs = -2.0 * np.sin(theta.T) / n
    bs[:, 0] = alt / n
    fwd = np.concatenate([ac, as_], axis=0)
    inv = np.concatenate([bc, bs], axis=1)
    return jnp.asarray(fwd, F32).astype(BF16), jnp.asarray(inv, F32).astype(BF16)


def _filter_features(seq_len):
    t_idx = np.arange(seq_len, dtype=np.float64)
    t_norm = np.linspace(0.0, 1.0, seq_len)
    bands = np.linspace(1e-4, HY_BANDS - 1, HY_BANDS)
    ang = (2.0 * math.pi * t_idx / seq_len)[:, None] * bands[None, :]
    feats = np.concatenate([t_norm[:, None], np.cos(ang), np.sin(ang)], axis=-1)
    feats = np.pad(feats, ((0, 0), (0, HY_PAD - HY_EMB)))
    deltas = np.abs(np.linspace(math.log(HY_TARGET) / HY_FAST_DECAY,
                                math.log(HY_TARGET) / HY_SLOW_DECAY, BRANCH))
    return (jnp.asarray(feats, F32), jnp.asarray(t_norm[:, None], F32),
            jnp.asarray(deltas[None, :], F32))


def _filter_kernel(seq_len, feats_ref, tn_ref, dl_ref, w1_ref, b1_ref, w2_ref, b2_ref, fr_ref,
                   w3f_ref, w3b_ref, fwd_ref, kr_ref, ki_ref, h_ref):
    sp = lambda x: _split(x, 2)

    @pl.when(pl.program_id(1) == 0)
    def _():
        fr = fr_ref[...]
        h = jnp.sin(fr * (_sdot(sp(feats_ref[...]), sp(w1_ref[...])) + b1_ref[...]))
        h_ref[...] = jnp.sin(fr * (_sdot(sp(h), sp(w2_ref[...])) + b2_ref[...]))

    h = sp(h_ref[...])
    decay = jnp.exp(-tn_ref[...] * dl_ref[...])
    row = lax.broadcasted_iota(jnp.int32, (seq_len, BRANCH), 0)
    fwd = _sdot(h, sp(w3f_ref[...])) * decay
    bwd = jnp.where(row == 0, 0.0, _sdot(h, sp(w3b_ref[...])) * decay)
    norm = (jnp.sum(jnp.abs(fwd), axis=0, keepdims=True)
            + jnp.sum(jnp.abs(bwd), axis=0, keepdims=True))
    fwd = fwd / norm
    bwd = bwd / norm
    even = fwd + bwd
    alt = jnp.where((row & 1) == 0, 1.0, -1.0)
    nyquist = jnp.sum(alt * even, axis=0, keepdims=True)
    kr_ref[...] = jnp.dot(fwd_ref[0:seq_len, :], even.astype(BF16), preferred_element_type=F32)
    ki = jnp.dot(fwd_ref[seq_len:2 * seq_len, :], (fwd - bwd).astype(BF16),
                 preferred_element_type=F32)
    ki_ref[...] = jnp.where(row == 0, nyquist, ki)


def _hyena_spectrum(seq_len, filt, fwd_mat):
    feats, t_norm, deltas = _filter_features(seq_len)
    w1, b1, w2, b2, freq, w3 = filt
    full = lambda a: pl.BlockSpec(a.shape, lambda l, o: (0,) * a.ndim)
    lyr = lambda r, n: pl.BlockSpec((None, r, n), lambda l, o: (l, 0, 0))
    out = pl.BlockSpec((None, None, seq_len, BRANCH), lambda l, o: (l, o, 0, 0))
    return pl.pallas_call(
        functools.partial(_filter_kernel, seq_len),
        grid=(DEPTH, HY_ORDER),
        in_specs=[
            full(feats), full(t_norm), full(deltas),
            lyr(HY_PAD, HY_PAD), lyr(1, HY_PAD), lyr(HY_PAD, HY_PAD), lyr(1, HY_PAD), lyr(1, HY_PAD),
            pl.BlockSpec((None, HY_PAD, BRANCH), lambda l, o: (l, 0, 2 * o)),
            pl.BlockSpec((None, HY_PAD, BRANCH), lambda l, o: (l, 0, 2 * o + 1)),
            full(fwd_mat),
        ],
        out_specs=[out, out],
        out_shape=[jax.ShapeDtypeStruct((DEPTH, HY_ORDER, seq_len, BRANCH), F32)] * 2,
        scratch_shapes=[pltpu.VMEM((seq_len, HY_PAD), F32)],
        compiler_params=_params("arbitrary", "arbitrary"),
        name=f"hyena_spectrum_{seq_len}",
    )(feats, t_norm, deltas, w1, b1, w2, b2, freq, w3, w3, fwd_mat)


def _hyena_chains(seq_len, width, x1_ref, x2_ref, hv_ref, g_ref, cw_ref, cb_ref, kr_ref, ki_ref,
                  hb_ref, fwd_ref, inv_ref, o_ref):
    row = lax.broadcasted_iota(jnp.int32, (seq_len, width), 0)
    first, last = row == 0, row == seq_len - 1

    def chain(rs, c0):
        cs = slice(c0, c0 + width)

        def short_conv(x_ref, piece):
            x = x_ref[rs, cs]
            w = cw_ref[:, piece * BRANCH + c0:piece * BRANCH + c0 + width]
            b = cb_ref[:, piece * BRANCH + c0:piece * BRANCH + c0 + width]
            prev = jnp.where(first, 0.0, pltpu.roll(x, 1, axis=0))
            nxt = jnp.where(last, 0.0, pltpu.roll(x, seq_len - 1, axis=0))
            return prev * w[0:1] + x * w[1:2] + nxt * w[2:3] + b

        def long_conv(z, order):
            zf = jnp.dot(fwd_ref[...], z.astype(BF16), preferred_element_type=F32)
            zr, zi = zf[:seq_len], zf[seq_len:]
            kr, kp = kr_ref[order, :, cs], ki_ref[order, :, cs]
            ki = jnp.where(first, 0.0, kp)
            kn = jnp.where(first, kp, kr)
            yf = jnp.concatenate([(zr * kr - zi * ki).astype(BF16),
                                  (zr * ki + zi * kn).astype(BF16)], axis=0)
            y = jnp.dot(inv_ref[...], yf, preferred_element_type=F32)
            return y + z * hb_ref[order:order + 1, cs]

        g1 = short_conv(x1_ref, 0)
        g2 = short_conv(x2_ref, 1)
        z = g1 * long_conv(short_conv(hv_ref, 2), 0)
        o_ref[rs, cs] = (_silu(g_ref[rs, cs]) * (g2 * long_conv(z, 1))).astype(BF16)

    for s in range(HY_ROWS // seq_len):
        for c0 in range(0, BRANCH, width):
            chain(slice(s * seq_len, (s + 1) * seq_len), c0)


def _hyena_kernel(x1_ref, x2_ref, hv_ref, g_ref, cw_ref, cb_ref, hb_ref,
                  krc_ref, kic_ref, fwdc_ref, invc_ref, krl_ref, kil_ref, fwdl_ref, invl_ref,
                  ymix_ref, o_ref):
    del ymix_ref
    common = (x1_ref, x2_ref, hv_ref, g_ref, cw_ref, cb_ref)
    is_ctx = pl.program_id(0) < N_CTX // HY_ROWS

    @pl.when(is_ctx)
    def _():
        _hyena_chains(SEQ, BRANCH, *common, krc_ref, kic_ref, hb_ref, fwdc_ref, invc_ref, o_ref)

    @pl.when(jnp.logical_not(is_ctx))
    def _():
        _hyena_chains(DEC_SEQ, HY_CT, *common, krl_ref, kil_ref, hb_ref, fwdl_ref, invl_ref, o_ref)


def _mixer_hyena(p, conv_w, conv_b3, hyena_bias, spec_ctx, mats_ctx, spec_lat, mats_lat, layer,
                 ymix):
    blk = lambda c: pl.BlockSpec((HY_ROWS, BRANCH), lambda i: (i, c))
    once = pl.Buffered(1)
    lyr = lambda a: pl.BlockSpec((None,) + a.shape[1:], lambda i: (layer,) + (0,) * (a.ndim - 1),
                                 pipeline_mode=once)
    full = lambda a: pl.BlockSpec(a.shape, lambda i: (0,) * a.ndim, pipeline_mode=once)
    consts = [*spec_ctx, *mats_ctx, *spec_lat, *mats_lat]
    return pl.pallas_call(
        _hyena_kernel,
        grid=(N_TOK // HY_ROWS,),
        in_specs=[blk(C_DX1), blk(C_DX2), blk(C_DV), blk(C_DG),
                  lyr(conv_w), lyr(conv_b3), lyr(hyena_bias),
                  lyr(spec_ctx[0]), lyr(spec_ctx[1]), full(mats_ctx[0]), full(mats_ctx[1]),
                  lyr(spec_lat[0]), lyr(spec_lat[1]), full(mats_lat[0]), full(mats_lat[1]),
                  pl.BlockSpec(memory_space=pl.ANY)],
        out_specs=pl.BlockSpec((HY_ROWS, BRANCH), lambda i: (i, M_HYENA)),
        out_shape=jax.ShapeDtypeStruct((N_TOK, D_MIX), BF16),
        input_output_aliases={7 + len(consts): 0},
        compiler_params=_params("arbitrary"),
        name="mixer_hyena",
    )(p, p, p, p, conv_w, conv_b3, hyena_bias, *consts, ymix)


OUT_TM = 512


def _outproj_kernel(alpha, n_x, n_out, ymix_ref, *refs):
    x_refs = refs[:n_x]
    gate_ref, w_ref, b_ref, lng_ref, lnb_ref = refs[n_x:n_x + 5]
    o_refs = refs[n_x + 5:]
    is_ctx = pl.program_id(0) < N_CTX // OUT_TM
    x = x_refs[0][...] if n_x == 1 else jnp.where(is_ctx, x_refs[0][...], x_refs[1][...])
    y = jnp.dot(ymix_ref[...], w_ref[...], preferred_element_type=F32) + b_ref[...]
    r = alpha * x + gate_ref[...] * y
    res = _layer_norm(r) * lng_ref[...] + lnb_ref[...]
    if n_out == 1:
        o_refs[0][...] = res
    else:
        @pl.when(is_ctx)
        def _():
            o_refs[0][...] = res

        @pl.when(jnp.logical_not(is_ctx))
        def _():
            o_refs[1][...] = res


def _out_projection(ymix, xs, mod4, w_out_bf, b_out3, ln_g3, ln_b3, layer, split_out):
    alpha = (2.0 * DEPTH) ** 0.25
    row = lambda i: _cond_row(i, OUT_TM)
    vec = pl.BlockSpec((None, 1, D_MODEL), lambda i: (layer, 0, 0))
    if split_out:
        out_shape = [jax.ShapeDtypeStruct((N_CTX, D_MODEL), F32),
                     jax.ShapeDtypeStruct((N_LAT, D_MODEL), F32)]
    else:
        out_shape = [jax.ShapeDtypeStruct((N_TOK, D_MODEL), F32)]
    return pl.pallas_call(
        functools.partial(_outproj_kernel, alpha, len(xs), len(out_shape)),
        grid=(N_TOK // OUT_TM,),
        in_specs=[pl.BlockSpec((OUT_TM, D_MIX), lambda i: (i, 0))] + _token_specs(xs, OUT_TM) + [
            pl.BlockSpec((None, None, 1, D_MODEL), lambda i: (layer, row(i), 0, 2)),
            pl.BlockSpec((None, D_MODEL, D_MODEL), lambda i: (layer, 0, 0),
                         pipeline_mode=pl.Buffered(1)),
            vec, vec, vec,
        ],
        out_specs=_token_specs(out_shape, OUT_TM),
        out_shape=out_shape,
        compiler_params=_params("arbitrary"),
        name="out_projection",
    )(ymix, *xs, mod4, w_out_bf, b_out3, ln_g3, ln_b3)


def kernel(x_prompt, x_sample, cache_k, cache_v, c, c_ctx, w_mod, b_mod, w_in, gmlp_w, gmlp_b,
           pool_w, pool_scale, lambda_qk, subln_w, conv_w, conv_b, filt_w1, filt_b1, filt_w2,
           filt_b2, filt_freq, filt_w3, hyena_bias, w_out, b_out, ln_g, ln_b):
    xs = (x_prompt.reshape(N_CTX, D_MODEL), x_sample.reshape(N_LAT, D_MODEL))
    cond = jnp.concatenate(
        [c_ctx[None, :], c, jnp.zeros((N_COND - 1 - DEC_BATCH, D_MODEL), F32)], axis=0)
    mod4 = _modulation(cond, w_mod, b_mod).reshape(DEPTH, N_COND, 1, 3 * D_MODEL)

    w_in_bf = w_in.astype(BF16)
    w_out_bf = w_out.astype(BF16)
    cache_k4 = cache_k.reshape(DEC_BATCH, DEPTH, PAST_LEN, BRANCH)
    cache_v4 = cache_v.reshape(DEC_BATCH, DEPTH, PAST_LEN, BRANCH)
    gmlp_b_rows = jnp.broadcast_to(gmlp_b[..., None], (DEPTH, N_GROUPS, CHUNK, GROUP_W))
    subln_row = subln_w.reshape(DEPTH, 1, HEAD_DIM_C)
    conv_b3 = conv_b.reshape(DEPTH, 1, 3 * BRANCH)
    b_out3 = b_out.reshape(DEPTH, 1, D_MODEL)
    ln_g3 = ln_g.reshape(DEPTH, 1, D_MODEL)
    ln_b3 = ln_b.reshape(DEPTH, 1, D_MODEL)

    pad_h = HY_PAD - HY_HIDDEN
    filt = (
        jnp.pad(filt_w1, ((0, 0), (0, HY_PAD - HY_EMB), (0, pad_h))),
        jnp.pad(filt_b1, ((0, 0), (0, pad_h))).reshape(DEPTH, 1, HY_PAD),
        jnp.pad(filt_w2, ((0, 0), (0, pad_h), (0, pad_h))),
        jnp.pad(filt_b2, ((0, 0), (0, pad_h))).reshape(DEPTH, 1, HY_PAD),
        jnp.pad(filt_freq, ((0, 0), (0, pad_h))).reshape(DEPTH, 1, HY_PAD),
        jnp.pad(filt_w3, ((0, 0), (0, pad_h), (0, 0))),
    )
    mats_ctx = _dft_matrices(SEQ)
    mats_lat = _dft_matrices(DEC_SEQ)
    spec_ctx = _hyena_spectrum(SEQ, filt, mats_ctx[0])
    spec_lat = _hyena_spectrum(DEC_SEQ, filt, mats_lat[0])

    caches = None
    for layer in range(DEPTH):
        lam_init = 0.8 - 0.6 * math.exp(-0.3 * layer)
        p = _in_projection(xs, mod4, w_in_bf, layer)
        ymix = _mixer_gmlp(p, gmlp_w[layer], gmlp_b_rows[layer])
        ymix = _mixer_pool(p, pool_w[layer], pool_scale[layer].reshape(1, BRANCH), ymix)
        ymix, new_k, new_v = _mixer_attn_ctx(p, lambda_qk, subln_row, layer, lam_init, ymix, caches)
        caches = (new_k, new_v)
        ymix = _mixer_attn_lat(p, cache_k4, cache_v4, lambda_qk, subln_row, layer, lam_init, ymix)
        ymix = _mixer_hyena(p, conv_w, conv_b3, hyena_bias, spec_ctx, mats_ctx, spec_lat, mats_lat,
                            layer, ymix)
        xs = tuple(_out_projection(ymix, xs, mod4, w_out_bf, b_out3, ln_g3, ln_b3, layer,
                                   split_out=layer == DEPTH - 1))

    y_prompt = xs[0].reshape(BATCH, SEQ, D_MODEL)
    y_sample = xs[1].reshape(DEC_BATCH, DEC_SEQ, D_MODEL)
    new_k, new_v = caches
    return (y_prompt, y_sample,
            new_k.reshape(BATCH, DEPTH, SEQ, N_HEADS_C, 2, QK_HALF),
            new_v.reshape(BATCH, DEPTH, SEQ, N_HEADS_C, HEAD_DIM_C))
```

```python
import functools
import math

import numpy as np
import jax
import jax.numpy as jnp
from jax import lax
from jax.experimental import pallas as pl
from jax.experimental.pallas import tpu as pltpu

F32 = jnp.float32
BF16 = jnp.bfloat16

D_MODEL = 2048
BATCH = 16
SEQ = 256
DEPTH = 2
DEC_BATCH = 4
DEC_SEQ = 1024
PAST_LEN = 512
GRID_W = 64
BRANCH = 512
N_GROUPS = 4
GROUP_W = 128
CHUNK = 128
POOL_WINDOWS = (2, 4, 8, 16)
N_HEADS_C = 4
HEAD_DIM_C = 128
QK_HALF = 64
ROPE_AXIS_DIM = 32
ROPE_BASE = 10000.0
HY_BANDS = 16
HY_EMB = 33
HY_HIDDEN = 64
HY_ORDER = 2
HY_FAST_DECAY = 0.3
HY_SLOW_DECAY = 1.5
HY_TARGET = 1e-2
N_IN_PIECES = 13
D_IN = N_IN_PIECES * BRANCH
LN_EPS = 1e-6

N_CTX = BATCH * SEQ
N_LAT = DEC_BATCH * DEC_SEQ
N_TOK = N_CTX + N_LAT
N_COND = 8
LANES = 128
HY_PAD = LANES
HY_CT = 256
VMEM_LIMIT = 56 * 1024 * 1024

(C_AU, C_AV, C_AG, C_BX, C_BG, C_Q, C_K, C_V, C_CG, C_DX1, C_DX2, C_DV, C_DG) = range(13)
D_MIX = 4 * BRANCH
(M_GMLP, M_POOL, M_ATTN, M_HYENA) = range(4)


def _silu(x):
    return x * jax.nn.sigmoid(x)


def _bdot(a, b):
    return jnp.dot(a.astype(BF16), b.astype(BF16), preferred_element_type=F32)


def _split(x, n_terms):
    hi = x.astype(BF16)
    if n_terms == 1:
        return (hi,)
    return (hi, (x - hi.astype(F32)).astype(BF16))


def _sdot(a_terms, b_terms):
    acc = jnp.dot(a_terms[0], b_terms[0], preferred_element_type=F32)
    if len(a_terms) > 1:
        acc = acc + jnp.dot(a_terms[1], b_terms[0], preferred_element_type=F32)
    if len(b_terms) > 1:
        acc = acc + jnp.dot(a_terms[0], b_terms[1], preferred_element_type=F32)
    return acc


def _layer_norm(x):
    mu = jnp.mean(x, axis=-1, keepdims=True)
    xc = x - mu
    var = jnp.mean(xc * xc, axis=-1, keepdims=True)
    return xc * lax.rsqrt(var + LN_EPS)


def _cond_row(tile, rows_per_tile):
    n_ctx_tiles = N_CTX // rows_per_tile
    tiles_per_batch = DEC_SEQ // rows_per_tile
    return jnp.where(tile < n_ctx_tiles, 0, 1 + (tile - n_ctx_tiles) // tiles_per_batch)


def _params(*semantics):
    return pltpu.CompilerParams(dimension_semantics=semantics, vmem_limit_bytes=VMEM_LIMIT)


MOD_TN = 512


def _mod_kernel(c_ref, w_ref, b_ref, o_ref):
    o_ref[...] = _bdot(_silu(c_ref[...]), w_ref[...]) + b_ref[...]


def _modulation(cond, w_mod, b_mod):
    n = 3 * D_MODEL
    return pl.pallas_call(
        _mod_kernel,
        grid=(DEPTH, n // MOD_TN),
        in_specs=[
            pl.BlockSpec((N_COND, D_MODEL), lambda l, j: (0, 0)),
            pl.BlockSpec((None, D_MODEL, MOD_TN), lambda l, j: (l, 0, j)),
            pl.BlockSpec((None, 1, MOD_TN), lambda l, j: (l, 0, j)),
        ],
        out_specs=pl.BlockSpec((None, N_COND, MOD_TN), lambda l, j: (l, 0, j)),
        out_shape=jax.ShapeDtypeStruct((DEPTH, N_COND, n), F32),
        compiler_params=_params("arbitrary", "arbitrary"),
        name="modulation",
    )(cond, w_mod, b_mod.reshape(DEPTH, 1, n))


IN_TM = 1024
IN_TN = 512
IN_LN_ROWS = 256


def _token_specs(xs, tm):
    n_ctx_tiles = N_CTX // tm
    if len(xs) == 1:
        maps = [lambda i, *_: (i, 0)]
    else:
        maps = [lambda i, *_: (jnp.minimum(i, n_ctx_tiles - 1), 0),
                lambda i, *_: (jnp.maximum(i - n_ctx_tiles, 0), 0)]
    return [pl.BlockSpec((tm, D_MODEL), m) for m in maps]


def _inproj_kernel(n_x, *refs):
    x_refs = refs[:n_x]
    scale_ref, shift_ref, w_ref, o_ref, h_ref = refs[n_x:]

    def fill_h(x_ref):
        def body(r, carry):
            rows = pl.ds(pl.multiple_of(r * IN_LN_ROWS, IN_LN_ROWS), IN_LN_ROWS)
            h = _layer_norm(x_ref[rows, :]) * (1.0 + scale_ref[...]) + shift_ref[...]
            h_ref[rows, :] = h.astype(BF16)
            return carry
        lax.fori_loop(0, IN_TM // IN_LN_ROWS, body, 0)

    first = pl.program_id(1) == 0
    if n_x == 1:
        pl.when(first)(lambda: fill_h(x_refs[0]))
    else:
        is_ctx = pl.program_id(0) < N_CTX // IN_TM
        pl.when(jnp.logical_and(first, is_ctx))(lambda: fill_h(x_refs[0]))
        pl.when(jnp.logical_and(first, jnp.logical_not(is_ctx)))(lambda: fill_h(x_refs[1]))

    o_ref[...] = jnp.dot(h_ref[...], w_ref[...], preferred_element_type=F32)


def _in_projection(xs, mod4, w_in_bf, layer):
    row = lambda i: _cond_row(i, IN_TM)
    return pl.pallas_call(
        functools.partial(_inproj_kernel, len(xs)),
        grid=(N_TOK // IN_TM, D_IN // IN_TN),
        in_specs=_token_specs(xs, IN_TM) + [
            pl.BlockSpec((None, None, 1, D_MODEL), lambda i, j: (layer, row(i), 0, 1)),
            pl.BlockSpec((None, None, 1, D_MODEL), lambda i, j: (layer, row(i), 0, 0)),
            pl.BlockSpec((None, D_MODEL, IN_TN), lambda i, j: (layer, 0, j)),
        ],
        out_specs=pl.BlockSpec((IN_TM, IN_TN), lambda i, j: (i, j)),
        out_shape=jax.ShapeDtypeStruct((N_TOK, D_IN), F32),
        scratch_shapes=[pltpu.VMEM((IN_TM, D_MODEL), BF16)],
        compiler_params=_params("arbitrary", "arbitrary"),
        name="in_projection",
    )(*xs, mod4, mod4, w_in_bf)


GM_TM = 512


def _gmlp_kernel(u_ref, v_ref, g_ref, w_ref, b_ref, o_ref):
    vn = _layer_norm(v_ref[...]).astype(BF16)
    ws = [w_ref[g].astype(BF16) for g in range(N_GROUPS)]
    for n in range(GM_TM // CHUNK):
        rows = slice(n * CHUNK, (n + 1) * CHUNK)
        for g in range(N_GROUPS):
            cols = slice(g * GROUP_W, (g + 1) * GROUP_W)
            mixed = jnp.dot(ws[g], vn[rows, cols], preferred_element_type=F32) + b_ref[g]
            o_ref[rows, cols] = (_silu(g_ref[rows, cols]) * u_ref[rows, cols] * mixed).astype(BF16)


def _mixer_gmlp(p, gmlp_w, gmlp_b_rows):
    blk = lambda c: pl.BlockSpec((GM_TM, BRANCH), lambda i: (i, c))
    return pl.pallas_call(
        _gmlp_kernel,
        grid=(N_TOK // GM_TM,),
        in_specs=[
            blk(C_AU), blk(C_AV), blk(C_AG),
            pl.BlockSpec((N_GROUPS, CHUNK, CHUNK), lambda i: (0, 0, 0)),
            pl.BlockSpec((N_GROUPS, CHUNK, GROUP_W), lambda i: (0, 0, 0)),
        ],
        out_specs=pl.BlockSpec((GM_TM, BRANCH), lambda i: (i, M_GMLP)),
        out_shape=jax.ShapeDtypeStruct((N_TOK, D_MIX), BF16),
        compiler_params=_params("arbitrary"),
        name="mixer_gmlp",
    )(p, p, p, gmlp_w, gmlp_b_rows)


POOL_TM = 1024


def _pool_kernel(x_ref, g_ref, w_ref, s_ref, ymix_ref, o_ref):
    del ymix_ref
    seq_len = jnp.where(pl.program_id(0) < N_CTX // POOL_TM, SEQ, DEC_SEQ)
    pos = lax.broadcasted_iota(jnp.int32, (POOL_TM, GROUP_W), 0) & (seq_len - 1)
    for g, win in enumerate(POOL_WINDOWS):
        cols = slice(g * GROUP_W, (g + 1) * GROUP_W)
        p = x_ref[:, cols]
        acc = p
        for d in range(-(win // 2), win // 2):
            if d == 0:
                continue
            shifted = pltpu.roll(p, (-d) % POOL_TM, axis=0)
            valid = (pos >= -d) if d < 0 else (pos < seq_len - d)
            acc = acc + jnp.where(valid, shifted, 0.0)
        count = jnp.minimum(pos + win // 2, seq_len) - jnp.maximum(pos - win // 2, 0)
        pooled = acc / count.astype(F32)
        y = _bdot(pooled - p, w_ref[g])
        o_ref[:, cols] = (_silu(g_ref[:, cols]) * (y * s_ref[:, cols])).astype(BF16)


def _mixer_pool(p, pool_w, pool_scale_row, ymix):
    blk = lambda c: pl.BlockSpec((POOL_TM, BRANCH), lambda i: (i, c))
    return pl.pallas_call(
        _pool_kernel,
        grid=(N_TOK // POOL_TM,),
        in_specs=[
            blk(C_BX), blk(C_BG),
            pl.BlockSpec((N_GROUPS, GROUP_W, GROUP_W), lambda i: (0, 0, 0)),
            pl.BlockSpec((1, BRANCH), lambda i: (0, 0)),
            pl.BlockSpec(memory_space=pl.ANY),
        ],
        out_specs=pl.BlockSpec((POOL_TM, BRANCH), lambda i: (i, M_POOL)),
        out_shape=jax.ShapeDtypeStruct((N_TOK, D_MIX), BF16),
        input_output_aliases={4: 0},
        compiler_params=_params("arbitrary"),
        name="mixer_pool",
    )(p, p, pool_w, pool_scale_row, ymix)


ATT_TQ = 256
V_ROWS = HEAD_DIM_C + 16
NT_DIMS = (((1,), (1,)), ((), ()))


def _lambda(lam_ref, lam_init):
    lq = lam_ref[...]
    a = jnp.sum(lq[0:1] * lq[1:2], axis=-1, keepdims=True)
    b = jnp.sum(lq[2:3] * lq[3:4], axis=-1, keepdims=True)
    return jnp.exp(a) - jnp.exp(b) + lam_init


def _map_masks():
    lane = lax.broadcasted_iota(jnp.int32, (1, HEAD_DIM_C), 1)
    m0 = (lane < QK_HALF).astype(F32)
    return m0, 1.0 - m0


def _scores_t(q, keys):
    return [lax.dot_general(k, q, NT_DIMS, preferred_element_type=F32) for k in keys]


def _softmax_v_t(s, vals_t):
    m = functools.reduce(jnp.maximum, [jnp.max(x, axis=0, keepdims=True) for x in s])
    acc = functools.reduce(
        jnp.add, [jnp.dot(v, jnp.exp2(x - m).astype(BF16), preferred_element_type=F32)
                  for x, v in zip(s, vals_t)])
    return acc[:HEAD_DIM_C] * (1.0 / acc[HEAD_DIM_C:HEAD_DIM_C + 1])


def _with_ones_rows(v_t):
    ones = jnp.ones((V_ROWS - HEAD_DIM_C, v_t.shape[1]), BF16)
    return jnp.concatenate([v_t.astype(BF16), ones], axis=0)


def _diff_attention(q_of, keys_of, vals_t_of, gate_of, store, lam, lam_init, subln):
    masks = _map_masks()
    tasks = [(h, mp) for h in range(N_HEADS_C) for mp in range(2)]
    per_head = {}

    def operands(h):
        if h not in per_head:
            q = q_of(h) * (QK_HALF ** -0.5 * math.log2(math.e))
            per_head[h] = (q, keys_of(h), vals_t_of(h))
        return per_head[h]

    def scores(h, mp):
        q, keys, _ = operands(h)
        return _scores_t((q * masks[mp]).astype(BF16), keys)

    nxt = scores(*tasks[0])
    outs = []
    for i, (h, mp) in enumerate(tasks):
        cur = nxt
        if i + 1 < len(tasks):
            nxt = scores(*tasks[i + 1])
        outs.append(_softmax_v_t(cur, operands(h)[2]))
        if mp == 1:
            o = (outs[0] - lam * outs[1]).T
            outs = []
            o = o * lax.rsqrt(jnp.mean(o * o, axis=-1, keepdims=True) + 1e-5)
            o = o * subln * (1.0 - lam_init)
            store(h, _silu(gate_of(h)) * o)


def _head_cols(h):
    return slice(h * HEAD_DIM_C, (h + 1) * HEAD_DIM_C)


def _attn_ctx_kernel(lam_init, q_ref, k_ref, v_ref, g_ref, lam_ref, sw_ref, *rest):
    o_ref, ko_ref, vo_ref = rest[-3:]
    ko_ref[...] = k_ref[...]
    vo_ref[...] = v_ref[...]

    def store(h, y):
        o_ref[:, _head_cols(h)] = y.astype(BF16)

    _diff_attention(
        lambda h: q_ref[:, _head_cols(h)],
        lambda h: [k_ref[:, _head_cols(h)].astype(BF16)],
        lambda h: [_with_ones_rows(v_ref[:, _head_cols(h)].T)],
        lambda h: g_ref[:, _head_cols(h)],
        store, _lambda(lam_ref, lam_init), lam_init, sw_ref[...])


def _mixer_attn_ctx(p, lambda_qk, subln_row, layer, lam_init, ymix, caches):
    blk = lambda c: pl.BlockSpec((SEQ, BRANCH), lambda b: (b, c))
    any_spec = pl.BlockSpec(memory_space=pl.ANY)
    cache_spec = pl.BlockSpec((None, None, SEQ, BRANCH), lambda b: (b, layer, 0, 0))
    cache_shape = jax.ShapeDtypeStruct((BATCH, DEPTH, SEQ, BRANCH), F32)
    in_specs = [
        blk(C_Q), blk(C_K), blk(C_V), blk(C_CG),
        pl.BlockSpec((None, 4, QK_HALF), lambda b: (layer, 0, 0)),
        pl.BlockSpec((None, 1, HEAD_DIM_C), lambda b: (layer, 0, 0)),
        any_spec,
    ]
    args = [p, p, p, p, lambda_qk, subln_row, ymix]
    aliases = {6: 0}
    if caches is not None:
        in_specs += [any_spec, any_spec]
        args += list(caches)
        aliases.update({7: 1, 8: 2})
    return pl.pallas_call(
        functools.partial(_attn_ctx_kernel, lam_init),
        grid=(BATCH,),
        in_specs=in_specs,
        out_specs=[pl.BlockSpec((SEQ, BRANCH), lambda b: (b, M_ATTN)), cache_spec, cache_spec],
        out_shape=[jax.ShapeDtypeStruct((N_TOK, D_MIX), BF16), cache_shape, cache_shape],
        input_output_aliases=aliases,
        compiler_params=_params("arbitrary"),
        name="mixer_attn_ctx",
    )(*args)


def _rope(x, cos, sin_signed):
    lane = lax.broadcasted_iota(jnp.int32, x.shape, 1)
    first_half = (lane & (ROPE_AXIS_DIM - 1)) < (ROPE_AXIS_DIM // 2)
    half = ROPE_AXIS_DIM // 2
    partner = jnp.where(first_half,
                        pltpu.roll(x, x.shape[1] - half, axis=1),
                        pltpu.roll(x, half, axis=1))
    return x * cos + partner * sin_signed


def _attn_lat_kernel(lam_init, q_ref, k_ref, v_ref, g_ref, ck_ref, cv_ref, cosq_ref, sinq_ref,
                     cosk_ref, sink_ref, lam_ref, sw_ref, ymix_ref, o_ref, kc_ref, kr_ref, vt_ref):
    del ymix_ref
    @pl.when(pl.program_id(1) == 0)
    def _():
        kc_ref[...] = ck_ref[...].astype(BF16)
        for h in range(N_HEADS_C):
            cols = slice(h * HEAD_DIM_C, (h + 1) * HEAD_DIM_C)
            kr_ref[:, cols] = _rope(k_ref[:, cols], cosk_ref[...], sink_ref[...]).astype(BF16)
            vt_ref[h, :, 0:PAST_LEN] = _with_ones_rows(cv_ref[:, cols].T)
            vt_ref[h, :, PAST_LEN:PAST_LEN + DEC_SEQ] = _with_ones_rows(v_ref[:, cols].T)

    def store(h, y):
        o_ref[:, _head_cols(h)] = y.astype(BF16)

    _diff_attention(
        lambda h: _rope(q_ref[:, _head_cols(h)], cosq_ref[...], sinq_ref[...]),
        lambda h: [kc_ref[:, _head_cols(h)], kr_ref[:, _head_cols(h)]],
        lambda h: [vt_ref[h, :, 0:PAST_LEN], vt_ref[h, :, PAST_LEN:PAST_LEN + DEC_SEQ]],
        lambda h: g_ref[:, _head_cols(h)],
        store, _lambda(lam_ref, lam_init), lam_init, sw_ref[...])


def _rope_tables():
    pos = np.arange(DEC_SEQ)
    row = (pos // GRID_W).astype(np.float64)
    col = (pos % GRID_W).astype(np.float64)
    half = ROPE_AXIS_DIM // 2
    inv = ROPE_BASE ** (-np.arange(0, ROPE_AXIS_DIM, 2, dtype=np.float64) / ROPE_AXIS_DIM)
    lane = np.arange(HEAD_DIM_C)
    axis_is_col = (lane // ROPE_AXIS_DIM) % 2 == 1
    idx = lane % ROPE_AXIS_DIM
    ang = np.where(axis_is_col[None, :], col[:, None], row[:, None]) * inv[idx % half][None, :]
    sign = np.where(idx < half, -1.0, 1.0)[None, :]
    return (jnp.asarray(np.cos(ang), F32), jnp.asarray(np.sin(ang) * sign, F32))


def _mixer_attn_lat(p, cache_k4, cache_v4, lambda_qk, subln_row, layer, lam_init, ymix):
    cos_t, sin_t = _rope_tables()
    q_tiles = DEC_SEQ // ATT_TQ
    q0 = N_CTX // ATT_TQ
    b0 = N_CTX // DEC_SEQ
    qblk = lambda c: pl.BlockSpec((ATT_TQ, BRANCH), lambda b, i: (q0 + b * q_tiles + i, c))
    kblk = lambda c: pl.BlockSpec((DEC_SEQ, BRANCH), lambda b, i: (b0 + b, c))
    cblk = pl.BlockSpec((None, None, PAST_LEN, BRANCH), lambda b, i: (b, layer, 0, 0))
    return pl.pallas_call(
        functools.partial(_attn_lat_kernel, lam_init),
        grid=(DEC_BATCH, q_tiles),
        in_specs=[
            qblk(C_Q), kblk(C_K), kblk(C_V), qblk(C_CG), cblk, cblk,
            pl.BlockSpec((ATT_TQ, HEAD_DIM_C), lambda b, i: (i, 0)),
            pl.BlockSpec((ATT_TQ, HEAD_DIM_C), lambda b, i: (i, 0)),
            pl.BlockSpec((DEC_SEQ, HEAD_DIM_C), lambda b, i: (0, 0)),
            pl.BlockSpec((DEC_SEQ, HEAD_DIM_C), lambda b, i: (0, 0)),
            pl.BlockSpec((None, 4, QK_HALF), lambda b, i: (layer, 0, 0)),
            pl.BlockSpec((None, 1, HEAD_DIM_C), lambda b, i: (layer, 0, 0)),
            pl.BlockSpec(memory_space=pl.ANY),
        ],
        out_specs=pl.BlockSpec((ATT_TQ, BRANCH), lambda b, i: (q0 + b * q_tiles + i, M_ATTN)),
        out_shape=jax.ShapeDtypeStruct((N_TOK, D_MIX), BF16),
        input_output_aliases={12: 0},
        scratch_shapes=[pltpu.VMEM((PAST_LEN, BRANCH), BF16), pltpu.VMEM((DEC_SEQ, BRANCH), BF16),
                        pltpu.VMEM((N_HEADS_C, V_ROWS, PAST_LEN + DEC_SEQ), BF16)],
        compiler_params=_params("arbitrary", "arbitrary"),
        name="mixer_attn_lat",
    )(p, p, p, p, cache_k4, cache_v4, cos_t, sin_t, cos_t, sin_t, lambda_qk, subln_row, ymix)


HY_ROWS = 1024


def _dft_matrices(seq_len):
    n = 2 * seq_len
    f = np.arange(seq_len, dtype=np.float64)[:, None]
    s = np.arange(seq_len, dtype=np.float64)[None, :]
    theta = 2.0 * np.pi * f * s / n
    alt = np.where(np.arange(seq_len) % 2 == 0, 1.0, -1.0)
    ac = np.cos(theta)
    as_ = -np.sin(theta)
    as_[0, :] = alt
    bc = 2.0 * np.cos(theta.T) / n
    bc[:, 0] = 1.0 / n
    bs = -2.0 * np.sin(theta.T) / n
    bs[:, 0] = alt / n
    fwd = np.concatenate([ac, as_], axis=0)
    inv = np.concatenate([bc, bs], axis=1)
    return jnp.asarray(fwd, F32).astype(BF16), jnp.asarray(inv, F32).astype(BF16)


def _filter_features(seq_len):
    t_idx = np.arange(seq_len, dtype=np.float64)
    t_norm = np.linspace(0.0, 1.0, seq_len)
    bands = np.linspace(1e-4, HY_BANDS - 1, HY_BANDS)
    ang = (2.0 * math.pi * t_idx / seq_len)[:, None] * bands[None, :]
    feats = np.concatenate([t_norm[:, None], np.cos(ang), np.sin(ang)], axis=-1)
    feats = np.pad(feats, ((0, 0), (0, HY_PAD - HY_EMB)))
    deltas = np.abs(np.linspace(math.log(HY_TARGET) / HY_FAST_DECAY,
                                math.log(HY_TARGET) / HY_SLOW_DECAY, BRANCH))
    return (jnp.asarray(feats, F32), jnp.asarray(t_norm[:, None], F32),
            jnp.asarray(deltas[None, :], F32))


def _filter_kernel(seq_len, feats_ref, tn_ref, dl_ref, w1_ref, b1_ref, w2_ref, b2_ref, fr_ref,
                   w3f_ref, w3b_ref, fwd_ref, kr_ref, ki_ref, h_ref):
    sp = lambda x: _split(x, 2)

    @pl.when(pl.program_id(1) == 0)
    def _():
        fr = fr_ref[...]
        h = jnp.sin(fr * (_sdot(sp(feats_ref[...]), sp(w1_ref[...])) + b1_ref[...]))
        h_ref[...] = jnp.sin(fr * (_sdot(sp(h), sp(w2_ref[...])) + b2_ref[...]))

    h = sp(h_ref[...])
    decay = jnp.exp(-tn_ref[...] * dl_ref[...])
    row = lax.broadcasted_iota(jnp.int32, (seq_len, BRANCH), 0)
    fwd = _sdot(h, sp(w3f_ref[...])) * decay
    bwd = jnp.where(row == 0, 0.0, _sdot(h, sp(w3b_ref[...])) * decay)
    norm = (jnp.sum(jnp.abs(fwd), axis=0, keepdims=True)
            + jnp.sum(jnp.abs(bwd), axis=0, keepdims=True))
    fwd = fwd / norm
    bwd = bwd / norm
    even = fwd + bwd
    alt = jnp.where((row & 1) == 0, 1.0, -1.0)
    nyquist = jnp.sum(alt * even, axis=0, keepdims=True)
    kr_ref[...] = jnp.dot(fwd_ref[0:seq_len, :], even.astype(BF16), preferred_element_type=F32)
    ki = jnp.dot(fwd_ref[seq_len:2 * seq_len, :], (fwd - bwd).astype(BF16),
                 preferred_element_type=F32)
    ki_ref[...] = jnp.where(row == 0, nyquist, ki)


def _hyena_spectrum(seq_len, filt, fwd_mat):
    feats, t_norm, deltas = _filter_features(seq_len)
    w1, b1, w2, b2, freq, w3 = filt
    full = lambda a: pl.BlockSpec(a.shape, lambda l, o: (0,) * a.ndim)
    lyr = lambda r, n: pl.BlockSpec((None, r, n), lambda l, o: (l, 0, 0))
    out = pl.BlockSpec((None, None, seq_len, BRANCH), lambda l, o: (l, o, 0, 0))
    return pl.pallas_call(
        functools.partial(_filter_kernel, seq_len),
        grid=(DEPTH, HY_ORDER),
        in_specs=[
            full(feats), full(t_norm), full(deltas),
            lyr(HY_PAD, HY_PAD), lyr(1, HY_PAD), lyr(HY_PAD, HY_PAD), lyr(1, HY_PAD), lyr(1, HY_PAD),
            pl.BlockSpec((None, HY_PAD, BRANCH), lambda l, o: (l, 0, 2 * o)),
            pl.BlockSpec((None, HY_PAD, BRANCH), lambda l, o: (l, 0, 2 * o + 1)),
            full(fwd_mat),
        ],
        out_specs=[out, out],
        out_shape=[jax.ShapeDtypeStruct((DEPTH, HY_ORDER, seq_len, BRANCH), F32)] * 2,
        scratch_shapes=[pltpu.VMEM((seq_len, HY_PAD), F32)],
        compiler_params=_params("arbitrary", "arbitrary"),
        name=f"hyena_spectrum_{seq_len}",
    )(feats, t_norm, deltas, w1, b1, w2, b2, freq, w3, w3, fwd_mat)


def _hyena_chains(seq_len, width, x1_ref, x2_ref, hv_ref, g_ref, cw_ref, cb_ref, kr_ref, ki_ref,
                  hb_ref, fwd_ref, inv_ref, o_ref):
    row = lax.broadcasted_iota(jnp.int32, (seq_len, width), 0)
    first, last = row == 0, row == seq_len - 1

    def chain(rs, c0):
        cs = slice(c0, c0 + width)

        def short_conv(x_ref, piece):
            x = x_ref[rs, cs]
            w = cw_ref[:, piece * BRANCH + c0:piece * BRANCH + c0 + width]
            b = cb_ref[:, piece * BRANCH + c0:piece * BRANCH + c0 + width]
            prev = jnp.where(first, 0.0, pltpu.roll(x, 1, axis=0))
            nxt = jnp.where(last, 0.0, pltpu.roll(x, seq_len - 1, axis=0))
            return prev * w[0:1] + x * w[1:2] + nxt * w[2:3] + b

        z = short_conv(hv_ref, 2)
        gates = (short_conv(x1_ref, 0), short_conv(x2_ref, 1))
        yield
        for order in range(HY_ORDER):
            zf = jnp.dot(fwd_ref[...], z.astype(BF16), preferred_element_type=F32)
            yield
            zr, zi = zf[:seq_len], zf[seq_len:]
            kr, kp = kr_ref[order, :, cs], ki_ref[order, :, cs]
            ki = jnp.where(first, 0.0, kp)
            kn = jnp.where(first, kp, kr)
            yf = jnp.concatenate([(zr * kr - zi * ki).astype(BF16),
                                  (zr * ki + zi * kn).astype(BF16)], axis=0)
            yield
            y = jnp.dot(inv_ref[...], yf, preferred_element_type=F32)
            yield
            z = gates[order] * (y + z * hb_ref[order:order + 1, cs])
        o_ref[rs, cs] = (_silu(g_ref[rs, cs]) * z).astype(BF16)

    waiting = [chain(slice(s * seq_len, (s + 1) * seq_len), c0)
               for s in range(HY_ROWS // seq_len) for c0 in range(0, BRANCH, width)]
    running = []
    while waiting or running:
        if waiting:
            running.append(waiting.pop(0))
        for gen in list(running):
            if next(gen, "done") == "done":
                running.remove(gen)


def _hyena_kernel(x1_ref, x2_ref, hv_ref, g_ref, cw_ref, cb_ref, hb_ref,
                  krc_ref, kic_ref, fwdc_ref, invc_ref, krl_ref, kil_ref, fwdl_ref, invl_ref,
                  ymix_ref, o_ref):
    del ymix_ref
    common = (x1_ref, x2_ref, hv_ref, g_ref, cw_ref, cb_ref)
    is_ctx = pl.program_id(0) < N_CTX // HY_ROWS

    @pl.when(is_ctx)
    def _():
        _hyena_chains(SEQ, BRANCH, *common, krc_ref, kic_ref, hb_ref, fwdc_ref, invc_ref, o_ref)

    @pl.when(jnp.logical_not(is_ctx))
    def _():
        _hyena_chains(DEC_SEQ, HY_CT, *common, krl_ref, kil_ref, hb_ref, fwdl_ref, invl_ref, o_ref)


def _mixer_hyena(p, conv_w, conv_b3, hyena_bias, spec_ctx, mats_ctx, spec_lat, mats_lat, layer,
                 ymix):
    blk = lambda c: pl.BlockSpec((HY_ROWS, BRANCH), lambda i: (i, c))
    once = pl.Buffered(1)
    lyr = lambda a: pl.BlockSpec((None,) + a.shape[1:], lambda i: (layer,) + (0,) * (a.ndim - 1),
                                 pipeline_mode=once)
    full = lambda a: pl.BlockSpec(a.shape, lambda i: (0,) * a.ndim, pipeline_mode=once)
    consts = [*spec_ctx, *mats_ctx, *spec_lat, *mats_lat]
    return pl.pallas_call(
        _hyena_kernel,
        grid=(N_TOK // HY_ROWS,),
        in_specs=[blk(C_DX1), blk(C_DX2), blk(C_DV), blk(C_DG),
                  lyr(conv_w), lyr(conv_b3), lyr(hyena_bias),
                  lyr(spec_ctx[0]), lyr(spec_ctx[1]), full(mats_ctx[0]), full(mats_ctx[1]),
                  lyr(spec_lat[0]), lyr(spec_lat[1]), full(mats_lat[0]), full(mats_lat[1]),
                  pl.BlockSpec(memory_space=pl.ANY)],
        out_specs=pl.BlockSpec((HY_ROWS, BRANCH), lambda i: (i, M_HYENA)),
        out_shape=jax.ShapeDtypeStruct((N_TOK, D_MIX), BF16),
        input_output_aliases={7 + len(consts): 0},
        compiler_params=_params("arbitrary"),
        name="mixer_hyena",
    )(p, p, p, p, conv_w, conv_b3, hyena_bias, *consts, ymix)


OUT_TM = 512


def _outproj_kernel(alpha, n_x, n_out, ymix_ref, *refs):
    x_refs = refs[:n_x]
    gate_ref, w_ref, b_ref, lng_ref, lnb_ref = refs[n_x:n_x + 5]
    o_refs = refs[n_x + 5:]
    is_ctx = pl.program_id(0) < N_CTX // OUT_TM
    x = x_refs[0][...] if n_x == 1 else jnp.where(is_ctx, x_refs[0][...], x_refs[1][...])
    y = jnp.dot(ymix_ref[...], w_ref[...], preferred_element_type=F32) + b_ref[...]
    r = alpha * x + gate_ref[...] * y
    res = _layer_norm(r) * lng_ref[...] + lnb_ref[...]
    if n_out == 1:
        o_refs[0][...] = res
    else:
        @pl.when(is_ctx)
        def _():
            o_refs[0][...] = res

        @pl.when(jnp.logical_not(is_ctx))
        def _():
            o_refs[1][...] = res


def _out_projection(ymix, xs, mod4, w_out_bf, b_out3, ln_g3, ln_b3, layer, split_out):
    alpha = (2.0 * DEPTH) ** 0.25
    row = lambda i: _cond_row(i, OUT_TM)
    vec = pl.BlockSpec((None, 1, D_MODEL), lambda i: (layer, 0, 0))
    if split_out:
        out_shape = [jax.ShapeDtypeStruct((N_CTX, D_MODEL), F32),
                     jax.ShapeDtypeStruct((N_LAT, D_MODEL), F32)]
    else:
        out_shape = [jax.ShapeDtypeStruct((N_TOK, D_MODEL), F32)]
    return pl.pallas_call(
        functools.partial(_outproj_kernel, alpha, len(xs), len(out_shape)),
        grid=(N_TOK // OUT_TM,),
        in_specs=[pl.BlockSpec((OUT_TM, D_MIX), lambda i: (i, 0))] + _token_specs(xs, OUT_TM) + [
            pl.BlockSpec((None, None, 1, D_MODEL), lambda i: (layer, row(i), 0, 2)),
            pl.BlockSpec((None, D_MODEL, D_MODEL), lambda i: (layer, 0, 0),
                         pipeline_mode=pl.Buffered(1)),
            vec, vec, vec,
        ],
        out_specs=_token_specs(out_shape, OUT_TM),
        out_shape=out_shape,
        compiler_params=_params("arbitrary"),
        name="out_projection",
    )(ymix, *xs, mod4, w_out_bf, b_out3, ln_g3, ln_b3)


def kernel(x_prompt, x_sample, cache_k, cache_v, c, c_ctx, w_mod, b_mod, w_in, gmlp_w, gmlp_b,
           pool_w, pool_scale, lambda_qk, subln_w, conv_w, conv_b, filt_w1, filt_b1, filt_w2,
           filt_b2, filt_freq, filt_w3, hyena_bias, w_out, b_out, ln_g, ln_b):
    xs = (x_prompt.reshape(N_CTX, D_MODEL), x_sample.reshape(N_LAT, D_MODEL))
    cond = jnp.concatenate(
        [c_ctx[None, :], c, jnp.zeros((N_COND - 1 - DEC_BATCH, D_MODEL), F32)], axis=0)
    mod4 = _modulation(cond, w_mod, b_mod).reshape(DEPTH, N_COND, 1, 3 * D_MODEL)

    w_in_bf = w_in.astype(BF16)
    w_out_bf = w_out.astype(BF16)
    cache_k4 = cache_k.reshape(DEC_BATCH, DEPTH, PAST_LEN, BRANCH)
    cache_v4 = cache_v.reshape(DEC_BATCH, DEPTH, PAST_LEN, BRANCH)
    gmlp_b_rows = jnp.broadcast_to(gmlp_b[..., None], (DEPTH, N_GROUPS, CHUNK, GROUP_W))
    subln_row = subln_w.reshape(DEPTH, 1, HEAD_DIM_C)
    conv_b3 = conv_b.reshape(DEPTH, 1, 3 * BRANCH)
    b_out3 = b_out.reshape(DEPTH, 1, D_MODEL)
    ln_g3 = ln_g.reshape(DEPTH, 1, D_MODEL)
    ln_b3 = ln_b.reshape(DEPTH, 1, D_MODEL)

    pad_h = HY_PAD - HY_HIDDEN
    filt = (
        jnp.pad(filt_w1, ((0, 0), (0, HY_PAD - HY_EMB), (0, pad_h))),
        jnp.pad(filt_b1, ((0, 0), (0, pad_h))).reshape(DEPTH, 1, HY_PAD),
        jnp.pad(filt_w2, ((0, 0), (0, pad_h), (0, pad_h))),
        jnp.pad(filt_b2, ((0, 0), (0, pad_h))).reshape(DEPTH, 1, HY_PAD),
        jnp.pad(filt_freq, ((0, 0), (0, pad_h))).reshape(DEPTH, 1, HY_PAD),
        jnp.pad(filt_w3, ((0, 0), (0, pad_h), (0, 0))),
    )
    mats_ctx = _dft_matrices(SEQ)
    mats_lat = _dft_matrices(DEC_SEQ)
    spec_ctx = _hyena_spectrum(SEQ, filt, mats_ctx[0])
    spec_lat = _hyena_spectrum(DEC_SEQ, filt, mats_lat[0])

    caches = None
    for layer in range(DEPTH):
        lam_init = 0.8 - 0.6 * math.exp(-0.3 * layer)
        p = _in_projection(xs, mod4, w_in_bf, layer)
        ymix = _mixer_gmlp(p, gmlp_w[layer], gmlp_b_rows[layer])
        ymix = _mixer_pool(p, pool_w[layer], pool_scale[layer].reshape(1, BRANCH), ymix)
        ymix, new_k, new_v = _mixer_attn_ctx(p, lambda_qk, subln_row, layer, lam_init, ymix, caches)
        caches = (new_k, new_v)
        ymix = _mixer_attn_lat(p, cache_k4, cache_v4, lambda_qk, subln_row, layer, lam_init, ymix)
        ymix = _mixer_hyena(p, conv_w, conv_b3, hyena_bias, spec_ctx, mats_ctx, spec_lat, mats_lat,
                            layer, ymix)
        xs = tuple(_out_projection(ymix, xs, mod4, w_out_bf, b_out3, ln_g3, ln_b3, layer,
                                   split_out=layer == DEPTH - 1))

    y_prompt = xs[0].reshape(BATCH, SEQ, D_MODEL)
    y_sample = xs[1].reshape(DEC_BATCH, DEC_SEQ, D_MODEL)
    new_k, new_v = caches
    return (y_prompt, y_sample,
            new_k.reshape(BATCH, DEPTH, SEQ, N_HEADS_C, 2, QK_HALF),
            new_v.reshape(BATCH, DEPTH, SEQ, N_HEADS_C, HEAD_DIM_C))
```

```python
import functools
import math

import numpy as np
import jax
import jax.numpy as jnp
from jax import lax
from jax.experimental import pallas as pl
from jax.experimental.pallas import tpu as pltpu

F32 = jnp.float32
BF16 = jnp.bfloat16

D_MODEL = 2048
BATCH = 16
SEQ = 256
DEPTH = 2
DEC_BATCH = 4
DEC_SEQ = 1024
PAST_LEN = 512
GRID_W = 64
BRANCH = 512
N_GROUPS = 4
GROUP_W = 128
CHUNK = 128
POOL_WINDOWS = (2, 4, 8, 16)
N_HEADS_C = 4
HEAD_DIM_C = 128
QK_HALF = 64
ROPE_AXIS_DIM = 32
ROPE_BASE = 10000.0
HY_BANDS = 16
HY_EMB = 33
HY_HIDDEN = 64
HY_ORDER = 2
HY_FAST_DECAY = 0.3
HY_SLOW_DECAY = 1.5
HY_TARGET = 1e-2
N_IN_PIECES = 13
D_IN = N_IN_PIECES * BRANCH
LN_EPS = 1e-6

N_CTX = BATCH * SEQ
N_LAT = DEC_BATCH * DEC_SEQ
N_TOK = N_CTX + N_LAT
N_COND = 8
LANES = 128
HY_PAD = LANES
HY_CT = 256
VMEM_LIMIT = 56 * 1024 * 1024

(C_AU, C_AV, C_AG, C_BX, C_BG, C_Q, C_K, C_V, C_CG, C_DX1, C_DX2, C_DV, C_DG) = range(13)
D_MIX = 4 * BRANCH
(M_GMLP, M_POOL, M_ATTN, M_HYENA) = range(4)


def _silu(x):
    return x * jax.nn.sigmoid(x)


def _bdot(a, b):
    return jnp.dot(a.astype(BF16), b.astype(BF16), preferred_element_type=F32)


def _split(x, n_terms):
    hi = x.astype(BF16)
    if n_terms == 1:
        return (hi,)
    return (hi, (x - hi.astype(F32)).astype(BF16))


def _sdot(a_terms, b_terms):
    acc = jnp.dot(a_terms[0], b_terms[0], preferred_element_type=F32)
    if len(a_terms) > 1:
        acc = acc + jnp.dot(a_terms[1], b_terms[0], preferred_element_type=F32)
    if len(b_terms) > 1:
        acc = acc + jnp.dot(a_terms[0], b_terms[1], preferred_element_type=F32)
    return acc


def _layer_norm(x):
    mu = jnp.mean(x, axis=-1, keepdims=True)
    xc = x - mu
    var = jnp.mean(xc * xc, axis=-1, keepdims=True)
    return xc * lax.rsqrt(var + LN_EPS)


def _cond_row(tile, rows_per_tile):
    n_ctx_tiles = N_CTX // rows_per_tile
    tiles_per_batch = DEC_SEQ // rows_per_tile
    return jnp.where(tile < n_ctx_tiles, 0, 1 + (tile - n_ctx_tiles) // tiles_per_batch)


def _params(*semantics):
    return pltpu.CompilerParams(dimension_semantics=semantics, vmem_limit_bytes=VMEM_LIMIT)


MOD_TN = 512


def _mod_kernel(c_ref, w_ref, b_ref, o_ref):
    o_ref[...] = _bdot(_silu(c_ref[...]), w_ref[...]) + b_ref[...]


def _modulation(cond, w_mod, b_mod):
    n = 3 * D_MODEL
    return pl.pallas_call(
        _mod_kernel,
        grid=(DEPTH, n // MOD_TN),
        in_specs=[
            pl.BlockSpec((N_COND, D_MODEL), lambda l, j: (0, 0)),
            pl.BlockSpec((None, D_MODEL, MOD_TN), lambda l, j: (l, 0, j)),
            pl.BlockSpec((None, 1, MOD_TN), lambda l, j: (l, 0, j)),
        ],
        out_specs=pl.BlockSpec((None, N_COND, MOD_TN), lambda l, j: (l, 0, j)),
        out_shape=jax.ShapeDtypeStruct((DEPTH, N_COND, n), F32),
        compiler_params=_params("arbitrary", "arbitrary"),
        name="modulation",
    )(cond, w_mod, b_mod.reshape(DEPTH, 1, n))


IN_TM = 1024
IN_TN = 512
IN_LN_ROWS = 256


def _token_specs(xs, tm, tile0=0):
    n_ctx_tiles = N_CTX // tm
    if len(xs) == 1:
        maps = [lambda i, *_: (i + tile0, 0)]
    else:
        maps = [lambda i, *_: (jnp.minimum(i + tile0, n_ctx_tiles - 1), 0),
                lambda i, *_: (jnp.maximum(i + tile0 - n_ctx_tiles, 0), 0)]
    return [pl.BlockSpec((tm, D_MODEL), m) for m in maps]


def _inproj_kernel(n_x, tile0, cast_w, *refs):
    x_refs = refs[:n_x]
    scale_ref, shift_ref, w_ref = refs[n_x:n_x + 3]
    if cast_w:
        o_ref, wb_ref, h_ref = refs[n_x + 3:]
    else:
        _, o_ref, h_ref = refs[n_x + 3:]

    def fill_h(x_ref):
        def body(r, carry):
            rows = pl.ds(pl.multiple_of(r * IN_LN_ROWS, IN_LN_ROWS), IN_LN_ROWS)
            h = _layer_norm(x_ref[rows, :]) * (1.0 + scale_ref[...]) + shift_ref[...]
            h_ref[rows, :] = h.astype(BF16)
            return carry
        lax.fori_loop(0, IN_TM // IN_LN_ROWS, body, 0)

    first = pl.program_id(1) == 0
    if n_x == 1:
        pl.when(first)(lambda: fill_h(x_refs[0]))
    else:
        is_ctx = pl.program_id(0) + tile0 < N_CTX // IN_TM
        pl.when(jnp.logical_and(first, is_ctx))(lambda: fill_h(x_refs[0]))
        pl.when(jnp.logical_and(first, jnp.logical_not(is_ctx)))(lambda: fill_h(x_refs[1]))

    if cast_w:
        w = w_ref[...].astype(BF16)
        wb_ref[...] = w
    else:
        w = w_ref[...]
    o_ref[...] = jnp.dot(h_ref[...], w, preferred_element_type=F32)


def _in_projection(xs, mod4, w_in, layer):
    n_col = D_IN // IN_TN

    def mod_spec(tile0, piece):
        return pl.BlockSpec((None, None, 1, D_MODEL),
                            lambda i, j: (layer, _cond_row(i + tile0, IN_TM), 0, piece))

    scratch = [pltpu.VMEM((IN_TM, D_MODEL), BF16)]
    p_shape = jax.ShapeDtypeStruct((N_TOK, D_IN), F32)
    p, w_bf = pl.pallas_call(
        functools.partial(_inproj_kernel, 1, 0, True),
        grid=(1, n_col),
        in_specs=_token_specs(xs[:1], IN_TM) + [
            mod_spec(0, 1), mod_spec(0, 0),
            pl.BlockSpec((None, D_MODEL, IN_TN), lambda i, j: (layer, 0, j)),
        ],
        out_specs=[pl.BlockSpec((IN_TM, IN_TN), lambda i, j: (i, j)),
                   pl.BlockSpec((D_MODEL, IN_TN), lambda i, j: (0, j))],
        out_shape=[p_shape, jax.ShapeDtypeStruct((D_MODEL, D_IN), BF16)],
        scratch_shapes=scratch,
        compiler_params=_params("arbitrary", "arbitrary"),
        name="in_projection_first",
    )(xs[0], mod4, mod4, w_in)
    return pl.pallas_call(
        functools.partial(_inproj_kernel, len(xs), 1, False),
        grid=(N_TOK // IN_TM - 1, n_col),
        in_specs=_token_specs(xs, IN_TM, 1) + [
            mod_spec(1, 1), mod_spec(1, 0),
            pl.BlockSpec((D_MODEL, IN_TN), lambda i, j: (0, j)),
            pl.BlockSpec(memory_space=pl.ANY),
        ],
        out_specs=pl.BlockSpec((IN_TM, IN_TN), lambda i, j: (i + 1, j)),
        out_shape=p_shape,
        input_output_aliases={len(xs) + 3: 0},
        scratch_shapes=scratch,
        compiler_params=_params("arbitrary", "arbitrary"),
        name="in_projection",
    )(*xs, mod4, mod4, w_bf, p)


AB_TM = 1024


def _gmlp(u_ref, v_ref, g_ref, w_ref, b_ref, o_ref):
    ws = [w_ref[g].astype(BF16) for g in range(N_GROUPS)]
    for r in range(0, AB_TM, CHUNK):
        rows = slice(r, r + CHUNK)
        vn = _layer_norm(v_ref[rows, :]).astype(BF16)
        for g in range(N_GROUPS):
            cols = slice(g * GROUP_W, (g + 1) * GROUP_W)
            mixed = jnp.dot(ws[g], vn[:, cols], preferred_element_type=F32) + b_ref[g]
            o_ref[rows, cols] = (_silu(g_ref[rows, cols]) * u_ref[rows, cols] * mixed).astype(BF16)


def _pool(x_ref, g_ref, w_ref, s_ref, o_ref):
    seq_len = jnp.where(pl.program_id(0) < N_CTX // AB_TM, SEQ, DEC_SEQ)
    pos = lax.broadcasted_iota(jnp.int32, (AB_TM, GROUP_W), 0) & (seq_len - 1)
    for g, win in enumerate(POOL_WINDOWS):
        cols = slice(g * GROUP_W, (g + 1) * GROUP_W)
        p = x_ref[:, cols]
        acc = p
        for d in range(-(win // 2), win // 2):
            if d == 0:
                continue
            shifted = pltpu.roll(p, (-d) % AB_TM, axis=0)
            valid = (pos >= -d) if d < 0 else (pos < seq_len - d)
            acc = acc + jnp.where(valid, shifted, 0.0)
        count = jnp.minimum(pos + win // 2, seq_len) - jnp.maximum(pos - win // 2, 0)
        pooled = acc / count.astype(F32)
        y = _bdot(pooled - p, w_ref[g])
        o_ref[:, BRANCH + g * GROUP_W:BRANCH + (g + 1) * GROUP_W] = (
            _silu(g_ref[:, cols]) * (y * s_ref[:, cols])).astype(BF16)


def _mixer_ab_kernel(au_ref, av_ref, ag_ref, bx_ref, bg_ref, gw_ref, gb_ref, pw_ref, ps_ref, o_ref):
    _gmlp(au_ref, av_ref, ag_ref, gw_ref, gb_ref, o_ref)
    _pool(bx_ref, bg_ref, pw_ref, ps_ref, o_ref)


def _mixer_gmlp_pool(p, gmlp_w, gmlp_b_rows, pool_w, pool_scale_row):
    assert M_POOL == M_GMLP + 1
    blk = lambda c: pl.BlockSpec((AB_TM, BRANCH), lambda i: (i, c))
    full = lambda a: pl.BlockSpec(a.shape, lambda i: (0,) * a.ndim)
    return pl.pallas_call(
        _mixer_ab_kernel,
        grid=(N_TOK // AB_TM,),
        in_specs=[blk(C_AU), blk(C_AV), blk(C_AG), blk(C_BX), blk(C_BG),
                  full(gmlp_w), full(gmlp_b_rows), full(pool_w), full(pool_scale_row)],
        out_specs=pl.BlockSpec((AB_TM, 2 * BRANCH), lambda i: (i, M_GMLP // 2)),
        out_shape=jax.ShapeDtypeStruct((N_TOK, D_MIX), BF16),
        compiler_params=_params("arbitrary"),
        name="mixer_gmlp_pool",
    )(p, p, p, p, p, gmlp_w, gmlp_b_rows, pool_w, pool_scale_row)


ATT_TQ = 256
V_ROWS = HEAD_DIM_C + 16
NT_DIMS = (((1,), (1,)), ((), ()))


def _lambda(lam_ref, lam_init):
    lq = lam_ref[...]
    a = jnp.sum(lq[0:1] * lq[1:2], axis=-1, keepdims=True)
    b = jnp.sum(lq[2:3] * lq[3:4], axis=-1, keepdims=True)
    return jnp.exp(a) - jnp.exp(b) + lam_init


def _map_masks():
    lane = lax.broadcasted_iota(jnp.int32, (1, HEAD_DIM_C), 1)
    m0 = (lane < QK_HALF).astype(F32)
    return m0, 1.0 - m0


def _scores_t(q, keys):
    return [lax.dot_general(k, q, NT_DIMS, preferred_element_type=F32) for k in keys]


def _softmax_v_t(s, vals_t):
    m = functools.reduce(jnp.maximum, [jnp.max(x, axis=0, keepdims=True) for x in s])
    acc = functools.reduce(
        jnp.add, [jnp.dot(v, jnp.exp2(x - m).astype(BF16), preferred_element_type=F32)
                  for x, v in zip(s, vals_t)])
    return acc[:HEAD_DIM_C] * (1.0 / acc[HEAD_DIM_C:HEAD_DIM_C + 1])


def _with_ones_rows(v_t):
    ones = jnp.ones((V_ROWS - HEAD_DIM_C, v_t.shape[1]), BF16)
    return jnp.concatenate([v_t.astype(BF16), ones], axis=0)


def _diff_attention(q_of, keys_of, vals_t_of, gate_of, store, lam, lam_init, subln):
    masks = _map_masks()
    tasks = [(h, mp) for h in range(N_HEADS_C) for mp in range(2)]
    per_head = {}

    def operands(h):
        if h not in per_head:
            q = q_of(h) * (QK_HALF ** -0.5 * math.log2(math.e))
            per_head[h] = (q, keys_of(h), vals_t_of(h))
        return per_head[h]

    def scores(h, mp):
        q, keys, _ = operands(h)
        return _scores_t((q * masks[mp]).astype(BF16), keys)

    nxt = scores(*tasks[0])
    outs = []
    for i, (h, mp) in enumerate(tasks):
        cur = nxt
        if i + 1 < len(tasks):
            nxt = scores(*tasks[i + 1])
        outs.append(_softmax_v_t(cur, operands(h)[2]))
        if mp == 1:
            o = (outs[0] - lam * outs[1]).T
            outs = []
            o = o * lax.rsqrt(jnp.mean(o * o, axis=-1, keepdims=True) + 1e-5)
            o = o * subln * (1.0 - lam_init)
            store(h, _silu(gate_of(h)) * o)


def _head_cols(h):
    return slice(h * HEAD_DIM_C, (h + 1) * HEAD_DIM_C)


def _attn_ctx_kernel(lam_init, q_ref, k_ref, v_ref, g_ref, lam_ref, sw_ref, *rest):
    o_ref, ko_ref, vo_ref = rest[-3:]
    ko_ref[...] = k_ref[...]
    vo_ref[...] = v_ref[...]

    def store(h, y):
        o_ref[:, _head_cols(h)] = y.astype(BF16)

    _diff_attention(
        lambda h: q_ref[:, _head_cols(h)],
        lambda h: [k_ref[:, _head_cols(h)].astype(BF16)],
        lambda h: [_with_ones_rows(v_ref[:, _head_cols(h)].T)],
        lambda h: g_ref[:, _head_cols(h)],
        store, _lambda(lam_ref, lam_init), lam_init, sw_ref[...])


def _mixer_attn_ctx(p, lambda_qk, subln_row, layer, lam_init, ymix, caches):
    blk = lambda c: pl.BlockSpec((SEQ, BRANCH), lambda b: (b, c))
    any_spec = pl.BlockSpec(memory_space=pl.ANY)
    cache_spec = pl.BlockSpec((None, None, SEQ, BRANCH), lambda b: (b, layer, 0, 0))
    cache_shape = jax.ShapeDtypeStruct((BATCH, DEPTH, SEQ, BRANCH), F32)
    in_specs = [
        blk(C_Q), blk(C_K), blk(C_V), blk(C_CG),
        pl.BlockSpec((None, 4, QK_HALF), lambda b: (layer, 0, 0)),
        pl.BlockSpec((None, 1, HEAD_DIM_C), lambda b: (layer, 0, 0)),
        any_spec,
    ]
    args = [p, p, p, p, lambda_qk, subln_row, ymix]
    aliases = {6: 0}
    if caches is not None:
        in_specs += [any_spec, any_spec]
        args += list(caches)
        aliases.update({7: 1, 8: 2})
    return pl.pallas_call(
        functools.partial(_attn_ctx_kernel, lam_init),
        grid=(BATCH,),
        in_specs=in_specs,
        out_specs=[pl.BlockSpec((SEQ, BRANCH), lambda b: (b, M_ATTN)), cache_spec, cache_spec],
        out_shape=[jax.ShapeDtypeStruct((N_TOK, D_MIX), BF16), cache_shape, cache_shape],
        input_output_aliases=aliases,
        compiler_params=_params("arbitrary"),
        name="mixer_attn_ctx",
    )(*args)


def _rope(x, cos, sin_signed):
    lane = lax.broadcasted_iota(jnp.int32, x.shape, 1)
    first_half = (lane & (ROPE_AXIS_DIM - 1)) < (ROPE_AXIS_DIM // 2)
    half = ROPE_AXIS_DIM // 2
    partner = jnp.where(first_half,
                        pltpu.roll(x, x.shape[1] - half, axis=1),
                        pltpu.roll(x, half, axis=1))
    return x * cos + partner * sin_signed


def _attn_lat_kernel(lam_init, q_ref, k_ref, v_ref, g_ref, ck_ref, cv_ref, cosq_ref, sinq_ref,
                     cosk_ref, sink_ref, lam_ref, sw_ref, ymix_ref, o_ref, kc_ref, kr_ref, vt_ref):
    del ymix_ref
    @pl.when(pl.program_id(1) == 0)
    def _():
        kc_ref[...] = ck_ref[...].astype(BF16)
        for h in range(N_HEADS_C):
            cols = slice(h * HEAD_DIM_C, (h + 1) * HEAD_DIM_C)
            kr_ref[:, cols] = _rope(k_ref[:, cols], cosk_ref[...], sink_ref[...]).astype(BF16)
            vt_ref[h, :, 0:PAST_LEN] = _with_ones_rows(cv_ref[:, cols].T)
            vt_ref[h, :, PAST_LEN:PAST_LEN + DEC_SEQ] = _with_ones_rows(v_ref[:, cols].T)

    def store(h, y):
        o_ref[:, _head_cols(h)] = y.astype(BF16)

    _diff_attention(
        lambda h: _rope(q_ref[:, _head_cols(h)], cosq_ref[...], sinq_ref[...]),
        lambda h: [kc_ref[:, _head_cols(h)], kr_ref[:, _head_cols(h)]],
        lambda h: [vt_ref[h, :, 0:PAST_LEN], vt_ref[h, :, PAST_LEN:PAST_LEN + DEC_SEQ]],
        lambda h: g_ref[:, _head_cols(h)],
        store, _lambda(lam_ref, lam_init), lam_init, sw_ref[...])


def _rope_tables():
    pos = np.arange(DEC_SEQ)
    row = (pos // GRID_W).astype(np.float64)
    col = (pos % GRID_W).astype(np.float64)
    half = ROPE_AXIS_DIM // 2
    inv = ROPE_BASE ** (-np.arange(0, ROPE_AXIS_DIM, 2, dtype=np.float64) / ROPE_AXIS_DIM)
    lane = np.arange(HEAD_DIM_C)
    axis_is_col = (lane // ROPE_AXIS_DIM) % 2 == 1
    idx = lane % ROPE_AXIS_DIM
    ang = np.where(axis_is_col[None, :], col[:, None], row[:, None]) * inv[idx % half][None, :]
    sign = np.where(idx < half, -1.0, 1.0)[None, :]
    return (jnp.asarray(np.cos(ang), F32), jnp.asarray(np.sin(ang) * sign, F32))


def _mixer_attn_lat(p, cache_k4, cache_v4, lambda_qk, subln_row, layer, lam_init, ymix):
    cos_t, sin_t = _rope_tables()
    q_tiles = DEC_SEQ // ATT_TQ
    q0 = N_CTX // ATT_TQ
    b0 = N_CTX // DEC_SEQ
    qblk = lambda c: pl.BlockSpec((ATT_TQ, BRANCH), lambda b, i: (q0 + b * q_tiles + i, c))
    kblk = lambda c: pl.BlockSpec((DEC_SEQ, BRANCH), lambda b, i: (b0 + b, c))
    cblk = pl.BlockSpec((None, None, PAST_LEN, BRANCH), lambda b, i: (b, layer, 0, 0))
    return pl.pallas_call(
        functools.partial(_attn_lat_kernel, lam_init),
        grid=(DEC_BATCH, q_tiles),
        in_specs=[
            qblk(C_Q), kblk(C_K), kblk(C_V), qblk(C_CG), cblk, cblk,
            pl.BlockSpec((ATT_TQ, HEAD_DIM_C), lambda b, i: (i, 0)),
            pl.BlockSpec((ATT_TQ, HEAD_DIM_C), lambda b, i: (i, 0)),
            pl.BlockSpec((DEC_SEQ, HEAD_DIM_C), lambda b, i: (0, 0)),
            pl.BlockSpec((DEC_SEQ, HEAD_DIM_C), lambda b, i: (0, 0)),
            pl.BlockSpec((None, 4, QK_HALF), lambda b, i: (layer, 0, 0)),
            pl.BlockSpec((None, 1, HEAD_DIM_C), lambda b, i: (layer, 0, 0)),
            pl.BlockSpec(memory_space=pl.ANY),
        ],
        out_specs=pl.BlockSpec((ATT_TQ, BRANCH), lambda b, i: (q0 + b * q_tiles + i, M_ATTN)),
        out_shape=jax.ShapeDtypeStruct((N_TOK, D_MIX), BF16),
        input_output_aliases={12: 0},
        scratch_shapes=[pltpu.VMEM((PAST_LEN, BRANCH), BF16), pltpu.VMEM((DEC_SEQ, BRANCH), BF16),
                        pltpu.VMEM((N_HEADS_C, V_ROWS, PAST_LEN + DEC_SEQ), BF16)],
        compiler_params=_params("arbitrary", "arbitrary"),
        name="mixer_attn_lat",
    )(p, p, p, p, cache_k4, cache_v4, cos_t, sin_t, cos_t, sin_t, lambda_qk, subln_row, ymix)


HY_ROWS = 1024


def _dft_matrices(seq_len):
    n = 2 * seq_len
    f = np.arange(seq_len, dtype=np.float64)[:, None]
    s = np.arange(seq_len, dtype=np.float64)[None, :]
    theta = 2.0 * np.pi * f * s / n
    alt = np.where(np.arange(seq_len) % 2 == 0, 1.0, -1.0)
    ac = np.cos(theta)
    as_ = -np.sin(theta)
    as_[0, :] = alt
    bc = 2.0 * np.cos(theta.T) / n
    bc[:, 0] = 1.0 / n
    bs = -2.0 * np.sin(theta.T) / n
    bs[:, 0] = alt / n
    fwd = np.concatenate([ac, as_], axis=0)
    inv = np.concatenate([bc, bs], axis=1)
    return jnp.asarray(fwd, F32).astype(BF16), jnp.asarray(inv, F32).astype(BF16)


def _filter_features(seq_len):
    t_idx = np.arange(seq_len, dtype=np.float64)
    t_norm = np.linspace(0.0, 1.0, seq_len)
    bands = np.linspace(1e-4, HY_BANDS - 1, HY_BANDS)
    ang = (2.0 * math.pi * t_idx / seq_len)[:, None] * bands[None, :]
    feats = np.concatenate([t_norm[:, None], np.cos(ang), np.sin(ang)], axis=-1)
    feats = np.pad(feats, ((0, 0), (0, HY_PAD - HY_EMB)))
    deltas = np.abs(np.linspace(math.log(HY_TARGET) / HY_FAST_DECAY,
                                math.log(HY_TARGET) / HY_SLOW_DECAY, BRANCH))
    return (jnp.asarray(feats, F32), jnp.asarray(t_norm[:, None], F32),
            jnp.asarray(deltas[None, :], F32))


def _filter_kernel(seq_len, feats_ref, tn_ref, dl_ref, w1_ref, b1_ref, w2_ref, b2_ref, fr_ref,
                   w3f_ref, w3b_ref, fwd_ref, kr_ref, ki_ref, h_ref):
    sp = lambda x: _split(x, 2)

    @pl.when(pl.program_id(1) == 0)
    def _():
        fr = fr_ref[...]
        h = jnp.sin(fr * (_sdot(sp(feats_ref[...]), sp(w1_ref[...])) + b1_ref[...]))
        h_ref[...] = jnp.sin(fr * (_sdot(sp(h), sp(w2_ref[...])) + b2_ref[...]))

    h = sp(h_ref[...])
    decay = jnp.exp(-tn_ref[...] * dl_ref[...])
    row = lax.broadcasted_iota(jnp.int32, (seq_len, BRANCH), 0)
    fwd = _sdot(h, sp(w3f_ref[...])) * decay
    bwd = jnp.where(row == 0, 0.0, _sdot(h, sp(w3b_ref[...])) * decay)
    norm = (jnp.sum(jnp.abs(fwd), axis=0, keepdims=True)
            + jnp.sum(jnp.abs(bwd), axis=0, keepdims=True))
    fwd = fwd / norm
    bwd = bwd / norm
    even = fwd + bwd
    alt = jnp.where((row & 1) == 0, 1.0, -1.0)
    nyquist = jnp.sum(alt * even, axis=0, keepdims=True)
    kr_ref[...] = jnp.dot(fwd_ref[0:seq_len, :], even.astype(BF16), preferred_element_type=F32)
    ki = jnp.dot(fwd_ref[seq_len:2 * seq_len, :], (fwd - bwd).astype(BF16),
                 preferred_element_type=F32)
    ki_ref[...] = jnp.where(row == 0, nyquist, ki)


def _hyena_spectrum(seq_len, filt, fwd_mat):
    feats, t_norm, deltas = _filter_features(seq_len)
    w1, b1, w2, b2, freq, w3 = filt
    full = lambda a: pl.BlockSpec(a.shape, lambda l, o: (0,) * a.ndim)
    lyr = lambda r, n: pl.BlockSpec((None, r, n), lambda l, o: (l, 0, 0))
    out = pl.BlockSpec((None, None, seq_len, BRANCH), lambda l, o: (l, o, 0, 0))
    return pl.pallas_call(
        functools.partial(_filter_kernel, seq_len),
        grid=(DEPTH, HY_ORDER),
        in_specs=[
            full(feats), full(t_norm), full(deltas),
            lyr(HY_PAD, HY_PAD), lyr(1, HY_PAD), lyr(HY_PAD, HY_PAD), lyr(1, HY_PAD), lyr(1, HY_PAD),
            pl.BlockSpec((None, HY_PAD, BRANCH), lambda l, o: (l, 0, 2 * o)),
            pl.BlockSpec((None, HY_PAD, BRANCH), lambda l, o: (l, 0, 2 * o + 1)),
            full(fwd_mat),
        ],
        out_specs=[out, out],
        out_shape=[jax.ShapeDtypeStruct((DEPTH, HY_ORDER, seq_len, BRANCH), F32)] * 2,
        scratch_shapes=[pltpu.VMEM((seq_len, HY_PAD), F32)],
        compiler_params=_params("arbitrary", "arbitrary"),
        name=f"hyena_spectrum_{seq_len}",
    )(feats, t_norm, deltas, w1, b1, w2, b2, freq, w3, w3, fwd_mat)


def _hyena_chains(seq_len, width, x1_ref, x2_ref, hv_ref, g_ref, cw_ref, cb_ref, kr_ref, ki_ref,
                  hb_ref, fwd_ref, inv_ref, o_ref):
    row = lax.broadcasted_iota(jnp.int32, (seq_len, width), 0)
    first, last = row == 0, row == seq_len - 1

    def chain(rs, c0):
        cs = slice(c0, c0 + width)

        def short_conv(x_ref, piece):
            x = x_ref[rs, cs]
            w = cw_ref[:, piece * BRANCH + c0:piece * BRANCH + c0 + width]
            b = cb_ref[:, piece * BRANCH + c0:piece * BRANCH + c0 + width]
            prev = jnp.where(first, 0.0, pltpu.roll(x, 1, axis=0))
            nxt = jnp.where(last, 0.0, pltpu.roll(x, seq_len - 1, axis=0))
            return prev * w[0:1] + x * w[1:2] + nxt * w[2:3] + b

        z = short_conv(hv_ref, 2)
        gates = (short_conv(x1_ref, 0), short_conv(x2_ref, 1))
        yield
        for order in range(HY_ORDER):
            zf = jnp.dot(fwd_ref[...], z.astype(BF16), preferred_element_type=F32)
            yield
            zr, zi = zf[:seq_len], zf[seq_len:]
            kr, kp = kr_ref[order, :, cs], ki_ref[order, :, cs]
            ki = jnp.where(first, 0.0, kp)
            kn = jnp.where(first, kp, kr)
            yf = jnp.concatenate([(zr * kr - zi * ki).astype(BF16),
                                  (zr * ki + zi * kn).astype(BF16)], axis=0)
            yield
            y = jnp.dot(inv_ref[...], yf, preferred_element_type=F32)
            yield
            z = gates[order] * (y + z * hb_ref[order:order + 1, cs])
        o_ref[rs, cs] = (_silu(g_ref[rs, cs]) * z).astype(BF16)

    waiting = [chain(slice(s * seq_len, (s + 1) * seq_len), c0)
               for s in range(HY_ROWS // seq_len) for c0 in range(0, BRANCH, width)]
    running = []
    while waiting or running:
        if waiting:
            running.append(waiting.pop(0))
        for gen in list(running):
            if next(gen, "done") == "done":
                running.remove(gen)


def _hyena_kernel(x1_ref, x2_ref, hv_ref, g_ref, cw_ref, cb_ref, hb_ref,
                  krc_ref, kic_ref, fwdc_ref, invc_ref, krl_ref, kil_ref, fwdl_ref, invl_ref,
                  ymix_ref, o_ref):
    del ymix_ref
    common = (x1_ref, x2_ref, hv_ref, g_ref, cw_ref, cb_ref)
    is_ctx = pl.program_id(0) < N_CTX // HY_ROWS

    @pl.when(is_ctx)
    def _():
        _hyena_chains(SEQ, BRANCH, *common, krc_ref, kic_ref, hb_ref, fwdc_ref, invc_ref, o_ref)

    @pl.when(jnp.logical_not(is_ctx))
    def _():
        _hyena_chains(DEC_SEQ, HY_CT, *common, krl_ref, kil_ref, hb_ref, fwdl_ref, invl_ref, o_ref)


def _mixer_hyena(p, conv_w, conv_b3, hyena_bias, spec_ctx, mats_ctx, spec_lat, mats_lat, layer,
                 ymix):
    blk = lambda c: pl.BlockSpec((HY_ROWS, BRANCH), lambda i: (i, c))
    once = pl.Buffered(1)
    lyr = lambda a: pl.BlockSpec((None,) + a.shape[1:], lambda i: (layer,) + (0,) * (a.ndim - 1),
                                 pipeline_mode=once)
    full = lambda a: pl.BlockSpec(a.shape, lambda i: (0,) * a.ndim, pipeline_mode=once)
    consts = [*spec_ctx, *mats_ctx, *spec_lat, *mats_lat]
    return pl.pallas_call(
        _hyena_kernel,
        grid=(N_TOK // HY_ROWS,),
        in_specs=[blk(C_DX1), blk(C_DX2), blk(C_DV), blk(C_DG),
                  lyr(conv_w), lyr(conv_b3), lyr(hyena_bias),
                  lyr(spec_ctx[0]), lyr(spec_ctx[1]), full(mats_ctx[0]), full(mats_ctx[1]),
                  lyr(spec_lat[0]), lyr(spec_lat[1]), full(mats_lat[0]), full(mats_lat[1]),
                  pl.BlockSpec(memory_space=pl.ANY)],
        out_specs=pl.BlockSpec((HY_ROWS, BRANCH), lambda i: (i, M_HYENA)),
        out_shape=jax.ShapeDtypeStruct((N_TOK, D_MIX), BF16),
        input_output_aliases={7 + len(consts): 0},
        compiler_params=_params("arbitrary"),
        name="mixer_hyena",
    )(p, p, p, p, conv_w, conv_b3, hyena_bias, *consts, ymix)


OUT_TM = 512


def _outproj_kernel(alpha, n_x, n_out, ymix_ref, *refs):
    x_refs = refs[:n_x]
    gate_ref, w_ref, b_ref, lng_ref, lnb_ref = refs[n_x:n_x + 5]
    o_refs = refs[n_x + 5:]
    is_ctx = pl.program_id(0) < N_CTX // OUT_TM
    x = x_refs[0][...] if n_x == 1 else jnp.where(is_ctx, x_refs[0][...], x_refs[1][...])
    y = jnp.dot(ymix_ref[...], w_ref[...], preferred_element_type=F32) + b_ref[...]
    r = alpha * x + gate_ref[...] * y
    res = _layer_norm(r) * lng_ref[...] + lnb_ref[...]
    if n_out == 1:
        o_refs[0][...] = res
    else:
        @pl.when(is_ctx)
        def _():
            o_refs[0][...] = res

        @pl.when(jnp.logical_not(is_ctx))
        def _():
            o_refs[1][...] = res


def _out_projection(ymix, xs, mod4, w_out_bf, b_out3, ln_g3, ln_b3, layer, split_out):
    alpha = (2.0 * DEPTH) ** 0.25
    row = lambda i: _cond_row(i, OUT_TM)
    vec = pl.BlockSpec((None, 1, D_MODEL), lambda i: (layer, 0, 0))
    if split_out:
        out_shape = [jax.ShapeDtypeStruct((N_CTX, D_MODEL), F32),
                     jax.ShapeDtypeStruct((N_LAT, D_MODEL), F32)]
    else:
        out_shape = [jax.ShapeDtypeStruct((N_TOK, D_MODEL), F32)]
    return pl.pallas_call(
        functools.partial(_outproj_kernel, alpha, len(xs), len(out_shape)),
        grid=(N_TOK // OUT_TM,),
        in_specs=[pl.BlockSpec((OUT_TM, D_MIX), lambda i: (i, 0))] + _token_specs(xs, OUT_TM) + [
            pl.BlockSpec((None, None, 1, D_MODEL), lambda i: (layer, row(i), 0, 2)),
            pl.BlockSpec((None, D_MODEL, D_MODEL), lambda i: (layer, 0, 0),
                         pipeline_mode=pl.Buffered(1)),
            vec, vec, vec,
        ],
        out_specs=_token_specs(out_shape, OUT_TM),
        out_shape=out_shape,
        compiler_params=_params("arbitrary"),
        name="out_projection",
    )(ymix, *xs, mod4, w_out_bf, b_out3, ln_g3, ln_b3)


def kernel(x_prompt, x_sample, cache_k, cache_v, c, c_ctx, w_mod, b_mod, w_in, gmlp_w, gmlp_b,
           pool_w, pool_scale, lambda_qk, subln_w, conv_w, conv_b, filt_w1, filt_b1, filt_w2,
           filt_b2, filt_freq, filt_w3, hyena_bias, w_out, b_out, ln_g, ln_b):
    xs = (x_prompt.reshape(N_CTX, D_MODEL), x_sample.reshape(N_LAT, D_MODEL))
    cond = jnp.concatenate(
        [c_ctx[None, :], c, jnp.zeros((N_COND - 1 - DEC_BATCH, D_MODEL), F32)], axis=0)
    mod4 = _modulation(cond, w_mod, b_mod).reshape(DEPTH, N_COND, 1, 3 * D_MODEL)

    w_out_bf = w_out.astype(BF16)
    cache_k4 = cache_k.reshape(DEC_BATCH, DEPTH, PAST_LEN, BRANCH)
    cache_v4 = cache_v.reshape(DEC_BATCH, DEPTH, PAST_LEN, BRANCH)
    gmlp_b_rows = jnp.broadcast_to(gmlp_b[..., None], (DEPTH, N_GROUPS, CHUNK, GROUP_W))
    subln_row = subln_w.reshape(DEPTH, 1, HEAD_DIM_C)
    conv_b3 = conv_b.reshape(DEPTH, 1, 3 * BRANCH)
    b_out3 = b_out.reshape(DEPTH, 1, D_MODEL)
    ln_g3 = ln_g.reshape(DEPTH, 1, D_MODEL)
    ln_b3 = ln_b.reshape(DEPTH, 1, D_MODEL)

    pad_h = HY_PAD - HY_HIDDEN
    filt = (
        jnp.pad(filt_w1, ((0, 0), (0, HY_PAD - HY_EMB), (0, pad_h))),
        jnp.pad(filt_b1, ((0, 0), (0, pad_h))).reshape(DEPTH, 1, HY_PAD),
        jnp.pad(filt_w2, ((0, 0), (0, pad_h), (0, pad_h))),
        jnp.pad(filt_b2, ((0, 0), (0, pad_h))).reshape(DEPTH, 1, HY_PAD),
        jnp.pad(filt_freq, ((0, 0), (0, pad_h))).reshape(DEPTH, 1, HY_PAD),
        jnp.pad(filt_w3, ((0, 0), (0, pad_h), (0, 0))),
    )
    mats_ctx = _dft_matrices(SEQ)
    mats_lat = _dft_matrices(DEC_SEQ)
    spec_ctx = _hyena_spectrum(SEQ, filt, mats_ctx[0])
    spec_lat = _hyena_spectrum(DEC_SEQ, filt, mats_lat[0])

    caches = None
    for layer in range(DEPTH):
        lam_init = 0.8 - 0.6 * math.exp(-0.3 * layer)
        p = _in_projection(xs, mod4, w_in, layer)
        ymix = _mixer_gmlp_pool(p, gmlp_w[layer], gmlp_b_rows[layer], pool_w[layer],
                                pool_scale[layer].reshape(1, BRANCH))
        ymix, new_k, new_v = _mixer_attn_ctx(p, lambda_qk, subln_row, layer, lam_init, ymix, caches)
        caches = (new_k, new_v)
        ymix = _mixer_attn_lat(p, cache_k4, cache_v4, lambda_qk, subln_row, layer, lam_init, ymix)
        ymix = _mixer_hyena(p, conv_w, conv_b3, hyena_bias, spec_ctx, mats_ctx, spec_lat, mats_lat,
                            layer, ymix)
        xs = tuple(_out_projection(ymix, xs, mod4, w_out_bf, b_out3, ln_g3, ln_b3, layer,
                                   split_out=layer == DEPTH - 1))

    y_prompt = xs[0].reshape(BATCH, SEQ, D_MODEL)
    y_sample = xs[1].reshape(DEC_BATCH, DEC_SEQ, D_MODEL)
    new_k, new_v = caches
    return (y_prompt, y_sample,
            new_k.reshape(BATCH, DEPTH, SEQ, N_HEADS_C, 2, QK_HALF),
            new_v.reshape(BATCH, DEPTH, SEQ, N_HEADS_C, HEAD_DIM_C))
```

```python
import functools
import math

import numpy as np
import jax
import jax.numpy as jnp
from jax import lax
from jax.experimental import pallas as pl
from jax.experimental.pallas import tpu as pltpu

F32 = jnp.float32
BF16 = jnp.bfloat16

D_MODEL = 2048
BATCH = 16
SEQ = 256
DEPTH = 2
DEC_BATCH = 4
DEC_SEQ = 1024
PAST_LEN = 512
GRID_W = 64
BRANCH = 512
N_GROUPS = 4
GROUP_W = 128
CHUNK = 128
POOL_WINDOWS = (2, 4, 8, 16)
N_HEADS_C = 4
HEAD_DIM_C = 128
QK_HALF = 64
ROPE_AXIS_DIM = 32
ROPE_BASE = 10000.0
HY_BANDS = 16
HY_EMB = 33
HY_HIDDEN = 64
HY_ORDER = 2
HY_FAST_DECAY = 0.3
HY_SLOW_DECAY = 1.5
HY_TARGET = 1e-2
N_IN_PIECES = 13
D_IN = N_IN_PIECES * BRANCH
LN_EPS = 1e-6

N_CTX = BATCH * SEQ
N_LAT = DEC_BATCH * DEC_SEQ
N_TOK = N_CTX + N_LAT
N_COND = 8
LANES = 128
HY_PAD = LANES
HY_CT = 256
VMEM_LIMIT = 56 * 1024 * 1024

(C_AU, C_AV, C_AG, C_BX, C_BG, C_Q, C_K, C_V, C_CG, C_DX1, C_DX2, C_DV, C_DG) = range(13)
D_MIX = 4 * BRANCH
(M_GMLP, M_POOL, M_ATTN, M_HYENA) = range(4)


def _silu(x):
    return x * jax.nn.sigmoid(x)


def _bdot(a, b):
    return jnp.dot(a.astype(BF16), b.astype(BF16), preferred_element_type=F32)


def _split(x, n_terms):
    hi = x.astype(BF16)
    if n_terms == 1:
        return (hi,)
    return (hi, (x - hi.astype(F32)).astype(BF16))


def _sdot(a_terms, b_terms):
    acc = jnp.dot(a_terms[0], b_terms[0], preferred_element_type=F32)
    if len(a_terms) > 1:
        acc = acc + jnp.dot(a_terms[1], b_terms[0], preferred_element_type=F32)
    if len(b_terms) > 1:
        acc = acc + jnp.dot(a_terms[0], b_terms[1], preferred_element_type=F32)
    return acc


def _layer_norm(x):
    mu = jnp.mean(x, axis=-1, keepdims=True)
    xc = x - mu
    var = jnp.mean(xc * xc, axis=-1, keepdims=True)
    return xc * lax.rsqrt(var + LN_EPS)


def _cond_row(tile, rows_per_tile):
    n_ctx_tiles = N_CTX // rows_per_tile
    tiles_per_batch = DEC_SEQ // rows_per_tile
    return jnp.where(tile < n_ctx_tiles, 0, 1 + (tile - n_ctx_tiles) // tiles_per_batch)


def _params(*semantics):
    return pltpu.CompilerParams(dimension_semantics=semantics, vmem_limit_bytes=VMEM_LIMIT)


MOD_TN = 512


def _mod_kernel(c_ref, w_ref, b_ref, o_ref):
    o_ref[...] = _bdot(_silu(c_ref[...]), w_ref[...]) + b_ref[...]


def _modulation(cond, w_mod, b_mod):
    n = 3 * D_MODEL
    return pl.pallas_call(
        _mod_kernel,
        grid=(DEPTH, n // MOD_TN),
        in_specs=[
            pl.BlockSpec((N_COND, D_MODEL), lambda l, j: (0, 0)),
            pl.BlockSpec((None, D_MODEL, MOD_TN), lambda l, j: (l, 0, j)),
            pl.BlockSpec((None, 1, MOD_TN), lambda l, j: (l, 0, j)),
        ],
        out_specs=pl.BlockSpec((None, N_COND, MOD_TN), lambda l, j: (l, 0, j)),
        out_shape=jax.ShapeDtypeStruct((DEPTH, N_COND, n), F32),
        compiler_params=_params("arbitrary", "arbitrary"),
        name="modulation",
    )(cond, w_mod, b_mod.reshape(DEPTH, 1, n))


IN_TM = 1024
IN_TN = 512
IN_LN_ROWS = 256


def _token_specs(xs, tm, tile_of=lambda i: i):
    n_ctx_tiles = N_CTX // tm
    if len(xs) == 1:
        maps = [lambda i, *_: (tile_of(i), 0)]
    else:
        maps = [lambda i, *_: (jnp.minimum(tile_of(i), n_ctx_tiles - 1), 0),
                lambda i, *_: (jnp.maximum(tile_of(i) - n_ctx_tiles, 0), 0)]
    return [pl.BlockSpec((tm, D_MODEL), m) for m in maps]


def _inproj_kernel(n_x, tile0, cast_w, *refs):
    x_refs = refs[:n_x]
    scale_ref, shift_ref, w_ref = refs[n_x:n_x + 3]
    if cast_w:
        o_ref, wb_ref, h_ref = refs[n_x + 3:]
    else:
        _, o_ref, h_ref = refs[n_x + 3:]

    def fill_h(x_ref):
        def body(r, carry):
            rows = pl.ds(pl.multiple_of(r * IN_LN_ROWS, IN_LN_ROWS), IN_LN_ROWS)
            h = _layer_norm(x_ref[rows, :]) * (1.0 + scale_ref[...]) + shift_ref[...]
            h_ref[rows, :] = h.astype(BF16)
            return carry
        lax.fori_loop(0, IN_TM // IN_LN_ROWS, body, 0)

    first = pl.program_id(1) == 0
    if n_x == 1:
        pl.when(first)(lambda: fill_h(x_refs[0]))
    else:
        is_ctx = pl.program_id(0) + tile0 < N_CTX // IN_TM
        pl.when(jnp.logical_and(first, is_ctx))(lambda: fill_h(x_refs[0]))
        pl.when(jnp.logical_and(first, jnp.logical_not(is_ctx)))(lambda: fill_h(x_refs[1]))

    if cast_w:
        w = w_ref[...].astype(BF16)
        wb_ref[...] = w
    else:
        w = w_ref[...]
    o_ref[...] = jnp.dot(h_ref[...], w, preferred_element_type=F32)


def _in_projection(xs, mod4, w_in, layer):
    n_col = D_IN // IN_TN

    def mod_spec(tile0, piece):
        return pl.BlockSpec((None, None, 1, D_MODEL),
                            lambda i, j: (layer, _cond_row(i + tile0, IN_TM), 0, piece))

    scratch = [pltpu.VMEM((IN_TM, D_MODEL), BF16)]
    p_shape = jax.ShapeDtypeStruct((N_TOK, D_IN), F32)
    p, w_bf = pl.pallas_call(
        functools.partial(_inproj_kernel, 1, 0, True),
        grid=(1, n_col),
        in_specs=_token_specs(xs[:1], IN_TM) + [
            mod_spec(0, 1), mod_spec(0, 0),
            pl.BlockSpec((None, D_MODEL, IN_TN), lambda i, j: (layer, 0, j)),
        ],
        out_specs=[pl.BlockSpec((IN_TM, IN_TN), lambda i, j: (i, j)),
                   pl.BlockSpec((D_MODEL, IN_TN), lambda i, j: (0, j))],
        out_shape=[p_shape, jax.ShapeDtypeStruct((D_MODEL, D_IN), BF16)],
        scratch_shapes=scratch,
        compiler_params=_params("arbitrary", "arbitrary"),
        name="in_projection_first",
    )(xs[0], mod4, mod4, w_in)
    return pl.pallas_call(
        functools.partial(_inproj_kernel, len(xs), 1, False),
        grid=(N_TOK // IN_TM - 1, n_col),
        in_specs=_token_specs(xs, IN_TM, lambda i: i + 1) + [
            mod_spec(1, 1), mod_spec(1, 0),
            pl.BlockSpec((D_MODEL, IN_TN), lambda i, j: (0, j)),
            pl.BlockSpec(memory_space=pl.ANY),
        ],
        out_specs=pl.BlockSpec((IN_TM, IN_TN), lambda i, j: (i + 1, j)),
        out_shape=p_shape,
        input_output_aliases={len(xs) + 3: 0},
        scratch_shapes=scratch,
        compiler_params=_params("arbitrary", "arbitrary"),
        name="in_projection",
    )(*xs, mod4, mod4, w_bf, p)


AB_TM = 1024


def _gmlp_tasks(u_ref, v_ref, g_ref, w_ref, b_ref, o_ref):
    ws = [w_ref[g].astype(BF16) for g in range(N_GROUPS)]

    def chunk(r):
        rows = slice(r, r + CHUNK)
        vn = _layer_norm(v_ref[rows, :]).astype(BF16)
        for g in range(N_GROUPS):
            cols = slice(g * GROUP_W, (g + 1) * GROUP_W)
            mixed = jnp.dot(ws[g], vn[:, cols], preferred_element_type=F32) + b_ref[g]
            o_ref[rows, cols] = (_silu(g_ref[rows, cols]) * u_ref[rows, cols] * mixed).astype(BF16)

    return [functools.partial(chunk, r) for r in range(0, AB_TM, CHUNK)]


POOL_BAND_GROUPS = (2, 3)
POOL_BAND_ROWS = 256
POOL_BAND_HALO = 128


def _window_sum_shifts(p, win, pos, seq_len):
    acc = p
    for d in range(-(win // 2), win // 2):
        if d == 0:
            continue
        shifted = pltpu.roll(p, (-d) % AB_TM, axis=0)
        valid = (pos >= -d) if d < 0 else (pos < seq_len - d)
        acc = acc + jnp.where(valid, shifted, 0.0)
    return acc


def _window_sum_band_tasks(p, band_ref, pieces):
    hi = p.astype(BF16)
    r1 = p - hi.astype(F32)
    mid = r1.astype(BF16)
    lo = (r1 - mid.astype(F32)).astype(BF16)
    terms = jnp.concatenate([hi, mid, lo], axis=1)

    def piece(r):
        k0, k1 = max(0, r - POOL_BAND_HALO), min(AB_TM, r + POOL_BAND_ROWS + POOL_BAND_HALO)
        s = jnp.dot(band_ref[r:r + POOL_BAND_ROWS, k0:k1], terms[k0:k1, :],
                    preferred_element_type=F32)
        pieces.append(s[:, :GROUP_W] + s[:, GROUP_W:2 * GROUP_W] + s[:, 2 * GROUP_W:])

    return [functools.partial(piece, r) for r in range(0, AB_TM, POOL_BAND_ROWS)]


def _pool_finish(g, acc, x_ref, g_ref, w_ref, s_ref, pos, seq_len, o_ref):
    cols = slice(g * GROUP_W, (g + 1) * GROUP_W)
    win = POOL_WINDOWS[g]
    count = jnp.minimum(pos + win // 2, seq_len) - jnp.maximum(pos - win // 2, 0)
    pooled = acc / count.astype(F32)
    y = _bdot(pooled - x_ref[:, cols], w_ref[g])
    o_ref[:, BRANCH + g * GROUP_W:BRANCH + (g + 1) * GROUP_W] = (
        _silu(g_ref[:, cols]) * (y * s_ref[:, cols])).astype(BF16)


def _mixer_ab_kernel(au_ref, av_ref, ag_ref, bx_ref, bg_ref, gw_ref, gb_ref, pw_ref, ps_ref,
                     band_a_ref, band_b_ref, wo_ref, o_ref, wo_bf_ref):
    wo_bf_ref[...] = wo_ref[...].astype(BF16)
    seq_len = jnp.where(pl.program_id(0) < N_CTX // AB_TM, SEQ, DEC_SEQ)
    pos = lax.broadcasted_iota(jnp.int32, (AB_TM, GROUP_W), 0) & (seq_len - 1)
    pool_refs = (bx_ref, bg_ref, pw_ref, ps_ref, pos, seq_len, o_ref)
    band_sums = {g: [] for g in POOL_BAND_GROUPS}
    mxu_tasks = [t for g, band in zip(POOL_BAND_GROUPS, (band_a_ref, band_b_ref))
                 for t in _window_sum_band_tasks(bx_ref[:, g * GROUP_W:(g + 1) * GROUP_W], band,
                                                 band_sums[g])]
    vpu_tasks = _gmlp_tasks(au_ref, av_ref, ag_ref, gw_ref, gb_ref, o_ref)
    for i in range(max(len(mxu_tasks), len(vpu_tasks))):
        for tasks in (mxu_tasks, vpu_tasks):
            if i < len(tasks):
                tasks[i]()
    for g, win in enumerate(POOL_WINDOWS):
        if g not in POOL_BAND_GROUPS:
            acc = _window_sum_shifts(bx_ref[:, g * GROUP_W:(g + 1) * GROUP_W], win, pos, seq_len)
            _pool_finish(g, acc, *pool_refs)
    for g in POOL_BAND_GROUPS:
        _pool_finish(g, jnp.concatenate(band_sums[g], axis=0), *pool_refs)


def _pool_bands():
    t = np.arange(AB_TM)
    bands = []
    for g in POOL_BAND_GROUPS:
        half = POOL_WINDOWS[g] // 2
        d = t[None, :] - t[:, None]
        near = (d >= -half) & (d < half)
        per_path = [near & ((t[None, :] // L) == (t[:, None] // L)) for L in (SEQ, DEC_SEQ)]
        bands.append(jnp.asarray(np.stack(per_path), F32).astype(BF16))
    return bands


def _mixer_gmlp_pool(p, gmlp_w, gmlp_b_rows, pool_w, pool_scale_row, w_out, layer):
    assert M_POOL == M_GMLP + 1
    n_steps = N_TOK // AB_TM
    wo_rows = D_MIX // n_steps
    blk = lambda c: pl.BlockSpec((AB_TM, BRANCH), lambda i: (i, c))
    full = lambda a: pl.BlockSpec(a.shape, lambda i: (0,) * a.ndim)
    band = pl.BlockSpec((None, AB_TM, AB_TM), lambda i: (jnp.where(i < N_CTX // AB_TM, 0, 1), 0, 0))
    return pl.pallas_call(
        _mixer_ab_kernel,
        grid=(n_steps,),
        in_specs=[blk(C_AU), blk(C_AV), blk(C_AG), blk(C_BX), blk(C_BG),
                  full(gmlp_w), full(gmlp_b_rows), full(pool_w), full(pool_scale_row), band, band,
                  pl.BlockSpec((None, wo_rows, D_MODEL), lambda i: (layer, i, 0))],
        out_specs=[pl.BlockSpec((AB_TM, 2 * BRANCH), lambda i: (i, M_GMLP // 2)),
                   pl.BlockSpec((wo_rows, D_MODEL), lambda i: (i, 0))],
        out_shape=[jax.ShapeDtypeStruct((N_TOK, D_MIX), BF16),
                   jax.ShapeDtypeStruct((D_MIX, D_MODEL), BF16)],
        compiler_params=_params("arbitrary"),
        name="mixer_gmlp_pool",
    )(p, p, p, p, p, gmlp_w, gmlp_b_rows, pool_w, pool_scale_row, *_pool_bands(), w_out)


ATT_TQ = 256
V_ROWS = HEAD_DIM_C + 16
NT_DIMS = (((1,), (1,)), ((), ()))


def _lambda(lam_ref, lam_init):
    lq = lam_ref[...]
    a = jnp.sum(lq[0:1] * lq[1:2], axis=-1, keepdims=True)
    b = jnp.sum(lq[2:3] * lq[3:4], axis=-1, keepdims=True)
    return jnp.exp(a) - jnp.exp(b) + lam_init


def _map_masks():
    lane = lax.broadcasted_iota(jnp.int32, (1, HEAD_DIM_C), 1)
    m0 = (lane < QK_HALF).astype(F32)
    return m0, 1.0 - m0


def _scores_t(q, keys):
    return [lax.dot_general(k, q, NT_DIMS, preferred_element_type=F32) for k in keys]


def _softmax_v_t(s, vals_t):
    m = functools.reduce(jnp.maximum, [jnp.max(x, axis=0, keepdims=True) for x in s])
    acc = functools.reduce(
        jnp.add, [jnp.dot(v, jnp.exp2(x - m).astype(BF16), preferred_element_type=F32)
                  for x, v in zip(s, vals_t)])
    return acc[:HEAD_DIM_C] * (1.0 / acc[HEAD_DIM_C:HEAD_DIM_C + 1])


def _with_ones_rows(v_t):
    ones = jnp.ones((V_ROWS - HEAD_DIM_C, v_t.shape[1]), BF16)
    return jnp.concatenate([v_t.astype(BF16), ones], axis=0)


def _diff_attention(q_of, keys_of, vals_t_of, gate_of, store, lam, lam_init, subln):
    masks = _map_masks()
    tasks = [(h, mp) for h in range(N_HEADS_C) for mp in range(2)]
    per_head = {}

    def operands(h):
        if h not in per_head:
            q = q_of(h) * (QK_HALF ** -0.5 * math.log2(math.e))
            per_head[h] = (q, keys_of(h), vals_t_of(h))
        return per_head[h]

    def scores(h, mp):
        q, keys, _ = operands(h)
        return _scores_t((q * masks[mp]).astype(BF16), keys)

    nxt = scores(*tasks[0])
    outs = []
    for i, (h, mp) in enumerate(tasks):
        cur = nxt
        if i + 1 < len(tasks):
            nxt = scores(*tasks[i + 1])
        outs.append(_softmax_v_t(cur, operands(h)[2]))
        if mp == 1:
            o = (outs[0] - lam * outs[1]).T
            outs = []
            o = o * lax.rsqrt(jnp.mean(o * o, axis=-1, keepdims=True) + 1e-5)
            o = o * subln * (1.0 - lam_init)
            store(h, _silu(gate_of(h)) * o)


def _head_cols(h):
    return slice(h * HEAD_DIM_C, (h + 1) * HEAD_DIM_C)


def _attn_ctx_kernel(lam_init, q_ref, k_ref, v_ref, g_ref, lam_ref, sw_ref, *rest):
    o_ref, ko_ref, vo_ref = rest[-3:]
    ko_ref[...] = k_ref[...]
    vo_ref[...] = v_ref[...]

    def store(h, y):
        o_ref[:, _head_cols(h)] = y.astype(BF16)

    _diff_attention(
        lambda h: q_ref[:, _head_cols(h)],
        lambda h: [k_ref[:, _head_cols(h)].astype(BF16)],
        lambda h: [_with_ones_rows(v_ref[:, _head_cols(h)].T)],
        lambda h: g_ref[:, _head_cols(h)],
        store, _lambda(lam_ref, lam_init), lam_init, sw_ref[...])


def _mixer_attn_ctx(p, lambda_qk, subln_row, layer, lam_init, ymix, caches):
    blk = lambda c: pl.BlockSpec((SEQ, BRANCH), lambda b: (b, c))
    any_spec = pl.BlockSpec(memory_space=pl.ANY)
    cache_spec = pl.BlockSpec((None, None, SEQ, BRANCH), lambda b: (b, layer, 0, 0))
    cache_shape = jax.ShapeDtypeStruct((BATCH, DEPTH, SEQ, BRANCH), F32)
    in_specs = [
        blk(C_Q), blk(C_K), blk(C_V), blk(C_CG),
        pl.BlockSpec((None, 4, QK_HALF), lambda b: (layer, 0, 0)),
        pl.BlockSpec((None, 1, HEAD_DIM_C), lambda b: (layer, 0, 0)),
        any_spec,
    ]
    args = [p, p, p, p, lambda_qk, subln_row, ymix]
    aliases = {6: 0}
    if caches is not None:
        in_specs += [any_spec, any_spec]
        args += list(caches)
        aliases.update({7: 1, 8: 2})
    return pl.pallas_call(
        functools.partial(_attn_ctx_kernel, lam_init),
        grid=(BATCH,),
        in_specs=in_specs,
        out_specs=[pl.BlockSpec((SEQ, BRANCH), lambda b: (b, M_ATTN)), cache_spec, cache_spec],
        out_shape=[jax.ShapeDtypeStruct((N_TOK, D_MIX), BF16), cache_shape, cache_shape],
        input_output_aliases=aliases,
        compiler_params=_params("arbitrary"),
        name="mixer_attn_ctx",
    )(*args)


def _rope(x, cos, sin_signed):
    lane = lax.broadcasted_iota(jnp.int32, x.shape, 1)
    first_half = (lane & (ROPE_AXIS_DIM - 1)) < (ROPE_AXIS_DIM // 2)
    half = ROPE_AXIS_DIM // 2
    partner = jnp.where(first_half,
                        pltpu.roll(x, x.shape[1] - half, axis=1),
                        pltpu.roll(x, half, axis=1))
    return x * cos + partner * sin_signed


def _attn_lat_kernel(lam_init, q_ref, k_ref, v_ref, g_ref, ck_ref, cv_ref, cosq_ref, sinq_ref,
                     cosk_ref, sink_ref, lam_ref, sw_ref, ymix_ref, o_ref, kc_ref, kr_ref, vt_ref):
    del ymix_ref
    @pl.when(pl.program_id(1) == 0)
    def _():
        kc_ref[...] = ck_ref[...].astype(BF16)
        for h in range(N_HEADS_C):
            cols = slice(h * HEAD_DIM_C, (h + 1) * HEAD_DIM_C)
            kr_ref[:, cols] = _rope(k_ref[:, cols], cosk_ref[...], sink_ref[...]).astype(BF16)
            vt_ref[h, :, 0:PAST_LEN] = _with_ones_rows(cv_ref[:, cols].T)
            vt_ref[h, :, PAST_LEN:PAST_LEN + DEC_SEQ] = _with_ones_rows(v_ref[:, cols].T)

    def store(h, y):
        o_ref[:, _head_cols(h)] = y.astype(BF16)

    _diff_attention(
        lambda h: _rope(q_ref[:, _head_cols(h)], cosq_ref[...], sinq_ref[...]),
        lambda h: [kc_ref[:, _head_cols(h)], kr_ref[:, _head_cols(h)]],
        lambda h: [vt_ref[h, :, 0:PAST_LEN], vt_ref[h, :, PAST_LEN:PAST_LEN + DEC_SEQ]],
        lambda h: g_ref[:, _head_cols(h)],
        store, _lambda(lam_ref, lam_init), lam_init, sw_ref[...])


def _rope_tables():
    pos = np.arange(DEC_SEQ)
    row = (pos // GRID_W).astype(np.float64)
    col = (pos % GRID_W).astype(np.float64)
    half = ROPE_AXIS_DIM // 2
    inv = ROPE_BASE ** (-np.arange(0, ROPE_AXIS_DIM, 2, dtype=np.float64) / ROPE_AXIS_DIM)
    lane = np.arange(HEAD_DIM_C)
    axis_is_col = (lane // ROPE_AXIS_DIM) % 2 == 1
    idx = lane % ROPE_AXIS_DIM
    ang = np.where(axis_is_col[None, :], col[:, None], row[:, None]) * inv[idx % half][None, :]
    sign = np.where(idx < half, -1.0, 1.0)[None, :]
    return (jnp.asarray(np.cos(ang), F32), jnp.asarray(np.sin(ang) * sign, F32))


def _mixer_attn_lat(p, cache_k4, cache_v4, lambda_qk, subln_row, layer, lam_init, ymix):
    cos_t, sin_t = _rope_tables()
    q_tiles = DEC_SEQ // ATT_TQ
    q0 = N_CTX // ATT_TQ
    b0 = N_CTX // DEC_SEQ
    qblk = lambda c: pl.BlockSpec((ATT_TQ, BRANCH), lambda b, i: (q0 + b * q_tiles + i, c))
    kblk = lambda c: pl.BlockSpec((DEC_SEQ, BRANCH), lambda b, i: (b0 + b, c))
    cblk = pl.BlockSpec((None, None, PAST_LEN, BRANCH), lambda b, i: (b, layer, 0, 0))
    return pl.pallas_call(
        functools.partial(_attn_lat_kernel, lam_init),
        grid=(DEC_BATCH, q_tiles),
        in_specs=[
            qblk(C_Q), kblk(C_K), kblk(C_V), qblk(C_CG), cblk, cblk,
            pl.BlockSpec((ATT_TQ, HEAD_DIM_C), lambda b, i: (i, 0)),
            pl.BlockSpec((ATT_TQ, HEAD_DIM_C), lambda b, i: (i, 0)),
            pl.BlockSpec((DEC_SEQ, HEAD_DIM_C), lambda b, i: (0, 0)),
            pl.BlockSpec((DEC_SEQ, HEAD_DIM_C), lambda b, i: (0, 0)),
            pl.BlockSpec((None, 4, QK_HALF), lambda b, i: (layer, 0, 0)),
            pl.BlockSpec((None, 1, HEAD_DIM_C), lambda b, i: (layer, 0, 0)),
            pl.BlockSpec(memory_space=pl.ANY),
        ],
        out_specs=pl.BlockSpec((ATT_TQ, BRANCH), lambda b, i: (q0 + b * q_tiles + i, M_ATTN)),
        out_shape=jax.ShapeDtypeStruct((N_TOK, D_MIX), BF16),
        input_output_aliases={12: 0},
        scratch_shapes=[pltpu.VMEM((PAST_LEN, BRANCH), BF16), pltpu.VMEM((DEC_SEQ, BRANCH), BF16),
                        pltpu.VMEM((N_HEADS_C, V_ROWS, PAST_LEN + DEC_SEQ), BF16)],
        compiler_params=_params("arbitrary", "arbitrary"),
        name="mixer_attn_lat",
    )(p, p, p, p, cache_k4, cache_v4, cos_t, sin_t, cos_t, sin_t, lambda_qk, subln_row, ymix)


HY_ROWS = 1024


def _dft_matrices(seq_len):
    n = 2 * seq_len
    f = np.arange(seq_len, dtype=np.float64)[:, None]
    s = np.arange(seq_len, dtype=np.float64)[None, :]
    theta = 2.0 * np.pi * f * s / n
    alt = np.where(np.arange(seq_len) % 2 == 0, 1.0, -1.0)
    ac = np.cos(theta)
    as_ = -np.sin(theta)
    as_[0, :] = alt
    bc = 2.0 * np.cos(theta.T) / n
    bc[:, 0] = 1.0 / n
    bs = -2.0 * np.sin(theta.T) / n
    bs[:, 0] = alt / n
    fwd = np.concatenate([ac, as_], axis=0)
    inv = np.concatenate([bc, bs], axis=1)
    return jnp.asarray(fwd, F32).astype(BF16), jnp.asarray(inv, F32).astype(BF16)


def _filter_features(seq_len):
    t_idx = np.arange(seq_len, dtype=np.float64)
    t_norm = np.linspace(0.0, 1.0, seq_len)
    bands = np.linspace(1e-4, HY_BANDS - 1, HY_BANDS)
    ang = (2.0 * math.pi * t_idx / seq_len)[:, None] * bands[None, :]
    feats = np.concatenate([t_norm[:, None], np.cos(ang), np.sin(ang)], axis=-1)
    feats = np.pad(feats, ((0, 0), (0, HY_PAD - HY_EMB)))
    deltas = np.abs(np.linspace(math.log(HY_TARGET) / HY_FAST_DECAY,
                                math.log(HY_TARGET) / HY_SLOW_DECAY, BRANCH))
    return (jnp.asarray(feats, F32), jnp.asarray(t_norm[:, None], F32),
            jnp.asarray(deltas[None, :], F32))


def _filter_kernel(seq_len, feats_ref, tn_ref, dl_ref, w1_ref, b1_ref, w2_ref, b2_ref, fr_ref,
                   w3f_ref, w3b_ref, fwd_ref, kr_ref, ki_ref, h_ref):
    sp = lambda x: _split(x, 2)

    @pl.when(pl.program_id(1) == 0)
    def _():
        fr = fr_ref[...]
        h = jnp.sin(fr * (_sdot(sp(feats_ref[...]), sp(w1_ref[...])) + b1_ref[...]))
        h_ref[...] = jnp.sin(fr * (_sdot(sp(h), sp(w2_ref[...])) + b2_ref[...]))

    h = sp(h_ref[...])
    decay = jnp.exp(-tn_ref[...] * dl_ref[...])
    row = lax.broadcasted_iota(jnp.int32, (seq_len, BRANCH), 0)
    fwd = _sdot(h, sp(w3f_ref[...])) * decay
    bwd = jnp.where(row == 0, 0.0, _sdot(h, sp(w3b_ref[...])) * decay)
    norm = (jnp.sum(jnp.abs(fwd), axis=0, keepdims=True)
            + jnp.sum(jnp.abs(bwd), axis=0, keepdims=True))
    fwd = fwd / norm
    bwd = bwd / norm
    even = fwd + bwd
    alt = jnp.where((row & 1) == 0, 1.0, -1.0)
    nyquist = jnp.sum(alt * even, axis=0, keepdims=True)
    kr_ref[...] = jnp.dot(fwd_ref[0:seq_len, :], even.astype(BF16), preferred_element_type=F32)
    ki = jnp.dot(fwd_ref[seq_len:2 * seq_len, :], (fwd - bwd).astype(BF16),
                 preferred_element_type=F32)
    ki_ref[...] = jnp.where(row == 0, nyquist, ki)


def _hyena_spectrum(seq_len, filt, fwd_mat):
    feats, t_norm, deltas = _filter_features(seq_len)
    w1, b1, w2, b2, freq, w3 = filt
    full = lambda a: pl.BlockSpec(a.shape, lambda l, o: (0,) * a.ndim)
    lyr = lambda r, n: pl.BlockSpec((None, r, n), lambda l, o: (l, 0, 0))
    out = pl.BlockSpec((None, None, seq_len, BRANCH), lambda l, o: (l, o, 0, 0))
    return pl.pallas_call(
        functools.partial(_filter_kernel, seq_len),
        grid=(DEPTH, HY_ORDER),
        in_specs=[
            full(feats), full(t_norm), full(deltas),
            lyr(HY_PAD, HY_PAD), lyr(1, HY_PAD), lyr(HY_PAD, HY_PAD), lyr(1, HY_PAD), lyr(1, HY_PAD),
            pl.BlockSpec((None, HY_PAD, BRANCH), lambda l, o: (l, 0, 2 * o)),
            pl.BlockSpec((None, HY_PAD, BRANCH), lambda l, o: (l, 0, 2 * o + 1)),
            full(fwd_mat),
        ],
        out_specs=[out, out],
        out_shape=[jax.ShapeDtypeStruct((DEPTH, HY_ORDER, seq_len, BRANCH), F32)] * 2,
        scratch_shapes=[pltpu.VMEM((seq_len, HY_PAD), F32)],
        compiler_params=_params("arbitrary", "arbitrary"),
        name=f"hyena_spectrum_{seq_len}",
    )(feats, t_norm, deltas, w1, b1, w2, b2, freq, w3, w3, fwd_mat)


def _hyena_chains(seq_len, width, x1_ref, x2_ref, hv_ref, g_ref, cw_ref, cb_ref, kr_ref, ki_ref,
                  hb_ref, fwd_ref, inv_ref, o_ref):
    row = lax.broadcasted_iota(jnp.int32, (seq_len, width), 0)
    first, last = row == 0, row == seq_len - 1

    def chain(rs, c0):
        cs = slice(c0, c0 + width)

        def short_conv(x_ref, piece):
            x = x_ref[rs, cs]
            w = cw_ref[:, piece * BRANCH + c0:piece * BRANCH + c0 + width]
            b = cb_ref[:, piece * BRANCH + c0:piece * BRANCH + c0 + width]
            prev = jnp.where(first, 0.0, pltpu.roll(x, 1, axis=0))
            nxt = jnp.where(last, 0.0, pltpu.roll(x, seq_len - 1, axis=0))
            return prev * w[0:1] + x * w[1:2] + nxt * w[2:3] + b

        z = short_conv(hv_ref, 2)
        gates = (short_conv(x1_ref, 0), short_conv(x2_ref, 1))
        yield
        for order in range(HY_ORDER):
            zf = jnp.dot(fwd_ref[...], z.astype(BF16), preferred_element_type=F32)
            yield
            zr, zi = zf[:seq_len], zf[seq_len:]
            kr, kp = kr_ref[order, :, cs], ki_ref[order, :, cs]
            ki = jnp.where(first, 0.0, kp)
            kn = jnp.where(first, kp, kr)
            yf = jnp.concatenate([(zr * kr - zi * ki).astype(BF16),
                                  (zr * ki + zi * kn).astype(BF16)], axis=0)
            yield
            y = jnp.dot(inv_ref[...], yf, preferred_element_type=F32)
            yield
            z = gates[order] * (y + z * hb_ref[order:order + 1, cs])
        o_ref[rs, cs] = (_silu(g_ref[rs, cs]) * z).astype(BF16)

    waiting = [chain(slice(s * seq_len, (s + 1) * seq_len), c0)
               for s in range(HY_ROWS // seq_len) for c0 in range(0, BRANCH, width)]
    running = []
    while waiting or running:
        if waiting:
            running.append(waiting.pop(0))
        for gen in list(running):
            if next(gen, "done") == "done":
                running.remove(gen)


def _hyena_kernel(x1_ref, x2_ref, hv_ref, g_ref, cw_ref, cb_ref, hb_ref,
                  krc_ref, kic_ref, fwdc_ref, invc_ref, krl_ref, kil_ref, fwdl_ref, invl_ref,
                  ymix_ref, o_ref):
    del ymix_ref
    common = (x1_ref, x2_ref, hv_ref, g_ref, cw_ref, cb_ref)
    is_ctx = pl.program_id(0) < N_CTX // HY_ROWS

    @pl.when(is_ctx)
    def _():
        _hyena_chains(SEQ, BRANCH, *common, krc_ref, kic_ref, hb_ref, fwdc_ref, invc_ref, o_ref)

    @pl.when(jnp.logical_not(is_ctx))
    def _():
        _hyena_chains(DEC_SEQ, HY_CT, *common, krl_ref, kil_ref, hb_ref, fwdl_ref, invl_ref, o_ref)


def _mixer_hyena(p, conv_w, conv_b3, hyena_bias, spec_ctx, mats_ctx, spec_lat, mats_lat, layer,
                 ymix):
    blk = lambda c: pl.BlockSpec((HY_ROWS, BRANCH), lambda i: (i, c))
    once = pl.Buffered(1)
    lyr = lambda a: pl.BlockSpec((None,) + a.shape[1:], lambda i: (layer,) + (0,) * (a.ndim - 1),
                                 pipeline_mode=once)
    full = lambda a: pl.BlockSpec(a.shape, lambda i: (0,) * a.ndim, pipeline_mode=once)
    consts = [*spec_ctx, *mats_ctx, *spec_lat, *mats_lat]
    return pl.pallas_call(
        _hyena_kernel,
        grid=(N_TOK // HY_ROWS,),
        in_specs=[blk(C_DX1), blk(C_DX2), blk(C_DV), blk(C_DG),
                  lyr(conv_w), lyr(conv_b3), lyr(hyena_bias),
                  lyr(spec_ctx[0]), lyr(spec_ctx[1]), full(mats_ctx[0]), full(mats_ctx[1]),
                  lyr(spec_lat[0]), lyr(spec_lat[1]), full(mats_lat[0]), full(mats_lat[1]),
                  pl.BlockSpec(memory_space=pl.ANY)],
        out_specs=pl.BlockSpec((HY_ROWS, BRANCH), lambda i: (i, M_HYENA)),
        out_shape=jax.ShapeDtypeStruct((N_TOK, D_MIX), BF16),
        input_output_aliases={7 + len(consts): 0},
        compiler_params=_params("arbitrary"),
        name="mixer_hyena",
    )(p, p, p, p, conv_w, conv_b3, hyena_bias, *consts, ymix)


OUT_TM = 512


def _outproj_kernel(alpha, n_x, n_out, ymix_ref, *refs):
    x_refs = refs[:n_x]
    gate_ref, w_ref, b_ref, lng_ref, lnb_ref = refs[n_x:n_x + 5]
    o_refs = refs[n_x + 5:]
    is_ctx = pl.program_id(0) < N_CTX // OUT_TM
    x = x_refs[0][...] if n_x == 1 else jnp.where(is_ctx, x_refs[0][...], x_refs[1][...])
    y = jnp.dot(ymix_ref[...], w_ref[...], preferred_element_type=F32) + b_ref[...]
    r = alpha * x + gate_ref[...] * y
    res = _layer_norm(r) * lng_ref[...] + lnb_ref[...]
    if n_out == 1:
        o_refs[0][...] = res
    else:
        @pl.when(is_ctx)
        def _():
            o_refs[0][...] = res

        @pl.when(jnp.logical_not(is_ctx))
        def _():
            o_refs[1][...] = res


def _out_projection(ymix, xs, mod4, w_out_bf, b_out3, ln_g3, ln_b3, layer, split_out):
    alpha = (2.0 * DEPTH) ** 0.25
    row = lambda i: _cond_row(i, OUT_TM)
    vec = pl.BlockSpec((None, 1, D_MODEL), lambda i: (layer, 0, 0))
    if split_out:
        out_shape = [jax.ShapeDtypeStruct((N_CTX, D_MODEL), F32),
                     jax.ShapeDtypeStruct((N_LAT, D_MODEL), F32)]
    else:
        out_shape = [jax.ShapeDtypeStruct((N_TOK, D_MODEL), F32)]
    return pl.pallas_call(
        functools.partial(_outproj_kernel, alpha, len(xs), len(out_shape)),
        grid=(N_TOK // OUT_TM,),
        in_specs=[pl.BlockSpec((OUT_TM, D_MIX), lambda i: (i, 0))] + _token_specs(xs, OUT_TM) + [
            pl.BlockSpec((None, None, 1, D_MODEL), lambda i: (layer, row(i), 0, 2)),
            pl.BlockSpec((D_MIX, D_MODEL), lambda i: (0, 0), pipeline_mode=pl.Buffered(1)),
            vec, vec, vec,
        ],
        out_specs=_token_specs(out_shape, OUT_TM),
        out_shape=out_shape,
        compiler_params=_params("arbitrary"),
        name="out_projection",
    )(ymix, *xs, mod4, w_out_bf, b_out3, ln_g3, ln_b3)


def kernel(x_prompt, x_sample, cache_k, cache_v, c, c_ctx, w_mod, b_mod, w_in, gmlp_w, gmlp_b,
           pool_w, pool_scale, lambda_qk, subln_w, conv_w, conv_b, filt_w1, filt_b1, filt_w2,
           filt_b2, filt_freq, filt_w3, hyena_bias, w_out, b_out, ln_g, ln_b):
    xs = (x_prompt.reshape(N_CTX, D_MODEL), x_sample.reshape(N_LAT, D_MODEL))
    cond = jnp.concatenate(
        [c_ctx[None, :], c, jnp.zeros((N_COND - 1 - DEC_BATCH, D_MODEL), F32)], axis=0)
    mod4 = _modulation(cond, w_mod, b_mod).reshape(DEPTH, N_COND, 1, 3 * D_MODEL)

    cache_k4 = cache_k.reshape(DEC_BATCH, DEPTH, PAST_LEN, BRANCH)
    cache_v4 = cache_v.reshape(DEC_BATCH, DEPTH, PAST_LEN, BRANCH)
    gmlp_b_rows = jnp.broadcast_to(gmlp_b[..., None], (DEPTH, N_GROUPS, CHUNK, GROUP_W))
    subln_row = subln_w.reshape(DEPTH, 1, HEAD_DIM_C)
    conv_b3 = conv_b.reshape(DEPTH, 1, 3 * BRANCH)
    b_out3 = b_out.reshape(DEPTH, 1, D_MODEL)
    ln_g3 = ln_g.reshape(DEPTH, 1, D_MODEL)
    ln_b3 = ln_b.reshape(DEPTH, 1, D_MODEL)

    pad_h = HY_PAD - HY_HIDDEN
    filt = (
        jnp.pad(filt_w1, ((0, 0), (0, HY_PAD - HY_EMB), (0, pad_h))),
        jnp.pad(filt_b1, ((0, 0), (0, pad_h))).reshape(DEPTH, 1, HY_PAD),
        jnp.pad(filt_w2, ((0, 0), (0, pad_h), (0, pad_h))),
        jnp.pad(filt_b2, ((0, 0), (0, pad_h))).reshape(DEPTH, 1, HY_PAD),
        jnp.pad(filt_freq, ((0, 0), (0, pad_h))).reshape(DEPTH, 1, HY_PAD),
        jnp.pad(filt_w3, ((0, 0), (0, pad_h), (0, 0))),
    )
    mats_ctx = _dft_matrices(SEQ)
    mats_lat = _dft_matrices(DEC_SEQ)
    spec_ctx = _hyena_spectrum(SEQ, filt, mats_ctx[0])
    spec_lat = _hyena_spectrum(DEC_SEQ, filt, mats_lat[0])

    caches = None
    for layer in range(DEPTH):
        lam_init = 0.8 - 0.6 * math.exp(-0.3 * layer)
        p = _in_projection(xs, mod4, w_in, layer)
        ymix, w_out_bf = _mixer_gmlp_pool(p, gmlp_w[layer], gmlp_b_rows[layer], pool_w[layer],
                                          pool_scale[layer].reshape(1, BRANCH), w_out, layer)
        ymix, new_k, new_v = _mixer_attn_ctx(p, lambda_qk, subln_row, layer, lam_init, ymix, caches)
        caches = (new_k, new_v)
        ymix = _mixer_attn_lat(p, cache_k4, cache_v4, lambda_qk, subln_row, layer, lam_init, ymix)
        ymix = _mixer_hyena(p, conv_w, conv_b3, hyena_bias, spec_ctx, mats_ctx, spec_lat, mats_lat,
                            layer, ymix)
        xs = tuple(_out_projection(ymix, xs, mod4, w_out_bf, b_out3, ln_g3, ln_b3, layer,
                                   split_out=layer == DEPTH - 1))

    y_prompt = xs[0].reshape(BATCH, SEQ, D_MODEL)
    y_sample = xs[1].reshape(DEC_BATCH, DEC_SEQ, D_MODEL)
    new_k, new_v = caches
    return (y_prompt, y_sample,
            new_k.reshape(BATCH, DEPTH, SEQ, N_HEADS_C, 2, QK_HALF),
            new_v.reshape(BATCH, DEPTH, SEQ, N_HEADS_C, HEAD_DIM_C))
```

```python
import functools
import math

import numpy as np
import jax
import jax.numpy as jnp
from jax import lax
from jax.experimental import pallas as pl
from jax.experimental.pallas import tpu as pltpu

F32 = jnp.float32
BF16 = jnp.bfloat16

D_MODEL = 2048
BATCH = 16
SEQ = 256
DEPTH = 2
DEC_BATCH = 4
DEC_SEQ = 1024
PAST_LEN = 512
GRID_W = 64
BRANCH = 512
N_GROUPS = 4
GROUP_W = 128
CHUNK = 128
POOL_WINDOWS = (2, 4, 8, 16)
N_HEADS_C = 4
HEAD_DIM_C = 128
QK_HALF = 64
ROPE_AXIS_DIM = 32
ROPE_BASE = 10000.0
HY_BANDS = 16
HY_EMB = 33
HY_HIDDEN = 64
HY_ORDER = 2
HY_FAST_DECAY = 0.3
HY_SLOW_DECAY = 1.5
HY_TARGET = 1e-2
N_IN_PIECES = 13
D_IN = N_IN_PIECES * BRANCH
LN_EPS = 1e-6

N_CTX = BATCH * SEQ
N_LAT = DEC_BATCH * DEC_SEQ
N_TOK = N_CTX + N_LAT
N_COND = 8
LANES = 128
HY_PAD = LANES
HY_CT = 256
VMEM_LIMIT = 56 * 1024 * 1024

(C_AU, C_AV, C_AG, C_BX, C_BG, C_Q, C_K, C_V, C_CG, C_DX1, C_DX2, C_DV, C_DG) = range(13)
D_MIX = 4 * BRANCH
(M_GMLP, M_POOL, M_ATTN, M_HYENA) = range(4)


def _silu(x):
    return x * jax.nn.sigmoid(x)


def _bdot(a, b):
    return jnp.dot(a.astype(BF16), b.astype(BF16), preferred_element_type=F32)


def _split(x, n_terms):
    hi = x.astype(BF16)
    if n_terms == 1:
        return (hi,)
    return (hi, (x - hi.astype(F32)).astype(BF16))


def _sdot(a_terms, b_terms):
    acc = jnp.dot(a_terms[0], b_terms[0], preferred_element_type=F32)
    if len(a_terms) > 1:
        acc = acc + jnp.dot(a_terms[1], b_terms[0], preferred_element_type=F32)
    if len(b_terms) > 1:
        acc = acc + jnp.dot(a_terms[0], b_terms[1], preferred_element_type=F32)
    return acc


def _layer_norm(x):
    mu = jnp.mean(x, axis=-1, keepdims=True)
    xc = x - mu
    var = jnp.mean(xc * xc, axis=-1, keepdims=True)
    return xc * lax.rsqrt(var + LN_EPS)


def _cond_row(tile, rows_per_tile):
    n_ctx_tiles = N_CTX // rows_per_tile
    tiles_per_batch = DEC_SEQ // rows_per_tile
    return jnp.where(tile < n_ctx_tiles, 0, 1 + (tile - n_ctx_tiles) // tiles_per_batch)


def _params(*semantics):
    return pltpu.CompilerParams(dimension_semantics=semantics, vmem_limit_bytes=VMEM_LIMIT)


MOD_TN = 512


def _mod_kernel(c_ref, w_ref, b_ref, o_ref):
    o_ref[...] = _bdot(_silu(c_ref[...]), w_ref[...]) + b_ref[...]


def _modulation(cond, w_mod, b_mod):
    n = 3 * D_MODEL
    return pl.pallas_call(
        _mod_kernel,
        grid=(DEPTH, n // MOD_TN),
        in_specs=[
            pl.BlockSpec((N_COND, D_MODEL), lambda l, j: (0, 0)),
            pl.BlockSpec((None, D_MODEL, MOD_TN), lambda l, j: (l, 0, j)),
            pl.BlockSpec((None, 1, MOD_TN), lambda l, j: (l, 0, j)),
        ],
        out_specs=pl.BlockSpec((None, N_COND, MOD_TN), lambda l, j: (l, 0, j)),
        out_shape=jax.ShapeDtypeStruct((DEPTH, N_COND, n), F32),
        compiler_params=_params("arbitrary", "arbitrary"),
        name="modulation",
    )(cond, w_mod, b_mod.reshape(DEPTH, 1, n))


IN_TM = 1024
IN_TN = 512
IN_LN_ROWS = 256


def _token_specs(xs, tm, tile_of=lambda i: i):
    n_ctx_tiles = N_CTX // tm
    if len(xs) == 1:
        maps = [lambda i, *_: (tile_of(i), 0)]
    else:
        maps = [lambda i, *_: (jnp.minimum(tile_of(i), n_ctx_tiles - 1), 0),
                lambda i, *_: (jnp.maximum(tile_of(i) - n_ctx_tiles, 0), 0)]
    return [pl.BlockSpec((tm, D_MODEL), m) for m in maps]


def _inproj_kernel(n_x, tile0, cast_w, *refs):
    x_refs = refs[:n_x]
    scale_ref, shift_ref, w_ref = refs[n_x:n_x + 3]
    if cast_w:
        o_ref, wb_ref, h_ref = refs[n_x + 3:]
    else:
        _, o_ref, h_ref = refs[n_x + 3:]

    def fill_h(x_ref):
        def body(r, carry):
            rows = pl.ds(pl.multiple_of(r * IN_LN_ROWS, IN_LN_ROWS), IN_LN_ROWS)
            h = _layer_norm(x_ref[rows, :]) * (1.0 + scale_ref[...]) + shift_ref[...]
            h_ref[rows, :] = h.astype(BF16)
            return carry
        lax.fori_loop(0, IN_TM // IN_LN_ROWS, body, 0)

    first = pl.program_id(1) == 0
    if n_x == 1:
        pl.when(first)(lambda: fill_h(x_refs[0]))
    else:
        is_ctx = pl.program_id(0) + tile0 < N_CTX // IN_TM
        pl.when(jnp.logical_and(first, is_ctx))(lambda: fill_h(x_refs[0]))
        pl.when(jnp.logical_and(first, jnp.logical_not(is_ctx)))(lambda: fill_h(x_refs[1]))

    if cast_w:
        w = w_ref[...].astype(BF16)
        wb_ref[...] = w
    else:
        w = w_ref[...]
    o_ref[...] = jnp.dot(h_ref[...], w, preferred_element_type=F32)


def _in_projection(xs, mod4, w_in, layer):
    n_col = D_IN // IN_TN

    def mod_spec(tile0, piece):
        return pl.BlockSpec((None, None, 1, D_MODEL),
                            lambda i, j: (layer, _cond_row(i + tile0, IN_TM), 0, piece))

    scratch = [pltpu.VMEM((IN_TM, D_MODEL), BF16)]
    p_shape = jax.ShapeDtypeStruct((N_TOK, D_IN), F32)
    p, w_bf = pl.pallas_call(
        functools.partial(_inproj_kernel, 1, 0, True),
        grid=(1, n_col),
        in_specs=_token_specs(xs[:1], IN_TM) + [
            mod_spec(0, 1), mod_spec(0, 0),
            pl.BlockSpec((None, D_MODEL, IN_TN), lambda i, j: (layer, 0, j)),
        ],
        out_specs=[pl.BlockSpec((IN_TM, IN_TN), lambda i, j: (i, j)),
                   pl.BlockSpec((D_MODEL, IN_TN), lambda i, j: (0, j))],
        out_shape=[p_shape, jax.ShapeDtypeStruct((D_MODEL, D_IN), BF16)],
        scratch_shapes=scratch,
        compiler_params=_params("arbitrary", "arbitrary"),
        name="in_projection_first",
    )(xs[0], mod4, mod4, w_in)
    return pl.pallas_call(
        functools.partial(_inproj_kernel, len(xs), 1, False),
        grid=(N_TOK // IN_TM - 1, n_col),
        in_specs=_token_specs(xs, IN_TM, lambda i: i + 1) + [
            mod_spec(1, 1), mod_spec(1, 0),
            pl.BlockSpec((D_MODEL, IN_TN), lambda i, j: (0, j)),
            pl.BlockSpec(memory_space=pl.ANY),
        ],
        out_specs=pl.BlockSpec((IN_TM, IN_TN), lambda i, j: (i + 1, j)),
        out_shape=p_shape,
        input_output_aliases={len(xs) + 3: 0},
        scratch_shapes=scratch,
        compiler_params=_params("arbitrary", "arbitrary"),
        name="in_projection",
    )(*xs, mod4, mod4, w_bf, p)


AB_TM = 1024


def _gmlp_tasks(u_ref, v_ref, g_ref, w_ref, b_ref, o_ref):
    ws = [w_ref[g].astype(BF16) for g in range(N_GROUPS)]

    def chunk(r):
        rows = slice(r, r + CHUNK)
        vn = _layer_norm(v_ref[rows, :]).astype(BF16)
        for g in range(N_GROUPS):
            cols = slice(g * GROUP_W, (g + 1) * GROUP_W)
            mixed = jnp.dot(ws[g], vn[:, cols], preferred_element_type=F32) + b_ref[g]
            o_ref[rows, cols] = (_silu(g_ref[rows, cols]) * u_ref[rows, cols] * mixed).astype(BF16)

    return [functools.partial(chunk, r) for r in range(0, AB_TM, CHUNK)]


POOL_BAND_GROUPS = (2, 3)
POOL_BAND_ROWS = 256
POOL_BAND_HALO = 128


def _window_sum_shifts(p, win, pos, seq_len):
    acc = p
    for d in range(-(win // 2), win // 2):
        if d == 0:
            continue
        shifted = pltpu.roll(p, (-d) % AB_TM, axis=0)
        valid = (pos >= -d) if d < 0 else (pos < seq_len - d)
        acc = acc + jnp.where(valid, shifted, 0.0)
    return acc


def _window_sum_band_tasks(p, band_ref, pieces):
    hi = p.astype(BF16)
    r1 = p - hi.astype(F32)
    mid = r1.astype(BF16)
    lo = (r1 - mid.astype(F32)).astype(BF16)
    terms = jnp.concatenate([hi, mid, lo], axis=1)

    def piece(r):
        k0, k1 = max(0, r - POOL_BAND_HALO), min(AB_TM, r + POOL_BAND_ROWS + POOL_BAND_HALO)
        s = jnp.dot(band_ref[r:r + POOL_BAND_ROWS, k0:k1], terms[k0:k1, :],
                    preferred_element_type=F32)
        pieces.append(s[:, :GROUP_W] + s[:, GROUP_W:2 * GROUP_W] + s[:, 2 * GROUP_W:])

    return [functools.partial(piece, r) for r in range(0, AB_TM, POOL_BAND_ROWS)]


def _pool_finish(g, acc, x_ref, g_ref, w_ref, s_ref, pos, seq_len, o_ref):
    cols = slice(g * GROUP_W, (g + 1) * GROUP_W)
    win = POOL_WINDOWS[g]
    count = jnp.minimum(pos + win // 2, seq_len) - jnp.maximum(pos - win // 2, 0)
    pooled = acc / count.astype(F32)
    y = _bdot(pooled - x_ref[:, cols], w_ref[g])
    o_ref[:, BRANCH + g * GROUP_W:BRANCH + (g + 1) * GROUP_W] = (
        _silu(g_ref[:, cols]) * (y * s_ref[:, cols])).astype(BF16)


def _mixer_ab_kernel(au_ref, av_ref, ag_ref, bx_ref, bg_ref, gw_ref, gb_ref, pw_ref, ps_ref,
                     band_a_ref, band_b_ref, wo_ref, o_ref, wo_bf_ref):
    wo_bf_ref[...] = wo_ref[...].astype(BF16)
    seq_len = jnp.where(pl.program_id(0) < N_CTX // AB_TM, SEQ, DEC_SEQ)
    pos = lax.broadcasted_iota(jnp.int32, (AB_TM, GROUP_W), 0) & (seq_len - 1)
    pool_refs = (bx_ref, bg_ref, pw_ref, ps_ref, pos, seq_len, o_ref)
    band_sums = {g: [] for g in POOL_BAND_GROUPS}
    mxu_tasks = [t for g, band in zip(POOL_BAND_GROUPS, (band_a_ref, band_b_ref))
                 for t in _window_sum_band_tasks(bx_ref[:, g * GROUP_W:(g + 1) * GROUP_W], band,
                                                 band_sums[g])]
    vpu_tasks = _gmlp_tasks(au_ref, av_ref, ag_ref, gw_ref, gb_ref, o_ref)
    for i in range(max(len(mxu_tasks), len(vpu_tasks))):
        for tasks in (mxu_tasks, vpu_tasks):
            if i < len(tasks):
                tasks[i]()
    for g, win in enumerate(POOL_WINDOWS):
        if g not in POOL_BAND_GROUPS:
            acc = _window_sum_shifts(bx_ref[:, g * GROUP_W:(g + 1) * GROUP_W], win, pos, seq_len)
            _pool_finish(g, acc, *pool_refs)
    for g in POOL_BAND_GROUPS:
        _pool_finish(g, jnp.concatenate(band_sums[g], axis=0), *pool_refs)


def _pool_bands():
    t = np.arange(AB_TM)
    bands = []
    for g in POOL_BAND_GROUPS:
        half = POOL_WINDOWS[g] // 2
        d = t[None, :] - t[:, None]
        near = (d >= -half) & (d < half)
        per_path = [near & ((t[None, :] // L) == (t[:, None] // L)) for L in (SEQ, DEC_SEQ)]
        bands.append(jnp.asarray(np.stack(per_path), F32).astype(BF16))
    return bands


def _mixer_gmlp_pool(p, gmlp_w, gmlp_b_rows, pool_w, pool_scale_row, w_out, layer):
    assert M_POOL == M_GMLP + 1
    n_steps = N_TOK // AB_TM
    wo_rows = D_MIX // n_steps
    blk = lambda c: pl.BlockSpec((AB_TM, BRANCH), lambda i: (i, c))
    full = lambda a: pl.BlockSpec(a.shape, lambda i: (0,) * a.ndim)
    band = pl.BlockSpec((None, AB_TM, AB_TM), lambda i: (jnp.where(i < N_CTX // AB_TM, 0, 1), 0, 0))
    return pl.pallas_call(
        _mixer_ab_kernel,
        grid=(n_steps,),
        in_specs=[blk(C_AU), blk(C_AV), blk(C_AG), blk(C_BX), blk(C_BG),
                  full(gmlp_w), full(gmlp_b_rows), full(pool_w), full(pool_scale_row), band, band,
                  pl.BlockSpec((None, wo_rows, D_MODEL), lambda i: (layer, i, 0))],
        out_specs=[pl.BlockSpec((AB_TM, 2 * BRANCH), lambda i: (i, M_GMLP // 2)),
                   pl.BlockSpec((wo_rows, D_MODEL), lambda i: (i, 0))],
        out_shape=[jax.ShapeDtypeStruct((N_TOK, D_MIX), BF16),
                   jax.ShapeDtypeStruct((D_MIX, D_MODEL), BF16)],
        compiler_params=_params("arbitrary"),
        name="mixer_gmlp_pool",
    )(p, p, p, p, p, gmlp_w, gmlp_b_rows, pool_w, pool_scale_row, *_pool_bands(), w_out)


ATT_TQ = 256
V_ROWS = HEAD_DIM_C + 16
NT_DIMS = (((1,), (1,)), ((), ()))


def _lambda(lam_ref, lam_init):
    lq = lam_ref[...]
    a = jnp.sum(lq[0:1] * lq[1:2], axis=-1, keepdims=True)
    b = jnp.sum(lq[2:3] * lq[3:4], axis=-1, keepdims=True)
    return jnp.exp(a) - jnp.exp(b) + lam_init


def _map_masks():
    lane = lax.broadcasted_iota(jnp.int32, (1, HEAD_DIM_C), 1)
    m0 = (lane < QK_HALF).astype(F32)
    return m0, 1.0 - m0


def _scores_t(q, keys):
    return [lax.dot_general(k, q, NT_DIMS, preferred_element_type=F32) for k in keys]


def _softmax_v_t(s, vals_t):
    m = functools.reduce(jnp.maximum, [jnp.max(x, axis=0, keepdims=True) for x in s])
    acc = functools.reduce(
        jnp.add, [jnp.dot(v, jnp.exp2(x - m).astype(BF16), preferred_element_type=F32)
                  for x, v in zip(s, vals_t)])
    return acc[:HEAD_DIM_C] * (1.0 / acc[HEAD_DIM_C:HEAD_DIM_C + 1])


def _with_ones_rows(v_t):
    ones = jnp.ones((V_ROWS - HEAD_DIM_C, v_t.shape[1]), BF16)
    return jnp.concatenate([v_t.astype(BF16), ones], axis=0)


def _diff_attention(q_of, keys_of, vals_t_of, gate_of, store, lam, lam_init, subln):
    masks = _map_masks()
    tasks = [(h, mp) for h in range(N_HEADS_C) for mp in range(2)]
    per_head = {}

    def operands(h):
        if h not in per_head:
            q = q_of(h) * (QK_HALF ** -0.5 * math.log2(math.e))
            per_head[h] = (q, keys_of(h), vals_t_of(h))
        return per_head[h]

    def scores(h, mp):
        q, keys, _ = operands(h)
        return _scores_t((q * masks[mp]).astype(BF16), keys)

    nxt = scores(*tasks[0])
    outs = []
    for i, (h, mp) in enumerate(tasks):
        cur = nxt
        if i + 1 < len(tasks):
            nxt = scores(*tasks[i + 1])
        outs.append(_softmax_v_t(cur, operands(h)[2]))
        if mp == 1:
            o = (outs[0] - lam * outs[1]).T
            outs = []
            o = o * lax.rsqrt(jnp.mean(o * o, axis=-1, keepdims=True) + 1e-5)
            o = o * subln * (1.0 - lam_init)
            store(h, _silu(gate_of(h)) * o)


def _head_cols(h):
    return slice(h * HEAD_DIM_C, (h + 1) * HEAD_DIM_C)


def _attn_ctx_kernel(lam_init, q_ref, k_ref, v_ref, g_ref, lam_ref, sw_ref, *rest):
    o_ref, ko_ref, vo_ref = rest[-3:]
    ko_ref[...] = k_ref[...]
    vo_ref[...] = v_ref[...]

    def store(h, y):
        o_ref[:, _head_cols(h)] = y.astype(BF16)

    _diff_attention(
        lambda h: q_ref[:, _head_cols(h)],
        lambda h: [k_ref[:, _head_cols(h)].astype(BF16)],
        lambda h: [_with_ones_rows(v_ref[:, _head_cols(h)].T)],
        lambda h: g_ref[:, _head_cols(h)],
        store, _lambda(lam_ref, lam_init), lam_init, sw_ref[...])


def _mixer_attn_ctx(p, lambda_qk, subln_row, layer, lam_init, ymix, caches):
    blk = lambda c: pl.BlockSpec((SEQ, BRANCH), lambda b: (b, c))
    any_spec = pl.BlockSpec(memory_space=pl.ANY)
    cache_spec = pl.BlockSpec((None, None, SEQ, BRANCH), lambda b: (b, layer, 0, 0))
    cache_shape = jax.ShapeDtypeStruct((BATCH, DEPTH, SEQ, BRANCH), F32)
    in_specs = [
        blk(C_Q), blk(C_K), blk(C_V), blk(C_CG),
        pl.BlockSpec((None, 4, QK_HALF), lambda b: (layer, 0, 0)),
        pl.BlockSpec((None, 1, HEAD_DIM_C), lambda b: (layer, 0, 0)),
        any_spec,
    ]
    args = [p, p, p, p, lambda_qk, subln_row, ymix]
    aliases = {6: 0}
    if caches is not None:
        in_specs += [any_spec, any_spec]
        args += list(caches)
        aliases.update({7: 1, 8: 2})
    return pl.pallas_call(
        functools.partial(_attn_ctx_kernel, lam_init),
        grid=(BATCH,),
        in_specs=in_specs,
        out_specs=[pl.BlockSpec((SEQ, BRANCH), lambda b: (b, M_ATTN)), cache_spec, cache_spec],
        out_shape=[jax.ShapeDtypeStruct((N_TOK, D_MIX), BF16), cache_shape, cache_shape],
        input_output_aliases=aliases,
        compiler_params=_params("arbitrary"),
        name="mixer_attn_ctx",
    )(*args)


def _rope(x, cos, sin_signed):
    lane = lax.broadcasted_iota(jnp.int32, x.shape, 1)
    first_half = (lane & (ROPE_AXIS_DIM - 1)) < (ROPE_AXIS_DIM // 2)
    half = ROPE_AXIS_DIM // 2
    partner = jnp.where(first_half,
                        pltpu.roll(x, x.shape[1] - half, axis=1),
                        pltpu.roll(x, half, axis=1))
    return x * cos + partner * sin_signed


def _attn_lat_kernel(lam_init, q_ref, k_ref, v_ref, g_ref, ck_ref, cv_ref, cosq_ref, sinq_ref,
                     cosk_ref, sink_ref, lam_ref, sw_ref, ymix_ref, o_ref, kc_ref, kr_ref, vt_ref):
    del ymix_ref
    @pl.when(pl.program_id(1) == 0)
    def _():
        kc_ref[...] = ck_ref[...].astype(BF16)
        for h in range(N_HEADS_C):
            cols = slice(h * HEAD_DIM_C, (h + 1) * HEAD_DIM_C)
            kr_ref[:, cols] = _rope(k_ref[:, cols], cosk_ref[...], sink_ref[...]).astype(BF16)
            vt_ref[h, :, 0:PAST_LEN] = _with_ones_rows(cv_ref[:, cols].T)
            vt_ref[h, :, PAST_LEN:PAST_LEN + DEC_SEQ] = _with_ones_rows(v_ref[:, cols].T)

    def store(h, y):
        o_ref[:, _head_cols(h)] = y.astype(BF16)

    _diff_attention(
        lambda h: _rope(q_ref[:, _head_cols(h)], cosq_ref[...], sinq_ref[...]),
        lambda h: [kc_ref[:, _head_cols(h)], kr_ref[:, _head_cols(h)]],
        lambda h: [vt_ref[h, :, 0:PAST_LEN], vt_ref[h, :, PAST_LEN:PAST_LEN + DEC_SEQ]],
        lambda h: g_ref[:, _head_cols(h)],
        store, _lambda(lam_ref, lam_init), lam_init, sw_ref[...])


def _rope_tables():
    pos = np.arange(DEC_SEQ)
    row = (pos // GRID_W).astype(np.float64)
    col = (pos % GRID_W).astype(np.float64)
    half = ROPE_AXIS_DIM // 2
    inv = ROPE_BASE ** (-np.arange(0, ROPE_AXIS_DIM, 2, dtype=np.float64) / ROPE_AXIS_DIM)
    lane = np.arange(HEAD_DIM_C)
    axis_is_col = (lane // ROPE_AXIS_DIM) % 2 == 1
    idx = lane % ROPE_AXIS_DIM
    ang = np.where(axis_is_col[None, :], col[:, None], row[:, None]) * inv[idx % half][None, :]
    sign = np.where(idx < half, -1.0, 1.0)[None, :]
    return (jnp.asarray(np.cos(ang), F32), jnp.asarray(np.sin(ang) * sign, F32))


def _mixer_attn_lat(p, cache_k4, cache_v4, lambda_qk, subln_row, layer, lam_init, ymix):
    cos_t, sin_t = _rope_tables()
    q_tiles = DEC_SEQ // ATT_TQ
    q0 = N_CTX // ATT_TQ
    b0 = N_CTX // DEC_SEQ
    qblk = lambda c: pl.BlockSpec((ATT_TQ, BRANCH), lambda b, i: (q0 + b * q_tiles + i, c))
    kblk = lambda c: pl.BlockSpec((DEC_SEQ, BRANCH), lambda b, i: (b0 + b, c))
    cblk = pl.BlockSpec((None, None, PAST_LEN, BRANCH), lambda b, i: (b, layer, 0, 0))
    return pl.pallas_call(
        functools.partial(_attn_lat_kernel, lam_init),
        grid=(DEC_BATCH, q_tiles),
        in_specs=[
            qblk(C_Q), kblk(C_K), kblk(C_V), qblk(C_CG), cblk, cblk,
            pl.BlockSpec((ATT_TQ, HEAD_DIM_C), lambda b, i: (i, 0)),
            pl.BlockSpec((ATT_TQ, HEAD_DIM_C), lambda b, i: (i, 0)),
            pl.BlockSpec((DEC_SEQ, HEAD_DIM_C), lambda b, i: (0, 0)),
            pl.BlockSpec((DEC_SEQ, HEAD_DIM_C), lambda b, i: (0, 0)),
            pl.BlockSpec((None, 4, QK_HALF), lambda b, i: (layer, 0, 0)),
            pl.BlockSpec((None, 1, HEAD_DIM_C), lambda b, i: (layer, 0, 0)),
            pl.BlockSpec(memory_space=pl.ANY),
        ],
        out_specs=pl.BlockSpec((ATT_TQ, BRANCH), lambda b, i: (q0 + b * q_tiles + i, M_ATTN)),
        out_shape=jax.ShapeDtypeStruct((N_TOK, D_MIX), BF16),
        input_output_aliases={12: 0},
        scratch_shapes=[pltpu.VMEM((PAST_LEN, BRANCH), BF16), pltpu.VMEM((DEC_SEQ, BRANCH), BF16),
                        pltpu.VMEM((N_HEADS_C, V_ROWS, PAST_LEN + DEC_SEQ), BF16)],
        compiler_params=_params("arbitrary", "arbitrary"),
        name="mixer_attn_lat",
    )(p, p, p, p, cache_k4, cache_v4, cos_t, sin_t, cos_t, sin_t, lambda_qk, subln_row, ymix)


HY_ROWS = 1024
HY_PIECE = 512


def _dft_matrices(seq_len):
    n = 2 * seq_len
    f = np.arange(seq_len, dtype=np.float64)[:, None]
    s = np.arange(seq_len, dtype=np.float64)[None, :]
    theta = 2.0 * np.pi * f * s / n
    alt = np.where(np.arange(seq_len) % 2 == 0, 1.0, -1.0)
    ac = np.cos(theta)
    as_ = -np.sin(theta)
    as_[0, :] = alt
    bc = 2.0 * np.cos(theta.T) / n
    bc[:, 0] = 1.0 / n
    bs = -2.0 * np.sin(theta.T) / n
    bs[:, 0] = alt / n
    fwd = np.concatenate([ac, as_], axis=0)
    inv = np.concatenate([bc, bs], axis=1)
    return jnp.asarray(fwd, F32).astype(BF16), jnp.asarray(inv, F32).astype(BF16)


def _filter_features(seq_len):
    t_idx = np.arange(seq_len, dtype=np.float64)
    t_norm = np.linspace(0.0, 1.0, seq_len)
    bands = np.linspace(1e-4, HY_BANDS - 1, HY_BANDS)
    ang = (2.0 * math.pi * t_idx / seq_len)[:, None] * bands[None, :]
    feats = np.concatenate([t_norm[:, None], np.cos(ang), np.sin(ang)], axis=-1)
    feats = np.pad(feats, ((0, 0), (0, HY_PAD - HY_EMB)))
    deltas = np.abs(np.linspace(math.log(HY_TARGET) / HY_FAST_DECAY,
                                math.log(HY_TARGET) / HY_SLOW_DECAY, BRANCH))
    return (jnp.asarray(feats, F32), jnp.asarray(t_norm[:, None], F32),
            jnp.asarray(deltas[None, :], F32))


def _filter_kernel(seq_len, feats_ref, tn_ref, dl_ref, w1_ref, b1_ref, w2_ref, b2_ref, fr_ref,
                   w3f_ref, w3b_ref, fwd_ref, kr_ref, ki_ref, h_ref):
    sp = lambda x: _split(x, 2)

    @pl.when(pl.program_id(1) == 0)
    def _():
        fr = fr_ref[...]
        h = jnp.sin(fr * (_sdot(sp(feats_ref[...]), sp(w1_ref[...])) + b1_ref[...]))
        h_ref[...] = jnp.sin(fr * (_sdot(sp(h), sp(w2_ref[...])) + b2_ref[...]))

    h = sp(h_ref[...])
    decay = jnp.exp(-tn_ref[...] * dl_ref[...])
    row = lax.broadcasted_iota(jnp.int32, (seq_len, BRANCH), 0)
    fwd = _sdot(h, sp(w3f_ref[...])) * decay
    bwd = jnp.where(row == 0, 0.0, _sdot(h, sp(w3b_ref[...])) * decay)
    norm = (jnp.sum(jnp.abs(fwd), axis=0, keepdims=True)
            + jnp.sum(jnp.abs(bwd), axis=0, keepdims=True))
    fwd = fwd / norm
    bwd = bwd / norm
    even = fwd + bwd
    alt = jnp.where((row & 1) == 0, 1.0, -1.0)
    nyquist = jnp.sum(alt * even, axis=0, keepdims=True)
    kr_ref[...] = jnp.dot(fwd_ref[0:seq_len, :], even.astype(BF16), preferred_element_type=F32)
    ki = jnp.dot(fwd_ref[seq_len:2 * seq_len, :], (fwd - bwd).astype(BF16),
                 preferred_element_type=F32)
    ki_ref[...] = jnp.where(row == 0, nyquist, ki)


def _hyena_spectrum(seq_len, filt, fwd_mat):
    feats, t_norm, deltas = _filter_features(seq_len)
    w1, b1, w2, b2, freq, w3 = filt
    full = lambda a: pl.BlockSpec(a.shape, lambda l, o: (0,) * a.ndim)
    lyr = lambda r, n: pl.BlockSpec((None, r, n), lambda l, o: (l, 0, 0))
    out = pl.BlockSpec((None, None, seq_len, BRANCH), lambda l, o: (l, o, 0, 0))
    return pl.pallas_call(
        functools.partial(_filter_kernel, seq_len),
        grid=(DEPTH, HY_ORDER),
        in_specs=[
            full(feats), full(t_norm), full(deltas),
            lyr(HY_PAD, HY_PAD), lyr(1, HY_PAD), lyr(HY_PAD, HY_PAD), lyr(1, HY_PAD), lyr(1, HY_PAD),
            pl.BlockSpec((None, HY_PAD, BRANCH), lambda l, o: (l, 0, 2 * o)),
            pl.BlockSpec((None, HY_PAD, BRANCH), lambda l, o: (l, 0, 2 * o + 1)),
            full(fwd_mat),
        ],
        out_specs=[out, out],
        out_shape=[jax.ShapeDtypeStruct((DEPTH, HY_ORDER, seq_len, BRANCH), F32)] * 2,
        scratch_shapes=[pltpu.VMEM((seq_len, HY_PAD), F32)],
        compiler_params=_params("arbitrary", "arbitrary"),
        name=f"hyena_spectrum_{seq_len}",
    )(feats, t_norm, deltas, w1, b1, w2, b2, freq, w3, w3, fwd_mat)


def _hyena_chains(seq_len, width, x1_ref, x2_ref, hv_ref, g_ref, cw_ref, cb_ref, kr_ref, ki_ref,
                  hb_ref, fwd_ref, inv_ref, o_ref):
    row = lax.broadcasted_iota(jnp.int32, (seq_len, width), 0)
    first, last = row == 0, row == seq_len - 1

    def chain(rs, c0):
        cs = slice(c0, c0 + width)

        def short_conv(x_ref, piece):
            x = x_ref[rs, cs]
            w = cw_ref[:, piece * BRANCH + c0:piece * BRANCH + c0 + width]
            b = cb_ref[:, piece * BRANCH + c0:piece * BRANCH + c0 + width]
            prev = jnp.where(first, 0.0, pltpu.roll(x, 1, axis=0))
            nxt = jnp.where(last, 0.0, pltpu.roll(x, seq_len - 1, axis=0))
            return prev * w[0:1] + x * w[1:2] + nxt * w[2:3] + b

        z = short_conv(hv_ref, 2)
        yield
        gate_refs = (x1_ref, x2_ref)
        piece = min(seq_len, HY_PIECE)
        for order in range(HY_ORDER):
            zb = z.astype(BF16)
            y_re, y_im = [], []
            for r in range(0, seq_len, piece):
                zr = jnp.dot(fwd_ref[r:r + piece, :], zb, preferred_element_type=F32)
                yield
                zi = jnp.dot(fwd_ref[seq_len + r:seq_len + r + piece, :], zb,
                             preferred_element_type=F32)
                yield
                kr, kp = kr_ref[order, r:r + piece, cs], ki_ref[order, r:r + piece, cs]
                ki, kn = kp, kr
                if r == 0:
                    dc = lax.broadcasted_iota(jnp.int32, (piece, width), 0) == 0
                    ki = jnp.where(dc, 0.0, kp)
                    kn = jnp.where(dc, kp, kr)
                y_re.append((zr * kr - zi * ki).astype(BF16))
                y_im.append((zr * ki + zi * kn).astype(BF16))
                yield
            yf = jnp.concatenate(y_re + y_im, axis=0)
            ys = []
            for r in range(0, seq_len, piece):
                ys.append(jnp.dot(inv_ref[r:r + piece, :], yf, preferred_element_type=F32))
                yield
                if r == 0:
                    gate = short_conv(gate_refs[order], order)
            y = ys[0] if len(ys) == 1 else jnp.concatenate(ys, axis=0)
            z = gate * (y + z * hb_ref[order:order + 1, cs])
        o_ref[rs, cs] = (_silu(g_ref[rs, cs]) * z).astype(BF16)

    waiting = [chain(slice(s * seq_len, (s + 1) * seq_len), c0)
               for s in range(HY_ROWS // seq_len) for c0 in range(0, BRANCH, width)]
    running = []
    while waiting or running:
        if waiting:
            running.append(waiting.pop(0))
        for gen in list(running):
            if next(gen, "done") == "done":
                running.remove(gen)


def _hyena_kernel(x1_ref, x2_ref, hv_ref, g_ref, cw_ref, cb_ref, hb_ref,
                  krc_ref, kic_ref, fwdc_ref, invc_ref, krl_ref, kil_ref, fwdl_ref, invl_ref,
                  ymix_ref, o_ref):
    del ymix_ref
    common = (x1_ref, x2_ref, hv_ref, g_ref, cw_ref, cb_ref)
    is_ctx = pl.program_id(0) < N_CTX // HY_ROWS

    @pl.when(is_ctx)
    def _():
        _hyena_chains(SEQ, BRANCH, *common, krc_ref, kic_ref, hb_ref, fwdc_ref, invc_ref, o_ref)

    @pl.when(jnp.logical_not(is_ctx))
    def _():
        _hyena_chains(DEC_SEQ, HY_CT, *common, krl_ref, kil_ref, hb_ref, fwdl_ref, invl_ref, o_ref)


def _mixer_hyena(p, conv_w, conv_b3, hyena_bias, spec_ctx, mats_ctx, spec_lat, mats_lat, layer,
                 ymix):
    blk = lambda c: pl.BlockSpec((HY_ROWS, BRANCH), lambda i: (i, c))
    once = pl.Buffered(1)
    lyr = lambda a: pl.BlockSpec((None,) + a.shape[1:], lambda i: (layer,) + (0,) * (a.ndim - 1),
                                 pipeline_mode=once)
    full = lambda a: pl.BlockSpec(a.shape, lambda i: (0,) * a.ndim, pipeline_mode=once)
    consts = [*spec_ctx, *mats_ctx, *spec_lat, *mats_lat]
    return pl.pallas_call(
        _hyena_kernel,
        grid=(N_TOK // HY_ROWS,),
        in_specs=[blk(C_DX1), blk(C_DX2), blk(C_DV), blk(C_DG),
                  lyr(conv_w), lyr(conv_b3), lyr(hyena_bias),
                  lyr(spec_ctx[0]), lyr(spec_ctx[1]), full(mats_ctx[0]), full(mats_ctx[1]),
                  lyr(spec_lat[0]), lyr(spec_lat[1]), full(mats_lat[0]), full(mats_lat[1]),
                  pl.BlockSpec(memory_space=pl.ANY)],
        out_specs=pl.BlockSpec((HY_ROWS, BRANCH), lambda i: (i, M_HYENA)),
        out_shape=jax.ShapeDtypeStruct((N_TOK, D_MIX), BF16),
        input_output_aliases={7 + len(consts): 0},
        compiler_params=_params("arbitrary"),
        name="mixer_hyena",
    )(p, p, p, p, conv_w, conv_b3, hyena_bias, *consts, ymix)


OUT_TM = 512


def _outproj_kernel(alpha, n_x, n_out, ymix_ref, *refs):
    x_refs = refs[:n_x]
    gate_ref, w_ref, b_ref, lng_ref, lnb_ref = refs[n_x:n_x + 5]
    o_refs = refs[n_x + 5:]
    is_ctx = pl.program_id(0) < N_CTX // OUT_TM
    x = x_refs[0][...] if n_x == 1 else jnp.where(is_ctx, x_refs[0][...], x_refs[1][...])
    y = jnp.dot(ymix_ref[...], w_ref[...], preferred_element_type=F32) + b_ref[...]
    r = alpha * x + gate_ref[...] * y
    res = _layer_norm(r) * lng_ref[...] + lnb_ref[...]
    if n_out == 1:
        o_refs[0][...] = res
    else:
        @pl.when(is_ctx)
        def _():
            o_refs[0][...] = res

        @pl.when(jnp.logical_not(is_ctx))
        def _():
            o_refs[1][...] = res


def _out_projection(ymix, xs, mod4, w_out_bf, b_out3, ln_g3, ln_b3, layer, split_out):
    alpha = (2.0 * DEPTH) ** 0.25
    row = lambda i: _cond_row(i, OUT_TM)
    vec = pl.BlockSpec((None, 1, D_MODEL), lambda i: (layer, 0, 0))
    if split_out:
        out_shape = [jax.ShapeDtypeStruct((N_CTX, D_MODEL), F32),
                     jax.ShapeDtypeStruct((N_LAT, D_MODEL), F32)]
    else:
        out_shape = [jax.ShapeDtypeStruct((N_TOK, D_MODEL), F32)]
    return pl.pallas_call(
        functools.partial(_outproj_kernel, alpha, len(xs), len(out_shape)),
        grid=(N_TOK // OUT_TM,),
        in_specs=[pl.BlockSpec((OUT_TM, D_MIX), lambda i: (i, 0))] + _token_specs(xs, OUT_TM) + [
            pl.BlockSpec((None, None, 1, D_MODEL), lambda i: (layer, row(i), 0, 2)),
            pl.BlockSpec((D_MIX, D_MODEL), lambda i: (0, 0), pipeline_mode=pl.Buffered(1)),
            vec, vec, vec,
        ],
        out_specs=_token_specs(out_shape, OUT_TM),
        out_shape=out_shape,
        compiler_params=_params("arbitrary"),
        name="out_projection",
    )(ymix, *xs, mod4, w_out_bf, b_out3, ln_g3, ln_b3)


def kernel(x_prompt, x_sample, cache_k, cache_v, c, c_ctx, w_mod, b_mod, w_in, gmlp_w, gmlp_b,
           pool_w, pool_scale, lambda_qk, subln_w, conv_w, conv_b, filt_w1, filt_b1, filt_w2,
           filt_b2, filt_freq, filt_w3, hyena_bias, w_out, b_out, ln_g, ln_b):
    xs = (x_prompt.reshape(N_CTX, D_MODEL), x_sample.reshape(N_LAT, D_MODEL))
    cond = jnp.concatenate(
        [c_ctx[None, :], c, jnp.zeros((N_COND - 1 - DEC_BATCH, D_MODEL), F32)], axis=0)
    mod4 = _modulation(cond, w_mod, b_mod).reshape(DEPTH, N_COND, 1, 3 * D_MODEL)

    cache_k4 = cache_k.reshape(DEC_BATCH, DEPTH, PAST_LEN, BRANCH)
    cache_v4 = cache_v.reshape(DEC_BATCH, DEPTH, PAST_LEN, BRANCH)
    gmlp_b_rows = jnp.broadcast_to(gmlp_b[..., None], (DEPTH, N_GROUPS, CHUNK, GROUP_W))
    subln_row = subln_w.reshape(DEPTH, 1, HEAD_DIM_C)
    conv_b3 = conv_b.reshape(DEPTH, 1, 3 * BRANCH)
    b_out3 = b_out.reshape(DEPTH, 1, D_MODEL)
    ln_g3 = ln_g.reshape(DEPTH, 1, D_MODEL)
    ln_b3 = ln_b.reshape(DEPTH, 1, D_MODEL)

    pad_h = HY_PAD - HY_HIDDEN
    filt = (
        jnp.pad(filt_w1, ((0, 0), (0, HY_PAD - HY_EMB), (0, pad_h))),
        jnp.pad(filt_b1, ((0, 0), (0, pad_h))).reshape(DEPTH, 1, HY_PAD),
        jnp.pad(filt_w2, ((0, 0), (0, pad_h), (0, pad_h))),
        jnp.pad(filt_b2, ((0, 0), (0, pad_h))).reshape(DEPTH, 1, HY_PAD),
        jnp.pad(filt_freq, ((0, 0), (0, pad_h))).reshape(DEPTH, 1, HY_PAD),
        jnp.pad(filt_w3, ((0, 0), (0, pad_h), (0, 0))),
    )
    mats_ctx = _dft_matrices(SEQ)
    mats_lat = _dft_matrices(DEC_SEQ)
    spec_ctx = _hyena_spectrum(SEQ, filt, mats_ctx[0])
    spec_lat = _hyena_spectrum(DEC_SEQ, filt, mats_lat[0])

    caches = None
    for layer in range(DEPTH):
        lam_init = 0.8 - 0.6 * math.exp(-0.3 * layer)
        p = _in_projection(xs, mod4, w_in, layer)
        ymix, w_out_bf = _mixer_gmlp_pool(p, gmlp_w[layer], gmlp_b_rows[layer], pool_w[layer],
                                          pool_scale[layer].reshape(1, BRANCH), w_out, layer)
        ymix, new_k, new_v = _mixer_attn_ctx(p, lambda_qk, subln_row, layer, lam_init, ymix, caches)
        caches = (new_k, new_v)
        ymix = _mixer_attn_lat(p, cache_k4, cache_v4, lambda_qk, subln_row, layer, lam_init, ymix)
        ymix = _mixer_hyena(p, conv_w, conv_b3, hyena_bias, spec_ctx, mats_ctx, spec_lat, mats_lat,
                            layer, ymix)
        xs = tuple(_out_projection(ymix, xs, mod4, w_out_bf, b_out3, ln_g3, ln_b3, layer,
                                   split_out=layer == DEPTH - 1))

    y_prompt = xs[0].reshape(BATCH, SEQ, D_MODEL)
    y_sample = xs[1].reshape(DEC_BATCH, DEC_SEQ, D_MODEL)
    new_k, new_v = caches
    return (y_prompt, y_sample,
            new_k.reshape(BATCH, DEPTH, SEQ, N_HEADS_C, 2, QK_HALF),
            new_v.reshape(BATCH, DEPTH, SEQ, N_HEADS_C, HEAD_DIM_C))
```

```python
import functools
import math

import numpy as np
import jax
import jax.numpy as jnp
from jax import lax
from jax.experimental import pallas as pl
from jax.experimental.pallas import tpu as pltpu

F32 = jnp.float32
BF16 = jnp.bfloat16

D_MODEL = 2048
BATCH = 16
SEQ = 256
DEPTH = 2
DEC_BATCH = 4
DEC_SEQ = 1024
PAST_LEN = 512
GRID_W = 64
BRANCH = 512
N_GROUPS = 4
GROUP_W = 128
CHUNK = 128
POOL_WINDOWS = (2, 4, 8, 16)
N_HEADS_C = 4
HEAD_DIM_C = 128
QK_HALF = 64
ROPE_AXIS_DIM = 32
ROPE_BASE = 10000.0
HY_BANDS = 16
HY_EMB = 33
HY_HIDDEN = 64
HY_ORDER = 2
HY_FAST_DECAY = 0.3
HY_SLOW_DECAY = 1.5
HY_TARGET = 1e-2
N_IN_PIECES = 13
D_IN = N_IN_PIECES * BRANCH
LN_EPS = 1e-6

N_CTX = BATCH * SEQ
N_LAT = DEC_BATCH * DEC_SEQ
N_TOK = N_CTX + N_LAT
N_COND = 8
LANES = 128
HY_PAD = LANES
HY_CT = 256
VMEM_LIMIT = 56 * 1024 * 1024

(C_AU, C_AV, C_AG, C_BX, C_BG, C_Q, C_K, C_V, C_CG, C_DX1, C_DX2, C_DV, C_DG) = range(13)
D_MIX = 4 * BRANCH
(M_GMLP, M_POOL, M_ATTN, M_HYENA) = range(4)


def _silu(x):
    return x * jax.nn.sigmoid(x)


def _bdot(a, b):
    return jnp.dot(a.astype(BF16), b.astype(BF16), preferred_element_type=F32)


def _split(x, n_terms):
    hi = x.astype(BF16)
    if n_terms == 1:
        return (hi,)
    return (hi, (x - hi.astype(F32)).astype(BF16))


def _sdot(a_terms, b_terms):
    acc = jnp.dot(a_terms[0], b_terms[0], preferred_element_type=F32)
    if len(a_terms) > 1:
        acc = acc + jnp.dot(a_terms[1], b_terms[0], preferred_element_type=F32)
    if len(b_terms) > 1:
        acc = acc + jnp.dot(a_terms[0], b_terms[1], preferred_element_type=F32)
    return acc


def _layer_norm(x):
    mu = jnp.mean(x, axis=-1, keepdims=True)
    xc = x - mu
    var = jnp.mean(xc * xc, axis=-1, keepdims=True)
    return xc * lax.rsqrt(var + LN_EPS)


def _cond_row(tile, rows_per_tile):
    n_ctx_tiles = N_CTX // rows_per_tile
    tiles_per_batch = DEC_SEQ // rows_per_tile
    return jnp.where(tile < n_ctx_tiles, 0, 1 + (tile - n_ctx_tiles) // tiles_per_batch)


def _params(*semantics):
    return pltpu.CompilerParams(dimension_semantics=semantics, vmem_limit_bytes=VMEM_LIMIT)


MOD_TN = 512


def _mod_kernel(c_ref, w_ref, b_ref, o_ref):
    o_ref[...] = _bdot(_silu(c_ref[...]), w_ref[...]) + b_ref[...]


def _modulation(cond, w_mod, b_mod):
    n = 3 * D_MODEL
    return pl.pallas_call(
        _mod_kernel,
        grid=(DEPTH, n // MOD_TN),
        in_specs=[
            pl.BlockSpec((N_COND, D_MODEL), lambda l, j: (0, 0)),
            pl.BlockSpec((None, D_MODEL, MOD_TN), lambda l, j: (l, 0, j)),
            pl.BlockSpec((None, 1, MOD_TN), lambda l, j: (l, 0, j)),
        ],
        out_specs=pl.BlockSpec((None, N_COND, MOD_TN), lambda l, j: (l, 0, j)),
        out_shape=jax.ShapeDtypeStruct((DEPTH, N_COND, n), F32),
        compiler_params=_params("arbitrary", "arbitrary"),
        name="modulation",
    )(cond, w_mod, b_mod.reshape(DEPTH, 1, n))


IN_TM = 1024
IN_TN = 512
IN_LN_ROWS = 256


def _token_specs(xs, tm, tile_of=lambda i: i):
    n_ctx_tiles = N_CTX // tm
    if len(xs) == 1:
        maps = [lambda i, *_: (tile_of(i), 0)]
    else:
        maps = [lambda i, *_: (jnp.minimum(tile_of(i), n_ctx_tiles - 1), 0),
                lambda i, *_: (jnp.maximum(tile_of(i) - n_ctx_tiles, 0), 0)]
    return [pl.BlockSpec((tm, D_MODEL), m) for m in maps]


def _inproj_kernel(n_x, tile0, cast_w, *refs):
    x_refs = refs[:n_x]
    scale_ref, shift_ref, w_ref = refs[n_x:n_x + 3]
    if cast_w:
        o_ref, wb_ref, h_ref = refs[n_x + 3:]
    else:
        _, o_ref, h_ref = refs[n_x + 3:]

    def weights():
        if not cast_w:
            return w_ref[...]
        w = w_ref[...].astype(BF16)
        wb_ref[...] = w
        return w

    def first_column_step(x_ref):
        w = weights()

        def norm(r):
            rows = slice(r, r + IN_LN_ROWS)
            h = _layer_norm(x_ref[rows, :]) * (1.0 + scale_ref[...]) + shift_ref[...]
            h = h.astype(BF16)
            h_ref[rows, :] = h
            return h

        h = norm(0)
        for r in range(0, IN_TM, IN_LN_ROWS):
            o_ref[r:r + IN_LN_ROWS, :] = jnp.dot(h, w, preferred_element_type=F32)
            if r + IN_LN_ROWS < IN_TM:
                h = norm(r + IN_LN_ROWS)

    first = pl.program_id(1) == 0
    if n_x == 1:
        pl.when(first)(lambda: first_column_step(x_refs[0]))
    else:
        is_ctx = pl.program_id(0) + tile0 < N_CTX // IN_TM
        pl.when(jnp.logical_and(first, is_ctx))(lambda: first_column_step(x_refs[0]))
        pl.when(jnp.logical_and(first, jnp.logical_not(is_ctx)))(
            lambda: first_column_step(x_refs[1]))

    @pl.when(jnp.logical_not(first))
    def _():
        o_ref[...] = jnp.dot(h_ref[...], weights(), preferred_element_type=F32)


def _in_projection(xs, mod4, w_in, layer):
    n_col = D_IN // IN_TN

    def mod_spec(tile0, piece):
        return pl.BlockSpec((None, None, 1, D_MODEL),
                            lambda i, j: (layer, _cond_row(i + tile0, IN_TM), 0, piece))

    scratch = [pltpu.VMEM((IN_TM, D_MODEL), BF16)]
    p_shape = jax.ShapeDtypeStruct((N_TOK, D_IN), F32)
    p, w_bf = pl.pallas_call(
        functools.partial(_inproj_kernel, 1, 0, True),
        grid=(1, n_col),
        in_specs=_token_specs(xs[:1], IN_TM) + [
            mod_spec(0, 1), mod_spec(0, 0),
            pl.BlockSpec((None, D_MODEL, IN_TN), lambda i, j: (layer, 0, j)),
        ],
        out_specs=[pl.BlockSpec((IN_TM, IN_TN), lambda i, j: (i, j)),
                   pl.BlockSpec((D_MODEL, IN_TN), lambda i, j: (0, j))],
        out_shape=[p_shape, jax.ShapeDtypeStruct((D_MODEL, D_IN), BF16)],
        scratch_shapes=scratch,
        compiler_params=_params("arbitrary", "arbitrary"),
        name="in_projection_first",
    )(xs[0], mod4, mod4, w_in)
    return pl.pallas_call(
        functools.partial(_inproj_kernel, len(xs), 1, False),
        grid=(N_TOK // IN_TM - 1, n_col),
        in_specs=_token_specs(xs, IN_TM, lambda i: i + 1) + [
            mod_spec(1, 1), mod_spec(1, 0),
            pl.BlockSpec((D_MODEL, IN_TN), lambda i, j: (0, j)),
            pl.BlockSpec(memory_space=pl.ANY),
        ],
        out_specs=pl.BlockSpec((IN_TM, IN_TN), lambda i, j: (i + 1, j)),
        out_shape=p_shape,
        input_output_aliases={len(xs) + 3: 0},
        scratch_shapes=scratch,
        compiler_params=_params("arbitrary", "arbitrary"),
        name="in_projection",
    )(*xs, mod4, mod4, w_bf, p)


AB_TM = 1024


def _gmlp_tasks(u_ref, v_ref, g_ref, w_ref, b_ref, o_ref):
    ws = [w_ref[g].astype(BF16) for g in range(N_GROUPS)]

    def chunk(r):
        rows = slice(r, r + CHUNK)
        vn = _layer_norm(v_ref[rows, :]).astype(BF16)
        for g in range(N_GROUPS):
            cols = slice(g * GROUP_W, (g + 1) * GROUP_W)
            mixed = jnp.dot(ws[g], vn[:, cols], preferred_element_type=F32) + b_ref[g]
            o_ref[rows, cols] = (_silu(g_ref[rows, cols]) * u_ref[rows, cols] * mixed).astype(BF16)

    return [functools.partial(chunk, r) for r in range(0, AB_TM, CHUNK)]


POOL_BAND_GROUPS = (2, 3)
POOL_BAND_ROWS = 256
POOL_BAND_HALO = 128


def _window_sum_shifts(p, win, pos, seq_len):
    acc = p
    for d in range(-(win // 2), win // 2):
        if d == 0:
            continue
        shifted = pltpu.roll(p, (-d) % AB_TM, axis=0)
        valid = (pos >= -d) if d < 0 else (pos < seq_len - d)
        acc = acc + jnp.where(valid, shifted, 0.0)
    return acc


def _window_sum_band_tasks(p, band_ref, pieces):
    hi = p.astype(BF16)
    r1 = p - hi.astype(F32)
    mid = r1.astype(BF16)
    lo = (r1 - mid.astype(F32)).astype(BF16)
    terms = jnp.concatenate([hi, mid, lo], axis=1)

    def piece(r):
        k0, k1 = max(0, r - POOL_BAND_HALO), min(AB_TM, r + POOL_BAND_ROWS + POOL_BAND_HALO)
        s = jnp.dot(band_ref[r:r + POOL_BAND_ROWS, k0:k1], terms[k0:k1, :],
                    preferred_element_type=F32)
        pieces.append(s[:, :GROUP_W] + s[:, GROUP_W:2 * GROUP_W] + s[:, 2 * GROUP_W:])

    return [functools.partial(piece, r) for r in range(0, AB_TM, POOL_BAND_ROWS)]


def _pool_finish(g, acc, x_ref, g_ref, w_ref, s_ref, pos, seq_len, o_ref):
    cols = slice(g * GROUP_W, (g + 1) * GROUP_W)
    win = POOL_WINDOWS[g]
    count = jnp.minimum(pos + win // 2, seq_len) - jnp.maximum(pos - win // 2, 0)
    pooled = acc / count.astype(F32)
    y = _bdot(pooled - x_ref[:, cols], w_ref[g])
    o_ref[:, BRANCH + g * GROUP_W:BRANCH + (g + 1) * GROUP_W] = (
        _silu(g_ref[:, cols]) * (y * s_ref[:, cols])).astype(BF16)


def _mixer_ab_kernel(au_ref, av_ref, ag_ref, bx_ref, bg_ref, gw_ref, gb_ref, pw_ref, ps_ref,
                     band_a_ref, band_b_ref, o_ref):
    seq_len = jnp.where(pl.program_id(0) < N_CTX // AB_TM, SEQ, DEC_SEQ)
    pos = lax.broadcasted_iota(jnp.int32, (AB_TM, GROUP_W), 0) & (seq_len - 1)
    pool_refs = (bx_ref, bg_ref, pw_ref, ps_ref, pos, seq_len, o_ref)
    band_sums = {g: [] for g in POOL_BAND_GROUPS}
    mxu_tasks = [t for g, band in zip(POOL_BAND_GROUPS, (band_a_ref, band_b_ref))
                 for t in _window_sum_band_tasks(bx_ref[:, g * GROUP_W:(g + 1) * GROUP_W], band,
                                                 band_sums[g])]
    vpu_tasks = _gmlp_tasks(au_ref, av_ref, ag_ref, gw_ref, gb_ref, o_ref)
    for i in range(max(len(mxu_tasks), len(vpu_tasks))):
        for tasks in (mxu_tasks, vpu_tasks):
            if i < len(tasks):
                tasks[i]()
    for g, win in enumerate(POOL_WINDOWS):
        if g not in POOL_BAND_GROUPS:
            acc = _window_sum_shifts(bx_ref[:, g * GROUP_W:(g + 1) * GROUP_W], win, pos, seq_len)
            _pool_finish(g, acc, *pool_refs)
    for g in POOL_BAND_GROUPS:
        _pool_finish(g, jnp.concatenate(band_sums[g], axis=0), *pool_refs)


def _pool_bands():
    t = np.arange(AB_TM)
    bands = []
    for g in POOL_BAND_GROUPS:
        half = POOL_WINDOWS[g] // 2
        d = t[None, :] - t[:, None]
        near = (d >= -half) & (d < half)
        per_path = [near & ((t[None, :] // L) == (t[:, None] // L)) for L in (SEQ, DEC_SEQ)]
        bands.append(jnp.asarray(np.stack(per_path), F32).astype(BF16))
    return bands


def _mixer_gmlp_pool(p, gmlp_w, gmlp_b_rows, pool_w, pool_scale_row):
    assert M_POOL == M_GMLP + 1
    n_steps = N_TOK // AB_TM
    blk = lambda c: pl.BlockSpec((AB_TM, BRANCH), lambda i: (i, c))
    full = lambda a: pl.BlockSpec(a.shape, lambda i: (0,) * a.ndim)
    band = pl.BlockSpec((None, AB_TM, AB_TM), lambda i: (jnp.where(i < N_CTX // AB_TM, 0, 1), 0, 0))
    return pl.pallas_call(
        _mixer_ab_kernel,
        grid=(n_steps,),
        in_specs=[blk(C_AU), blk(C_AV), blk(C_AG), blk(C_BX), blk(C_BG),
                  full(gmlp_w), full(gmlp_b_rows), full(pool_w), full(pool_scale_row), band, band],
        out_specs=pl.BlockSpec((AB_TM, 2 * BRANCH), lambda i: (i, M_GMLP // 2)),
        out_shape=jax.ShapeDtypeStruct((N_TOK, D_MIX), BF16),
        compiler_params=_params("arbitrary"),
        name="mixer_gmlp_pool",
    )(p, p, p, p, p, gmlp_w, gmlp_b_rows, pool_w, pool_scale_row, *_pool_bands())


ATT_TQ = 256
V_ROWS = HEAD_DIM_C + 16
NT_DIMS = (((1,), (1,)), ((), ()))


def _lambda(lam_ref, lam_init):
    lq = lam_ref[...]
    a = jnp.sum(lq[0:1] * lq[1:2], axis=-1, keepdims=True)
    b = jnp.sum(lq[2:3] * lq[3:4], axis=-1, keepdims=True)
    return jnp.exp(a) - jnp.exp(b) + lam_init


def _map_masks():
    lane = lax.broadcasted_iota(jnp.int32, (1, HEAD_DIM_C), 1)
    m0 = (lane < QK_HALF).astype(F32)
    return m0, 1.0 - m0


def _scores_t(q, keys):
    return [lax.dot_general(k, q, NT_DIMS, preferred_element_type=F32) for k in keys]


def _softmax_v_t(s, vals_t):
    m = functools.reduce(jnp.maximum, [jnp.max(x, axis=0, keepdims=True) for x in s])
    acc = functools.reduce(
        jnp.add, [jnp.dot(v, jnp.exp2(x - m).astype(BF16), preferred_element_type=F32)
                  for x, v in zip(s, vals_t)])
    return acc[:HEAD_DIM_C] * (1.0 / acc[HEAD_DIM_C:HEAD_DIM_C + 1])


def _with_ones_rows(v_t):
    ones = jnp.ones((V_ROWS - HEAD_DIM_C, v_t.shape[1]), BF16)
    return jnp.concatenate([v_t.astype(BF16), ones], axis=0)


def _diff_attention(q_of, keys_of, vals_t_of, gate_of, store, lam, lam_init, subln):
    masks = _map_masks()
    tasks = [(h, mp) for h in range(N_HEADS_C) for mp in range(2)]
    per_head = {}

    def operands(h):
        if h not in per_head:
            q = q_of(h) * (QK_HALF ** -0.5 * math.log2(math.e))
            per_head[h] = (q, keys_of(h), vals_t_of(h))
        return per_head[h]

    def scores(h, mp):
        q, keys, _ = operands(h)
        return _scores_t((q * masks[mp]).astype(BF16), keys)

    nxt = scores(*tasks[0])
    outs = []
    for i, (h, mp) in enumerate(tasks):
        cur = nxt
        if i + 1 < len(tasks):
            nxt = scores(*tasks[i + 1])
        outs.append(_softmax_v_t(cur, operands(h)[2]))
        if mp == 1:
            o = (outs[0] - lam * outs[1]).T
            outs = []
            o = o * lax.rsqrt(jnp.mean(o * o, axis=-1, keepdims=True) + 1e-5)
            o = o * subln * (1.0 - lam_init)
            store(h, _silu(gate_of(h)) * o)


def _head_cols(h):
    return slice(h * HEAD_DIM_C, (h + 1) * HEAD_DIM_C)


def _attn_ctx_kernel(lam_init, q_ref, k_ref, v_ref, g_ref, lam_ref, sw_ref, *rest):
    o_ref, ko_ref, vo_ref = rest[-3:]
    ko_ref[...] = k_ref[...]
    vo_ref[...] = v_ref[...]

    def store(h, y):
        o_ref[:, _head_cols(h)] = y.astype(BF16)

    _diff_attention(
        lambda h: q_ref[:, _head_cols(h)],
        lambda h: [k_ref[:, _head_cols(h)].astype(BF16)],
        lambda h: [_with_ones_rows(v_ref[:, _head_cols(h)].T)],
        lambda h: g_ref[:, _head_cols(h)],
        store, _lambda(lam_ref, lam_init), lam_init, sw_ref[...])


def _mixer_attn_ctx(p, lambda_qk, subln_row, layer, lam_init, ymix, caches):
    blk = lambda c: pl.BlockSpec((SEQ, BRANCH), lambda b: (b, c))
    any_spec = pl.BlockSpec(memory_space=pl.ANY)
    cache_spec = pl.BlockSpec((None, None, SEQ, BRANCH), lambda b: (b, layer, 0, 0))
    cache_shape = jax.ShapeDtypeStruct((BATCH, DEPTH, SEQ, BRANCH), F32)
    in_specs = [
        blk(C_Q), blk(C_K), blk(C_V), blk(C_CG),
        pl.BlockSpec((None, 4, QK_HALF), lambda b: (layer, 0, 0)),
        pl.BlockSpec((None, 1, HEAD_DIM_C), lambda b: (layer, 0, 0)),
        any_spec,
    ]
    args = [p, p, p, p, lambda_qk, subln_row, ymix]
    aliases = {6: 0}
    if caches is not None:
        in_specs += [any_spec, any_spec]
        args += list(caches)
        aliases.update({7: 1, 8: 2})
    return pl.pallas_call(
        functools.partial(_attn_ctx_kernel, lam_init),
        grid=(BATCH,),
        in_specs=in_specs,
        out_specs=[pl.BlockSpec((SEQ, BRANCH), lambda b: (b, M_ATTN)), cache_spec, cache_spec],
        out_shape=[jax.ShapeDtypeStruct((N_TOK, D_MIX), BF16), cache_shape, cache_shape],
        input_output_aliases=aliases,
        compiler_params=_params("arbitrary"),
        name="mixer_attn_ctx",
    )(*args)


def _rope(x, cos, sin_signed):
    lane = lax.broadcasted_iota(jnp.int32, x.shape, 1)
    first_half = (lane & (ROPE_AXIS_DIM - 1)) < (ROPE_AXIS_DIM // 2)
    half = ROPE_AXIS_DIM // 2
    partner = jnp.where(first_half,
                        pltpu.roll(x, x.shape[1] - half, axis=1),
                        pltpu.roll(x, half, axis=1))
    return x * cos + partner * sin_signed


def _attn_lat_kernel(lam_init, q_ref, k_ref, v_ref, g_ref, ck_ref, cv_ref, cosq_ref, sinq_ref,
                     cosk_ref, sink_ref, lam_ref, sw_ref, wo_ref, ymix_ref, o_ref, wo_bf_ref,
                     kc_ref, kr_ref, vt_ref):
    del ymix_ref
    wo_bf_ref[...] = wo_ref[...].astype(BF16)

    @pl.when(pl.program_id(1) == 0)
    def _():
        kc_ref[...] = ck_ref[...].astype(BF16)
        for h in range(N_HEADS_C):
            cols = slice(h * HEAD_DIM_C, (h + 1) * HEAD_DIM_C)
            kr_ref[:, cols] = _rope(k_ref[:, cols], cosk_ref[...], sink_ref[...]).astype(BF16)
            vt_ref[h, :, 0:PAST_LEN] = _with_ones_rows(cv_ref[:, cols].T)
            vt_ref[h, :, PAST_LEN:PAST_LEN + DEC_SEQ] = _with_ones_rows(v_ref[:, cols].T)

    def store(h, y):
        o_ref[:, _head_cols(h)] = y.astype(BF16)

    _diff_attention(
        lambda h: _rope(q_ref[:, _head_cols(h)], cosq_ref[...], sinq_ref[...]),
        lambda h: [kc_ref[:, _head_cols(h)], kr_ref[:, _head_cols(h)]],
        lambda h: [vt_ref[h, :, 0:PAST_LEN], vt_ref[h, :, PAST_LEN:PAST_LEN + DEC_SEQ]],
        lambda h: g_ref[:, _head_cols(h)],
        store, _lambda(lam_ref, lam_init), lam_init, sw_ref[...])


def _rope_tables():
    pos = np.arange(DEC_SEQ)
    row = (pos // GRID_W).astype(np.float64)
    col = (pos % GRID_W).astype(np.float64)
    half = ROPE_AXIS_DIM // 2
    inv = ROPE_BASE ** (-np.arange(0, ROPE_AXIS_DIM, 2, dtype=np.float64) / ROPE_AXIS_DIM)
    lane = np.arange(HEAD_DIM_C)
    axis_is_col = (lane // ROPE_AXIS_DIM) % 2 == 1
    idx = lane % ROPE_AXIS_DIM
    ang = np.where(axis_is_col[None, :], col[:, None], row[:, None]) * inv[idx % half][None, :]
    sign = np.where(idx < half, -1.0, 1.0)[None, :]
    return (jnp.asarray(np.cos(ang), F32), jnp.asarray(np.sin(ang) * sign, F32))


def _mixer_attn_lat(p, cache_k4, cache_v4, lambda_qk, subln_row, layer, lam_init, w_out, ymix):
    cos_t, sin_t = _rope_tables()
    q_tiles = DEC_SEQ // ATT_TQ
    q0 = N_CTX // ATT_TQ
    b0 = N_CTX // DEC_SEQ
    wo_rows = D_MIX // (DEC_BATCH * q_tiles)
    qblk = lambda c: pl.BlockSpec((ATT_TQ, BRANCH), lambda b, i: (q0 + b * q_tiles + i, c))
    kblk = lambda c: pl.BlockSpec((DEC_SEQ, BRANCH), lambda b, i: (b0 + b, c))
    cblk = pl.BlockSpec((None, None, PAST_LEN, BRANCH), lambda b, i: (b, layer, 0, 0))
    return pl.pallas_call(
        functools.partial(_attn_lat_kernel, lam_init),
        grid=(DEC_BATCH, q_tiles),
        in_specs=[
            qblk(C_Q), kblk(C_K), kblk(C_V), qblk(C_CG), cblk, cblk,
            pl.BlockSpec((ATT_TQ, HEAD_DIM_C), lambda b, i: (i, 0)),
            pl.BlockSpec((ATT_TQ, HEAD_DIM_C), lambda b, i: (i, 0)),
            pl.BlockSpec((DEC_SEQ, HEAD_DIM_C), lambda b, i: (0, 0)),
            pl.BlockSpec((DEC_SEQ, HEAD_DIM_C), lambda b, i: (0, 0)),
            pl.BlockSpec((None, 4, QK_HALF), lambda b, i: (layer, 0, 0)),
            pl.BlockSpec((None, 1, HEAD_DIM_C), lambda b, i: (layer, 0, 0)),
            pl.BlockSpec((None, wo_rows, D_MODEL), lambda b, i: (layer, b * q_tiles + i, 0)),
            pl.BlockSpec(memory_space=pl.ANY),
        ],
        out_specs=[pl.BlockSpec((ATT_TQ, BRANCH), lambda b, i: (q0 + b * q_tiles + i, M_ATTN)),
                   pl.BlockSpec((wo_rows, D_MODEL), lambda b, i: (b * q_tiles + i, 0))],
        out_shape=[jax.ShapeDtypeStruct((N_TOK, D_MIX), BF16),
                   jax.ShapeDtypeStruct((D_MIX, D_MODEL), BF16)],
        input_output_aliases={13: 0},
        scratch_shapes=[pltpu.VMEM((PAST_LEN, BRANCH), BF16), pltpu.VMEM((DEC_SEQ, BRANCH), BF16),
                        pltpu.VMEM((N_HEADS_C, V_ROWS, PAST_LEN + DEC_SEQ), BF16)],
        compiler_params=_params("arbitrary", "arbitrary"),
        name="mixer_attn_lat",
    )(p, p, p, p, cache_k4, cache_v4, cos_t, sin_t, cos_t, sin_t, lambda_qk, subln_row, w_out, ymix)


HY_ROWS = 1024
HY_PIECE = 512


def _dft_matrices(seq_len):
    n = 2 * seq_len
    f = np.arange(seq_len, dtype=np.float64)[:, None]
    s = np.arange(seq_len, dtype=np.float64)[None, :]
    theta = 2.0 * np.pi * f * s / n
    alt = np.where(np.arange(seq_len) % 2 == 0, 1.0, -1.0)
    ac = np.cos(theta)
    as_ = -np.sin(theta)
    as_[0, :] = alt
    bc = 2.0 * np.cos(theta.T) / n
    bc[:, 0] = 1.0 / n
    bs = -2.0 * np.sin(theta.T) / n
    bs[:, 0] = alt / n
    fwd = np.concatenate([ac, as_], axis=0)
    inv = np.concatenate([bc, bs], axis=1)
    return jnp.asarray(fwd, F32).astype(BF16), jnp.asarray(inv, F32).astype(BF16)


def _filter_features(seq_len):
    t_idx = np.arange(seq_len, dtype=np.float64)
    t_norm = np.linspace(0.0, 1.0, seq_len)
    bands = np.linspace(1e-4, HY_BANDS - 1, HY_BANDS)
    ang = (2.0 * math.pi * t_idx / seq_len)[:, None] * bands[None, :]
    feats = np.concatenate([t_norm[:, None], np.cos(ang), np.sin(ang)], axis=-1)
    feats = np.pad(feats, ((0, 0), (0, HY_PAD - HY_EMB)))
    deltas = np.abs(np.linspace(math.log(HY_TARGET) / HY_FAST_DECAY,
                                math.log(HY_TARGET) / HY_SLOW_DECAY, BRANCH))
    return (jnp.asarray(feats, F32), jnp.asarray(t_norm[:, None], F32),
            jnp.asarray(deltas[None, :], F32))


def _filter_kernel(seq_len, feats_ref, tn_ref, dl_ref, w1_ref, b1_ref, w2_ref, b2_ref, fr_ref,
                   w3f_ref, w3b_ref, fwd_ref, kr_ref, ki_ref, h_ref):
    sp = lambda x: _split(x, 2)

    @pl.when(pl.program_id(1) == 0)
    def _():
        fr = fr_ref[...]
        h = jnp.sin(fr * (_sdot(sp(feats_ref[...]), sp(w1_ref[...])) + b1_ref[...]))
        h_ref[...] = jnp.sin(fr * (_sdot(sp(h), sp(w2_ref[...])) + b2_ref[...]))

    h = sp(h_ref[...])
    decay = jnp.exp(-tn_ref[...] * dl_ref[...])
    row = lax.broadcasted_iota(jnp.int32, (seq_len, BRANCH), 0)
    fwd = _sdot(h, sp(w3f_ref[...])) * decay
    bwd = jnp.where(row == 0, 0.0, _sdot(h, sp(w3b_ref[...])) * decay)
    norm = (jnp.sum(jnp.abs(fwd), axis=0, keepdims=True)
            + jnp.sum(jnp.abs(bwd), axis=0, keepdims=True))
    fwd = fwd / norm
    bwd = bwd / norm
    even = fwd + bwd
    alt = jnp.where((row & 1) == 0, 1.0, -1.0)
    nyquist = jnp.sum(alt * even, axis=0, keepdims=True)
    kr_ref[...] = jnp.dot(fwd_ref[0:seq_len, :], even.astype(BF16), preferred_element_type=F32)
    ki = jnp.dot(fwd_ref[seq_len:2 * seq_len, :], (fwd - bwd).astype(BF16),
                 preferred_element_type=F32)
    ki_ref[...] = jnp.where(row == 0, nyquist, ki)


def _hyena_spectrum(seq_len, filt, fwd_mat):
    feats, t_norm, deltas = _filter_features(seq_len)
    w1, b1, w2, b2, freq, w3 = filt
    full = lambda a: pl.BlockSpec(a.shape, lambda l, o: (0,) * a.ndim)
    lyr = lambda r, n: pl.BlockSpec((None, r, n), lambda l, o: (l, 0, 0))
    out = pl.BlockSpec((None, None, seq_len, BRANCH), lambda l, o: (l, o, 0, 0))
    return pl.pallas_call(
        functools.partial(_filter_kernel, seq_len),
        grid=(DEPTH, HY_ORDER),
        in_specs=[
            full(feats), full(t_norm), full(deltas),
            lyr(HY_PAD, HY_PAD), lyr(1, HY_PAD), lyr(HY_PAD, HY_PAD), lyr(1, HY_PAD), lyr(1, HY_PAD),
            pl.BlockSpec((None, HY_PAD, BRANCH), lambda l, o: (l, 0, 2 * o)),
            pl.BlockSpec((None, HY_PAD, BRANCH), lambda l, o: (l, 0, 2 * o + 1)),
            full(fwd_mat),
        ],
        out_specs=[out, out],
        out_shape=[jax.ShapeDtypeStruct((DEPTH, HY_ORDER, seq_len, BRANCH), F32)] * 2,
        scratch_shapes=[pltpu.VMEM((seq_len, HY_PAD), F32)],
        compiler_params=_params("arbitrary", "arbitrary"),
        name=f"hyena_spectrum_{seq_len}",
    )(feats, t_norm, deltas, w1, b1, w2, b2, freq, w3, w3, fwd_mat)


def _hyena_chains(seq_len, width, x1_ref, x2_ref, hv_ref, g_ref, cw_ref, cb_ref, kr_ref, ki_ref,
                  hb_ref, fwd_ref, inv_ref, o_ref):
    row = lax.broadcasted_iota(jnp.int32, (seq_len, width), 0)
    first, last = row == 0, row == seq_len - 1

    def chain(rs, c0):
        cs = slice(c0, c0 + width)

        def short_conv(x_ref, piece):
            x = x_ref[rs, cs]
            w = cw_ref[:, piece * BRANCH + c0:piece * BRANCH + c0 + width]
            b = cb_ref[:, piece * BRANCH + c0:piece * BRANCH + c0 + width]
            prev = jnp.where(first, 0.0, pltpu.roll(x, 1, axis=0))
            nxt = jnp.where(last, 0.0, pltpu.roll(x, seq_len - 1, axis=0))
            return prev * w[0:1] + x * w[1:2] + nxt * w[2:3] + b

        z = short_conv(hv_ref, 2)
        yield
        gate_refs = (x1_ref, x2_ref)
        piece = min(seq_len, HY_PIECE)
        for order in range(HY_ORDER):
            zb = z.astype(BF16)
            y_re, y_im = [], []
            for r in range(0, seq_len, piece):
                zr = jnp.dot(fwd_ref[r:r + piece, :], zb, preferred_element_type=F32)
                yield
                zi = jnp.dot(fwd_ref[seq_len + r:seq_len + r + piece, :], zb,
                             preferred_element_type=F32)
                yield
                kr, kp = kr_ref[order, r:r + piece, cs], ki_ref[order, r:r + piece, cs]
                ki, kn = kp, kr
                if r == 0:
                    dc = lax.broadcasted_iota(jnp.int32, (piece, width), 0) == 0
                    ki = jnp.where(dc, 0.0, kp)
                    kn = jnp.where(dc, kp, kr)
                y_re.append((zr * kr - zi * ki).astype(BF16))
                y_im.append((zr * ki + zi * kn).astype(BF16))
                yield
            yf = jnp.concatenate(y_re + y_im, axis=0)
            ys = []
            for r in range(0, seq_len, piece):
                ys.append(jnp.dot(inv_ref[r:r + piece, :], yf, preferred_element_type=F32))
                yield
                if r == 0:
                    gate = short_conv(gate_refs[order], order)
            y = ys[0] if len(ys) == 1 else jnp.concatenate(ys, axis=0)
            z = gate * (y + z * hb_ref[order:order + 1, cs])
        o_ref[rs, cs] = (_silu(g_ref[rs, cs]) * z).astype(BF16)

    waiting = [chain(slice(s * seq_len, (s + 1) * seq_len), c0)
               for s in range(HY_ROWS // seq_len) for c0 in range(0, BRANCH, width)]
    running = []
    while waiting or running:
        if waiting:
            running.append(waiting.pop(0))
        for gen in list(running):
            if next(gen, "done") == "done":
                running.remove(gen)


def _hyena_kernel(x1_ref, x2_ref, hv_ref, g_ref, cw_ref, cb_ref, hb_ref,
                  krc_ref, kic_ref, fwdc_ref, invc_ref, krl_ref, kil_ref, fwdl_ref, invl_ref,
                  ymix_ref, o_ref):
    del ymix_ref
    common = (x1_ref, x2_ref, hv_ref, g_ref, cw_ref, cb_ref)
    is_ctx = pl.program_id(0) < N_CTX // HY_ROWS

    @pl.when(is_ctx)
    def _():
        _hyena_chains(SEQ, BRANCH, *common, krc_ref, kic_ref, hb_ref, fwdc_ref, invc_ref, o_ref)

    @pl.when(jnp.logical_not(is_ctx))
    def _():
        _hyena_chains(DEC_SEQ, HY_CT, *common, krl_ref, kil_ref, hb_ref, fwdl_ref, invl_ref, o_ref)


def _mixer_hyena(p, conv_w, conv_b3, hyena_bias, spec_ctx, mats_ctx, spec_lat, mats_lat, layer,
                 ymix):
    blk = lambda c: pl.BlockSpec((HY_ROWS, BRANCH), lambda i: (i, c))
    once = pl.Buffered(1)
    lyr = lambda a: pl.BlockSpec((None,) + a.shape[1:], lambda i: (layer,) + (0,) * (a.ndim - 1),
                                 pipeline_mode=once)
    full = lambda a: pl.BlockSpec(a.shape, lambda i: (0,) * a.ndim, pipeline_mode=once)
    consts = [*spec_ctx, *mats_ctx, *spec_lat, *mats_lat]
    return pl.pallas_call(
        _hyena_kernel,
        grid=(N_TOK // HY_ROWS,),
        in_specs=[blk(C_DX1), blk(C_DX2), blk(C_DV), blk(C_DG),
                  lyr(conv_w), lyr(conv_b3), lyr(hyena_bias),
                  lyr(spec_ctx[0]), lyr(spec_ctx[1]), full(mats_ctx[0]), full(mats_ctx[1]),
                  lyr(spec_lat[0]), lyr(spec_lat[1]), full(mats_lat[0]), full(mats_lat[1]),
                  pl.BlockSpec(memory_space=pl.ANY)],
        out_specs=pl.BlockSpec((HY_ROWS, BRANCH), lambda i: (i, M_HYENA)),
        out_shape=jax.ShapeDtypeStruct((N_TOK, D_MIX), BF16),
        input_output_aliases={7 + len(consts): 0},
        compiler_params=_params("arbitrary"),
        name="mixer_hyena",
    )(p, p, p, p, conv_w, conv_b3, hyena_bias, *consts, ymix)


OUT_TM = 512


def _outproj_kernel(alpha, n_x, n_out, ymix_ref, *refs):
    x_refs = refs[:n_x]
    gate_ref, w_ref, b_ref, lng_ref, lnb_ref = refs[n_x:n_x + 5]
    o_refs = refs[n_x + 5:]
    is_ctx = pl.program_id(0) < N_CTX // OUT_TM

    x = x_refs[0][...] if n_x == 1 else jnp.where(is_ctx, x_refs[0][...], x_refs[1][...])
    y = jnp.dot(ymix_ref[...], w_ref[...], preferred_element_type=F32) + b_ref[...]
    r = alpha * x + gate_ref[...] * y
    res = _layer_norm(r) * lng_ref[...] + lnb_ref[...]
    if n_out == 1:
        o_refs[0][...] = res
    else:
        @pl.when(is_ctx)
        def _():
            o_refs[0][...] = res

        @pl.when(jnp.logical_not(is_ctx))
        def _():
            o_refs[1][...] = res


def _out_projection(ymix, xs, mod4, w_out_bf, b_out3, ln_g3, ln_b3, layer, split_out):
    alpha = (2.0 * DEPTH) ** 0.25
    row = lambda i: _cond_row(i, OUT_TM)
    vec = pl.BlockSpec((None, 1, D_MODEL), lambda i: (layer, 0, 0))
    if split_out:
        out_shape = [jax.ShapeDtypeStruct((N_CTX, D_MODEL), F32),
                     jax.ShapeDtypeStruct((N_LAT, D_MODEL), F32)]
    else:
        out_shape = [jax.ShapeDtypeStruct((N_TOK, D_MODEL), F32)]
    return pl.pallas_call(
        functools.partial(_outproj_kernel, alpha, len(xs), len(out_shape)),
        grid=(N_TOK // OUT_TM,),
        in_specs=[pl.BlockSpec((OUT_TM, D_MIX), lambda i: (i, 0))] + _token_specs(xs, OUT_TM) + [
            pl.BlockSpec((None, None, 1, D_MODEL), lambda i: (layer, row(i), 0, 2)),
            pl.BlockSpec((D_MIX, D_MODEL), lambda i: (0, 0), pipeline_mode=pl.Buffered(1)),
            vec, vec, vec,
        ],
        out_specs=_token_specs(out_shape, OUT_TM),
        out_shape=out_shape,
        compiler_params=_params("arbitrary"),
        name="out_projection",
    )(ymix, *xs, mod4, w_out_bf, b_out3, ln_g3, ln_b3)


def kernel(x_prompt, x_sample, cache_k, cache_v, c, c_ctx, w_mod, b_mod, w_in, gmlp_w, gmlp_b,
           pool_w, pool_scale, lambda_qk, subln_w, conv_w, conv_b, filt_w1, filt_b1, filt_w2,
           filt_b2, filt_freq, filt_w3, hyena_bias, w_out, b_out, ln_g, ln_b):
    xs = (x_prompt.reshape(N_CTX, D_MODEL), x_sample.reshape(N_LAT, D_MODEL))
    cond = jnp.concatenate(
        [c_ctx[None, :], c, jnp.zeros((N_COND - 1 - DEC_BATCH, D_MODEL), F32)], axis=0)
    mod4 = _modulation(cond, w_mod, b_mod).reshape(DEPTH, N_COND, 1, 3 * D_MODEL)

    cache_k4 = cache_k.reshape(DEC_BATCH, DEPTH, PAST_LEN, BRANCH)
    cache_v4 = cache_v.reshape(DEC_BATCH, DEPTH, PAST_LEN, BRANCH)
    gmlp_b_rows = jnp.broadcast_to(gmlp_b[..., None], (DEPTH, N_GROUPS, CHUNK, GROUP_W))
    subln_row = subln_w.reshape(DEPTH, 1, HEAD_DIM_C)
    conv_b3 = conv_b.reshape(DEPTH, 1, 3 * BRANCH)
    b_out3 = b_out.reshape(DEPTH, 1, D_MODEL)
    ln_g3 = ln_g.reshape(DEPTH, 1, D_MODEL)
    ln_b3 = ln_b.reshape(DEPTH, 1, D_MODEL)

    pad_h = HY_PAD - HY_HIDDEN
    filt = (
        jnp.pad(filt_w1, ((0, 0), (0, HY_PAD - HY_EMB), (0, pad_h))),
        jnp.pad(filt_b1, ((0, 0), (0, pad_h))).reshape(DEPTH, 1, HY_PAD),
        jnp.pad(filt_w2, ((0, 0), (0, pad_h), (0, pad_h))),
        jnp.pad(filt_b2, ((0, 0), (0, pad_h))).reshape(DEPTH, 1, HY_PAD),
        jnp.pad(filt_freq, ((0, 0), (0, pad_h))).reshape(DEPTH, 1, HY_PAD),
        jnp.pad(filt_w3, ((0, 0), (0, pad_h), (0, 0))),
    )
    mats_ctx = _dft_matrices(SEQ)
    mats_lat = _dft_matrices(DEC_SEQ)
    spec_ctx = _hyena_spectrum(SEQ, filt, mats_ctx[0])
    spec_lat = _hyena_spectrum(DEC_SEQ, filt, mats_lat[0])

    caches = None
    for layer in range(DEPTH):
        lam_init = 0.8 - 0.6 * math.exp(-0.3 * layer)
        p = _in_projection(xs, mod4, w_in, layer)
        ymix = _mixer_gmlp_pool(p, gmlp_w[layer], gmlp_b_rows[layer], pool_w[layer],
                                pool_scale[layer].reshape(1, BRANCH))
        ymix, new_k, new_v = _mixer_attn_ctx(p, lambda_qk, subln_row, layer, lam_init, ymix, caches)
        caches = (new_k, new_v)
        ymix, w_out_bf = _mixer_attn_lat(p, cache_k4, cache_v4, lambda_qk, subln_row, layer,
                                         lam_init, w_out, ymix)
        ymix = _mixer_hyena(p, conv_w, conv_b3, hyena_bias, spec_ctx, mats_ctx, spec_lat, mats_lat,
                            layer, ymix)
        xs = tuple(_out_projection(ymix, xs, mod4, w_out_bf, b_out3, ln_g3, ln_b3, layer,
                                   split_out=layer == DEPTH - 1))

    y_prompt = xs[0].reshape(BATCH, SEQ, D_MODEL)
    y_sample = xs[1].reshape(DEC_BATCH, DEC_SEQ, D_MODEL)
    new_k, new_v = caches
    return (y_prompt, y_sample,
            new_k.reshape(BATCH, DEPTH, SEQ, N_HEADS_C, 2, QK_HALF),
            new_v.reshape(BATCH, DEPTH, SEQ, N_HEADS_C, HEAD_DIM_C))
```

```python
import functools
import math

import numpy as np
import jax
import jax.numpy as jnp
from jax import lax
from jax.experimental import pallas as pl
from jax.experimental.pallas import tpu as pltpu

F32 = jnp.float32
BF16 = jnp.bfloat16

D_MODEL = 2048
BATCH = 16
SEQ = 256
DEPTH = 2
DEC_BATCH = 4
DEC_SEQ = 1024
PAST_LEN = 512
GRID_W = 64
BRANCH = 512
N_GROUPS = 4
GROUP_W = 128
CHUNK = 128
POOL_WINDOWS = (2, 4, 8, 16)
N_HEADS_C = 4
HEAD_DIM_C = 128
QK_HALF = 64
ROPE_AXIS_DIM = 32
ROPE_BASE = 10000.0
HY_BANDS = 16
HY_EMB = 33
HY_HIDDEN = 64
HY_ORDER = 2
HY_FAST_DECAY = 0.3
HY_SLOW_DECAY = 1.5
HY_TARGET = 1e-2
N_IN_PIECES = 13
D_IN = N_IN_PIECES * BRANCH
LN_EPS = 1e-6

N_CTX = BATCH * SEQ
N_LAT = DEC_BATCH * DEC_SEQ
N_TOK = N_CTX + N_LAT
N_COND = 8
LANES = 128
HY_PAD = LANES
HY_CT = 256
VMEM_LIMIT = 56 * 1024 * 1024

(C_AU, C_AV, C_AG, C_BX, C_BG, C_Q, C_K, C_V, C_CG, C_DX1, C_DX2, C_DV, C_DG) = range(13)
D_MIX = 4 * BRANCH
(M_GMLP, M_POOL, M_ATTN, M_HYENA) = range(4)


def _silu(x):
    return x * jax.nn.sigmoid(x)


def _bdot(a, b):
    return jnp.dot(a.astype(BF16), b.astype(BF16), preferred_element_type=F32)


def _split(x, n_terms):
    hi = x.astype(BF16)
    if n_terms == 1:
        return (hi,)
    return (hi, (x - hi.astype(F32)).astype(BF16))


def _sdot(a_terms, b_terms):
    acc = jnp.dot(a_terms[0], b_terms[0], preferred_element_type=F32)
    if len(a_terms) > 1:
        acc = acc + jnp.dot(a_terms[1], b_terms[0], preferred_element_type=F32)
    if len(b_terms) > 1:
        acc = acc + jnp.dot(a_terms[0], b_terms[1], preferred_element_type=F32)
    return acc


def _layer_norm(x):
    mu = jnp.mean(x, axis=-1, keepdims=True)
    xc = x - mu
    var = jnp.mean(xc * xc, axis=-1, keepdims=True)
    return xc * lax.rsqrt(var + LN_EPS)


def _cond_row(tile, rows_per_tile):
    n_ctx_tiles = N_CTX // rows_per_tile
    tiles_per_batch = DEC_SEQ // rows_per_tile
    return jnp.where(tile < n_ctx_tiles, 0, 1 + (tile - n_ctx_tiles) // tiles_per_batch)


def _params(*semantics):
    return pltpu.CompilerParams(dimension_semantics=semantics, vmem_limit_bytes=VMEM_LIMIT)


MOD_TN = 512


def _mod_kernel(c_ref, w_ref, b_ref, o_ref):
    o_ref[...] = _bdot(_silu(c_ref[...]), w_ref[...]) + b_ref[...]


def _modulation(cond, w_mod, b_mod):
    n = 3 * D_MODEL
    return pl.pallas_call(
        _mod_kernel,
        grid=(DEPTH, n // MOD_TN),
        in_specs=[
            pl.BlockSpec((N_COND, D_MODEL), lambda l, j: (0, 0)),
            pl.BlockSpec((None, D_MODEL, MOD_TN), lambda l, j: (l, 0, j)),
            pl.BlockSpec((None, 1, MOD_TN), lambda l, j: (l, 0, j)),
        ],
        out_specs=pl.BlockSpec((None, N_COND, MOD_TN), lambda l, j: (l, 0, j)),
        out_shape=jax.ShapeDtypeStruct((DEPTH, N_COND, n), F32),
        compiler_params=_params("arbitrary", "arbitrary"),
        name="modulation",
    )(cond, w_mod, b_mod.reshape(DEPTH, 1, n))


IN_TM = 1024
IN_TN = 512
IN_LN_ROWS = 256


def _token_specs(xs, tm, tile_of=lambda i: i):
    n_ctx_tiles = N_CTX // tm
    if len(xs) == 1:
        maps = [lambda i, *_: (tile_of(i), 0)]
    else:
        maps = [lambda i, *_: (jnp.minimum(tile_of(i), n_ctx_tiles - 1), 0),
                lambda i, *_: (jnp.maximum(tile_of(i) - n_ctx_tiles, 0), 0)]
    return [pl.BlockSpec((tm, D_MODEL), m) for m in maps]


def _inproj_kernel(n_x, tile0, cast_w, *refs):
    x_refs = refs[:n_x]
    scale_ref, shift_ref, w_ref = refs[n_x:n_x + 3]
    if cast_w:
        o_ref, wb_ref, h_ref = refs[n_x + 3:]
    else:
        _, o_ref, h_ref = refs[n_x + 3:]

    def weights():
        if not cast_w:
            return w_ref[...]
        w = w_ref[...].astype(BF16)
        wb_ref[...] = w
        return w

    def first_column_step(x_ref):
        w = weights()

        def norm(r):
            rows = slice(r, r + IN_LN_ROWS)
            h = _layer_norm(x_ref[rows, :]) * (1.0 + scale_ref[...]) + shift_ref[...]
            h = h.astype(BF16)
            h_ref[rows, :] = h
            return h

        h = norm(0)
        for r in range(0, IN_TM, IN_LN_ROWS):
            o_ref[r:r + IN_LN_ROWS, :] = jnp.dot(h, w, preferred_element_type=F32)
            if r + IN_LN_ROWS < IN_TM:
                h = norm(r + IN_LN_ROWS)

    first = pl.program_id(1) == 0
    if n_x == 1:
        pl.when(first)(lambda: first_column_step(x_refs[0]))
    else:
        is_ctx = pl.program_id(0) + tile0 < N_CTX // IN_TM
        pl.when(jnp.logical_and(first, is_ctx))(lambda: first_column_step(x_refs[0]))
        pl.when(jnp.logical_and(first, jnp.logical_not(is_ctx)))(
            lambda: first_column_step(x_refs[1]))

    @pl.when(jnp.logical_not(first))
    def _():
        o_ref[...] = jnp.dot(h_ref[...], weights(), preferred_element_type=F32)


def _in_projection(xs, mod4, w_in, layer):
    n_col = D_IN // IN_TN

    def mod_spec(tile0, piece):
        return pl.BlockSpec((None, None, 1, D_MODEL),
                            lambda i, j: (layer, _cond_row(i + tile0, IN_TM), 0, piece))

    scratch = [pltpu.VMEM((IN_TM, D_MODEL), BF16)]
    p_shape = jax.ShapeDtypeStruct((N_TOK, D_IN), F32)
    p, w_bf = pl.pallas_call(
        functools.partial(_inproj_kernel, 1, 0, True),
        grid=(1, n_col),
        in_specs=_token_specs(xs[:1], IN_TM) + [
            mod_spec(0, 1), mod_spec(0, 0),
            pl.BlockSpec((None, D_MODEL, IN_TN), lambda i, j: (layer, 0, j)),
        ],
        out_specs=[pl.BlockSpec((IN_TM, IN_TN), lambda i, j: (i, j)),
                   pl.BlockSpec((D_MODEL, IN_TN), lambda i, j: (0, j))],
        out_shape=[p_shape, jax.ShapeDtypeStruct((D_MODEL, D_IN), BF16)],
        scratch_shapes=scratch,
        compiler_params=_params("arbitrary", "arbitrary"),
        name="in_projection_first",
    )(xs[0], mod4, mod4, w_in)
    return pl.pallas_call(
        functools.partial(_inproj_kernel, len(xs), 1, False),
        grid=(N_TOK // IN_TM - 1, n_col),
        in_specs=_token_specs(xs, IN_TM, lambda i: i + 1) + [
            mod_spec(1, 1), mod_spec(1, 0),
            pl.BlockSpec((D_MODEL, IN_TN), lambda i, j: (0, j)),
            pl.BlockSpec(memory_space=pl.ANY),
        ],
        out_specs=pl.BlockSpec((IN_TM, IN_TN), lambda i, j: (i + 1, j)),
        out_shape=p_shape,
        input_output_aliases={len(xs) + 3: 0},
        scratch_shapes=scratch,
        compiler_params=_params("arbitrary", "arbitrary"),
        name="in_projection",
    )(*xs, mod4, mod4, w_bf, p)


AB_TM = 1024


def _gmlp_tasks(u_ref, v_ref, g_ref, w_ref, b_ref, o_ref):
    ws = [w_ref[g].astype(BF16) for g in range(N_GROUPS)]

    def chunk(r):
        rows = slice(r, r + CHUNK)
        vn = _layer_norm(v_ref[rows, :]).astype(BF16)
        for g in range(N_GROUPS):
            cols = slice(g * GROUP_W, (g + 1) * GROUP_W)
            mixed = jnp.dot(ws[g], vn[:, cols], preferred_element_type=F32) + b_ref[g]
            o_ref[rows, cols] = (_silu(g_ref[rows, cols]) * u_ref[rows, cols] * mixed).astype(BF16)

    return [functools.partial(chunk, r) for r in range(0, AB_TM, CHUNK)]


POOL_BAND_GROUPS = (2, 3)
POOL_BAND_ROWS = 256
POOL_BAND_HALO = 128


def _window_sum_shifts(p, win, pos, seq_len):
    acc = p
    for d in range(-(win // 2), win // 2):
        if d == 0:
            continue
        shifted = pltpu.roll(p, (-d) % AB_TM, axis=0)
        valid = (pos >= -d) if d < 0 else (pos < seq_len - d)
        acc = acc + jnp.where(valid, shifted, 0.0)
    return acc


def _window_sum_band_tasks(p, band_ref, pieces):
    hi = p.astype(BF16)
    r1 = p - hi.astype(F32)
    mid = r1.astype(BF16)
    lo = (r1 - mid.astype(F32)).astype(BF16)
    terms = jnp.concatenate([hi, mid, lo], axis=1)

    def piece(r):
        k0, k1 = max(0, r - POOL_BAND_HALO), min(AB_TM, r + POOL_BAND_ROWS + POOL_BAND_HALO)
        s = jnp.dot(band_ref[r:r + POOL_BAND_ROWS, k0:k1], terms[k0:k1, :],
                    preferred_element_type=F32)
        pieces.append(s[:, :GROUP_W] + s[:, GROUP_W:2 * GROUP_W] + s[:, 2 * GROUP_W:])

    return [functools.partial(piece, r) for r in range(0, AB_TM, POOL_BAND_ROWS)]


def _pool_finish(g, acc, x_ref, g_ref, w_ref, s_ref, pos, seq_len, o_ref):
    cols = slice(g * GROUP_W, (g + 1) * GROUP_W)
    win = POOL_WINDOWS[g]
    count = jnp.minimum(pos + win // 2, seq_len) - jnp.maximum(pos - win // 2, 0)
    pooled = acc / count.astype(F32)
    y = _bdot(pooled - x_ref[:, cols], w_ref[g])
    o_ref[:, BRANCH + g * GROUP_W:BRANCH + (g + 1) * GROUP_W] = (
        _silu(g_ref[:, cols]) * (y * s_ref[:, cols])).astype(BF16)


def _mixer_ab_kernel(au_ref, av_ref, ag_ref, bx_ref, bg_ref, gw_ref, gb_ref, pw_ref, ps_ref,
                     band_a_ref, band_b_ref, o_ref):
    seq_len = jnp.where(pl.program_id(0) < N_CTX // AB_TM, SEQ, DEC_SEQ)
    pos = lax.broadcasted_iota(jnp.int32, (AB_TM, GROUP_W), 0) & (seq_len - 1)
    pool_refs = (bx_ref, bg_ref, pw_ref, ps_ref, pos, seq_len, o_ref)
    band_sums = {g: [] for g in POOL_BAND_GROUPS}
    mxu_tasks = [t for g, band in zip(POOL_BAND_GROUPS, (band_a_ref, band_b_ref))
                 for t in _window_sum_band_tasks(bx_ref[:, g * GROUP_W:(g + 1) * GROUP_W], band,
                                                 band_sums[g])]
    vpu_tasks = _gmlp_tasks(au_ref, av_ref, ag_ref, gw_ref, gb_ref, o_ref)
    for i in range(max(len(mxu_tasks), len(vpu_tasks))):
        for tasks in (mxu_tasks, vpu_tasks):
            if i < len(tasks):
                tasks[i]()
    for g, win in enumerate(POOL_WINDOWS):
        if g not in POOL_BAND_GROUPS:
            acc = _window_sum_shifts(bx_ref[:, g * GROUP_W:(g + 1) * GROUP_W], win, pos, seq_len)
            _pool_finish(g, acc, *pool_refs)
    for g in POOL_BAND_GROUPS:
        _pool_finish(g, jnp.concatenate(band_sums[g], axis=0), *pool_refs)


def _pool_bands():
    t = np.arange(AB_TM)
    bands = []
    for g in POOL_BAND_GROUPS:
        half = POOL_WINDOWS[g] // 2
        d = t[None, :] - t[:, None]
        near = (d >= -half) & (d < half)
        per_path = [near & ((t[None, :] // L) == (t[:, None] // L)) for L in (SEQ, DEC_SEQ)]
        bands.append(jnp.asarray(np.stack(per_path), F32).astype(BF16))
    return bands


def _mixer_gmlp_pool(p, gmlp_w, gmlp_b_rows, pool_w, pool_scale_row):
    assert M_POOL == M_GMLP + 1
    n_steps = N_TOK // AB_TM
    blk = lambda c: pl.BlockSpec((AB_TM, BRANCH), lambda i: (i, c))
    full = lambda a: pl.BlockSpec(a.shape, lambda i: (0,) * a.ndim)
    band = pl.BlockSpec((None, AB_TM, AB_TM), lambda i: (jnp.where(i < N_CTX // AB_TM, 0, 1), 0, 0))
    return pl.pallas_call(
        _mixer_ab_kernel,
        grid=(n_steps,),
        in_specs=[blk(C_AU), blk(C_AV), blk(C_AG), blk(C_BX), blk(C_BG),
                  full(gmlp_w), full(gmlp_b_rows), full(pool_w), full(pool_scale_row), band, band],
        out_specs=pl.BlockSpec((AB_TM, 2 * BRANCH), lambda i: (i, M_GMLP // 2)),
        out_shape=jax.ShapeDtypeStruct((N_TOK, D_MIX), BF16),
        compiler_params=_params("arbitrary"),
        name="mixer_gmlp_pool",
    )(p, p, p, p, p, gmlp_w, gmlp_b_rows, pool_w, pool_scale_row, *_pool_bands())


ATT_TQ = 256
V_ROWS = HEAD_DIM_C + 16
NT_DIMS = (((1,), (1,)), ((), ()))


def _lambda(lam_ref, lam_init):
    lq = lam_ref[...]
    a = jnp.sum(lq[0:1] * lq[1:2], axis=-1, keepdims=True)
    b = jnp.sum(lq[2:3] * lq[3:4], axis=-1, keepdims=True)
    return jnp.exp(a) - jnp.exp(b) + lam_init


def _map_masks():
    lane = lax.broadcasted_iota(jnp.int32, (1, HEAD_DIM_C), 1)
    m0 = (lane < QK_HALF).astype(F32)
    return m0, 1.0 - m0


def _scores_t(q, keys):
    return [lax.dot_general(k, q, NT_DIMS, preferred_element_type=F32) for k in keys]


def _softmax_v_t(s, vals_t):
    m = functools.reduce(jnp.maximum, [jnp.max(x, axis=0, keepdims=True) for x in s])
    acc = functools.reduce(
        jnp.add, [jnp.dot(v, jnp.exp2(x - m).astype(BF16), preferred_element_type=F32)
                  for x, v in zip(s, vals_t)])
    return acc[:HEAD_DIM_C] * (1.0 / acc[HEAD_DIM_C:HEAD_DIM_C + 1])


def _with_ones_rows(v_t):
    ones = jnp.ones((V_ROWS - HEAD_DIM_C, v_t.shape[1]), BF16)
    return jnp.concatenate([v_t.astype(BF16), ones], axis=0)


def _diff_attention(q_of, keys_of, vals_t_of, gate_of, store, lam, lam_init, subln):
    masks = _map_masks()
    tasks = [(h, mp) for h in range(N_HEADS_C) for mp in range(2)]
    per_head = {}

    def operands(h):
        if h not in per_head:
            q = q_of(h) * (QK_HALF ** -0.5 * math.log2(math.e))
            per_head[h] = (q, keys_of(h), vals_t_of(h))
        return per_head[h]

    def scores(h, mp):
        q, keys, _ = operands(h)
        return _scores_t((q * masks[mp]).astype(BF16), keys)

    nxt = scores(*tasks[0])
    outs = []
    for i, (h, mp) in enumerate(tasks):
        cur = nxt
        if i + 1 < len(tasks):
            nxt = scores(*tasks[i + 1])
        outs.append(_softmax_v_t(cur, operands(h)[2]))
        if mp == 1:
            o = (outs[0] - lam * outs[1]).T
            outs = []
            o = o * lax.rsqrt(jnp.mean(o * o, axis=-1, keepdims=True) + 1e-5)
            o = o * subln * (1.0 - lam_init)
            store(h, _silu(gate_of(h)) * o)


def _head_cols(h):
    return slice(h * HEAD_DIM_C, (h + 1) * HEAD_DIM_C)


def _attn_ctx_kernel(lam_init, q_ref, k_ref, v_ref, g_ref, lam_ref, sw_ref, *rest):
    o_ref, ko_ref, vo_ref = rest[-3:]
    ko_ref[...] = k_ref[...]
    for h in range(N_HEADS_C):
        vo_ref[:, h, :] = v_ref[:, _head_cols(h)]

    def store(h, y):
        o_ref[:, _head_cols(h)] = y.astype(BF16)

    _diff_attention(
        lambda h: q_ref[:, _head_cols(h)],
        lambda h: [k_ref[:, _head_cols(h)].astype(BF16)],
        lambda h: [_with_ones_rows(v_ref[:, _head_cols(h)].T)],
        lambda h: g_ref[:, _head_cols(h)],
        store, _lambda(lam_ref, lam_init), lam_init, sw_ref[...])


def _mixer_attn_ctx(p, lambda_qk, subln_row, layer, lam_init, ymix, caches):
    blk = lambda c: pl.BlockSpec((SEQ, BRANCH), lambda b: (b, c))
    any_spec = pl.BlockSpec(memory_space=pl.ANY)
    k_spec = pl.BlockSpec((None, None, SEQ, BRANCH), lambda b: (b, layer, 0, 0))
    v_spec = pl.BlockSpec((None, None, SEQ, N_HEADS_C, HEAD_DIM_C), lambda b: (b, layer, 0, 0, 0))
    k_shape = jax.ShapeDtypeStruct((BATCH, DEPTH, SEQ, BRANCH), F32)
    v_shape = jax.ShapeDtypeStruct((BATCH, DEPTH, SEQ, N_HEADS_C, HEAD_DIM_C), F32)
    in_specs = [
        blk(C_Q), blk(C_K), blk(C_V), blk(C_CG),
        pl.BlockSpec((None, 4, QK_HALF), lambda b: (layer, 0, 0)),
        pl.BlockSpec((None, 1, HEAD_DIM_C), lambda b: (layer, 0, 0)),
        any_spec,
    ]
    args = [p, p, p, p, lambda_qk, subln_row, ymix]
    aliases = {6: 0}
    if caches is not None:
        in_specs += [any_spec, any_spec]
        args += list(caches)
        aliases.update({7: 1, 8: 2})
    return pl.pallas_call(
        functools.partial(_attn_ctx_kernel, lam_init),
        grid=(BATCH,),
        in_specs=in_specs,
        out_specs=[pl.BlockSpec((SEQ, BRANCH), lambda b: (b, M_ATTN)), k_spec, v_spec],
        out_shape=[jax.ShapeDtypeStruct((N_TOK, D_MIX), BF16), k_shape, v_shape],
        input_output_aliases=aliases,
        compiler_params=_params("arbitrary"),
        name="mixer_attn_ctx",
    )(*args)


def _rope(x, cos, sin_signed):
    lane = lax.broadcasted_iota(jnp.int32, x.shape, 1)
    first_half = (lane & (ROPE_AXIS_DIM - 1)) < (ROPE_AXIS_DIM // 2)
    half = ROPE_AXIS_DIM // 2
    partner = jnp.where(first_half,
                        pltpu.roll(x, x.shape[1] - half, axis=1),
                        pltpu.roll(x, half, axis=1))
    return x * cos + partner * sin_signed


def _attn_lat_kernel(lam_init, q_ref, k_ref, v_ref, g_ref, ck_ref, cv_ref, cosq_ref, sinq_ref,
                     cosk_ref, sink_ref, lam_ref, sw_ref, wo_ref, ymix_ref, o_ref, wo_bf_ref,
                     kc_ref, kr_ref, vt_ref):
    del ymix_ref
    wo_bf_ref[...] = wo_ref[...].astype(BF16)

    @pl.when(pl.program_id(1) == 0)
    def _():
        kc_ref[...] = ck_ref[...].astype(BF16)
        for h in range(N_HEADS_C):
            cols = slice(h * HEAD_DIM_C, (h + 1) * HEAD_DIM_C)
            kr_ref[:, cols] = _rope(k_ref[:, cols], cosk_ref[...], sink_ref[...]).astype(BF16)
            vt_ref[h, :, 0:PAST_LEN] = _with_ones_rows(cv_ref[:, h, :].T)
            vt_ref[h, :, PAST_LEN:PAST_LEN + DEC_SEQ] = _with_ones_rows(v_ref[:, cols].T)

    def store(h, y):
        o_ref[:, _head_cols(h)] = y.astype(BF16)

    _diff_attention(
        lambda h: _rope(q_ref[:, _head_cols(h)], cosq_ref[...], sinq_ref[...]),
        lambda h: [kc_ref[:, _head_cols(h)], kr_ref[:, _head_cols(h)]],
        lambda h: [vt_ref[h, :, 0:PAST_LEN], vt_ref[h, :, PAST_LEN:PAST_LEN + DEC_SEQ]],
        lambda h: g_ref[:, _head_cols(h)],
        store, _lambda(lam_ref, lam_init), lam_init, sw_ref[...])


def _rope_tables():
    pos = np.arange(DEC_SEQ)
    row = (pos // GRID_W).astype(np.float64)
    col = (pos % GRID_W).astype(np.float64)
    half = ROPE_AXIS_DIM // 2
    inv = ROPE_BASE ** (-np.arange(0, ROPE_AXIS_DIM, 2, dtype=np.float64) / ROPE_AXIS_DIM)
    lane = np.arange(HEAD_DIM_C)
    axis_is_col = (lane // ROPE_AXIS_DIM) % 2 == 1
    idx = lane % ROPE_AXIS_DIM
    ang = np.where(axis_is_col[None, :], col[:, None], row[:, None]) * inv[idx % half][None, :]
    sign = np.where(idx < half, -1.0, 1.0)[None, :]
    return (jnp.asarray(np.cos(ang), F32), jnp.asarray(np.sin(ang) * sign, F32))


def _mixer_attn_lat(p, cache_k4, cache_v, lambda_qk, subln_row, layer, lam_init, w_out, ymix):
    cos_t, sin_t = _rope_tables()
    q_tiles = DEC_SEQ // ATT_TQ
    q0 = N_CTX // ATT_TQ
    b0 = N_CTX // DEC_SEQ
    wo_rows = D_MIX // (DEC_BATCH * q_tiles)
    qblk = lambda c: pl.BlockSpec((ATT_TQ, BRANCH), lambda b, i: (q0 + b * q_tiles + i, c))
    kblk = lambda c: pl.BlockSpec((DEC_SEQ, BRANCH), lambda b, i: (b0 + b, c))
    cblk = pl.BlockSpec((None, None, PAST_LEN, BRANCH), lambda b, i: (b, layer, 0, 0))
    cvblk = pl.BlockSpec((None, None, PAST_LEN, N_HEADS_C, HEAD_DIM_C),
                         lambda b, i: (b, layer, 0, 0, 0))
    return pl.pallas_call(
        functools.partial(_attn_lat_kernel, lam_init),
        grid=(DEC_BATCH, q_tiles),
        in_specs=[
            qblk(C_Q), kblk(C_K), kblk(C_V), qblk(C_CG), cblk, cvblk,
            pl.BlockSpec((ATT_TQ, HEAD_DIM_C), lambda b, i: (i, 0)),
            pl.BlockSpec((ATT_TQ, HEAD_DIM_C), lambda b, i: (i, 0)),
            pl.BlockSpec((DEC_SEQ, HEAD_DIM_C), lambda b, i: (0, 0)),
            pl.BlockSpec((DEC_SEQ, HEAD_DIM_C), lambda b, i: (0, 0)),
            pl.BlockSpec((None, 4, QK_HALF), lambda b, i: (layer, 0, 0)),
            pl.BlockSpec((None, 1, HEAD_DIM_C), lambda b, i: (layer, 0, 0)),
            pl.BlockSpec((None, wo_rows, D_MODEL), lambda b, i: (layer, b * q_tiles + i, 0)),
            pl.BlockSpec(memory_space=pl.ANY),
        ],
        out_specs=[pl.BlockSpec((ATT_TQ, BRANCH), lambda b, i: (q0 + b * q_tiles + i, M_ATTN)),
                   pl.BlockSpec((wo_rows, D_MODEL), lambda b, i: (b * q_tiles + i, 0))],
        out_shape=[jax.ShapeDtypeStruct((N_TOK, D_MIX), BF16),
                   jax.ShapeDtypeStruct((D_MIX, D_MODEL), BF16)],
        input_output_aliases={13: 0},
        scratch_shapes=[pltpu.VMEM((PAST_LEN, BRANCH), BF16), pltpu.VMEM((DEC_SEQ, BRANCH), BF16),
                        pltpu.VMEM((N_HEADS_C, V_ROWS, PAST_LEN + DEC_SEQ), BF16)],
        compiler_params=_params("arbitrary", "arbitrary"),
        name="mixer_attn_lat",
    )(p, p, p, p, cache_k4, cache_v, cos_t, sin_t, cos_t, sin_t, lambda_qk, subln_row, w_out, ymix)


HY_ROWS = 1024
HY_PIECE = 512


def _dft_matrices(seq_len):
    n = 2 * seq_len
    f = np.arange(seq_len, dtype=np.float64)[:, None]
    s = np.arange(seq_len, dtype=np.float64)[None, :]
    theta = 2.0 * np.pi * f * s / n
    alt = np.where(np.arange(seq_len) % 2 == 0, 1.0, -1.0)
    ac = np.cos(theta)
    as_ = -np.sin(theta)
    as_[0, :] = alt
    bc = 2.0 * np.cos(theta.T) / n
    bc[:, 0] = 1.0 / n
    bs = -2.0 * np.sin(theta.T) / n
    bs[:, 0] = alt / n
    fwd = np.concatenate([ac, as_], axis=0)
    inv = np.concatenate([bc, bs], axis=1)
    return jnp.asarray(fwd, F32).astype(BF16), jnp.asarray(inv, F32).astype(BF16)


def _filter_features(seq_len):
    t_idx = np.arange(seq_len, dtype=np.float64)
    t_norm = np.linspace(0.0, 1.0, seq_len)
    bands = np.linspace(1e-4, HY_BANDS - 1, HY_BANDS)
    ang = (2.0 * math.pi * t_idx / seq_len)[:, None] * bands[None, :]
    feats = np.concatenate([t_norm[:, None], np.cos(ang), np.sin(ang)], axis=-1)
    feats = np.pad(feats, ((0, 0), (0, HY_PAD - HY_EMB)))
    deltas = np.abs(np.linspace(math.log(HY_TARGET) / HY_FAST_DECAY,
                                math.log(HY_TARGET) / HY_SLOW_DECAY, BRANCH))
    return (jnp.asarray(feats, F32), jnp.asarray(t_norm[:, None], F32),
            jnp.asarray(deltas[None, :], F32))


def _filter_kernel(seq_len, feats_ref, tn_ref, dl_ref, w1_ref, b1_ref, w2_ref, b2_ref, fr_ref,
                   w3f_ref, w3b_ref, fwd_ref, kr_ref, ki_ref, h_ref):
    sp = lambda x: _split(x, 2)

    @pl.when(pl.program_id(1) == 0)
    def _():
        fr = fr_ref[...]
        h = jnp.sin(fr * (_sdot(sp(feats_ref[...]), sp(w1_ref[...])) + b1_ref[...]))
        h_ref[...] = jnp.sin(fr * (_sdot(sp(h), sp(w2_ref[...])) + b2_ref[...]))

    h = sp(h_ref[...])
    decay = jnp.exp(-tn_ref[...] * dl_ref[...])
    row = lax.broadcasted_iota(jnp.int32, (seq_len, BRANCH), 0)
    fwd = _sdot(h, sp(w3f_ref[...])) * decay
    bwd = jnp.where(row == 0, 0.0, _sdot(h, sp(w3b_ref[...])) * decay)
    norm = (jnp.sum(jnp.abs(fwd), axis=0, keepdims=True)
            + jnp.sum(jnp.abs(bwd), axis=0, keepdims=True))
    fwd = fwd / norm
    bwd = bwd / norm
    even = fwd + bwd
    alt = jnp.where((row & 1) == 0, 1.0, -1.0)
    nyquist = jnp.sum(alt * even, axis=0, keepdims=True)
    kr_ref[...] = jnp.dot(fwd_ref[0:seq_len, :], even.astype(BF16), preferred_element_type=F32)
    ki = jnp.dot(fwd_ref[seq_len:2 * seq_len, :], (fwd - bwd).astype(BF16),
                 preferred_element_type=F32)
    ki_ref[...] = jnp.where(row == 0, nyquist, ki)


def _hyena_spectrum(seq_len, filt, fwd_mat):
    feats, t_norm, deltas = _filter_features(seq_len)
    w1, b1, w2, b2, freq, w3 = filt
    full = lambda a: pl.BlockSpec(a.shape, lambda l, o: (0,) * a.ndim)
    lyr = lambda r, n: pl.BlockSpec((None, r, n), lambda l, o: (l, 0, 0))
    out = pl.BlockSpec((None, None, seq_len, BRANCH), lambda l, o: (l, o, 0, 0))
    return pl.pallas_call(
        functools.partial(_filter_kernel, seq_len),
        grid=(DEPTH, HY_ORDER),
        in_specs=[
            full(feats), full(t_norm), full(deltas),
            lyr(HY_PAD, HY_PAD), lyr(1, HY_PAD), lyr(HY_PAD, HY_PAD), lyr(1, HY_PAD), lyr(1, HY_PAD),
            pl.BlockSpec((None, HY_PAD, BRANCH), lambda l, o: (l, 0, 2 * o)),
            pl.BlockSpec((None, HY_PAD, BRANCH), lambda l, o: (l, 0, 2 * o + 1)),
            full(fwd_mat),
        ],
        out_specs=[out, out],
        out_shape=[jax.ShapeDtypeStruct((DEPTH, HY_ORDER, seq_len, BRANCH), F32)] * 2,
        scratch_shapes=[pltpu.VMEM((seq_len, HY_PAD), F32)],
        compiler_params=_params("arbitrary", "arbitrary"),
        name=f"hyena_spectrum_{seq_len}",
    )(feats, t_norm, deltas, w1, b1, w2, b2, freq, w3, w3, fwd_mat)


def _hyena_chains(seq_len, width, x1_ref, x2_ref, hv_ref, g_ref, cw_ref, cb_ref, kr_ref, ki_ref,
                  hb_ref, fwd_ref, inv_ref, o_ref):
    row = lax.broadcasted_iota(jnp.int32, (seq_len, width), 0)
    first, last = row == 0, row == seq_len - 1

    def chain(rs, c0):
        cs = slice(c0, c0 + width)

        def short_conv(x_ref, piece):
            x = x_ref[rs, cs]
            w = cw_ref[:, piece * BRANCH + c0:piece * BRANCH + c0 + width]
            b = cb_ref[:, piece * BRANCH + c0:piece * BRANCH + c0 + width]
            prev = jnp.where(first, 0.0, pltpu.roll(x, 1, axis=0))
            nxt = jnp.where(last, 0.0, pltpu.roll(x, seq_len - 1, axis=0))
            return prev * w[0:1] + x * w[1:2] + nxt * w[2:3] + b

        z = short_conv(hv_ref, 2)
        yield
        gate_refs = (x1_ref, x2_ref)
        piece = min(seq_len, HY_PIECE)
        for order in range(HY_ORDER):
            zb = z.astype(BF16)
            y_re, y_im = [], []
            for r in range(0, seq_len, piece):
                zr = jnp.dot(fwd_ref[r:r + piece, :], zb, preferred_element_type=F32)
                yield
                zi = jnp.dot(fwd_ref[seq_len + r:seq_len + r + piece, :], zb,
                             preferred_element_type=F32)
                yield
                kr, kp = kr_ref[order, r:r + piece, cs], ki_ref[order, r:r + piece, cs]
                ki, kn = kp, kr
                if r == 0:
                    dc = lax.broadcasted_iota(jnp.int32, (piece, width), 0) == 0
                    ki = jnp.where(dc, 0.0, kp)
                    kn = jnp.where(dc, kp, kr)
                y_re.append((zr * kr - zi * ki).astype(BF16))
                y_im.append((zr * ki + zi * kn).astype(BF16))
                yield
            yf = jnp.concatenate(y_re + y_im, axis=0)
            ys = []
            for r in range(0, seq_len, piece):
                ys.append(jnp.dot(inv_ref[r:r + piece, :], yf, preferred_element_type=F32))
                yield
                if r == 0:
                    gate = short_conv(gate_refs[order], order)
            y = ys[0] if len(ys) == 1 else jnp.concatenate(ys, axis=0)
            z = gate * (y + z * hb_ref[order:order + 1, cs])
        o_ref[rs, cs] = (_silu(g_ref[rs, cs]) * z).astype(BF16)

    waiting = [chain(slice(s * seq_len, (s + 1) * seq_len), c0)
               for s in range(HY_ROWS // seq_len) for c0 in range(0, BRANCH, width)]
    running = []
    while waiting or running:
        if waiting:
            running.append(waiting.pop(0))
        for gen in list(running):
            if next(gen, "done") == "done":
                running.remove(gen)


def _hyena_kernel(x1_ref, x2_ref, hv_ref, g_ref, cw_ref, cb_ref, hb_ref,
                  krc_ref, kic_ref, fwdc_ref, invc_ref, krl_ref, kil_ref, fwdl_ref, invl_ref,
                  ymix_ref, o_ref):
    del ymix_ref
    common = (x1_ref, x2_ref, hv_ref, g_ref, cw_ref, cb_ref)
    is_ctx = pl.program_id(0) < N_CTX // HY_ROWS

    @pl.when(is_ctx)
    def _():
        _hyena_chains(SEQ, BRANCH, *common, krc_ref, kic_ref, hb_ref, fwdc_ref, invc_ref, o_ref)

    @pl.when(jnp.logical_not(is_ctx))
    def _():
        _hyena_chains(DEC_SEQ, HY_CT, *common, krl_ref, kil_ref, hb_ref, fwdl_ref, invl_ref, o_ref)


def _mixer_hyena(p, conv_w, conv_b3, hyena_bias, spec_ctx, mats_ctx, spec_lat, mats_lat, layer,
                 ymix):
    blk = lambda c: pl.BlockSpec((HY_ROWS, BRANCH), lambda i: (i, c))
    once = pl.Buffered(1)
    lyr = lambda a: pl.BlockSpec((None,) + a.shape[1:], lambda i: (layer,) + (0,) * (a.ndim - 1),
                                 pipeline_mode=once)
    full = lambda a: pl.BlockSpec(a.shape, lambda i: (0,) * a.ndim, pipeline_mode=once)
    consts = [*spec_ctx, *mats_ctx, *spec_lat, *mats_lat]
    return pl.pallas_call(
        _hyena_kernel,
        grid=(N_TOK // HY_ROWS,),
        in_specs=[blk(C_DX1), blk(C_DX2), blk(C_DV), blk(C_DG),
                  lyr(conv_w), lyr(conv_b3), lyr(hyena_bias),
                  lyr(spec_ctx[0]), lyr(spec_ctx[1]), full(mats_ctx[0]), full(mats_ctx[1]),
                  lyr(spec_lat[0]), lyr(spec_lat[1]), full(mats_lat[0]), full(mats_lat[1]),
                  pl.BlockSpec(memory_space=pl.ANY)],
        out_specs=pl.BlockSpec((HY_ROWS, BRANCH), lambda i: (i, M_HYENA)),
        out_shape=jax.ShapeDtypeStruct((N_TOK, D_MIX), BF16),
        input_output_aliases={7 + len(consts): 0},
        compiler_params=_params("arbitrary"),
        name="mixer_hyena",
    )(p, p, p, p, conv_w, conv_b3, hyena_bias, *consts, ymix)


OUT_TM = 512


def _outproj_kernel(alpha, n_x, n_out, ymix_ref, *refs):
    x_refs = refs[:n_x]
    gate_ref, w_ref, b_ref, lng_ref, lnb_ref = refs[n_x:n_x + 5]
    o_refs = refs[n_x + 5:]
    is_ctx = pl.program_id(0) < N_CTX // OUT_TM

    x = x_refs[0][...] if n_x == 1 else jnp.where(is_ctx, x_refs[0][...], x_refs[1][...])
    y = jnp.dot(ymix_ref[...], w_ref[...], preferred_element_type=F32) + b_ref[...]
    r = alpha * x + gate_ref[...] * y
    res = _layer_norm(r) * lng_ref[...] + lnb_ref[...]
    if n_out == 1:
        o_refs[0][...] = res
    else:
        @pl.when(is_ctx)
        def _():
            o_refs[0][...] = res

        @pl.when(jnp.logical_not(is_ctx))
        def _():
            o_refs[1][...] = res


def _out_projection(ymix, xs, mod4, w_out_bf, b_out3, ln_g3, ln_b3, layer, split_out):
    alpha = (2.0 * DEPTH) ** 0.25
    row = lambda i: _cond_row(i, OUT_TM)
    vec = pl.BlockSpec((None, 1, D_MODEL), lambda i: (layer, 0, 0))
    if split_out:
        out_shape = [jax.ShapeDtypeStruct((N_CTX, D_MODEL), F32),
                     jax.ShapeDtypeStruct((N_LAT, D_MODEL), F32)]
    else:
        out_shape = [jax.ShapeDtypeStruct((N_TOK, D_MODEL), F32)]
    return pl.pallas_call(
        functools.partial(_outproj_kernel, alpha, len(xs), len(out_shape)),
        grid=(N_TOK // OUT_TM,),
        in_specs=[pl.BlockSpec((OUT_TM, D_MIX), lambda i: (i, 0))] + _token_specs(xs, OUT_TM) + [
            pl.BlockSpec((None, None, 1, D_MODEL), lambda i: (layer, row(i), 0, 2)),
            pl.BlockSpec((D_MIX, D_MODEL), lambda i: (0, 0), pipeline_mode=pl.Buffered(1)),
            vec, vec, vec,
        ],
        out_specs=_token_specs(out_shape, OUT_TM),
        out_shape=out_shape,
        compiler_params=_params("arbitrary"),
        name="out_projection",
    )(ymix, *xs, mod4, w_out_bf, b_out3, ln_g3, ln_b3)


def kernel(x_prompt, x_sample, cache_k, cache_v, c, c_ctx, w_mod, b_mod, w_in, gmlp_w, gmlp_b,
           pool_w, pool_scale, lambda_qk, subln_w, conv_w, conv_b, filt_w1, filt_b1, filt_w2,
           filt_b2, filt_freq, filt_w3, hyena_bias, w_out, b_out, ln_g, ln_b):
    xs = (x_prompt.reshape(N_CTX, D_MODEL), x_sample.reshape(N_LAT, D_MODEL))
    cond = jnp.concatenate(
        [c_ctx[None, :], c, jnp.zeros((N_COND - 1 - DEC_BATCH, D_MODEL), F32)], axis=0)
    mod4 = _modulation(cond, w_mod, b_mod).reshape(DEPTH, N_COND, 1, 3 * D_MODEL)

    cache_k4 = cache_k.reshape(DEC_BATCH, DEPTH, PAST_LEN, BRANCH)
    gmlp_b_rows = jnp.broadcast_to(gmlp_b[..., None], (DEPTH, N_GROUPS, CHUNK, GROUP_W))
    subln_row = subln_w.reshape(DEPTH, 1, HEAD_DIM_C)
    conv_b3 = conv_b.reshape(DEPTH, 1, 3 * BRANCH)
    b_out3 = b_out.reshape(DEPTH, 1, D_MODEL)
    ln_g3 = ln_g.reshape(DEPTH, 1, D_MODEL)
    ln_b3 = ln_b.reshape(DEPTH, 1, D_MODEL)

    pad_h = HY_PAD - HY_HIDDEN
    filt = (
        jnp.pad(filt_w1, ((0, 0), (0, HY_PAD - HY_EMB), (0, pad_h))),
        jnp.pad(filt_b1, ((0, 0), (0, pad_h))).reshape(DEPTH, 1, HY_PAD),
        jnp.pad(filt_w2, ((0, 0), (0, pad_h), (0, pad_h))),
        jnp.pad(filt_b2, ((0, 0), (0, pad_h))).reshape(DEPTH, 1, HY_PAD),
        jnp.pad(filt_freq, ((0, 0), (0, pad_h))).reshape(DEPTH, 1, HY_PAD),
        jnp.pad(filt_w3, ((0, 0), (0, pad_h), (0, 0))),
    )
    mats_ctx = _dft_matrices(SEQ)
    mats_lat = _dft_matrices(DEC_SEQ)
    spec_ctx = _hyena_spectrum(SEQ, filt, mats_ctx[0])
    spec_lat = _hyena_spectrum(DEC_SEQ, filt, mats_lat[0])

    caches = None
    for layer in range(DEPTH):
        lam_init = 0.8 - 0.6 * math.exp(-0.3 * layer)
        p = _in_projection(xs, mod4, w_in, layer)
        ymix = _mixer_gmlp_pool(p, gmlp_w[layer], gmlp_b_rows[layer], pool_w[layer],
                                pool_scale[layer].reshape(1, BRANCH))
        ymix, new_k, new_v = _mixer_attn_ctx(p, lambda_qk, subln_row, layer, lam_init, ymix, caches)
        caches = (new_k, new_v)
        ymix, w_out_bf = _mixer_attn_lat(p, cache_k4, cache_v, lambda_qk, subln_row, layer,
                                         lam_init, w_out, ymix)
        ymix = _mixer_hyena(p, conv_w, conv_b3, hyena_bias, spec_ctx, mats_ctx, spec_lat, mats_lat,
                            layer, ymix)
        xs = tuple(_out_projection(ymix, xs, mod4, w_out_bf, b_out3, ln_g3, ln_b3, layer,
                                   split_out=layer == DEPTH - 1))

    y_prompt = xs[0].reshape(BATCH, SEQ, D_MODEL)
    y_sample = xs[1].reshape(DEC_BATCH, DEC_SEQ, D_MODEL)
    new_k, new_v = caches
    return (y_prompt, y_sample,
            new_k.reshape(BATCH, DEPTH, SEQ, N_HEADS_C, 2, QK_HALF), new_v)
```

```python
import functools
import math

import numpy as np
import jax
import jax.numpy as jnp
from jax import lax
from jax.experimental import pallas as pl
from jax.experimental.pallas import tpu as pltpu

F32 = jnp.float32
BF16 = jnp.bfloat16

D_MODEL = 2048
BATCH = 16
SEQ = 256
DEPTH = 2
DEC_BATCH = 4
DEC_SEQ = 1024
PAST_LEN = 512
GRID_W = 64
BRANCH = 512
N_GROUPS = 4
GROUP_W = 128
CHUNK = 128
POOL_WINDOWS = (2, 4, 8, 16)
N_HEADS_C = 4
HEAD_DIM_C = 128
QK_HALF = 64
ROPE_AXIS_DIM = 32
ROPE_BASE = 10000.0
HY_BANDS = 16
HY_EMB = 33
HY_HIDDEN = 64
HY_ORDER = 2
HY_FAST_DECAY = 0.3
HY_SLOW_DECAY = 1.5
HY_TARGET = 1e-2
N_IN_PIECES = 13
D_IN = N_IN_PIECES * BRANCH
LN_EPS = 1e-6

N_CTX = BATCH * SEQ
N_LAT = DEC_BATCH * DEC_SEQ
N_TOK = N_CTX + N_LAT
N_COND = 8
LANES = 128
HY_PAD = LANES
HY_CT = 256
VMEM_LIMIT = 56 * 1024 * 1024

(C_AU, C_AV, C_AG, C_BX, C_BG, C_Q, C_K, C_V, C_CG, C_DX1, C_DX2, C_DV, C_DG) = range(13)
D_MIX = 4 * BRANCH
(M_GMLP, M_POOL, M_ATTN, M_HYENA) = range(4)


def _silu(x):
    return x * jax.nn.sigmoid(x)


def _bdot(a, b):
    return jnp.dot(a.astype(BF16), b.astype(BF16), preferred_element_type=F32)


def _split(x, n_terms):
    hi = x.astype(BF16)
    if n_terms == 1:
        return (hi,)
    return (hi, (x - hi.astype(F32)).astype(BF16))


def _sdot(a_terms, b_terms):
    acc = jnp.dot(a_terms[0], b_terms[0], preferred_element_type=F32)
    if len(a_terms) > 1:
        acc = acc + jnp.dot(a_terms[1], b_terms[0], preferred_element_type=F32)
    if len(b_terms) > 1:
        acc = acc + jnp.dot(a_terms[0], b_terms[1], preferred_element_type=F32)
    return acc


def _layer_norm(x):
    mu = jnp.mean(x, axis=-1, keepdims=True)
    xc = x - mu
    var = jnp.mean(xc * xc, axis=-1, keepdims=True)
    return xc * lax.rsqrt(var + LN_EPS)


def _cond_row(tile, rows_per_tile):
    n_ctx_tiles = N_CTX // rows_per_tile
    tiles_per_batch = DEC_SEQ // rows_per_tile
    return jnp.where(tile < n_ctx_tiles, 0, 1 + (tile - n_ctx_tiles) // tiles_per_batch)


def _params(*semantics):
    return pltpu.CompilerParams(dimension_semantics=semantics, vmem_limit_bytes=VMEM_LIMIT)


MOD_TN = 1024


def _mod_kernel(c_ref, w_ref, b_ref, o_ref):
    o_ref[...] = _bdot(_silu(c_ref[...]), w_ref[...]) + b_ref[...]


def _modulation(cond, w_mod, b_mod):
    n = 3 * D_MODEL
    return pl.pallas_call(
        _mod_kernel,
        grid=(DEPTH, n // MOD_TN),
        in_specs=[
            pl.BlockSpec((N_COND, D_MODEL), lambda l, j: (0, 0)),
            pl.BlockSpec((None, D_MODEL, MOD_TN), lambda l, j: (l, 0, j)),
            pl.BlockSpec((None, 1, MOD_TN), lambda l, j: (l, 0, j)),
        ],
        out_specs=pl.BlockSpec((None, N_COND, MOD_TN), lambda l, j: (l, 0, j)),
        out_shape=jax.ShapeDtypeStruct((DEPTH, N_COND, n), F32),
        compiler_params=_params("arbitrary", "arbitrary"),
        name="modulation",
    )(cond, w_mod, b_mod.reshape(DEPTH, 1, n))


IN_TM = 1024
IN_TN = 512
IN_LN_ROWS = 256
IN_CAST_K = 512


def _token_specs(xs, tm, tile_of=lambda i: i):
    n_ctx_tiles = N_CTX // tm
    if len(xs) == 1:
        maps = [lambda i, *_: (tile_of(i), 0)]
    else:
        maps = [lambda i, *_: (jnp.minimum(tile_of(i), n_ctx_tiles - 1), 0),
                lambda i, *_: (jnp.maximum(tile_of(i) - n_ctx_tiles, 0), 0)]
    return [pl.BlockSpec((tm, D_MODEL), m) for m in maps]


def _inproj_kernel(n_x, tile0, cast_w, *refs):
    x_refs = refs[:n_x]
    scale_ref, shift_ref, w_ref = refs[n_x:n_x + 3]
    if cast_w:
        o_ref, wb_ref, h_ref = refs[n_x + 3:]
    else:
        _, o_ref, h_ref = refs[n_x + 3:]

    def weights():
        if not cast_w:
            return w_ref[...]
        w = w_ref[...].astype(BF16)
        wb_ref[...] = w
        return w

    def first_column_step(x_ref):
        w = weights()

        def norm(r):
            rows = slice(r, r + IN_LN_ROWS)
            h = _layer_norm(x_ref[rows, :]) * (1.0 + scale_ref[...]) + shift_ref[...]
            h = h.astype(BF16)
            h_ref[rows, :] = h
            return h

        h = norm(0)
        for r in range(0, IN_TM, IN_LN_ROWS):
            o_ref[r:r + IN_LN_ROWS, :] = jnp.dot(h, w, preferred_element_type=F32)
            if r + IN_LN_ROWS < IN_TM:
                h = norm(r + IN_LN_ROWS)

    first = pl.program_id(1) == 0
    if n_x == 1:
        pl.when(first)(lambda: first_column_step(x_refs[0]))
    else:
        is_ctx = pl.program_id(0) + tile0 < N_CTX // IN_TM
        pl.when(jnp.logical_and(first, is_ctx))(lambda: first_column_step(x_refs[0]))
        pl.when(jnp.logical_and(first, jnp.logical_not(is_ctx)))(
            lambda: first_column_step(x_refs[1]))

    @pl.when(jnp.logical_not(first))
    def _():
        if not cast_w:
            o_ref[...] = jnp.dot(h_ref[...], w_ref[...], preferred_element_type=F32)
            return
        acc = None
        for k in range(0, D_MODEL, IN_CAST_K):
            w = w_ref[k:k + IN_CAST_K, :].astype(BF16)
            wb_ref[k:k + IN_CAST_K, :] = w
            part = jnp.dot(h_ref[:, k:k + IN_CAST_K], w, preferred_element_type=F32)
            acc = part if acc is None else acc + part
        o_ref[...] = acc


def _in_projection(xs, mod4, w_in, layer):
    n_col = D_IN // IN_TN

    def mod_spec(tile0, piece):
        return pl.BlockSpec((None, None, 1, D_MODEL),
                            lambda i, j: (layer, _cond_row(i + tile0, IN_TM), 0, piece))

    scratch = [pltpu.VMEM((IN_TM, D_MODEL), BF16)]
    p_shape = jax.ShapeDtypeStruct((N_TOK, D_IN), F32)
    p, w_bf = pl.pallas_call(
        functools.partial(_inproj_kernel, 1, 0, True),
        grid=(1, n_col),
        in_specs=_token_specs(xs[:1], IN_TM) + [
            mod_spec(0, 1), mod_spec(0, 0),
            pl.BlockSpec((None, D_MODEL, IN_TN), lambda i, j: (layer, 0, j)),
        ],
        out_specs=[pl.BlockSpec((IN_TM, IN_TN), lambda i, j: (i, j)),
                   pl.BlockSpec((D_MODEL, IN_TN), lambda i, j: (0, j))],
        out_shape=[p_shape, jax.ShapeDtypeStruct((D_MODEL, D_IN), BF16)],
        scratch_shapes=scratch,
        compiler_params=_params("arbitrary", "arbitrary"),
        name="in_projection_first",
    )(xs[0], mod4, mod4, w_in)
    return pl.pallas_call(
        functools.partial(_inproj_kernel, len(xs), 1, False),
        grid=(N_TOK // IN_TM - 1, n_col),
        in_specs=_token_specs(xs, IN_TM, lambda i: i + 1) + [
            mod_spec(1, 1), mod_spec(1, 0),
            pl.BlockSpec((D_MODEL, IN_TN), lambda i, j: (0, j)),
            pl.BlockSpec(memory_space=pl.ANY),
        ],
        out_specs=pl.BlockSpec((IN_TM, IN_TN), lambda i, j: (i + 1, j)),
        out_shape=p_shape,
        input_output_aliases={len(xs) + 3: 0},
        scratch_shapes=scratch,
        compiler_params=_params("arbitrary", "arbitrary"),
        name="in_projection",
    )(*xs, mod4, mod4, w_bf, p)


AB_TM = 1024


def _gmlp_tasks(u_ref, v_ref, g_ref, w_ref, b_ref, o_ref):
    ws = [w_ref[g].astype(BF16) for g in range(N_GROUPS)]

    def chunk(r):
        rows = slice(r, r + CHUNK)
        vn = _layer_norm(v_ref[rows, :]).astype(BF16)
        for g in range(N_GROUPS):
            cols = slice(g * GROUP_W, (g + 1) * GROUP_W)
            mixed = jnp.dot(ws[g], vn[:, cols], preferred_element_type=F32) + b_ref[g]
            o_ref[rows, cols] = (_silu(g_ref[rows, cols]) * u_ref[rows, cols] * mixed).astype(BF16)

    return [functools.partial(chunk, r) for r in range(0, AB_TM, CHUNK)]


POOL_BAND_GROUPS = (2, 3)
POOL_BAND_ROWS = 256
POOL_BAND_HALO = 128


def _window_sum_shifts(p, win, pos, seq_len):
    acc = p
    for d in range(-(win // 2), win // 2):
        if d == 0:
            continue
        shifted = pltpu.roll(p, (-d) % AB_TM, axis=0)
        valid = (pos >= -d) if d < 0 else (pos < seq_len - d)
        acc = acc + jnp.where(valid, shifted, 0.0)
    return acc


def _window_sum_band_tasks(p, band_ref, pieces):
    hi = p.astype(BF16)
    r1 = p - hi.astype(F32)
    mid = r1.astype(BF16)
    lo = (r1 - mid.astype(F32)).astype(BF16)
    terms = jnp.concatenate([hi, mid, lo], axis=1)

    def piece(r):
        k0, k1 = max(0, r - POOL_BAND_HALO), min(AB_TM, r + POOL_BAND_ROWS + POOL_BAND_HALO)
        s = jnp.dot(band_ref[r:r + POOL_BAND_ROWS, k0:k1], terms[k0:k1, :],
                    preferred_element_type=F32)
        pieces.append(s[:, :GROUP_W] + s[:, GROUP_W:2 * GROUP_W] + s[:, 2 * GROUP_W:])

    return [functools.partial(piece, r) for r in range(0, AB_TM, POOL_BAND_ROWS)]


def _pool_finish(g, acc, x_ref, g_ref, w_ref, s_ref, pos, seq_len, o_ref):
    cols = slice(g * GROUP_W, (g + 1) * GROUP_W)
    win = POOL_WINDOWS[g]
    count = jnp.minimum(pos + win // 2, seq_len) - jnp.maximum(pos - win // 2, 0)
    pooled = acc / count.astype(F32)
    y = _bdot(pooled - x_ref[:, cols], w_ref[g])
    o_ref[:, BRANCH + g * GROUP_W:BRANCH + (g + 1) * GROUP_W] = (
        _silu(g_ref[:, cols]) * (y * s_ref[:, cols])).astype(BF16)


def _mixer_ab_kernel(au_ref, av_ref, ag_ref, bx_ref, bg_ref, gw_ref, gb_ref, pw_ref, ps_ref,
                     band_a_ref, band_b_ref, o_ref):
    seq_len = jnp.where(pl.program_id(0) < N_CTX // AB_TM, SEQ, DEC_SEQ)
    pos = lax.broadcasted_iota(jnp.int32, (AB_TM, GROUP_W), 0) & (seq_len - 1)
    pool_refs = (bx_ref, bg_ref, pw_ref, ps_ref, pos, seq_len, o_ref)
    band_sums = {g: [] for g in POOL_BAND_GROUPS}
    mxu_tasks = [t for g, band in zip(POOL_BAND_GROUPS, (band_a_ref, band_b_ref))
                 for t in _window_sum_band_tasks(bx_ref[:, g * GROUP_W:(g + 1) * GROUP_W], band,
                                                 band_sums[g])]
    vpu_tasks = _gmlp_tasks(au_ref, av_ref, ag_ref, gw_ref, gb_ref, o_ref)
    for i in range(max(len(mxu_tasks), len(vpu_tasks))):
        for tasks in (mxu_tasks, vpu_tasks):
            if i < len(tasks):
                tasks[i]()
    for g, win in enumerate(POOL_WINDOWS):
        if g not in POOL_BAND_GROUPS:
            acc = _window_sum_shifts(bx_ref[:, g * GROUP_W:(g + 1) * GROUP_W], win, pos, seq_len)
            _pool_finish(g, acc, *pool_refs)
    for g in POOL_BAND_GROUPS:
        _pool_finish(g, jnp.concatenate(band_sums[g], axis=0), *pool_refs)


def _pool_bands():
    t = np.arange(AB_TM)
    bands = []
    for g in POOL_BAND_GROUPS:
        half = POOL_WINDOWS[g] // 2
        d = t[None, :] - t[:, None]
        near = (d >= -half) & (d < half)
        per_path = [near & ((t[None, :] // L) == (t[:, None] // L)) for L in (SEQ, DEC_SEQ)]
        bands.append(jnp.asarray(np.stack(per_path), F32).astype(BF16))
    return bands


def _mixer_gmlp_pool(p, gmlp_w, gmlp_b_rows, pool_w, pool_scale_row):
    assert M_POOL == M_GMLP + 1
    n_steps = N_TOK // AB_TM
    blk = lambda c: pl.BlockSpec((AB_TM, BRANCH), lambda i: (i, c))
    full = lambda a: pl.BlockSpec(a.shape, lambda i: (0,) * a.ndim)
    band = pl.BlockSpec((None, AB_TM, AB_TM), lambda i: (jnp.where(i < N_CTX // AB_TM, 0, 1), 0, 0))
    return pl.pallas_call(
        _mixer_ab_kernel,
        grid=(n_steps,),
        in_specs=[blk(C_AU), blk(C_AV), blk(C_AG), blk(C_BX), blk(C_BG),
                  full(gmlp_w), full(gmlp_b_rows), full(pool_w), full(pool_scale_row), band, band],
        out_specs=pl.BlockSpec((AB_TM, 2 * BRANCH), lambda i: (i, M_GMLP // 2)),
        out_shape=jax.ShapeDtypeStruct((N_TOK, D_MIX), BF16),
        compiler_params=_params("arbitrary"),
        name="mixer_gmlp_pool",
    )(p, p, p, p, p, gmlp_w, gmlp_b_rows, pool_w, pool_scale_row, *_pool_bands())


ATT_TQ = 256
V_ROWS = HEAD_DIM_C + 16
NT_DIMS = (((1,), (1,)), ((), ()))


def _lambda(lam_ref, lam_init):
    lq = lam_ref[...]
    a = jnp.sum(lq[0:1] * lq[1:2], axis=-1, keepdims=True)
    b = jnp.sum(lq[2:3] * lq[3:4], axis=-1, keepdims=True)
    return jnp.exp(a) - jnp.exp(b) + lam_init


def _map_masks():
    lane = lax.broadcasted_iota(jnp.int32, (1, HEAD_DIM_C), 1)
    m0 = (lane < QK_HALF).astype(F32)
    return m0, 1.0 - m0


def _scores_t(q, keys):
    return [lax.dot_general(k, q, NT_DIMS, preferred_element_type=F32) for k in keys]


def _softmax_v_t(s, vals_t):
    m = functools.reduce(jnp.maximum, [jnp.max(x, axis=0, keepdims=True) for x in s])
    acc = functools.reduce(
        jnp.add, [jnp.dot(v, jnp.exp2(x - m).astype(BF16), preferred_element_type=F32)
                  for x, v in zip(s, vals_t)])
    return acc[:HEAD_DIM_C] * (1.0 / acc[HEAD_DIM_C:HEAD_DIM_C + 1])


def _with_ones_rows(v_t):
    ones = jnp.ones((V_ROWS - HEAD_DIM_C, v_t.shape[1]), BF16)
    return jnp.concatenate([v_t.astype(BF16), ones], axis=0)


def _diff_attention(q_of, keys_of, vals_t_of, gate_of, store, lam, lam_init, subln):
    masks = _map_masks()
    tasks = [(h, mp) for h in range(N_HEADS_C) for mp in range(2)]
    per_head = {}

    def operands(h):
        if h not in per_head:
            q = q_of(h) * (QK_HALF ** -0.5 * math.log2(math.e))
            per_head[h] = (q, keys_of(h), vals_t_of(h))
        return per_head[h]

    def scores(h, mp):
        q, keys, _ = operands(h)
        return _scores_t((q * masks[mp]).astype(BF16), keys)

    nxt = scores(*tasks[0])
    outs = []
    for i, (h, mp) in enumerate(tasks):
        cur = nxt
        if i + 1 < len(tasks):
            nxt = scores(*tasks[i + 1])
        outs.append(_softmax_v_t(cur, operands(h)[2]))
        if mp == 1:
            o = (outs[0] - lam * outs[1]).T
            outs = []
            o = o * lax.rsqrt(jnp.mean(o * o, axis=-1, keepdims=True) + 1e-5)
            o = o * subln * (1.0 - lam_init)
            store(h, _silu(gate_of(h)) * o)


def _head_cols(h):
    return slice(h * HEAD_DIM_C, (h + 1) * HEAD_DIM_C)


def _attn_ctx_kernel(lam_init, q_ref, k_ref, v_ref, g_ref, lam_ref, sw_ref, *rest):
    o_ref, ko_ref, vo_ref = rest[-3:]
    ko_ref[...] = k_ref[...]
    for h in range(N_HEADS_C):
        vo_ref[:, h, :] = v_ref[:, _head_cols(h)]

    def store(h, y):
        o_ref[:, _head_cols(h)] = y.astype(BF16)

    _diff_attention(
        lambda h: q_ref[:, _head_cols(h)],
        lambda h: [k_ref[:, _head_cols(h)].astype(BF16)],
        lambda h: [_with_ones_rows(v_ref[:, _head_cols(h)].T)],
        lambda h: g_ref[:, _head_cols(h)],
        store, _lambda(lam_ref, lam_init), lam_init, sw_ref[...])


def _mixer_attn_ctx(p, lambda_qk, subln_row, layer, lam_init, ymix, caches):
    blk = lambda c: pl.BlockSpec((SEQ, BRANCH), lambda b: (b, c))
    any_spec = pl.BlockSpec(memory_space=pl.ANY)
    k_spec = pl.BlockSpec((None, None, SEQ, BRANCH), lambda b: (b, layer, 0, 0))
    v_spec = pl.BlockSpec((None, None, SEQ, N_HEADS_C, HEAD_DIM_C), lambda b: (b, layer, 0, 0, 0))
    k_shape = jax.ShapeDtypeStruct((BATCH, DEPTH, SEQ, BRANCH), F32)
    v_shape = jax.ShapeDtypeStruct((BATCH, DEPTH, SEQ, N_HEADS_C, HEAD_DIM_C), F32)
    in_specs = [
        blk(C_Q), blk(C_K), blk(C_V), blk(C_CG),
        pl.BlockSpec((None, 4, QK_HALF), lambda b: (layer, 0, 0)),
        pl.BlockSpec((None, 1, HEAD_DIM_C), lambda b: (layer, 0, 0)),
        any_spec,
    ]
    args = [p, p, p, p, lambda_qk, subln_row, ymix]
    aliases = {6: 0}
    if caches is not None:
        in_specs += [any_spec, any_spec]
        args += list(caches)
        aliases.update({7: 1, 8: 2})
    return pl.pallas_call(
        functools.partial(_attn_ctx_kernel, lam_init),
        grid=(BATCH,),
        in_specs=in_specs,
        out_specs=[pl.BlockSpec((SEQ, BRANCH), lambda b: (b, M_ATTN)), k_spec, v_spec],
        out_shape=[jax.ShapeDtypeStruct((N_TOK, D_MIX), BF16), k_shape, v_shape],
        input_output_aliases=aliases,
        compiler_params=_params("arbitrary"),
        name="mixer_attn_ctx",
    )(*args)


def _rope(x, cos, sin_signed):
    lane = lax.broadcasted_iota(jnp.int32, x.shape, 1)
    first_half = (lane & (ROPE_AXIS_DIM - 1)) < (ROPE_AXIS_DIM // 2)
    half = ROPE_AXIS_DIM // 2
    partner = jnp.where(first_half,
                        pltpu.roll(x, x.shape[1] - half, axis=1),
                        pltpu.roll(x, half, axis=1))
    return x * cos + partner * sin_signed


def _attn_lat_kernel(lam_init, q_ref, k_ref, v_ref, g_ref, ck_ref, cv_ref, cosq_ref, sinq_ref,
                     cosk_ref, sink_ref, lam_ref, sw_ref, wo_ref, ymix_ref, o_ref, wo_bf_ref,
                     kc_ref, kr_ref, vt_ref):
    del ymix_ref
    wo_bf_ref[...] = wo_ref[...].astype(BF16)

    @pl.when(pl.program_id(1) == 0)
    def _():
        kc_ref[...] = ck_ref[...].astype(BF16)
        for h in range(N_HEADS_C):
            cols = slice(h * HEAD_DIM_C, (h + 1) * HEAD_DIM_C)
            kr_ref[:, cols] = _rope(k_ref[:, cols], cosk_ref[...], sink_ref[...]).astype(BF16)
            vt_ref[h, :, 0:PAST_LEN] = _with_ones_rows(cv_ref[:, h, :].T)
            vt_ref[h, :, PAST_LEN:PAST_LEN + DEC_SEQ] = _with_ones_rows(v_ref[:, cols].T)

    def store(h, y):
        o_ref[:, _head_cols(h)] = y.astype(BF16)

    _diff_attention(
        lambda h: _rope(q_ref[:, _head_cols(h)], cosq_ref[...], sinq_ref[...]),
        lambda h: [kc_ref[:, _head_cols(h)], kr_ref[:, _head_cols(h)]],
        lambda h: [vt_ref[h, :, 0:PAST_LEN], vt_ref[h, :, PAST_LEN:PAST_LEN + DEC_SEQ]],
        lambda h: g_ref[:, _head_cols(h)],
        store, _lambda(lam_ref, lam_init), lam_init, sw_ref[...])


def _rope_tables():
    pos = np.arange(DEC_SEQ)
    row = (pos // GRID_W).astype(np.float64)
    col = (pos % GRID_W).astype(np.float64)
    half = ROPE_AXIS_DIM // 2
    inv = ROPE_BASE ** (-np.arange(0, ROPE_AXIS_DIM, 2, dtype=np.float64) / ROPE_AXIS_DIM)
    lane = np.arange(HEAD_DIM_C)
    axis_is_col = (lane // ROPE_AXIS_DIM) % 2 == 1
    idx = lane % ROPE_AXIS_DIM
    ang = np.where(axis_is_col[None, :], col[:, None], row[:, None]) * inv[idx % half][None, :]
    sign = np.where(idx < half, -1.0, 1.0)[None, :]
    return (jnp.asarray(np.cos(ang), F32), jnp.asarray(np.sin(ang) * sign, F32))


def _mixer_attn_lat(p, cache_k4, cache_v, lambda_qk, subln_row, layer, lam_init, w_out, ymix):
    cos_t, sin_t = _rope_tables()
    q_tiles = DEC_SEQ // ATT_TQ
    q0 = N_CTX // ATT_TQ
    b0 = N_CTX // DEC_SEQ
    wo_rows = D_MIX // (DEC_BATCH * q_tiles)
    qblk = lambda c: pl.BlockSpec((ATT_TQ, BRANCH), lambda b, i: (q0 + b * q_tiles + i, c))
    kblk = lambda c: pl.BlockSpec((DEC_SEQ, BRANCH), lambda b, i: (b0 + b, c))
    cblk = pl.BlockSpec((None, None, PAST_LEN, BRANCH), lambda b, i: (b, layer, 0, 0))
    cvblk = pl.BlockSpec((None, None, PAST_LEN, N_HEADS_C, HEAD_DIM_C),
                         lambda b, i: (b, layer, 0, 0, 0))
    return pl.pallas_call(
        functools.partial(_attn_lat_kernel, lam_init),
        grid=(DEC_BATCH, q_tiles),
        in_specs=[
            qblk(C_Q), kblk(C_K), kblk(C_V), qblk(C_CG), cblk, cvblk,
            pl.BlockSpec((ATT_TQ, HEAD_DIM_C), lambda b, i: (i, 0)),
            pl.BlockSpec((ATT_TQ, HEAD_DIM_C), lambda b, i: (i, 0)),
            pl.BlockSpec((DEC_SEQ, HEAD_DIM_C), lambda b, i: (0, 0)),
            pl.BlockSpec((DEC_SEQ, HEAD_DIM_C), lambda b, i: (0, 0)),
            pl.BlockSpec((None, 4, QK_HALF), lambda b, i: (layer, 0, 0)),
            pl.BlockSpec((None, 1, HEAD_DIM_C), lambda b, i: (layer, 0, 0)),
            pl.BlockSpec((None, wo_rows, D_MODEL), lambda b, i: (layer, b * q_tiles + i, 0)),
            pl.BlockSpec(memory_space=pl.ANY),
        ],
        out_specs=[pl.BlockSpec((ATT_TQ, BRANCH), lambda b, i: (q0 + b * q_tiles + i, M_ATTN)),
                   pl.BlockSpec((wo_rows, D_MODEL), lambda b, i: (b * q_tiles + i, 0))],
        out_shape=[jax.ShapeDtypeStruct((N_TOK, D_MIX), BF16),
                   jax.ShapeDtypeStruct((D_MIX, D_MODEL), BF16)],
        input_output_aliases={13: 0},
        scratch_shapes=[pltpu.VMEM((PAST_LEN, BRANCH), BF16), pltpu.VMEM((DEC_SEQ, BRANCH), BF16),
                        pltpu.VMEM((N_HEADS_C, V_ROWS, PAST_LEN + DEC_SEQ), BF16)],
        compiler_params=_params("arbitrary", "arbitrary"),
        name="mixer_attn_lat",
    )(p, p, p, p, cache_k4, cache_v, cos_t, sin_t, cos_t, sin_t, lambda_qk, subln_row, w_out, ymix)


HY_ROWS = 1024
HY_PIECE = 512


def _dft_matrices(seq_len):
    n = 2 * seq_len
    f = np.arange(seq_len, dtype=np.float64)[:, None]
    s = np.arange(seq_len, dtype=np.float64)[None, :]
    theta = 2.0 * np.pi * f * s / n
    alt = np.where(np.arange(seq_len) % 2 == 0, 1.0, -1.0)
    ac = np.cos(theta)
    as_ = -np.sin(theta)
    as_[0, :] = alt
    bc = 2.0 * np.cos(theta.T) / n
    bc[:, 0] = 1.0 / n
    bs = -2.0 * np.sin(theta.T) / n
    bs[:, 0] = alt / n
    fwd = np.concatenate([ac, as_], axis=0)
    inv = np.concatenate([bc, bs], axis=1)
    return jnp.asarray(fwd, F32).astype(BF16), jnp.asarray(inv, F32).astype(BF16)


def _filter_features(seq_len):
    t_idx = np.arange(seq_len, dtype=np.float64)
    t_norm = np.linspace(0.0, 1.0, seq_len)
    bands = np.linspace(1e-4, HY_BANDS - 1, HY_BANDS)
    ang = (2.0 * math.pi * t_idx / seq_len)[:, None] * bands[None, :]
    feats = np.concatenate([t_norm[:, None], np.cos(ang), np.sin(ang)], axis=-1)
    feats = np.pad(feats, ((0, 0), (0, HY_PAD - HY_EMB)))
    deltas = np.abs(np.linspace(math.log(HY_TARGET) / HY_FAST_DECAY,
                                math.log(HY_TARGET) / HY_SLOW_DECAY, BRANCH))
    return (jnp.asarray(feats, F32), jnp.asarray(t_norm[:, None], F32),
            jnp.asarray(deltas[None, :], F32))


def _filter_kernel(seq_len, feats_ref, tn_ref, dl_ref, w1_ref, b1_ref, w2_ref, b2_ref, fr_ref,
                   w3f_ref, w3b_ref, fwd_ref, kr_ref, ki_ref, h_ref):
    sp = lambda x: _split(x, 2)

    @pl.when(pl.program_id(1) == 0)
    def _():
        fr = fr_ref[...]
        h = jnp.sin(fr * (_sdot(sp(feats_ref[...]), sp(w1_ref[...])) + b1_ref[...]))
        h_ref[...] = jnp.sin(fr * (_sdot(sp(h), sp(w2_ref[...])) + b2_ref[...]))

    h = h_ref[...]
    decay = jnp.exp(-tn_ref[...] * dl_ref[...])
    row = lax.broadcasted_iota(jnp.int32, (seq_len, BRANCH), 0)
    fwd = _bdot(h, w3f_ref[...]) * decay
    bwd = jnp.where(row == 0, 0.0, _bdot(h, w3b_ref[...]) * decay)
    norm = (jnp.sum(jnp.abs(fwd), axis=0, keepdims=True)
            + jnp.sum(jnp.abs(bwd), axis=0, keepdims=True))
    fwd = fwd / norm
    bwd = bwd / norm
    even = fwd + bwd
    alt = jnp.where((row & 1) == 0, 1.0, -1.0)
    nyquist = jnp.sum(alt * even, axis=0, keepdims=True)
    kr_ref[...] = jnp.dot(fwd_ref[0:seq_len, :], even.astype(BF16), preferred_element_type=F32)
    ki = jnp.dot(fwd_ref[seq_len:2 * seq_len, :], (fwd - bwd).astype(BF16),
                 preferred_element_type=F32)
    ki_ref[...] = jnp.where(row == 0, nyquist, ki)


def _hyena_spectrum(seq_len, filt, fwd_mat):
    feats, t_norm, deltas = _filter_features(seq_len)
    w1, b1, w2, b2, freq, w3 = filt
    full = lambda a: pl.BlockSpec(a.shape, lambda l, o: (0,) * a.ndim)
    lyr = lambda r, n: pl.BlockSpec((None, r, n), lambda l, o: (l, 0, 0))
    out = pl.BlockSpec((None, None, seq_len, BRANCH), lambda l, o: (l, o, 0, 0))
    return pl.pallas_call(
        functools.partial(_filter_kernel, seq_len),
        grid=(DEPTH, HY_ORDER),
        in_specs=[
            full(feats), full(t_norm), full(deltas),
            lyr(HY_PAD, HY_PAD), lyr(1, HY_PAD), lyr(HY_PAD, HY_PAD), lyr(1, HY_PAD), lyr(1, HY_PAD),
            pl.BlockSpec((None, HY_PAD, BRANCH), lambda l, o: (l, 0, 2 * o)),
            pl.BlockSpec((None, HY_PAD, BRANCH), lambda l, o: (l, 0, 2 * o + 1)),
            full(fwd_mat),
        ],
        out_specs=[out, out],
        out_shape=[jax.ShapeDtypeStruct((DEPTH, HY_ORDER, seq_len, BRANCH), F32)] * 2,
        scratch_shapes=[pltpu.VMEM((seq_len, HY_PAD), F32)],
        compiler_params=_params("arbitrary", "arbitrary"),
        name=f"hyena_spectrum_{seq_len}",
    )(feats, t_norm, deltas, w1, b1, w2, b2, freq, w3, w3, fwd_mat)


def _hyena_chains(seq_len, width, x1_ref, x2_ref, hv_ref, g_ref, cw_ref, cb_ref, kr_ref, ki_ref,
                  hb_ref, fwd_ref, inv_ref, o_ref):
    row = lax.broadcasted_iota(jnp.int32, (seq_len, width), 0)
    first, last = row == 0, row == seq_len - 1

    def chain(rs, c0):
        cs = slice(c0, c0 + width)

        def short_conv(x_ref, piece):
            x = x_ref[rs, cs]
            w = cw_ref[:, piece * BRANCH + c0:piece * BRANCH + c0 + width]
            b = cb_ref[:, piece * BRANCH + c0:piece * BRANCH + c0 + width]
            prev = jnp.where(first, 0.0, pltpu.roll(x, 1, axis=0))
            nxt = jnp.where(last, 0.0, pltpu.roll(x, seq_len - 1, axis=0))
            return prev * w[0:1] + x * w[1:2] + nxt * w[2:3] + b

        z = short_conv(hv_ref, 2)
        yield
        gate_refs = (x1_ref, x2_ref)
        piece = min(seq_len, HY_PIECE)
        for order in range(HY_ORDER):
            zb = z.astype(BF16)
            y_re, y_im = [], []
            for r in range(0, seq_len, piece):
                zr = jnp.dot(fwd_ref[r:r + piece, :], zb, preferred_element_type=F32)
                yield
                zi = jnp.dot(fwd_ref[seq_len + r:seq_len + r + piece, :], zb,
                             preferred_element_type=F32)
                yield
                kr, kp = kr_ref[order, r:r + piece, cs], ki_ref[order, r:r + piece, cs]
                ki, kn = kp, kr
                if r == 0:
                    dc = lax.broadcasted_iota(jnp.int32, (piece, width), 0) == 0
                    ki = jnp.where(dc, 0.0, kp)
                    kn = jnp.where(dc, kp, kr)
                y_re.append((zr * kr - zi * ki).astype(BF16))
                y_im.append((zr * ki + zi * kn).astype(BF16))
                yield
            yf = jnp.concatenate(y_re + y_im, axis=0)
            ys = []
            for r in range(0, seq_len, piece):
                ys.append(jnp.dot(inv_ref[r:r + piece, :], yf, preferred_element_type=F32))
                yield
                if r == 0:
                    gate = short_conv(gate_refs[order], order)
            y = ys[0] if len(ys) == 1 else jnp.concatenate(ys, axis=0)
            z = gate * (y + z * hb_ref[order:order + 1, cs])
        o_ref[rs, cs] = (_silu(g_ref[rs, cs]) * z).astype(BF16)

    waiting = [chain(slice(s * seq_len, (s + 1) * seq_len), c0)
               for s in range(HY_ROWS // seq_len) for c0 in range(0, BRANCH, width)]
    running = []
    while waiting or running:
        if waiting:
            running.append(waiting.pop(0))
        for gen in list(running):
            if next(gen, "done") == "done":
                running.remove(gen)


def _hyena_kernel(x1_ref, x2_ref, hv_ref, g_ref, cw_ref, cb_ref, hb_ref,
                  krc_ref, kic_ref, fwdc_ref, invc_ref, krl_ref, kil_ref, fwdl_ref, invl_ref,
                  ymix_ref, o_ref):
    del ymix_ref
    common = (x1_ref, x2_ref, hv_ref, g_ref, cw_ref, cb_ref)
    is_ctx = pl.program_id(0) < N_CTX // HY_ROWS

    @pl.when(is_ctx)
    def _():
        _hyena_chains(SEQ, BRANCH, *common, krc_ref, kic_ref, hb_ref, fwdc_ref, invc_ref, o_ref)

    @pl.when(jnp.logical_not(is_ctx))
    def _():
        _hyena_chains(DEC_SEQ, HY_CT, *common, krl_ref, kil_ref, hb_ref, fwdl_ref, invl_ref, o_ref)


def _mixer_hyena(p, conv_w, conv_b3, hyena_bias, spec_ctx, mats_ctx, spec_lat, mats_lat, layer,
                 ymix):
    blk = lambda c: pl.BlockSpec((HY_ROWS, BRANCH), lambda i: (i, c))
    once = pl.Buffered(1)
    lyr = lambda a: pl.BlockSpec((None,) + a.shape[1:], lambda i: (layer,) + (0,) * (a.ndim - 1),
                                 pipeline_mode=once)
    full = lambda a: pl.BlockSpec(a.shape, lambda i: (0,) * a.ndim, pipeline_mode=once)
    consts = [*spec_ctx, *mats_ctx, *spec_lat, *mats_lat]
    return pl.pallas_call(
        _hyena_kernel,
        grid=(N_TOK // HY_ROWS,),
        in_specs=[blk(C_DX1), blk(C_DX2), blk(C_DV), blk(C_DG),
                  lyr(conv_w), lyr(conv_b3), lyr(hyena_bias),
                  lyr(spec_ctx[0]), lyr(spec_ctx[1]), full(mats_ctx[0]), full(mats_ctx[1]),
                  lyr(spec_lat[0]), lyr(spec_lat[1]), full(mats_lat[0]), full(mats_lat[1]),
                  pl.BlockSpec(memory_space=pl.ANY)],
        out_specs=pl.BlockSpec((HY_ROWS, BRANCH), lambda i: (i, M_HYENA)),
        out_shape=jax.ShapeDtypeStruct((N_TOK, D_MIX), BF16),
        input_output_aliases={7 + len(consts): 0},
        compiler_params=_params("arbitrary"),
        name="mixer_hyena",
    )(p, p, p, p, conv_w, conv_b3, hyena_bias, *consts, ymix)


OUT_TM = 512


def _outproj_kernel(alpha, n_x, n_out, ymix_ref, *refs):
    x_refs = refs[:n_x]
    gate_ref, w_ref, b_ref, lng_ref, lnb_ref = refs[n_x:n_x + 5]
    o_refs = refs[n_x + 5:]
    is_ctx = pl.program_id(0) < N_CTX // OUT_TM

    def tile(x_ref, o_ref):
        y = jnp.dot(ymix_ref[...], w_ref[...], preferred_element_type=F32) + b_ref[...]
        r = alpha * x_ref[...] + gate_ref[...] * y
        o_ref[...] = _layer_norm(r) * lng_ref[...] + lnb_ref[...]

    if n_x == 1 and n_out == 1:
        tile(x_refs[0], o_refs[0])
    else:
        pl.when(is_ctx)(lambda: tile(x_refs[0], o_refs[0]))
        pl.when(jnp.logical_not(is_ctx))(lambda: tile(x_refs[-1], o_refs[-1]))


def _out_projection(ymix, xs, mod4, w_out_bf, b_out3, ln_g3, ln_b3, layer, split_out):
    alpha = (2.0 * DEPTH) ** 0.25
    row = lambda i: _cond_row(i, OUT_TM)
    vec = pl.BlockSpec((None, 1, D_MODEL), lambda i: (layer, 0, 0))
    if split_out:
        out_shape = [jax.ShapeDtypeStruct((N_CTX, D_MODEL), F32),
                     jax.ShapeDtypeStruct((N_LAT, D_MODEL), F32)]
    else:
        out_shape = [jax.ShapeDtypeStruct((N_TOK, D_MODEL), F32)]
    return pl.pallas_call(
        functools.partial(_outproj_kernel, alpha, len(xs), len(out_shape)),
        grid=(N_TOK // OUT_TM,),
        in_specs=[pl.BlockSpec((OUT_TM, D_MIX), lambda i: (i, 0))] + _token_specs(xs, OUT_TM) + [
            pl.BlockSpec((None, None, 1, D_MODEL), lambda i: (layer, row(i), 0, 2)),
            pl.BlockSpec((D_MIX, D_MODEL), lambda i: (0, 0), pipeline_mode=pl.Buffered(1)),
            vec, vec, vec,
        ],
        out_specs=_token_specs(out_shape, OUT_TM),
        out_shape=out_shape,
        compiler_params=_params("arbitrary"),
        name="out_projection",
    )(ymix, *xs, mod4, w_out_bf, b_out3, ln_g3, ln_b3)


def kernel(x_prompt, x_sample, cache_k, cache_v, c, c_ctx, w_mod, b_mod, w_in, gmlp_w, gmlp_b,
           pool_w, pool_scale, lambda_qk, subln_w, conv_w, conv_b, filt_w1, filt_b1, filt_w2,
           filt_b2, filt_freq, filt_w3, hyena_bias, w_out, b_out, ln_g, ln_b):
    xs = (x_prompt.reshape(N_CTX, D_MODEL), x_sample.reshape(N_LAT, D_MODEL))
    cond = jnp.concatenate(
        [c_ctx[None, :], c, jnp.zeros((N_COND - 1 - DEC_BATCH, D_MODEL), F32)], axis=0)
    mod4 = _modulation(cond, w_mod, b_mod).reshape(DEPTH, N_COND, 1, 3 * D_MODEL)

    cache_k4 = cache_k.reshape(DEC_BATCH, DEPTH, PAST_LEN, BRANCH)
    gmlp_b_rows = jnp.broadcast_to(gmlp_b[..., None], (DEPTH, N_GROUPS, CHUNK, GROUP_W))
    subln_row = subln_w.reshape(DEPTH, 1, HEAD_DIM_C)
    conv_b3 = conv_b.reshape(DEPTH, 1, 3 * BRANCH)
    b_out3 = b_out.reshape(DEPTH, 1, D_MODEL)
    ln_g3 = ln_g.reshape(DEPTH, 1, D_MODEL)
    ln_b3 = ln_b.reshape(DEPTH, 1, D_MODEL)

    pad_h = HY_PAD - HY_HIDDEN
    filt = (
        jnp.pad(filt_w1, ((0, 0), (0, HY_PAD - HY_EMB), (0, pad_h))),
        jnp.pad(filt_b1, ((0, 0), (0, pad_h))).reshape(DEPTH, 1, HY_PAD),
        jnp.pad(filt_w2, ((0, 0), (0, pad_h), (0, pad_h))),
        jnp.pad(filt_b2, ((0, 0), (0, pad_h))).reshape(DEPTH, 1, HY_PAD),
        jnp.pad(filt_freq, ((0, 0), (0, pad_h))).reshape(DEPTH, 1, HY_PAD),
        jnp.pad(filt_w3, ((0, 0), (0, pad_h), (0, 0))),
    )
    mats_ctx = _dft_matrices(SEQ)
    mats_lat = _dft_matrices(DEC_SEQ)
    spec_ctx = _hyena_spectrum(SEQ, filt, mats_ctx[0])
    spec_lat = _hyena_spectrum(DEC_SEQ, filt, mats_lat[0])

    caches = None
    for layer in range(DEPTH):
        lam_init = 0.8 - 0.6 * math.exp(-0.3 * layer)
        p = _in_projection(xs, mod4, w_in, layer)
        ymix = _mixer_gmlp_pool(p, gmlp_w[layer], gmlp_b_rows[layer], pool_w[layer],
                                pool_scale[layer].reshape(1, BRANCH))
        ymix, new_k, new_v = _mixer_attn_ctx(p, lambda_qk, subln_row, layer, lam_init, ymix, caches)
        caches = (new_k, new_v)
        ymix, w_out_bf = _mixer_attn_lat(p, cache_k4, cache_v, lambda_qk, subln_row, layer,
                                         lam_init, w_out, ymix)
        ymix = _mixer_hyena(p, conv_w, conv_b3, hyena_bias, spec_ctx, mats_ctx, spec_lat, mats_lat,
                            layer, ymix)
        xs = tuple(_out_projection(ymix, xs, mod4, w_out_bf, b_out3, ln_g3, ln_b3, layer,
                                   split_out=layer == DEPTH - 1))

    y_prompt = xs[0].reshape(BATCH, SEQ, D_MODEL)
    y_sample = xs[1].reshape(DEC_BATCH, DEC_SEQ, D_MODEL)
    new_k, new_v = caches
    return (y_prompt, y_sample,
            new_k.reshape(BATCH, DEPTH, SEQ, N_HEADS_C, 2, QK_HALF), new_v)
```

```python
import functools
import math

import numpy as np
import jax
import jax.numpy as jnp
from jax import lax
from jax.experimental import pallas as pl
from jax.experimental.pallas import tpu as pltpu

F32 = jnp.float32
BF16 = jnp.bfloat16

D_MODEL = 2048
BATCH = 16
SEQ = 256
DEPTH = 2
DEC_BATCH = 4
DEC_SEQ = 1024
PAST_LEN = 512
GRID_W = 64
BRANCH = 512
N_GROUPS = 4
GROUP_W = 128
CHUNK = 128
POOL_WINDOWS = (2, 4, 8, 16)
N_HEADS_C = 4
HEAD_DIM_C = 128
QK_HALF = 64
ROPE_AXIS_DIM = 32
ROPE_BASE = 10000.0
HY_BANDS = 16
HY_EMB = 33
HY_HIDDEN = 64
HY_ORDER = 2
HY_FAST_DECAY = 0.3
HY_SLOW_DECAY = 1.5
HY_TARGET = 1e-2
N_IN_PIECES = 13
D_IN = N_IN_PIECES * BRANCH
LN_EPS = 1e-6

N_CTX = BATCH * SEQ
N_LAT = DEC_BATCH * DEC_SEQ
N_TOK = N_CTX + N_LAT
N_COND = 8
LANES = 128
HY_PAD = LANES
HY_CT = 256
VMEM_LIMIT = 56 * 1024 * 1024

(C_AU, C_AV, C_AG, C_BX, C_BG, C_Q, C_K, C_V, C_CG, C_DX1, C_DX2, C_DV, C_DG) = range(13)
D_MIX = 4 * BRANCH
(M_GMLP, M_POOL, M_ATTN, M_HYENA) = range(4)


def _silu(x):
    return x * jax.nn.sigmoid(x)


def _bdot(a, b):
    return jnp.dot(a.astype(BF16), b.astype(BF16), preferred_element_type=F32)


def _split(x, n_terms):
    hi = x.astype(BF16)
    if n_terms == 1:
        return (hi,)
    return (hi, (x - hi.astype(F32)).astype(BF16))


def _sdot(a_terms, b_terms):
    acc = jnp.dot(a_terms[0], b_terms[0], preferred_element_type=F32)
    if len(a_terms) > 1:
        acc = acc + jnp.dot(a_terms[1], b_terms[0], preferred_element_type=F32)
    if len(b_terms) > 1:
        acc = acc + jnp.dot(a_terms[0], b_terms[1], preferred_element_type=F32)
    return acc


def _layer_norm(x):
    mu = jnp.mean(x, axis=-1, keepdims=True)
    xc = x - mu
    var = jnp.mean(xc * xc, axis=-1, keepdims=True)
    return xc * lax.rsqrt(var + LN_EPS)


def _cond_row(tile, rows_per_tile):
    n_ctx_tiles = N_CTX // rows_per_tile
    tiles_per_batch = DEC_SEQ // rows_per_tile
    return jnp.where(tile < n_ctx_tiles, 0, 1 + (tile - n_ctx_tiles) // tiles_per_batch)


def _params(*semantics):
    return pltpu.CompilerParams(dimension_semantics=semantics, vmem_limit_bytes=VMEM_LIMIT)


MOD_TN = 1024


def _mod_kernel(c_ref, w_ref, b_ref, o_ref):
    o_ref[...] = _bdot(_silu(c_ref[...]), w_ref[...]) + b_ref[...]


def _modulation(cond, w_mod, b_mod):
    n = 3 * D_MODEL
    return pl.pallas_call(
        _mod_kernel,
        grid=(DEPTH, n // MOD_TN),
        in_specs=[
            pl.BlockSpec((N_COND, D_MODEL), lambda l, j: (0, 0)),
            pl.BlockSpec((None, D_MODEL, MOD_TN), lambda l, j: (l, 0, j)),
            pl.BlockSpec((None, 1, MOD_TN), lambda l, j: (l, 0, j)),
        ],
        out_specs=pl.BlockSpec((None, N_COND, MOD_TN), lambda l, j: (l, 0, j)),
        out_shape=jax.ShapeDtypeStruct((DEPTH, N_COND, n), F32),
        compiler_params=_params("arbitrary", "arbitrary"),
        name="modulation",
    )(cond, w_mod, b_mod.reshape(DEPTH, 1, n))


IN_TM = 1024
IN_FIRST_TM = 2048
IN_TN = 512
IN_LN_ROWS = 256
IN_CAST_K = 512


def _token_specs(xs, tm, tile_of=lambda i: i):
    n_ctx_tiles = N_CTX // tm
    if len(xs) == 1:
        maps = [lambda i, *_: (tile_of(i), 0)]
    else:
        maps = [lambda i, *_: (jnp.minimum(tile_of(i), n_ctx_tiles - 1), 0),
                lambda i, *_: (jnp.maximum(tile_of(i) - n_ctx_tiles, 0), 0)]
    return [pl.BlockSpec((tm, D_MODEL), m) for m in maps]


def _inproj_kernel(n_x, tm, tile0, cast_w, *refs):
    x_refs = refs[:n_x]
    scale_ref, shift_ref, w_ref = refs[n_x:n_x + 3]
    if cast_w:
        o_ref, wb_ref, h_ref = refs[n_x + 3:]
    else:
        _, o_ref, h_ref = refs[n_x + 3:]

    def weights():
        if not cast_w:
            return w_ref[...]
        w = w_ref[...].astype(BF16)
        wb_ref[...] = w
        return w

    def first_column_step(x_ref):
        w = weights()

        def norm(r):
            rows = slice(r, r + IN_LN_ROWS)
            h = _layer_norm(x_ref[rows, :]) * (1.0 + scale_ref[...]) + shift_ref[...]
            h = h.astype(BF16)
            h_ref[rows, :] = h
            return h

        h = norm(0)
        for r in range(0, tm, IN_LN_ROWS):
            o_ref[r:r + IN_LN_ROWS, :] = jnp.dot(h, w, preferred_element_type=F32)
            if r + IN_LN_ROWS < tm:
                h = norm(r + IN_LN_ROWS)

    first = pl.program_id(1) == 0
    if n_x == 1:
        pl.when(first)(lambda: first_column_step(x_refs[0]))
    else:
        is_ctx = pl.program_id(0) + tile0 < N_CTX // IN_TM
        pl.when(jnp.logical_and(first, is_ctx))(lambda: first_column_step(x_refs[0]))
        pl.when(jnp.logical_and(first, jnp.logical_not(is_ctx)))(
            lambda: first_column_step(x_refs[1]))

    @pl.when(jnp.logical_not(first))
    def _():
        if not cast_w:
            o_ref[...] = jnp.dot(h_ref[...], w_ref[...], preferred_element_type=F32)
            return
        acc = None
        for k in range(0, D_MODEL, IN_CAST_K):
            w = w_ref[k:k + IN_CAST_K, :].astype(BF16)
            wb_ref[k:k + IN_CAST_K, :] = w
            part = jnp.dot(h_ref[:, k:k + IN_CAST_K], w, preferred_element_type=F32)
            acc = part if acc is None else acc + part
        o_ref[...] = acc


def _in_projection(xs, mod4, w_in, layer):
    assert IN_FIRST_TM <= N_CTX and IN_FIRST_TM % IN_TM == 0
    n_col = D_IN // IN_TN
    first_tiles = IN_FIRST_TM // IN_TM

    def mod_spec(tile0, piece):
        return pl.BlockSpec((None, None, 1, D_MODEL),
                            lambda i, j: (layer, _cond_row(i + tile0, IN_TM), 0, piece))

    p_shape = jax.ShapeDtypeStruct((N_TOK, D_IN), F32)
    p, w_bf = pl.pallas_call(
        functools.partial(_inproj_kernel, 1, IN_FIRST_TM, 0, True),
        grid=(1, n_col),
        in_specs=[
            pl.BlockSpec((IN_FIRST_TM, D_MODEL), lambda i, j: (0, 0), pipeline_mode=pl.Buffered(1)),
            mod_spec(0, 1), mod_spec(0, 0),
            pl.BlockSpec((None, D_MODEL, IN_TN), lambda i, j: (layer, 0, j)),
        ],
        out_specs=[pl.BlockSpec((IN_FIRST_TM, IN_TN), lambda i, j: (0, j)),
                   pl.BlockSpec((D_MODEL, IN_TN), lambda i, j: (0, j))],
        out_shape=[p_shape, jax.ShapeDtypeStruct((D_MODEL, D_IN), BF16)],
        scratch_shapes=[pltpu.VMEM((IN_FIRST_TM, D_MODEL), BF16)],
        compiler_params=_params("arbitrary", "arbitrary"),
        name="in_projection_first",
    )(xs[0], mod4, mod4, w_in)
    return pl.pallas_call(
        functools.partial(_inproj_kernel, len(xs), IN_TM, first_tiles, False),
        grid=(N_TOK // IN_TM - first_tiles, n_col),
        in_specs=_token_specs(xs, IN_TM, lambda i: i + first_tiles) + [
            mod_spec(first_tiles, 1), mod_spec(first_tiles, 0),
            pl.BlockSpec((D_MODEL, IN_TN), lambda i, j: (0, j)),
            pl.BlockSpec(memory_space=pl.ANY),
        ],
        out_specs=pl.BlockSpec((IN_TM, IN_TN), lambda i, j: (i + first_tiles, j)),
        out_shape=p_shape,
        input_output_aliases={len(xs) + 3: 0},
        scratch_shapes=[pltpu.VMEM((IN_TM, D_MODEL), BF16)],
        compiler_params=_params("arbitrary", "arbitrary"),
        name="in_projection",
    )(*xs, mod4, mod4, w_bf, p)


AB_TM = 1024


def _gmlp_tasks(u_ref, v_ref, g_ref, w_ref, b_ref, o_ref):
    ws = [w_ref[g].astype(BF16) for g in range(N_GROUPS)]

    def chunk(r):
        rows = slice(r, r + CHUNK)
        vn = _layer_norm(v_ref[rows, :]).astype(BF16)
        for g in range(N_GROUPS):
            cols = slice(g * GROUP_W, (g + 1) * GROUP_W)
            mixed = jnp.dot(ws[g], vn[:, cols], preferred_element_type=F32) + b_ref[g]
            o_ref[rows, cols] = (_silu(g_ref[rows, cols]) * u_ref[rows, cols] * mixed).astype(BF16)

    return [functools.partial(chunk, r) for r in range(0, AB_TM, CHUNK)]


POOL_BAND_GROUPS = (2, 3)
POOL_BAND_ROWS = 256
POOL_BAND_HALO = 128


def _window_sum_shifts(p, win, pos, seq_len):
    acc = p
    for d in range(-(win // 2), win // 2):
        if d == 0:
            continue
        shifted = pltpu.roll(p, (-d) % AB_TM, axis=0)
        valid = (pos >= -d) if d < 0 else (pos < seq_len - d)
        acc = acc + jnp.where(valid, shifted, 0.0)
    return acc


def _window_sum_band_tasks(p, band_ref, pieces):
    hi = p.astype(BF16)
    r1 = p - hi.astype(F32)
    mid = r1.astype(BF16)
    lo = (r1 - mid.astype(F32)).astype(BF16)
    terms = jnp.concatenate([hi, mid, lo], axis=1)

    def piece(r):
        k0, k1 = max(0, r - POOL_BAND_HALO), min(AB_TM, r + POOL_BAND_ROWS + POOL_BAND_HALO)
        s = jnp.dot(band_ref[r:r + POOL_BAND_ROWS, k0:k1], terms[k0:k1, :],
                    preferred_element_type=F32)
        pieces.append(s[:, :GROUP_W] + s[:, GROUP_W:2 * GROUP_W] + s[:, 2 * GROUP_W:])

    return [functools.partial(piece, r) for r in range(0, AB_TM, POOL_BAND_ROWS)]


def _pool_finish(g, acc, x_ref, g_ref, w_ref, s_ref, pos, seq_len, o_ref):
    cols = slice(g * GROUP_W, (g + 1) * GROUP_W)
    win = POOL_WINDOWS[g]
    count = jnp.minimum(pos + win // 2, seq_len) - jnp.maximum(pos - win // 2, 0)
    pooled = acc / count.astype(F32)
    y = _bdot(pooled - x_ref[:, cols], w_ref[g])
    o_ref[:, BRANCH + g * GROUP_W:BRANCH + (g + 1) * GROUP_W] = (
        _silu(g_ref[:, cols]) * (y * s_ref[:, cols])).astype(BF16)


def _mixer_ab_kernel(au_ref, av_ref, ag_ref, bx_ref, bg_ref, gw_ref, gb_ref, pw_ref, ps_ref,
                     band_a_ref, band_b_ref, o_ref):
    seq_len = jnp.where(pl.program_id(0) < N_CTX // AB_TM, SEQ, DEC_SEQ)
    pos = lax.broadcasted_iota(jnp.int32, (AB_TM, GROUP_W), 0) & (seq_len - 1)
    pool_refs = (bx_ref, bg_ref, pw_ref, ps_ref, pos, seq_len, o_ref)
    band_sums = {g: [] for g in POOL_BAND_GROUPS}
    mxu_tasks = [t for g, band in zip(POOL_BAND_GROUPS, (band_a_ref, band_b_ref))
                 for t in _window_sum_band_tasks(bx_ref[:, g * GROUP_W:(g + 1) * GROUP_W], band,
                                                 band_sums[g])]
    vpu_tasks = _gmlp_tasks(au_ref, av_ref, ag_ref, gw_ref, gb_ref, o_ref)
    for i in range(max(len(mxu_tasks), len(vpu_tasks))):
        for tasks in (mxu_tasks, vpu_tasks):
            if i < len(tasks):
                tasks[i]()
    for g, win in enumerate(POOL_WINDOWS):
        if g not in POOL_BAND_GROUPS:
            acc = _window_sum_shifts(bx_ref[:, g * GROUP_W:(g + 1) * GROUP_W], win, pos, seq_len)
            _pool_finish(g, acc, *pool_refs)
    for g in POOL_BAND_GROUPS:
        _pool_finish(g, jnp.concatenate(band_sums[g], axis=0), *pool_refs)


def _pool_bands():
    t = np.arange(AB_TM)
    bands = []
    for g in POOL_BAND_GROUPS:
        half = POOL_WINDOWS[g] // 2
        d = t[None, :] - t[:, None]
        near = (d >= -half) & (d < half)
        per_path = [near & ((t[None, :] // L) == (t[:, None] // L)) for L in (SEQ, DEC_SEQ)]
        bands.append(jnp.asarray(np.stack(per_path), F32).astype(BF16))
    return bands


def _mixer_gmlp_pool(p, gmlp_w, gmlp_b_rows, pool_w, pool_scale_row):
    assert M_POOL == M_GMLP + 1
    n_steps = N_TOK // AB_TM
    blk = lambda c: pl.BlockSpec((AB_TM, BRANCH), lambda i: (i, c))
    full = lambda a: pl.BlockSpec(a.shape, lambda i: (0,) * a.ndim)
    band = pl.BlockSpec((None, AB_TM, AB_TM), lambda i: (jnp.where(i < N_CTX // AB_TM, 0, 1), 0, 0))
    return pl.pallas_call(
        _mixer_ab_kernel,
        grid=(n_steps,),
        in_specs=[blk(C_AU), blk(C_AV), blk(C_AG), blk(C_BX), blk(C_BG),
                  full(gmlp_w), full(gmlp_b_rows), full(pool_w), full(pool_scale_row), band, band],
        out_specs=pl.BlockSpec((AB_TM, 2 * BRANCH), lambda i: (i, M_GMLP // 2)),
        out_shape=jax.ShapeDtypeStruct((N_TOK, D_MIX), BF16),
        compiler_params=_params("arbitrary"),
        name="mixer_gmlp_pool",
    )(p, p, p, p, p, gmlp_w, gmlp_b_rows, pool_w, pool_scale_row, *_pool_bands())


ATT_TQ = 256
V_ROWS = HEAD_DIM_C + 16
NT_DIMS = (((1,), (1,)), ((), ()))


def _lambda(lam_ref, lam_init):
    lq = lam_ref[...]
    a = jnp.sum(lq[0:1] * lq[1:2], axis=-1, keepdims=True)
    b = jnp.sum(lq[2:3] * lq[3:4], axis=-1, keepdims=True)
    return jnp.exp(a) - jnp.exp(b) + lam_init


def _map_masks():
    lane = lax.broadcasted_iota(jnp.int32, (1, HEAD_DIM_C), 1)
    m0 = (lane < QK_HALF).astype(F32)
    return m0, 1.0 - m0


def _scores_t(q, keys):
    return [lax.dot_general(k, q, NT_DIMS, preferred_element_type=F32) for k in keys]


def _softmax_v_t(s, vals_t):
    m = functools.reduce(jnp.maximum, [jnp.max(x, axis=0, keepdims=True) for x in s])
    acc = functools.reduce(
        jnp.add, [jnp.dot(v, jnp.exp2(x - m).astype(BF16), preferred_element_type=F32)
                  for x, v in zip(s, vals_t)])
    return acc[:HEAD_DIM_C] * (1.0 / acc[HEAD_DIM_C:HEAD_DIM_C + 1])


def _with_ones_rows(v_t):
    ones = jnp.ones((V_ROWS - HEAD_DIM_C, v_t.shape[1]), BF16)
    return jnp.concatenate([v_t.astype(BF16), ones], axis=0)


def _diff_attention(q_of, keys_of, vals_t_of, gate_of, store, lam, lam_init, subln):
    masks = _map_masks()
    tasks = [(h, mp) for h in range(N_HEADS_C) for mp in range(2)]
    per_head = {}

    def operands(h):
        if h not in per_head:
            q = q_of(h) * (QK_HALF ** -0.5 * math.log2(math.e))
            per_head[h] = (q, keys_of(h), vals_t_of(h))
        return per_head[h]

    def scores(h, mp):
        q, keys, _ = operands(h)
        return _scores_t((q * masks[mp]).astype(BF16), keys)

    nxt = scores(*tasks[0])
    outs = []
    for i, (h, mp) in enumerate(tasks):
        cur = nxt
        if i + 1 < len(tasks):
            nxt = scores(*tasks[i + 1])
        outs.append(_softmax_v_t(cur, operands(h)[2]))
        if mp == 1:
            o = (outs[0] - lam * outs[1]).T
            outs = []
            o = o * lax.rsqrt(jnp.mean(o * o, axis=-1, keepdims=True) + 1e-5)
            o = o * subln * (1.0 - lam_init)
            store(h, _silu(gate_of(h)) * o)


def _head_cols(h):
    return slice(h * HEAD_DIM_C, (h + 1) * HEAD_DIM_C)


def _attn_ctx_kernel(lam_init, q_ref, k_ref, v_ref, g_ref, lam_ref, sw_ref, *rest):
    o_ref, ko_ref, vo_ref = rest[-3:]
    ko_ref[...] = k_ref[...]
    for h in range(N_HEADS_C):
        vo_ref[:, h, :] = v_ref[:, _head_cols(h)]

    def store(h, y):
        o_ref[:, _head_cols(h)] = y.astype(BF16)

    _diff_attention(
        lambda h: q_ref[:, _head_cols(h)],
        lambda h: [k_ref[:, _head_cols(h)].astype(BF16)],
        lambda h: [_with_ones_rows(v_ref[:, _head_cols(h)].T)],
        lambda h: g_ref[:, _head_cols(h)],
        store, _lambda(lam_ref, lam_init), lam_init, sw_ref[...])


def _mixer_attn_ctx(p, lambda_qk, subln_row, layer, lam_init, ymix, caches):
    blk = lambda c: pl.BlockSpec((SEQ, BRANCH), lambda b: (b, c))
    any_spec = pl.BlockSpec(memory_space=pl.ANY)
    k_spec = pl.BlockSpec((None, None, SEQ, BRANCH), lambda b: (b, layer, 0, 0))
    v_spec = pl.BlockSpec((None, None, SEQ, N_HEADS_C, HEAD_DIM_C), lambda b: (b, layer, 0, 0, 0))
    k_shape = jax.ShapeDtypeStruct((BATCH, DEPTH, SEQ, BRANCH), F32)
    v_shape = jax.ShapeDtypeStruct((BATCH, DEPTH, SEQ, N_HEADS_C, HEAD_DIM_C), F32)
    in_specs = [
        blk(C_Q), blk(C_K), blk(C_V), blk(C_CG),
        pl.BlockSpec((None, 4, QK_HALF), lambda b: (layer, 0, 0)),
        pl.BlockSpec((None, 1, HEAD_DIM_C), lambda b: (layer, 0, 0)),
        any_spec,
    ]
    args = [p, p, p, p, lambda_qk, subln_row, ymix]
    aliases = {6: 0}
    if caches is not None:
        in_specs += [any_spec, any_spec]
        args += list(caches)
        aliases.update({7: 1, 8: 2})
    return pl.pallas_call(
        functools.partial(_attn_ctx_kernel, lam_init),
        grid=(BATCH,),
        in_specs=in_specs,
        out_specs=[pl.BlockSpec((SEQ, BRANCH), lambda b: (b, M_ATTN)), k_spec, v_spec],
        out_shape=[jax.ShapeDtypeStruct((N_TOK, D_MIX), BF16), k_shape, v_shape],
        input_output_aliases=aliases,
        compiler_params=_params("arbitrary"),
        name="mixer_attn_ctx",
    )(*args)


def _rope(x, cos, sin_signed):
    lane = lax.broadcasted_iota(jnp.int32, x.shape, 1)
    first_half = (lane & (ROPE_AXIS_DIM - 1)) < (ROPE_AXIS_DIM // 2)
    half = ROPE_AXIS_DIM // 2
    partner = jnp.where(first_half,
                        pltpu.roll(x, x.shape[1] - half, axis=1),
                        pltpu.roll(x, half, axis=1))
    return x * cos + partner * sin_signed


def _attn_lat_kernel(lam_init, q_ref, k_ref, v_ref, g_ref, ck_ref, cv_ref, cosq_ref, sinq_ref,
                     cosk_ref, sink_ref, lam_ref, sw_ref, wo_ref, ymix_ref, o_ref, wo_bf_ref,
                     kc_ref, kr_ref, vt_ref):
    del ymix_ref
    wo_bf_ref[...] = wo_ref[...].astype(BF16)

    @pl.when(pl.program_id(1) == 0)
    def _():
        kc_ref[...] = ck_ref[...].astype(BF16)
        for h in range(N_HEADS_C):
            cols = slice(h * HEAD_DIM_C, (h + 1) * HEAD_DIM_C)
            kr_ref[:, cols] = _rope(k_ref[:, cols], cosk_ref[...], sink_ref[...]).astype(BF16)
            vt_ref[h, :, 0:PAST_LEN] = _with_ones_rows(cv_ref[:, h, :].T)
            vt_ref[h, :, PAST_LEN:PAST_LEN + DEC_SEQ] = _with_ones_rows(v_ref[:, cols].T)

    def store(h, y):
        o_ref[:, _head_cols(h)] = y.astype(BF16)

    _diff_attention(
        lambda h: _rope(q_ref[:, _head_cols(h)], cosq_ref[...], sinq_ref[...]),
        lambda h: [kc_ref[:, _head_cols(h)], kr_ref[:, _head_cols(h)]],
        lambda h: [vt_ref[h, :, 0:PAST_LEN], vt_ref[h, :, PAST_LEN:PAST_LEN + DEC_SEQ]],
        lambda h: g_ref[:, _head_cols(h)],
        store, _lambda(lam_ref, lam_init), lam_init, sw_ref[...])


def _rope_tables():
    pos = np.arange(DEC_SEQ)
    row = (pos // GRID_W).astype(np.float64)
    col = (pos % GRID_W).astype(np.float64)
    half = ROPE_AXIS_DIM // 2
    inv = ROPE_BASE ** (-np.arange(0, ROPE_AXIS_DIM, 2, dtype=np.float64) / ROPE_AXIS_DIM)
    lane = np.arange(HEAD_DIM_C)
    axis_is_col = (lane // ROPE_AXIS_DIM) % 2 == 1
    idx = lane % ROPE_AXIS_DIM
    ang = np.where(axis_is_col[None, :], col[:, None], row[:, None]) * inv[idx % half][None, :]
    sign = np.where(idx < half, -1.0, 1.0)[None, :]
    return (jnp.asarray(np.cos(ang), F32), jnp.asarray(np.sin(ang) * sign, F32))


def _mixer_attn_lat(p, cache_k4, cache_v, lambda_qk, subln_row, layer, lam_init, w_out, ymix):
    cos_t, sin_t = _rope_tables()
    q_tiles = DEC_SEQ // ATT_TQ
    q0 = N_CTX // ATT_TQ
    b0 = N_CTX // DEC_SEQ
    wo_rows = D_MIX // (DEC_BATCH * q_tiles)
    qblk = lambda c: pl.BlockSpec((ATT_TQ, BRANCH), lambda b, i: (q0 + b * q_tiles + i, c))
    kblk = lambda c: pl.BlockSpec((DEC_SEQ, BRANCH), lambda b, i: (b0 + b, c))
    cblk = pl.BlockSpec((None, None, PAST_LEN, BRANCH), lambda b, i: (b, layer, 0, 0))
    cvblk = pl.BlockSpec((None, None, PAST_LEN, N_HEADS_C, HEAD_DIM_C),
                         lambda b, i: (b, layer, 0, 0, 0))
    return pl.pallas_call(
        functools.partial(_attn_lat_kernel, lam_init),
        grid=(DEC_BATCH, q_tiles),
        in_specs=[
            qblk(C_Q), kblk(C_K), kblk(C_V), qblk(C_CG), cblk, cvblk,
            pl.BlockSpec((ATT_TQ, HEAD_DIM_C), lambda b, i: (i, 0)),
            pl.BlockSpec((ATT_TQ, HEAD_DIM_C), lambda b, i: (i, 0)),
            pl.BlockSpec((DEC_SEQ, HEAD_DIM_C), lambda b, i: (0, 0)),
            pl.BlockSpec((DEC_SEQ, HEAD_DIM_C), lambda b, i: (0, 0)),
            pl.BlockSpec((None, 4, QK_HALF), lambda b, i: (layer, 0, 0)),
            pl.BlockSpec((None, 1, HEAD_DIM_C), lambda b, i: (layer, 0, 0)),
            pl.BlockSpec((None, wo_rows, D_MODEL), lambda b, i: (layer, b * q_tiles + i, 0)),
            pl.BlockSpec(memory_space=pl.ANY),
        ],
        out_specs=[pl.BlockSpec((ATT_TQ, BRANCH), lambda b, i: (q0 + b * q_tiles + i, M_ATTN)),
                   pl.BlockSpec((wo_rows, D_MODEL), lambda b, i: (b * q_tiles + i, 0))],
        out_shape=[jax.ShapeDtypeStruct((N_TOK, D_MIX), BF16),
                   jax.ShapeDtypeStruct((D_MIX, D_MODEL), BF16)],
        input_output_aliases={13: 0},
        scratch_shapes=[pltpu.VMEM((PAST_LEN, BRANCH), BF16), pltpu.VMEM((DEC_SEQ, BRANCH), BF16),
                        pltpu.VMEM((N_HEADS_C, V_ROWS, PAST_LEN + DEC_SEQ), BF16)],
        compiler_params=_params("arbitrary", "arbitrary"),
        name="mixer_attn_lat",
    )(p, p, p, p, cache_k4, cache_v, cos_t, sin_t, cos_t, sin_t, lambda_qk, subln_row, w_out, ymix)


HY_ROWS = 1024
HY_PIECE = 512


def _dft_matrices(seq_len):
    n = 2 * seq_len
    f = np.arange(seq_len, dtype=np.float64)[:, None]
    s = np.arange(seq_len, dtype=np.float64)[None, :]
    theta = 2.0 * np.pi * f * s / n
    alt = np.where(np.arange(seq_len) % 2 == 0, 1.0, -1.0)
    ac = np.cos(theta)
    as_ = -np.sin(theta)
    as_[0, :] = alt
    bc = 2.0 * np.cos(theta.T) / n
    bc[:, 0] = 1.0 / n
    bs = -2.0 * np.sin(theta.T) / n
    bs[:, 0] = alt / n
    fwd = np.concatenate([ac, as_], axis=0)
    inv = np.concatenate([bc, bs], axis=1)
    return jnp.asarray(fwd, F32).astype(BF16), jnp.asarray(inv, F32).astype(BF16)


def _filter_features(seq_len):
    t_idx = np.arange(seq_len, dtype=np.float64)
    t_norm = np.linspace(0.0, 1.0, seq_len)
    bands = np.linspace(1e-4, HY_BANDS - 1, HY_BANDS)
    ang = (2.0 * math.pi * t_idx / seq_len)[:, None] * bands[None, :]
    feats = np.concatenate([t_norm[:, None], np.cos(ang), np.sin(ang)], axis=-1)
    feats = np.pad(feats, ((0, 0), (0, HY_PAD - HY_EMB)))
    deltas = np.abs(np.linspace(math.log(HY_TARGET) / HY_FAST_DECAY,
                                math.log(HY_TARGET) / HY_SLOW_DECAY, BRANCH))
    return (jnp.asarray(feats, F32), jnp.asarray(t_norm[:, None], F32),
            jnp.asarray(deltas[None, :], F32))


def _filter_kernel(seq_len, feats_ref, tn_ref, dl_ref, w1_ref, b1_ref, w2_ref, b2_ref, fr_ref,
                   w3f_ref, w3b_ref, fwd_ref, kr_ref, ki_ref, h_ref):
    sp = lambda x: _split(x, 2)

    @pl.when(pl.program_id(1) == 0)
    def _():
        fr = fr_ref[...]
        h = jnp.sin(fr * (_sdot(sp(feats_ref[...]), sp(w1_ref[...])) + b1_ref[...]))
        h_ref[...] = jnp.sin(fr * (_sdot(sp(h), sp(w2_ref[...])) + b2_ref[...]))

    h = h_ref[...]
    decay = jnp.exp(-tn_ref[...] * dl_ref[...])
    row = lax.broadcasted_iota(jnp.int32, (seq_len, BRANCH), 0)
    fwd = _bdot(h, w3f_ref[...]) * decay
    bwd = jnp.where(row == 0, 0.0, _bdot(h, w3b_ref[...]) * decay)
    norm = (jnp.sum(jnp.abs(fwd), axis=0, keepdims=True)
            + jnp.sum(jnp.abs(bwd), axis=0, keepdims=True))
    fwd = fwd / norm
    bwd = bwd / norm
    even = fwd + bwd
    alt = jnp.where((row & 1) == 0, 1.0, -1.0)
    nyquist = jnp.sum(alt * even, axis=0, keepdims=True)
    kr_ref[...] = jnp.dot(fwd_ref[0:seq_len, :], even.astype(BF16), preferred_element_type=F32)
    ki = jnp.dot(fwd_ref[seq_len:2 * seq_len, :], (fwd - bwd).astype(BF16),
                 preferred_element_type=F32)
    ki_ref[...] = jnp.where(row == 0, nyquist, ki)


def _hyena_spectrum(seq_len, filt, fwd_mat):
    feats, t_norm, deltas = _filter_features(seq_len)
    w1, b1, w2, b2, freq, w3 = filt
    full = lambda a: pl.BlockSpec(a.shape, lambda l, o: (0,) * a.ndim)
    lyr = lambda r, n: pl.BlockSpec((None, r, n), lambda l, o: (l, 0, 0))
    out = pl.BlockSpec((None, None, seq_len, BRANCH), lambda l, o: (l, o, 0, 0))
    return pl.pallas_call(
        functools.partial(_filter_kernel, seq_len),
        grid=(DEPTH, HY_ORDER),
        in_specs=[
            full(feats), full(t_norm), full(deltas),
            lyr(HY_PAD, HY_PAD), lyr(1, HY_PAD), lyr(HY_PAD, HY_PAD), lyr(1, HY_PAD), lyr(1, HY_PAD),
            pl.BlockSpec((None, HY_PAD, BRANCH), lambda l, o: (l, 0, 2 * o)),
            pl.BlockSpec((None, HY_PAD, BRANCH), lambda l, o: (l, 0, 2 * o + 1)),
            full(fwd_mat),
        ],
        out_specs=[out, out],
        out_shape=[jax.ShapeDtypeStruct((DEPTH, HY_ORDER, seq_len, BRANCH), F32)] * 2,
        scratch_shapes=[pltpu.VMEM((seq_len, HY_PAD), F32)],
        compiler_params=_params("arbitrary", "arbitrary"),
        name=f"hyena_spectrum_{seq_len}",
    )(feats, t_norm, deltas, w1, b1, w2, b2, freq, w3, w3, fwd_mat)


def _hyena_chains(seq_len, width, x1_ref, x2_ref, hv_ref, g_ref, cw_ref, cb_ref, kr_ref, ki_ref,
                  hb_ref, fwd_ref, inv_ref, o_ref):
    row = lax.broadcasted_iota(jnp.int32, (seq_len, width), 0)
    first, last = row == 0, row == seq_len - 1

    def chain(rs, c0):
        cs = slice(c0, c0 + width)

        def short_conv(x_ref, piece):
            x = x_ref[rs, cs]
            w = cw_ref[:, piece * BRANCH + c0:piece * BRANCH + c0 + width]
            b = cb_ref[:, piece * BRANCH + c0:piece * BRANCH + c0 + width]
            prev = jnp.where(first, 0.0, pltpu.roll(x, 1, axis=0))
            nxt = jnp.where(last, 0.0, pltpu.roll(x, seq_len - 1, axis=0))
            return prev * w[0:1] + x * w[1:2] + nxt * w[2:3] + b

        z = short_conv(hv_ref, 2)
        yield
        gate_refs = (x1_ref, x2_ref)
        piece = min(seq_len, HY_PIECE)
        for order in range(HY_ORDER):
            zb = z.astype(BF16)
            y_re, y_im = [], []
            for r in range(0, seq_len, piece):
                zr = jnp.dot(fwd_ref[r:r + piece, :], zb, preferred_element_type=F32)
                yield
                zi = jnp.dot(fwd_ref[seq_len + r:seq_len + r + piece, :], zb,
                             preferred_element_type=F32)
                yield
                kr, kp = kr_ref[order, r:r + piece, cs], ki_ref[order, r:r + piece, cs]
                ki, kn = kp, kr
                if r == 0:
                    dc = lax.broadcasted_iota(jnp.int32, (piece, width), 0) == 0
                    ki = jnp.where(dc, 0.0, kp)
                    kn = jnp.where(dc, kp, kr)
                y_re.append((zr * kr - zi * ki).astype(BF16))
                y_im.append((zr * ki + zi * kn).astype(BF16))
                yield
            yf = jnp.concatenate(y_re + y_im, axis=0)
            ys = []
            for r in range(0, seq_len, piece):
                ys.append(jnp.dot(inv_ref[r:r + piece, :], yf, preferred_element_type=F32))
                yield
                if r == 0:
                    gate = short_conv(gate_refs[order], order)
            y = ys[0] if len(ys) == 1 else jnp.concatenate(ys, axis=0)
            z = gate * (y + z * hb_ref[order:order + 1, cs])
        o_ref[rs, cs] = (_silu(g_ref[rs, cs]) * z).astype(BF16)

    waiting = [chain(slice(s * seq_len, (s + 1) * seq_len), c0)
               for s in range(HY_ROWS // seq_len) for c0 in range(0, BRANCH, width)]
    running = []
    while waiting or running:
        if waiting:
            running.append(waiting.pop(0))
        for gen in list(running):
            if next(gen, "done") == "done":
                running.remove(gen)


def _hyena_kernel(x1_ref, x2_ref, hv_ref, g_ref, cw_ref, cb_ref, hb_ref,
                  krc_ref, kic_ref, fwdc_ref, invc_ref, krl_ref, kil_ref, fwdl_ref, invl_ref,
                  ymix_ref, o_ref):
    del ymix_ref
    common = (x1_ref, x2_ref, hv_ref, g_ref, cw_ref, cb_ref)
    is_ctx = pl.program_id(0) < N_CTX // HY_ROWS

    @pl.when(is_ctx)
    def _():
        _hyena_chains(SEQ, BRANCH, *common, krc_ref, kic_ref, hb_ref, fwdc_ref, invc_ref, o_ref)

    @pl.when(jnp.logical_not(is_ctx))
    def _():
        _hyena_chains(DEC_SEQ, HY_CT, *common, krl_ref, kil_ref, hb_ref, fwdl_ref, invl_ref, o_ref)


def _mixer_hyena(p, conv_w, conv_b3, hyena_bias, spec_ctx, mats_ctx, spec_lat, mats_lat, layer,
                 ymix):
    blk = lambda c: pl.BlockSpec((HY_ROWS, BRANCH), lambda i: (i, c))
    once = pl.Buffered(1)
    lyr = lambda a: pl.BlockSpec((None,) + a.shape[1:], lambda i: (layer,) + (0,) * (a.ndim - 1),
                                 pipeline_mode=once)
    full = lambda a: pl.BlockSpec(a.shape, lambda i: (0,) * a.ndim, pipeline_mode=once)
    consts = [*spec_ctx, *mats_ctx, *spec_lat, *mats_lat]
    return pl.pallas_call(
        _hyena_kernel,
        grid=(N_TOK // HY_ROWS,),
        in_specs=[blk(C_DX1), blk(C_DX2), blk(C_DV), blk(C_DG),
                  lyr(conv_w), lyr(conv_b3), lyr(hyena_bias),
                  lyr(spec_ctx[0]), lyr(spec_ctx[1]), full(mats_ctx[0]), full(mats_ctx[1]),
                  lyr(spec_lat[0]), lyr(spec_lat[1]), full(mats_lat[0]), full(mats_lat[1]),
                  pl.BlockSpec(memory_space=pl.ANY)],
        out_specs=pl.BlockSpec((HY_ROWS, BRANCH), lambda i: (i, M_HYENA)),
        out_shape=jax.ShapeDtypeStruct((N_TOK, D_MIX), BF16),
        input_output_aliases={7 + len(consts): 0},
        compiler_params=_params("arbitrary"),
        name="mixer_hyena",
    )(p, p, p, p, conv_w, conv_b3, hyena_bias, *consts, ymix)


OUT_TM = 512


def _outproj_kernel(alpha, n_x, n_out, ymix_ref, *refs):
    x_refs = refs[:n_x]
    gate_ref, w_ref, b_ref, lng_ref, lnb_ref = refs[n_x:n_x + 5]
    o_refs = refs[n_x + 5:]
    is_ctx = pl.program_id(0) < N_CTX // OUT_TM

    def tile(x_ref, o_ref):
        y = jnp.dot(ymix_ref[...], w_ref[...], preferred_element_type=F32) + b_ref[...]
        r = alpha * x_ref[...] + gate_ref[...] * y
        o_ref[...] = _layer_norm(r) * lng_ref[...] + lnb_ref[...]

    if n_x == 1 and n_out == 1:
        tile(x_refs[0], o_refs[0])
    else:
        pl.when(is_ctx)(lambda: tile(x_refs[0], o_refs[0]))
        pl.when(jnp.logical_not(is_ctx))(lambda: tile(x_refs[-1], o_refs[-1]))


def _out_projection(ymix, xs, mod4, w_out_bf, b_out3, ln_g3, ln_b3, layer, split_out):
    alpha = (2.0 * DEPTH) ** 0.25
    row = lambda i: _cond_row(i, OUT_TM)
    vec = pl.BlockSpec((None, 1, D_MODEL), lambda i: (layer, 0, 0))
    if split_out:
        out_shape = [jax.ShapeDtypeStruct((N_CTX, D_MODEL), F32),
                     jax.ShapeDtypeStruct((N_LAT, D_MODEL), F32)]
    else:
        out_shape = [jax.ShapeDtypeStruct((N_TOK, D_MODEL), F32)]
    return pl.pallas_call(
        functools.partial(_outproj_kernel, alpha, len(xs), len(out_shape)),
        grid=(N_TOK // OUT_TM,),
        in_specs=[pl.BlockSpec((OUT_TM, D_MIX), lambda i: (i, 0))] + _token_specs(xs, OUT_TM) + [
            pl.BlockSpec((None, None, 1, D_MODEL), lambda i: (layer, row(i), 0, 2)),
            pl.BlockSpec((D_MIX, D_MODEL), lambda i: (0, 0), pipeline_mode=pl.Buffered(1)),
            vec, vec, vec,
        ],
        out_specs=_token_specs(out_shape, OUT_TM),
        out_shape=out_shape,
        compiler_params=_params("arbitrary"),
        name="out_projection",
    )(ymix, *xs, mod4, w_out_bf, b_out3, ln_g3, ln_b3)


def kernel(x_prompt, x_sample, cache_k, cache_v, c, c_ctx, w_mod, b_mod, w_in, gmlp_w, gmlp_b,
           pool_w, pool_scale, lambda_qk, subln_w, conv_w, conv_b, filt_w1, filt_b1, filt_w2,
           filt_b2, filt_freq, filt_w3, hyena_bias, w_out, b_out, ln_g, ln_b):
    xs = (x_prompt.reshape(N_CTX, D_MODEL), x_sample.reshape(N_LAT, D_MODEL))
    cond = jnp.concatenate(
        [c_ctx[None, :], c, jnp.zeros((N_COND - 1 - DEC_BATCH, D_MODEL), F32)], axis=0)
    mod4 = _modulation(cond, w_mod, b_mod).reshape(DEPTH, N_COND, 1, 3 * D_MODEL)

    cache_k4 = cache_k.reshape(DEC_BATCH, DEPTH, PAST_LEN, BRANCH)
    gmlp_b_rows = jnp.broadcast_to(gmlp_b[..., None], (DEPTH, N_GROUPS, CHUNK, GROUP_W))
    subln_row = subln_w.reshape(DEPTH, 1, HEAD_DIM_C)
    conv_b3 = conv_b.reshape(DEPTH, 1, 3 * BRANCH)
    b_out3 = b_out.reshape(DEPTH, 1, D_MODEL)
    ln_g3 = ln_g.reshape(DEPTH, 1, D_MODEL)
    ln_b3 = ln_b.reshape(DEPTH, 1, D_MODEL)

    pad_h = HY_PAD - HY_HIDDEN
    filt = (
        jnp.pad(filt_w1, ((0, 0), (0, HY_PAD - HY_EMB), (0, pad_h))),
        jnp.pad(filt_b1, ((0, 0), (0, pad_h))).reshape(DEPTH, 1, HY_PAD),
        jnp.pad(filt_w2, ((0, 0), (0, pad_h), (0, pad_h))),
        jnp.pad(filt_b2, ((0, 0), (0, pad_h))).reshape(DEPTH, 1, HY_PAD),
        jnp.pad(filt_freq, ((0, 0), (0, pad_h))).reshape(DEPTH, 1, HY_PAD),
        jnp.pad(filt_w3, ((0, 0), (0, pad_h), (0, 0))),
    )
    mats_ctx = _dft_matrices(SEQ)
    mats_lat = _dft_matrices(DEC_SEQ)
    spec_ctx = _hyena_spectrum(SEQ, filt, mats_ctx[0])
    spec_lat = _hyena_spectrum(DEC_SEQ, filt, mats_lat[0])

    caches = None
    for layer in range(DEPTH):
        lam_init = 0.8 - 0.6 * math.exp(-0.3 * layer)
        p = _in_projection(xs, mod4, w_in, layer)
        ymix = _mixer_gmlp_pool(p, gmlp_w[layer], gmlp_b_rows[layer], pool_w[layer],
                                pool_scale[layer].reshape(1, BRANCH))
        ymix, new_k, new_v = _mixer_attn_ctx(p, lambda_qk, subln_row, layer, lam_init, ymix, caches)
        caches = (new_k, new_v)
        ymix, w_out_bf = _mixer_attn_lat(p, cache_k4, cache_v, lambda_qk, subln_row, layer,
                                         lam_init, w_out, ymix)
        ymix = _mixer_hyena(p, conv_w, conv_b3, hyena_bias, spec_ctx, mats_ctx, spec_lat, mats_lat,
                            layer, ymix)
        xs = tuple(_out_projection(ymix, xs, mod4, w_out_bf, b_out3, ln_g3, ln_b3, layer,
                                   split_out=layer == DEPTH - 1))

    y_prompt = xs[0].reshape(BATCH, SEQ, D_MODEL)
    y_sample = xs[1].reshape(DEC_BATCH, DEC_SEQ, D_MODEL)
    new_k, new_v = caches
    return (y_prompt, y_sample,
            new_k.reshape(BATCH, DEPTH, SEQ, N_HEADS_C, 2, QK_HALF), new_v)
```

```python
import functools
import math

import numpy as np
import jax
import jax.numpy as jnp
from jax import lax
from jax.experimental import pallas as pl
from jax.experimental.pallas import tpu as pltpu

F32 = jnp.float32
BF16 = jnp.bfloat16

D_MODEL = 2048
BATCH = 16
SEQ = 256
DEPTH = 2
DEC_BATCH = 4
DEC_SEQ = 1024
PAST_LEN = 512
GRID_W = 64
BRANCH = 512
N_GROUPS = 4
GROUP_W = 128
CHUNK = 128
POOL_WINDOWS = (2, 4, 8, 16)
N_HEADS_C = 4
HEAD_DIM_C = 128
QK_HALF = 64
ROPE_AXIS_DIM = 32
ROPE_BASE = 10000.0
HY_BANDS = 16
HY_EMB = 33
HY_HIDDEN = 64
HY_ORDER = 2
HY_FAST_DECAY = 0.3
HY_SLOW_DECAY = 1.5
HY_TARGET = 1e-2
N_IN_PIECES = 13
D_IN = N_IN_PIECES * BRANCH
LN_EPS = 1e-6

N_CTX = BATCH * SEQ
N_LAT = DEC_BATCH * DEC_SEQ
N_TOK = N_CTX + N_LAT
N_COND = 8
LANES = 128
HY_PAD = LANES
HY_CT = 256
VMEM_LIMIT = 56 * 1024 * 1024

(C_AU, C_AV, C_AG, C_BX, C_BG, C_Q, C_K, C_V, C_CG, C_DX1, C_DX2, C_DV, C_DG) = range(13)
D_MIX = 4 * BRANCH
(M_GMLP, M_POOL, M_ATTN, M_HYENA) = range(4)


def _silu(x):
    return x * jax.nn.sigmoid(x)


def _bdot(a, b):
    return jnp.dot(a.astype(BF16), b.astype(BF16), preferred_element_type=F32)


def _split(x, n_terms):
    hi = x.astype(BF16)
    if n_terms == 1:
        return (hi,)
    return (hi, (x - hi.astype(F32)).astype(BF16))


def _sdot(a_terms, b_terms):
    acc = jnp.dot(a_terms[0], b_terms[0], preferred_element_type=F32)
    if len(a_terms) > 1:
        acc = acc + jnp.dot(a_terms[1], b_terms[0], preferred_element_type=F32)
    if len(b_terms) > 1:
        acc = acc + jnp.dot(a_terms[0], b_terms[1], preferred_element_type=F32)
    return acc


def _layer_norm(x):
    mu = jnp.mean(x, axis=-1, keepdims=True)
    xc = x - mu
    var = jnp.mean(xc * xc, axis=-1, keepdims=True)
    return xc * lax.rsqrt(var + LN_EPS)


def _cond_row(tile, rows_per_tile):
    n_ctx_tiles = N_CTX // rows_per_tile
    tiles_per_batch = DEC_SEQ // rows_per_tile
    return jnp.where(tile < n_ctx_tiles, 0, 1 + (tile - n_ctx_tiles) // tiles_per_batch)


def _params(*semantics):
    return pltpu.CompilerParams(dimension_semantics=semantics, vmem_limit_bytes=VMEM_LIMIT)


MOD_TN = 1024


def _mod_kernel(c_ref, w_ref, b_ref, o_ref):
    o_ref[...] = _bdot(_silu(c_ref[...]), w_ref[...]) + b_ref[...]


def _modulation(cond, w_mod, b_mod):
    n = 3 * D_MODEL
    return pl.pallas_call(
        _mod_kernel,
        grid=(DEPTH, n // MOD_TN),
        in_specs=[
            pl.BlockSpec((N_COND, D_MODEL), lambda l, j: (0, 0)),
            pl.BlockSpec((None, D_MODEL, MOD_TN), lambda l, j: (l, 0, j)),
            pl.BlockSpec((None, 1, MOD_TN), lambda l, j: (l, 0, j)),
        ],
        out_specs=pl.BlockSpec((None, N_COND, MOD_TN), lambda l, j: (l, 0, j)),
        out_shape=jax.ShapeDtypeStruct((DEPTH, N_COND, n), F32),
        compiler_params=_params("arbitrary", "arbitrary"),
        name="modulation",
    )(cond, w_mod, b_mod.reshape(DEPTH, 1, n))


IN_TM = 1024
IN_FIRST_TM = 2048
IN_TN = 512
IN_LN_ROWS = 256
IN_CAST_K = 512


def _token_specs(xs, tm, tile_of=lambda i: i):
    n_ctx_tiles = N_CTX // tm
    if len(xs) == 1:
        maps = [lambda i, *_: (tile_of(i), 0)]
    else:
        maps = [lambda i, *_: (jnp.minimum(tile_of(i), n_ctx_tiles - 1), 0),
                lambda i, *_: (jnp.maximum(tile_of(i) - n_ctx_tiles, 0), 0)]
    return [pl.BlockSpec((tm, D_MODEL), m) for m in maps]


def _inproj_kernel(n_x, tm, tile0, cast_w, *refs):
    x_refs = refs[:n_x]
    scale_ref, shift_ref, w_ref = refs[n_x:n_x + 3]
    if cast_w:
        o_ref, wb_ref, h_ref = refs[n_x + 3:]
    else:
        _, o_ref, h_ref = refs[n_x + 3:]

    def weights():
        if not cast_w:
            return w_ref[...]
        w = w_ref[...].astype(BF16)
        wb_ref[...] = w
        return w

    def first_column_step(x_ref):
        w = weights()

        def norm(r):
            rows = slice(r, r + IN_LN_ROWS)
            h = _layer_norm(x_ref[rows, :]) * (1.0 + scale_ref[...]) + shift_ref[...]
            h = h.astype(BF16)
            h_ref[rows, :] = h
            return h

        h = norm(0)
        for r in range(0, tm, IN_LN_ROWS):
            o_ref[r:r + IN_LN_ROWS, :] = jnp.dot(h, w, preferred_element_type=F32)
            if r + IN_LN_ROWS < tm:
                h = norm(r + IN_LN_ROWS)

    first = pl.program_id(1) == 0
    if n_x == 1:
        pl.when(first)(lambda: first_column_step(x_refs[0]))
    else:
        is_ctx = pl.program_id(0) + tile0 < N_CTX // IN_TM
        pl.when(jnp.logical_and(first, is_ctx))(lambda: first_column_step(x_refs[0]))
        pl.when(jnp.logical_and(first, jnp.logical_not(is_ctx)))(
            lambda: first_column_step(x_refs[1]))

    @pl.when(jnp.logical_not(first))
    def _():
        acc = None
        for k in range(0, D_MODEL, IN_CAST_K):
            w = w_ref[k:k + IN_CAST_K, :]
            if cast_w:
                w = w.astype(BF16)
                wb_ref[k:k + IN_CAST_K, :] = w
            part = jnp.dot(h_ref[:, k:k + IN_CAST_K], w, preferred_element_type=F32)
            acc = part if acc is None else acc + part
        o_ref[...] = acc


def _in_projection(xs, mod4, w_in, layer):
    assert IN_FIRST_TM <= N_CTX and IN_FIRST_TM % IN_TM == 0
    n_col = D_IN // IN_TN
    first_tiles = IN_FIRST_TM // IN_TM

    def mod_spec(tile0, piece):
        return pl.BlockSpec((None, None, 1, D_MODEL),
                            lambda i, j: (layer, _cond_row(i + tile0, IN_TM), 0, piece))

    p_shape = jax.ShapeDtypeStruct((N_TOK, D_IN), F32)
    p, w_bf = pl.pallas_call(
        functools.partial(_inproj_kernel, 1, IN_FIRST_TM, 0, True),
        grid=(1, n_col),
        in_specs=[
            pl.BlockSpec((IN_FIRST_TM, D_MODEL), lambda i, j: (0, 0), pipeline_mode=pl.Buffered(1)),
            mod_spec(0, 1), mod_spec(0, 0),
            pl.BlockSpec((None, D_MODEL, IN_TN), lambda i, j: (layer, 0, j)),
        ],
        out_specs=[pl.BlockSpec((IN_FIRST_TM, IN_TN), lambda i, j: (0, j)),
                   pl.BlockSpec((D_MODEL, IN_TN), lambda i, j: (0, j))],
        out_shape=[p_shape, jax.ShapeDtypeStruct((D_MODEL, D_IN), BF16)],
        scratch_shapes=[pltpu.VMEM((IN_FIRST_TM, D_MODEL), BF16)],
        compiler_params=_params("arbitrary", "arbitrary"),
        name="in_projection_first",
    )(xs[0], mod4, mod4, w_in)
    return pl.pallas_call(
        functools.partial(_inproj_kernel, len(xs), IN_TM, first_tiles, False),
        grid=(N_TOK // IN_TM - first_tiles, n_col),
        in_specs=_token_specs(xs, IN_TM, lambda i: i + first_tiles) + [
            mod_spec(first_tiles, 1), mod_spec(first_tiles, 0),
            pl.BlockSpec((D_MODEL, IN_TN), lambda i, j: (0, j)),
            pl.BlockSpec(memory_space=pl.ANY),
        ],
        out_specs=pl.BlockSpec((IN_TM, IN_TN), lambda i, j: (i + first_tiles, j)),
        out_shape=p_shape,
        input_output_aliases={len(xs) + 3: 0},
        scratch_shapes=[pltpu.VMEM((IN_TM, D_MODEL), BF16)],
        compiler_params=_params("arbitrary", "arbitrary"),
        name="in_projection",
    )(*xs, mod4, mod4, w_bf, p)


AB_TM = 1024


def _gmlp_tasks(u_ref, v_ref, g_ref, w_ref, b_ref, o_ref):
    ws = [w_ref[g].astype(BF16) for g in range(N_GROUPS)]

    def chunk(r):
        rows = slice(r, r + CHUNK)
        vn = _layer_norm(v_ref[rows, :]).astype(BF16)
        for g in range(N_GROUPS):
            cols = slice(g * GROUP_W, (g + 1) * GROUP_W)
            mixed = jnp.dot(ws[g], vn[:, cols], preferred_element_type=F32) + b_ref[g]
            o_ref[rows, cols] = (_silu(g_ref[rows, cols]) * u_ref[rows, cols] * mixed).astype(BF16)

    return [functools.partial(chunk, r) for r in range(0, AB_TM, CHUNK)]


POOL_BAND_GROUPS = (2, 3)
POOL_BAND_ROWS = 256
POOL_BAND_HALO = 128


def _window_sum_shifts(p, win, pos, seq_len):
    acc = p
    for d in range(-(win // 2), win // 2):
        if d == 0:
            continue
        shifted = pltpu.roll(p, (-d) % AB_TM, axis=0)
        valid = (pos >= -d) if d < 0 else (pos < seq_len - d)
        acc = acc + jnp.where(valid, shifted, 0.0)
    return acc


def _window_sum_band_tasks(p, band_ref, pieces):
    hi = p.astype(BF16)
    r1 = p - hi.astype(F32)
    mid = r1.astype(BF16)
    lo = (r1 - mid.astype(F32)).astype(BF16)
    terms = jnp.concatenate([hi, mid, lo], axis=1)

    def piece(r):
        k0, k1 = max(0, r - POOL_BAND_HALO), min(AB_TM, r + POOL_BAND_ROWS + POOL_BAND_HALO)
        s = jnp.dot(band_ref[r:r + POOL_BAND_ROWS, k0:k1], terms[k0:k1, :],
                    preferred_element_type=F32)
        pieces.append(s[:, :GROUP_W] + s[:, GROUP_W:2 * GROUP_W] + s[:, 2 * GROUP_W:])

    return [functools.partial(piece, r) for r in range(0, AB_TM, POOL_BAND_ROWS)]


def _pool_finish(g, acc, x_ref, g_ref, w_ref, s_ref, pos, seq_len, o_ref):
    cols = slice(g * GROUP_W, (g + 1) * GROUP_W)
    win = POOL_WINDOWS[g]
    count = jnp.minimum(pos + win // 2, seq_len) - jnp.maximum(pos - win // 2, 0)
    pooled = acc / count.astype(F32)
    y = _bdot(pooled - x_ref[:, cols], w_ref[g])
    o_ref[:, BRANCH + g * GROUP_W:BRANCH + (g + 1) * GROUP_W] = (
        _silu(g_ref[:, cols]) * (y * s_ref[:, cols])).astype(BF16)


def _mixer_ab_kernel(au_ref, av_ref, ag_ref, bx_ref, bg_ref, gw_ref, gb_ref, pw_ref, ps_ref,
                     band_a_ref, band_b_ref, o_ref):
    seq_len = jnp.where(pl.program_id(0) < N_CTX // AB_TM, SEQ, DEC_SEQ)
    pos = lax.broadcasted_iota(jnp.int32, (AB_TM, GROUP_W), 0) & (seq_len - 1)
    pool_refs = (bx_ref, bg_ref, pw_ref, ps_ref, pos, seq_len, o_ref)
    band_sums = {g: [] for g in POOL_BAND_GROUPS}
    mxu_tasks = [t for g, band in zip(POOL_BAND_GROUPS, (band_a_ref, band_b_ref))
                 for t in _window_sum_band_tasks(bx_ref[:, g * GROUP_W:(g + 1) * GROUP_W], band,
                                                 band_sums[g])]
    vpu_tasks = _gmlp_tasks(au_ref, av_ref, ag_ref, gw_ref, gb_ref, o_ref)
    for i in range(max(len(mxu_tasks), len(vpu_tasks))):
        for tasks in (mxu_tasks, vpu_tasks):
            if i < len(tasks):
                tasks[i]()
    for g, win in enumerate(POOL_WINDOWS):
        if g not in POOL_BAND_GROUPS:
            acc = _window_sum_shifts(bx_ref[:, g * GROUP_W:(g + 1) * GROUP_W], win, pos, seq_len)
            _pool_finish(g, acc, *pool_refs)
    for g in POOL_BAND_GROUPS:
        _pool_finish(g, jnp.concatenate(band_sums[g], axis=0), *pool_refs)


def _pool_bands():
    t = np.arange(AB_TM)
    bands = []
    for g in POOL_BAND_GROUPS:
        half = POOL_WINDOWS[g] // 2
        d = t[None, :] - t[:, None]
        near = (d >= -half) & (d < half)
        per_path = [near & ((t[None, :] // L) == (t[:, None] // L)) for L in (SEQ, DEC_SEQ)]
        bands.append(jnp.asarray(np.stack(per_path), F32).astype(BF16))
    return bands


def _mixer_gmlp_pool(p, gmlp_w, gmlp_b_rows, pool_w, pool_scale_row):
    assert M_POOL == M_GMLP + 1
    n_steps = N_TOK // AB_TM
    blk = lambda c: pl.BlockSpec((AB_TM, BRANCH), lambda i: (i, c))
    full = lambda a: pl.BlockSpec(a.shape, lambda i: (0,) * a.ndim)
    band = pl.BlockSpec((None, AB_TM, AB_TM), lambda i: (jnp.where(i < N_CTX // AB_TM, 0, 1), 0, 0))
    return pl.pallas_call(
        _mixer_ab_kernel,
        grid=(n_steps,),
        in_specs=[blk(C_AU), blk(C_AV), blk(C_AG), blk(C_BX), blk(C_BG),
                  full(gmlp_w), full(gmlp_b_rows), full(pool_w), full(pool_scale_row), band, band],
        out_specs=pl.BlockSpec((AB_TM, 2 * BRANCH), lambda i: (i, M_GMLP // 2)),
        out_shape=jax.ShapeDtypeStruct((N_TOK, D_MIX), BF16),
        compiler_params=_params("arbitrary"),
        name="mixer_gmlp_pool",
    )(p, p, p, p, p, gmlp_w, gmlp_b_rows, pool_w, pool_scale_row, *_pool_bands())


ATT_TQ = 1024
V_ROWS = HEAD_DIM_C + 16
NT_DIMS = (((1,), (1,)), ((), ()))


def _lambda(lam_ref, lam_init):
    lq = lam_ref[...]
    a = jnp.sum(lq[0:1] * lq[1:2], axis=-1, keepdims=True)
    b = jnp.sum(lq[2:3] * lq[3:4], axis=-1, keepdims=True)
    return jnp.exp(a) - jnp.exp(b) + lam_init


def _map_masks():
    lane = lax.broadcasted_iota(jnp.int32, (1, HEAD_DIM_C), 1)
    m0 = (lane < QK_HALF).astype(F32)
    return m0, 1.0 - m0


def _scores_t(q, keys):
    return [lax.dot_general(k, q, NT_DIMS, preferred_element_type=F32) for k in keys]


def _softmax_v_t(s, vals_t):
    m = functools.reduce(jnp.maximum, [jnp.max(x, axis=0, keepdims=True) for x in s])
    acc = functools.reduce(
        jnp.add, [jnp.dot(v, jnp.exp2(x - m).astype(BF16), preferred_element_type=F32)
                  for x, v in zip(s, vals_t)])
    return acc[:HEAD_DIM_C] * (1.0 / acc[HEAD_DIM_C:HEAD_DIM_C + 1])


def _with_ones_rows(v_t):
    ones = jnp.ones((V_ROWS - HEAD_DIM_C, v_t.shape[1]), BF16)
    return jnp.concatenate([v_t.astype(BF16), ones], axis=0)


def _diff_attention(q_of, keys_of, vals_t_of, gate_of, store, lam, lam_init, subln):
    masks = _map_masks()
    tasks = [(h, mp) for h in range(N_HEADS_C) for mp in range(2)]
    per_head = {}

    def operands(h):
        if h not in per_head:
            q = q_of(h) * (QK_HALF ** -0.5 * math.log2(math.e))
            per_head[h] = (q, keys_of(h), vals_t_of(h))
        return per_head[h]

    def scores(h, mp):
        q, keys, _ = operands(h)
        return _scores_t((q * masks[mp]).astype(BF16), keys)

    nxt = scores(*tasks[0])
    outs = []
    for i, (h, mp) in enumerate(tasks):
        cur = nxt
        if i + 1 < len(tasks):
            nxt = scores(*tasks[i + 1])
        outs.append(_softmax_v_t(cur, operands(h)[2]))
        if mp == 1:
            o = (outs[0] - lam * outs[1]).T
            outs = []
            o = o * lax.rsqrt(jnp.mean(o * o, axis=-1, keepdims=True) + 1e-5)
            o = o * subln * (1.0 - lam_init)
            store(h, _silu(gate_of(h)) * o)


def _head_cols(h):
    return slice(h * HEAD_DIM_C, (h + 1) * HEAD_DIM_C)


def _attn_ctx_kernel(lam_init, q_ref, k_ref, v_ref, g_ref, lam_ref, sw_ref, *rest):
    o_ref, ko_ref, vo_ref = rest[-3:]
    ko_ref[...] = k_ref[...]
    for h in range(N_HEADS_C):
        vo_ref[:, h, :] = v_ref[:, _head_cols(h)]

    def store(h, y):
        o_ref[:, _head_cols(h)] = y.astype(BF16)

    _diff_attention(
        lambda h: q_ref[:, _head_cols(h)],
        lambda h: [k_ref[:, _head_cols(h)].astype(BF16)],
        lambda h: [_with_ones_rows(v_ref[:, _head_cols(h)].T)],
        lambda h: g_ref[:, _head_cols(h)],
        store, _lambda(lam_ref, lam_init), lam_init, sw_ref[...])


def _mixer_attn_ctx(p, lambda_qk, subln_row, layer, lam_init, ymix, caches):
    blk = lambda c: pl.BlockSpec((SEQ, BRANCH), lambda b: (b, c))
    any_spec = pl.BlockSpec(memory_space=pl.ANY)
    k_spec = pl.BlockSpec((None, None, SEQ, BRANCH), lambda b: (b, layer, 0, 0))
    v_spec = pl.BlockSpec((None, None, SEQ, N_HEADS_C, HEAD_DIM_C), lambda b: (b, layer, 0, 0, 0))
    k_shape = jax.ShapeDtypeStruct((BATCH, DEPTH, SEQ, BRANCH), F32)
    v_shape = jax.ShapeDtypeStruct((BATCH, DEPTH, SEQ, N_HEADS_C, HEAD_DIM_C), F32)
    in_specs = [
        blk(C_Q), blk(C_K), blk(C_V), blk(C_CG),
        pl.BlockSpec((None, 4, QK_HALF), lambda b: (layer, 0, 0)),
        pl.BlockSpec((None, 1, HEAD_DIM_C), lambda b: (layer, 0, 0)),
        any_spec,
    ]
    args = [p, p, p, p, lambda_qk, subln_row, ymix]
    aliases = {6: 0}
    if caches is not None:
        in_specs += [any_spec, any_spec]
        args += list(caches)
        aliases.update({7: 1, 8: 2})
    return pl.pallas_call(
        functools.partial(_attn_ctx_kernel, lam_init),
        grid=(BATCH,),
        in_specs=in_specs,
        out_specs=[pl.BlockSpec((SEQ, BRANCH), lambda b: (b, M_ATTN)), k_spec, v_spec],
        out_shape=[jax.ShapeDtypeStruct((N_TOK, D_MIX), BF16), k_shape, v_shape],
        input_output_aliases=aliases,
        compiler_params=_params("arbitrary"),
        name="mixer_attn_ctx",
    )(*args)


def _rope(x, cos, sin_signed):
    lane = lax.broadcasted_iota(jnp.int32, x.shape, 1)
    first_half = (lane & (ROPE_AXIS_DIM - 1)) < (ROPE_AXIS_DIM // 2)
    half = ROPE_AXIS_DIM // 2
    partner = jnp.where(first_half,
                        pltpu.roll(x, x.shape[1] - half, axis=1),
                        pltpu.roll(x, half, axis=1))
    return x * cos + partner * sin_signed


def _attn_lat_kernel(lam_init, q_ref, k_ref, v_ref, g_ref, ck_ref, cv_ref, cosq_ref, sinq_ref,
                     cosk_ref, sink_ref, lam_ref, sw_ref, wo_ref, ymix_ref, o_ref, wo_bf_ref,
                     kc_ref, kr_ref, vt_ref):
    del ymix_ref
    wo_bf_ref[...] = wo_ref[...].astype(BF16)

    @pl.when(pl.program_id(1) == 0)
    def _():
        kc_ref[...] = ck_ref[...].astype(BF16)
        for h in range(N_HEADS_C):
            cols = slice(h * HEAD_DIM_C, (h + 1) * HEAD_DIM_C)
            kr_ref[:, cols] = _rope(k_ref[:, cols], cosk_ref[...], sink_ref[...]).astype(BF16)
            vt_ref[h, :, 0:PAST_LEN] = _with_ones_rows(cv_ref[:, h, :].T)
            vt_ref[h, :, PAST_LEN:PAST_LEN + DEC_SEQ] = _with_ones_rows(v_ref[:, cols].T)

    def store(h, y):
        o_ref[:, _head_cols(h)] = y.astype(BF16)

    _diff_attention(
        lambda h: _rope(q_ref[:, _head_cols(h)], cosq_ref[...], sinq_ref[...]),
        lambda h: [kc_ref[:, _head_cols(h)], kr_ref[:, _head_cols(h)]],
        lambda h: [vt_ref[h, :, 0:PAST_LEN], vt_ref[h, :, PAST_LEN:PAST_LEN + DEC_SEQ]],
        lambda h: g_ref[:, _head_cols(h)],
        store, _lambda(lam_ref, lam_init), lam_init, sw_ref[...])


def _rope_tables():
    pos = np.arange(DEC_SEQ)
    row = (pos // GRID_W).astype(np.float64)
    col = (pos % GRID_W).astype(np.float64)
    half = ROPE_AXIS_DIM // 2
    inv = ROPE_BASE ** (-np.arange(0, ROPE_AXIS_DIM, 2, dtype=np.float64) / ROPE_AXIS_DIM)
    lane = np.arange(HEAD_DIM_C)
    axis_is_col = (lane // ROPE_AXIS_DIM) % 2 == 1
    idx = lane % ROPE_AXIS_DIM
    ang = np.where(axis_is_col[None, :], col[:, None], row[:, None]) * inv[idx % half][None, :]
    sign = np.where(idx < half, -1.0, 1.0)[None, :]
    return (jnp.asarray(np.cos(ang), F32), jnp.asarray(np.sin(ang) * sign, F32))


def _mixer_attn_lat(p, cache_k4, cache_v, lambda_qk, subln_row, layer, lam_init, w_out, ymix):
    cos_t, sin_t = _rope_tables()
    q_tiles = DEC_SEQ // ATT_TQ
    q0 = N_CTX // ATT_TQ
    b0 = N_CTX // DEC_SEQ
    wo_rows = D_MIX // (DEC_BATCH * q_tiles)
    qblk = lambda c: pl.BlockSpec((ATT_TQ, BRANCH), lambda b, i: (q0 + b * q_tiles + i, c))
    kblk = lambda c: pl.BlockSpec((DEC_SEQ, BRANCH), lambda b, i: (b0 + b, c))
    cblk = pl.BlockSpec((None, None, PAST_LEN, BRANCH), lambda b, i: (b, layer, 0, 0))
    cvblk = pl.BlockSpec((None, None, PAST_LEN, N_HEADS_C, HEAD_DIM_C),
                         lambda b, i: (b, layer, 0, 0, 0))
    return pl.pallas_call(
        functools.partial(_attn_lat_kernel, lam_init),
        grid=(DEC_BATCH, q_tiles),
        in_specs=[
            qblk(C_Q), kblk(C_K), kblk(C_V), qblk(C_CG), cblk, cvblk,
            pl.BlockSpec((ATT_TQ, HEAD_DIM_C), lambda b, i: (i, 0)),
            pl.BlockSpec((ATT_TQ, HEAD_DIM_C), lambda b, i: (i, 0)),
            pl.BlockSpec((DEC_SEQ, HEAD_DIM_C), lambda b, i: (0, 0)),
            pl.BlockSpec((DEC_SEQ, HEAD_DIM_C), lambda b, i: (0, 0)),
            pl.BlockSpec((None, 4, QK_HALF), lambda b, i: (layer, 0, 0)),
            pl.BlockSpec((None, 1, HEAD_DIM_C), lambda b, i: (layer, 0, 0)),
            pl.BlockSpec((None, wo_rows, D_MODEL), lambda b, i: (layer, b * q_tiles + i, 0)),
            pl.BlockSpec(memory_space=pl.ANY),
        ],
        out_specs=[pl.BlockSpec((ATT_TQ, BRANCH), lambda b, i: (q0 + b * q_tiles + i, M_ATTN)),
                   pl.BlockSpec((wo_rows, D_MODEL), lambda b, i: (b * q_tiles + i, 0))],
        out_shape=[jax.ShapeDtypeStruct((N_TOK, D_MIX), BF16),
                   jax.ShapeDtypeStruct((D_MIX, D_MODEL), BF16)],
        input_output_aliases={13: 0},
        scratch_shapes=[pltpu.VMEM((PAST_LEN, BRANCH), BF16), pltpu.VMEM((DEC_SEQ, BRANCH), BF16),
                        pltpu.VMEM((N_HEADS_C, V_ROWS, PAST_LEN + DEC_SEQ), BF16)],
        compiler_params=_params("arbitrary", "arbitrary"),
        name="mixer_attn_lat",
    )(p, p, p, p, cache_k4, cache_v, cos_t, sin_t, cos_t, sin_t, lambda_qk, subln_row, w_out, ymix)


HY_ROWS = 1024
HY_PIECE = 512


def _dft_matrices(seq_len):
    n = 2 * seq_len
    f = np.arange(seq_len, dtype=np.float64)[:, None]
    s = np.arange(seq_len, dtype=np.float64)[None, :]
    theta = 2.0 * np.pi * f * s / n
    alt = np.where(np.arange(seq_len) % 2 == 0, 1.0, -1.0)
    ac = np.cos(theta)
    as_ = -np.sin(theta)
    as_[0, :] = alt
    bc = 2.0 * np.cos(theta.T) / n
    bc[:, 0] = 1.0 / n
    bs = -2.0 * np.sin(theta.T) / n
    bs[:, 0] = alt / n
    fwd = np.concatenate([ac, as_], axis=0)
    inv = np.concatenate([bc, bs], axis=1)
    return jnp.asarray(fwd, F32).astype(BF16), jnp.asarray(inv, F32).astype(BF16)


def _filter_features(seq_len):
    t_idx = np.arange(seq_len, dtype=np.float64)
    t_norm = np.linspace(0.0, 1.0, seq_len)
    bands = np.linspace(1e-4, HY_BANDS - 1, HY_BANDS)
    ang = (2.0 * math.pi * t_idx / seq_len)[:, None] * bands[None, :]
    feats = np.concatenate([t_norm[:, None], np.cos(ang), np.sin(ang)], axis=-1)
    feats = np.pad(feats, ((0, 0), (0, HY_PAD - HY_EMB)))
    deltas = np.abs(np.linspace(math.log(HY_TARGET) / HY_FAST_DECAY,
                                math.log(HY_TARGET) / HY_SLOW_DECAY, BRANCH))
    return (jnp.asarray(feats, F32), jnp.asarray(t_norm[:, None], F32),
            jnp.asarray(deltas[None, :], F32))


def _filter_kernel(seq_len, feats_ref, tn_ref, dl_ref, w1_ref, b1_ref, w2_ref, b2_ref, fr_ref,
                   w3f_ref, w3b_ref, fwd_ref, kr_ref, ki_ref, h_ref):
    sp = lambda x: _split(x, 2)

    @pl.when(pl.program_id(1) == 0)
    def _():
        fr = fr_ref[...]
        h = jnp.sin(fr * (_sdot(sp(feats_ref[...]), sp(w1_ref[...])) + b1_ref[...]))
        h_ref[...] = jnp.sin(fr * (_sdot(sp(h), sp(w2_ref[...])) + b2_ref[...]))

    h = h_ref[...]
    decay = jnp.exp(-tn_ref[...] * dl_ref[...])
    row = lax.broadcasted_iota(jnp.int32, (seq_len, BRANCH), 0)
    fwd = _bdot(h, w3f_ref[...]) * decay
    bwd = jnp.where(row == 0, 0.0, _bdot(h, w3b_ref[...]) * decay)
    norm = (jnp.sum(jnp.abs(fwd), axis=0, keepdims=True)
            + jnp.sum(jnp.abs(bwd), axis=0, keepdims=True))
    fwd = fwd / norm
    bwd = bwd / norm
    even = fwd + bwd
    alt = jnp.where((row & 1) == 0, 1.0, -1.0)
    nyquist = jnp.sum(alt * even, axis=0, keepdims=True)
    kr_ref[...] = jnp.dot(fwd_ref[0:seq_len, :], even.astype(BF16), preferred_element_type=F32)
    ki = jnp.dot(fwd_ref[seq_len:2 * seq_len, :], (fwd - bwd).astype(BF16),
                 preferred_element_type=F32)
    ki_ref[...] = jnp.where(row == 0, nyquist, ki)


def _hyena_spectrum(seq_len, filt, fwd_mat):
    feats, t_norm, deltas = _filter_features(seq_len)
    w1, b1, w2, b2, freq, w3 = filt
    full = lambda a: pl.BlockSpec(a.shape, lambda l, o: (0,) * a.ndim)
    lyr = lambda r, n: pl.BlockSpec((None, r, n), lambda l, o: (l, 0, 0))
    out = pl.BlockSpec((None, None, seq_len, BRANCH), lambda l, o: (l, o, 0, 0))
    return pl.pallas_call(
        functools.partial(_filter_kernel, seq_len),
        grid=(DEPTH, HY_ORDER),
        in_specs=[
            full(feats), full(t_norm), full(deltas),
            lyr(HY_PAD, HY_PAD), lyr(1, HY_PAD), lyr(HY_PAD, HY_PAD), lyr(1, HY_PAD), lyr(1, HY_PAD),
            pl.BlockSpec((None, HY_PAD, BRANCH), lambda l, o: (l, 0, 2 * o)),
            pl.BlockSpec((None, HY_PAD, BRANCH), lambda l, o: (l, 0, 2 * o + 1)),
            full(fwd_mat),
        ],
        out_specs=[out, out],
        out_shape=[jax.ShapeDtypeStruct((DEPTH, HY_ORDER, seq_len, BRANCH), F32)] * 2,
        scratch_shapes=[pltpu.VMEM((seq_len, HY_PAD), F32)],
        compiler_params=_params("arbitrary", "arbitrary"),
        name=f"hyena_spectrum_{seq_len}",
    )(feats, t_norm, deltas, w1, b1, w2, b2, freq, w3, w3, fwd_mat)


def _hyena_chains(seq_len, width, x1_ref, x2_ref, hv_ref, g_ref, cw_ref, cb_ref, kr_ref, ki_ref,
                  hb_ref, fwd_ref, inv_ref, o_ref):
    row = lax.broadcasted_iota(jnp.int32, (seq_len, width), 0)
    first, last = row == 0, row == seq_len - 1

    def chain(rs, c0):
        cs = slice(c0, c0 + width)

        def short_conv(x_ref, piece):
            x = x_ref[rs, cs]
            w = cw_ref[:, piece * BRANCH + c0:piece * BRANCH + c0 + width]
            b = cb_ref[:, piece * BRANCH + c0:piece * BRANCH + c0 + width]
            prev = jnp.where(first, 0.0, pltpu.roll(x, 1, axis=0))
            nxt = jnp.where(last, 0.0, pltpu.roll(x, seq_len - 1, axis=0))
            return prev * w[0:1] + x * w[1:2] + nxt * w[2:3] + b

        z = short_conv(hv_ref, 2)
        yield
        gate_refs = (x1_ref, x2_ref)
        piece = min(seq_len, HY_PIECE)
        for order in range(HY_ORDER):
            zb = z.astype(BF16)
            y_re, y_im = [], []
            for r in range(0, seq_len, piece):
                zr = jnp.dot(fwd_ref[r:r + piece, :], zb, preferred_element_type=F32)
                yield
                zi = jnp.dot(fwd_ref[seq_len + r:seq_len + r + piece, :], zb,
                             preferred_element_type=F32)
                yield
                kr, kp = kr_ref[order, r:r + piece, cs], ki_ref[order, r:r + piece, cs]
                ki, kn = kp, kr
                if r == 0:
                    dc = lax.broadcasted_iota(jnp.int32, (piece, width), 0) == 0
                    ki = jnp.where(dc, 0.0, kp)
                    kn = jnp.where(dc, kp, kr)
                y_re.append((zr * kr - zi * ki).astype(BF16))
                y_im.append((zr * ki + zi * kn).astype(BF16))
                yield
            yf = jnp.concatenate(y_re + y_im, axis=0)
            ys = []
            for r in range(0, seq_len, piece):
                ys.append(jnp.dot(inv_ref[r:r + piece, :], yf, preferred_element_type=F32))
                yield
                if r == 0:
                    gate = short_conv(gate_refs[order], order)
            y = ys[0] if len(ys) == 1 else jnp.concatenate(ys, axis=0)
            z = gate * (y + z * hb_ref[order:order + 1, cs])
        o_ref[rs, cs] = (_silu(g_ref[rs, cs]) * z).astype(BF16)

    waiting = [chain(slice(s * seq_len, (s + 1) * seq_len), c0)
               for s in range(HY_ROWS // seq_len) for c0 in range(0, BRANCH, width)]
    running = []
    while waiting or running:
        if waiting:
            running.append(waiting.pop(0))
        for gen in list(running):
            if next(gen, "done") == "done":
                running.remove(gen)


def _hyena_kernel(x1_ref, x2_ref, hv_ref, g_ref, cw_ref, cb_ref, hb_ref,
                  krc_ref, kic_ref, fwdc_ref, invc_ref, krl_ref, kil_ref, fwdl_ref, invl_ref,
                  ymix_ref, o_ref):
    del ymix_ref
    common = (x1_ref, x2_ref, hv_ref, g_ref, cw_ref, cb_ref)
    is_ctx = pl.program_id(0) < N_CTX // HY_ROWS

    @pl.when(is_ctx)
    def _():
        _hyena_chains(SEQ, BRANCH, *common, krc_ref, kic_ref, hb_ref, fwdc_ref, invc_ref, o_ref)

    @pl.when(jnp.logical_not(is_ctx))
    def _():
        _hyena_chains(DEC_SEQ, HY_CT, *common, krl_ref, kil_ref, hb_ref, fwdl_ref, invl_ref, o_ref)


def _mixer_hyena(p, conv_w, conv_b3, hyena_bias, spec_ctx, mats_ctx, spec_lat, mats_lat, layer,
                 ymix):
    blk = lambda c: pl.BlockSpec((HY_ROWS, BRANCH), lambda i: (i, c))
    once = pl.Buffered(1)
    lyr = lambda a: pl.BlockSpec((None,) + a.shape[1:], lambda i: (layer,) + (0,) * (a.ndim - 1),
                                 pipeline_mode=once)
    full = lambda a: pl.BlockSpec(a.shape, lambda i: (0,) * a.ndim, pipeline_mode=once)
    consts = [*spec_ctx, *mats_ctx, *spec_lat, *mats_lat]
    return pl.pallas_call(
        _hyena_kernel,
        grid=(N_TOK // HY_ROWS,),
        in_specs=[blk(C_DX1), blk(C_DX2), blk(C_DV), blk(C_DG),
                  lyr(conv_w), lyr(conv_b3), lyr(hyena_bias),
                  lyr(spec_ctx[0]), lyr(spec_ctx[1]), full(mats_ctx[0]), full(mats_ctx[1]),
                  lyr(spec_lat[0]), lyr(spec_lat[1]), full(mats_lat[0]), full(mats_lat[1]),
                  pl.BlockSpec(memory_space=pl.ANY)],
        out_specs=pl.BlockSpec((HY_ROWS, BRANCH), lambda i: (i, M_HYENA)),
        out_shape=jax.ShapeDtypeStruct((N_TOK, D_MIX), BF16),
        input_output_aliases={7 + len(consts): 0},
        compiler_params=_params("arbitrary"),
        name="mixer_hyena",
    )(p, p, p, p, conv_w, conv_b3, hyena_bias, *consts, ymix)


OUT_TM = 512


def _outproj_kernel(alpha, n_x, n_out, ymix_ref, *refs):
    x_refs = refs[:n_x]
    gate_ref, w_ref, b_ref, lng_ref, lnb_ref = refs[n_x:n_x + 5]
    o_refs = refs[n_x + 5:]
    is_ctx = pl.program_id(0) < N_CTX // OUT_TM

    def tile(x_ref, o_ref):
        y = jnp.dot(ymix_ref[...], w_ref[...], preferred_element_type=F32) + b_ref[...]
        r = alpha * x_ref[...] + gate_ref[...] * y
        o_ref[...] = _layer_norm(r) * lng_ref[...] + lnb_ref[...]

    if n_x == 1 and n_out == 1:
        tile(x_refs[0], o_refs[0])
    else:
        pl.when(is_ctx)(lambda: tile(x_refs[0], o_refs[0]))
        pl.when(jnp.logical_not(is_ctx))(lambda: tile(x_refs[-1], o_refs[-1]))


def _out_projection(ymix, xs, mod4, w_out_bf, b_out3, ln_g3, ln_b3, layer, split_out):
    alpha = (2.0 * DEPTH) ** 0.25
    row = lambda i: _cond_row(i, OUT_TM)
    vec = pl.BlockSpec((None, 1, D_MODEL), lambda i: (layer, 0, 0))
    if split_out:
        out_shape = [jax.ShapeDtypeStruct((N_CTX, D_MODEL), F32),
                     jax.ShapeDtypeStruct((N_LAT, D_MODEL), F32)]
    else:
        out_shape = [jax.ShapeDtypeStruct((N_TOK, D_MODEL), F32)]
    return pl.pallas_call(
        functools.partial(_outproj_kernel, alpha, len(xs), len(out_shape)),
        grid=(N_TOK // OUT_TM,),
        in_specs=[pl.BlockSpec((OUT_TM, D_MIX), lambda i: (i, 0))] + _token_specs(xs, OUT_TM) + [
            pl.BlockSpec((None, None, 1, D_MODEL), lambda i: (layer, row(i), 0, 2)),
            pl.BlockSpec((D_MIX, D_MODEL), lambda i: (0, 0), pipeline_mode=pl.Buffered(1)),
            vec, vec, vec,
        ],
        out_specs=_token_specs(out_shape, OUT_TM),
        out_shape=out_shape,
        compiler_params=_params("arbitrary"),
        name="out_projection",
    )(ymix, *xs, mod4, w_out_bf, b_out3, ln_g3, ln_b3)


def kernel(x_prompt, x_sample, cache_k, cache_v, c, c_ctx, w_mod, b_mod, w_in, gmlp_w, gmlp_b,
           pool_w, pool_scale, lambda_qk, subln_w, conv_w, conv_b, filt_w1, filt_b1, filt_w2,
           filt_b2, filt_freq, filt_w3, hyena_bias, w_out, b_out, ln_g, ln_b):
    xs = (x_prompt.reshape(N_CTX, D_MODEL), x_sample.reshape(N_LAT, D_MODEL))
    cond = jnp.concatenate(
        [c_ctx[None, :], c, jnp.zeros((N_COND - 1 - DEC_BATCH, D_MODEL), F32)], axis=0)
    mod4 = _modulation(cond, w_mod, b_mod).reshape(DEPTH, N_COND, 1, 3 * D_MODEL)

    cache_k4 = cache_k.reshape(DEC_BATCH, DEPTH, PAST_LEN, BRANCH)
    gmlp_b_rows = jnp.broadcast_to(gmlp_b[..., None], (DEPTH, N_GROUPS, CHUNK, GROUP_W))
    subln_row = subln_w.reshape(DEPTH, 1, HEAD_DIM_C)
    conv_b3 = conv_b.reshape(DEPTH, 1, 3 * BRANCH)
    b_out3 = b_out.reshape(DEPTH, 1, D_MODEL)
    ln_g3 = ln_g.reshape(DEPTH, 1, D_MODEL)
    ln_b3 = ln_b.reshape(DEPTH, 1, D_MODEL)

    pad_h = HY_PAD - HY_HIDDEN
    filt = (
        jnp.pad(filt_w1, ((0, 0), (0, HY_PAD - HY_EMB), (0, pad_h))),
        jnp.pad(filt_b1, ((0, 0), (0, pad_h))).reshape(DEPTH, 1, HY_PAD),
        jnp.pad(filt_w2, ((0, 0), (0, pad_h), (0, pad_h))),
        jnp.pad(filt_b2, ((0, 0), (0, pad_h))).reshape(DEPTH, 1, HY_PAD),
        jnp.pad(filt_freq, ((0, 0), (0, pad_h))).reshape(DEPTH, 1, HY_PAD),
        jnp.pad(filt_w3, ((0, 0), (0, pad_h), (0, 0))),
    )
    mats_ctx = _dft_matrices(SEQ)
    mats_lat = _dft_matrices(DEC_SEQ)
    spec_ctx = _hyena_spectrum(SEQ, filt, mats_ctx[0])
    spec_lat = _hyena_spectrum(DEC_SEQ, filt, mats_lat[0])

    caches = None
    for layer in range(DEPTH):
        lam_init = 0.8 - 0.6 * math.exp(-0.3 * layer)
        p = _in_projection(xs, mod4, w_in, layer)
        ymix = _mixer_gmlp_pool(p, gmlp_w[layer], gmlp_b_rows[layer], pool_w[layer],
                                pool_scale[layer].reshape(1, BRANCH))
        ymix, new_k, new_v = _mixer_attn_ctx(p, lambda_qk, subln_row, layer, lam_init, ymix, caches)
        caches = (new_k, new_v)
        ymix, w_out_bf = _mixer_attn_lat(p, cache_k4, cache_v, lambda_qk, subln_row, layer,
                                         lam_init, w_out, ymix)
        ymix = _mixer_hyena(p, conv_w, conv_b3, hyena_bias, spec_ctx, mats_ctx, spec_lat, mats_lat,
                            layer, ymix)
        xs = tuple(_out_projection(ymix, xs, mod4, w_out_bf, b_out3, ln_g3, ln_b3, layer,
                                   split_out=layer == DEPTH - 1))

    y_prompt = xs[0].reshape(BATCH, SEQ, D_MODEL)
    y_sample = xs[1].reshape(DEC_BATCH, DEC_SEQ, D_MODEL)
    new_k, new_v = caches
    return (y_prompt, y_sample,
            new_k.reshape(BATCH, DEPTH, SEQ, N_HEADS_C, 2, QK_HALF), new_v)
```

```python
import functools
import math

import numpy as np
import jax
import jax.numpy as jnp
from jax import lax
from jax.experimental import pallas as pl
from jax.experimental.pallas import tpu as pltpu

F32 = jnp.float32
BF16 = jnp.bfloat16

D_MODEL = 2048
BATCH = 16
SEQ = 256
DEPTH = 2
DEC_BATCH = 4
DEC_SEQ = 1024
PAST_LEN = 512
GRID_W = 64
BRANCH = 512
N_GROUPS = 4
GROUP_W = 128
CHUNK = 128
POOL_WINDOWS = (2, 4, 8, 16)
N_HEADS_C = 4
HEAD_DIM_C = 128
QK_HALF = 64
ROPE_AXIS_DIM = 32
ROPE_BASE = 10000.0
HY_BANDS = 16
HY_EMB = 33
HY_HIDDEN = 64
HY_ORDER = 2
HY_FAST_DECAY = 0.3
HY_SLOW_DECAY = 1.5
HY_TARGET = 1e-2
N_IN_PIECES = 13
D_IN = N_IN_PIECES * BRANCH
LN_EPS = 1e-6

N_CTX = BATCH * SEQ
N_LAT = DEC_BATCH * DEC_SEQ
N_TOK = N_CTX + N_LAT
N_COND = 8
LANES = 128
HY_PAD = LANES
HY_CT = 256
VMEM_LIMIT = 56 * 1024 * 1024

(C_AU, C_AV, C_AG, C_BX, C_BG, C_Q, C_K, C_V, C_CG, C_DX1, C_DX2, C_DV, C_DG) = range(13)
D_MIX = 4 * BRANCH
(M_GMLP, M_POOL, M_ATTN, M_HYENA) = range(4)


def _silu(x):
    return x * jax.nn.sigmoid(x)


def _bdot(a, b):
    return jnp.dot(a.astype(BF16), b.astype(BF16), preferred_element_type=F32)


def _split(x, n_terms):
    hi = x.astype(BF16)
    if n_terms == 1:
        return (hi,)
    return (hi, (x - hi.astype(F32)).astype(BF16))


def _sdot(a_terms, b_terms):
    acc = jnp.dot(a_terms[0], b_terms[0], preferred_element_type=F32)
    if len(a_terms) > 1:
        acc = acc + jnp.dot(a_terms[1], b_terms[0], preferred_element_type=F32)
    if len(b_terms) > 1:
        acc = acc + jnp.dot(a_terms[0], b_terms[1], preferred_element_type=F32)
    return acc


def _layer_norm(x):
    mu = jnp.mean(x, axis=-1, keepdims=True)
    xc = x - mu
    var = jnp.mean(xc * xc, axis=-1, keepdims=True)
    return xc * lax.rsqrt(var + LN_EPS)


def _cond_row(tile, rows_per_tile):
    n_ctx_tiles = N_CTX // rows_per_tile
    tiles_per_batch = DEC_SEQ // rows_per_tile
    return jnp.where(tile < n_ctx_tiles, 0, 1 + (tile - n_ctx_tiles) // tiles_per_batch)


def _params(*semantics):
    return pltpu.CompilerParams(dimension_semantics=semantics, vmem_limit_bytes=VMEM_LIMIT)


MOD_TN = 1024


def _mod_kernel(c_ref, w_ref, b_ref, o_ref):
    o_ref[...] = _bdot(_silu(c_ref[...]), w_ref[...]) + b_ref[...]


def _modulation(cond, w_mod, b_mod):
    n = 3 * D_MODEL
    return pl.pallas_call(
        _mod_kernel,
        grid=(DEPTH, n // MOD_TN),
        in_specs=[
            pl.BlockSpec((N_COND, D_MODEL), lambda l, j: (0, 0)),
            pl.BlockSpec((None, D_MODEL, MOD_TN), lambda l, j: (l, 0, j)),
            pl.BlockSpec((None, 1, MOD_TN), lambda l, j: (l, 0, j)),
        ],
        out_specs=pl.BlockSpec((None, N_COND, MOD_TN), lambda l, j: (l, 0, j)),
        out_shape=jax.ShapeDtypeStruct((DEPTH, N_COND, n), F32),
        compiler_params=_params("arbitrary", "arbitrary"),
        name="modulation",
    )(cond, w_mod, b_mod.reshape(DEPTH, 1, n))


IN_TM = 1024
IN_FIRST_TM = 2048
IN_TN = 512
IN_LN_ROWS = 256
IN_CAST_K = 512


def _token_specs(xs, tm, tile_of=lambda i: i):
    n_ctx_tiles = N_CTX // tm
    if len(xs) == 1:
        maps = [lambda i, *_: (tile_of(i), 0)]
    else:
        maps = [lambda i, *_: (jnp.minimum(tile_of(i), n_ctx_tiles - 1), 0),
                lambda i, *_: (jnp.maximum(tile_of(i) - n_ctx_tiles, 0), 0)]
    return [pl.BlockSpec((tm, D_MODEL), m) for m in maps]


def _inproj_kernel(n_x, tm, tile0, cast_w, *refs):
    x_refs = refs[:n_x]
    scale_ref, shift_ref, w_ref = refs[n_x:n_x + 3]
    if cast_w:
        o_ref, wb_ref, h_ref = refs[n_x + 3:]
    else:
        _, o_ref, h_ref = refs[n_x + 3:]

    def weights():
        if not cast_w:
            return w_ref[...]
        w = w_ref[...].astype(BF16)
        wb_ref[...] = w
        return w

    def first_column_step(x_ref):
        w = weights()

        def norm(r):
            rows = slice(r, r + IN_LN_ROWS)
            h = _layer_norm(x_ref[rows, :]) * (1.0 + scale_ref[...]) + shift_ref[...]
            h = h.astype(BF16)
            h_ref[rows, :] = h
            return h

        h = norm(0)
        for r in range(0, tm, IN_LN_ROWS):
            o_ref[r:r + IN_LN_ROWS, :] = jnp.dot(h, w, preferred_element_type=F32)
            if r + IN_LN_ROWS < tm:
                h = norm(r + IN_LN_ROWS)

    first = pl.program_id(1) == 0
    if n_x == 1:
        pl.when(first)(lambda: first_column_step(x_refs[0]))
    else:
        is_ctx = pl.program_id(0) + tile0 < N_CTX // IN_TM
        pl.when(jnp.logical_and(first, is_ctx))(lambda: first_column_step(x_refs[0]))
        pl.when(jnp.logical_and(first, jnp.logical_not(is_ctx)))(
            lambda: first_column_step(x_refs[1]))

    @pl.when(jnp.logical_not(first))
    def _():
        acc = None
        for k in range(0, D_MODEL, IN_CAST_K):
            w = w_ref[k:k + IN_CAST_K, :]
            if cast_w:
                w = w.astype(BF16)
                wb_ref[k:k + IN_CAST_K, :] = w
            part = jnp.dot(h_ref[:, k:k + IN_CAST_K], w, preferred_element_type=F32)
            acc = part if acc is None else acc + part
        o_ref[...] = acc


def _in_projection(xs, mod4, w_in, layer):
    assert IN_FIRST_TM <= N_CTX and IN_FIRST_TM % IN_TM == 0
    n_col = D_IN // IN_TN
    first_tiles = IN_FIRST_TM // IN_TM

    def mod_spec(tile0, piece):
        return pl.BlockSpec((None, None, 1, D_MODEL),
                            lambda i, j: (layer, _cond_row(i + tile0, IN_TM), 0, piece))

    p_shape = jax.ShapeDtypeStruct((N_TOK, D_IN), F32)
    p, w_bf = pl.pallas_call(
        functools.partial(_inproj_kernel, 1, IN_FIRST_TM, 0, True),
        grid=(1, n_col),
        in_specs=[
            pl.BlockSpec((IN_FIRST_TM, D_MODEL), lambda i, j: (0, 0), pipeline_mode=pl.Buffered(1)),
            mod_spec(0, 1), mod_spec(0, 0),
            pl.BlockSpec((None, D_MODEL, IN_TN), lambda i, j: (layer, 0, j)),
        ],
        out_specs=[pl.BlockSpec((IN_FIRST_TM, IN_TN), lambda i, j: (0, j)),
                   pl.BlockSpec((D_MODEL, IN_TN), lambda i, j: (0, j))],
        out_shape=[p_shape, jax.ShapeDtypeStruct((D_MODEL, D_IN), BF16)],
        scratch_shapes=[pltpu.VMEM((IN_FIRST_TM, D_MODEL), BF16)],
        compiler_params=_params("arbitrary", "arbitrary"),
        name="in_projection_first",
    )(xs[0], mod4, mod4, w_in)
    return pl.pallas_call(
        functools.partial(_inproj_kernel, len(xs), IN_TM, first_tiles, False),
        grid=(N_TOK // IN_TM - first_tiles, n_col),
        in_specs=_token_specs(xs, IN_TM, lambda i: i + first_tiles) + [
            mod_spec(first_tiles, 1), mod_spec(first_tiles, 0),
            pl.BlockSpec((D_MODEL, IN_TN), lambda i, j: (0, j)),
            pl.BlockSpec(memory_space=pl.ANY),
        ],
        out_specs=pl.BlockSpec((IN_TM, IN_TN), lambda i, j: (i + first_tiles, j)),
        out_shape=p_shape,
        input_output_aliases={len(xs) + 3: 0},
        scratch_shapes=[pltpu.VMEM((IN_TM, D_MODEL), BF16)],
        compiler_params=_params("arbitrary", "arbitrary"),
        name="in_projection",
    )(*xs, mod4, mod4, w_bf, p)


AB_TM = 1024


def _gmlp_tasks(u_ref, v_ref, g_ref, w_ref, b_ref, o_ref):
    ws = [w_ref[g].astype(BF16) for g in range(N_GROUPS)]

    def chunk(r):
        rows = slice(r, r + CHUNK)
        vn = _layer_norm(v_ref[rows, :]).astype(BF16)
        for g in range(N_GROUPS):
            cols = slice(g * GROUP_W, (g + 1) * GROUP_W)
            mixed = jnp.dot(ws[g], vn[:, cols], preferred_element_type=F32) + b_ref[g]
            o_ref[rows, cols] = (_silu(g_ref[rows, cols]) * u_ref[rows, cols] * mixed).astype(BF16)

    return [functools.partial(chunk, r) for r in range(0, AB_TM, CHUNK)]


POOL_BAND_GROUPS = (2, 3)
POOL_BAND_ROWS = 256
POOL_BAND_HALO = 128


def _window_sum_shifts(p, win, pos, seq_len):
    acc = p
    for d in range(-(win // 2), win // 2):
        if d == 0:
            continue
        shifted = pltpu.roll(p, (-d) % AB_TM, axis=0)
        valid = (pos >= -d) if d < 0 else (pos < seq_len - d)
        acc = acc + jnp.where(valid, shifted, 0.0)
    return acc


def _window_sum_band_tasks(p, band_ref, pieces):
    hi = p.astype(BF16)
    r1 = p - hi.astype(F32)
    mid = r1.astype(BF16)
    lo = (r1 - mid.astype(F32)).astype(BF16)
    terms = jnp.concatenate([hi, mid, lo], axis=1)

    def piece(r):
        k0, k1 = max(0, r - POOL_BAND_HALO), min(AB_TM, r + POOL_BAND_ROWS + POOL_BAND_HALO)
        s = jnp.dot(band_ref[r:r + POOL_BAND_ROWS, k0:k1], terms[k0:k1, :],
                    preferred_element_type=F32)
        pieces.append(s[:, :GROUP_W] + s[:, GROUP_W:2 * GROUP_W] + s[:, 2 * GROUP_W:])

    return [functools.partial(piece, r) for r in range(0, AB_TM, POOL_BAND_ROWS)]


def _pool_finish(g, acc, x_ref, g_ref, w_ref, s_ref, pos, seq_len, o_ref):
    cols = slice(g * GROUP_W, (g + 1) * GROUP_W)
    win = POOL_WINDOWS[g]
    count = jnp.minimum(pos + win // 2, seq_len) - jnp.maximum(pos - win // 2, 0)
    pooled = acc / count.astype(F32)
    y = _bdot(pooled - x_ref[:, cols], w_ref[g])
    o_ref[:, BRANCH + g * GROUP_W:BRANCH + (g + 1) * GROUP_W] = (
        _silu(g_ref[:, cols]) * (y * s_ref[:, cols])).astype(BF16)


def _mixer_ab_kernel(au_ref, av_ref, ag_ref, bx_ref, bg_ref, gw_ref, gb_ref, pw_ref, ps_ref,
                     band_a_ref, band_b_ref, o_ref):
    seq_len = jnp.where(pl.program_id(0) < N_CTX // AB_TM, SEQ, DEC_SEQ)
    pos = lax.broadcasted_iota(jnp.int32, (AB_TM, GROUP_W), 0) & (seq_len - 1)
    pool_refs = (bx_ref, bg_ref, pw_ref, ps_ref, pos, seq_len, o_ref)
    band_sums = {g: [] for g in POOL_BAND_GROUPS}
    mxu_tasks = [t for g, band in zip(POOL_BAND_GROUPS, (band_a_ref, band_b_ref))
                 for t in _window_sum_band_tasks(bx_ref[:, g * GROUP_W:(g + 1) * GROUP_W], band,
                                                 band_sums[g])]
    vpu_tasks = _gmlp_tasks(au_ref, av_ref, ag_ref, gw_ref, gb_ref, o_ref)
    for i in range(max(len(mxu_tasks), len(vpu_tasks))):
        for tasks in (mxu_tasks, vpu_tasks):
            if i < len(tasks):
                tasks[i]()
    for g, win in enumerate(POOL_WINDOWS):
        if g not in POOL_BAND_GROUPS:
            acc = _window_sum_shifts(bx_ref[:, g * GROUP_W:(g + 1) * GROUP_W], win, pos, seq_len)
            _pool_finish(g, acc, *pool_refs)
    for g in POOL_BAND_GROUPS:
        _pool_finish(g, jnp.concatenate(band_sums[g], axis=0), *pool_refs)


def _pool_bands():
    t = np.arange(AB_TM)
    bands = []
    for g in POOL_BAND_GROUPS:
        half = POOL_WINDOWS[g] // 2
        d = t[None, :] - t[:, None]
        near = (d >= -half) & (d < half)
        per_path = [near & ((t[None, :] // L) == (t[:, None] // L)) for L in (SEQ, DEC_SEQ)]
        bands.append(jnp.asarray(np.stack(per_path), F32).astype(BF16))
    return bands


def _mixer_gmlp_pool(p, gmlp_w, gmlp_b_rows, pool_w, pool_scale_row):
    assert M_POOL == M_GMLP + 1
    n_steps = N_TOK // AB_TM
    blk = lambda c: pl.BlockSpec((AB_TM, BRANCH), lambda i: (i, c))
    full = lambda a: pl.BlockSpec(a.shape, lambda i: (0,) * a.ndim)
    band = pl.BlockSpec((None, AB_TM, AB_TM), lambda i: (jnp.where(i < N_CTX // AB_TM, 0, 1), 0, 0))
    return pl.pallas_call(
        _mixer_ab_kernel,
        grid=(n_steps,),
        in_specs=[blk(C_AU), blk(C_AV), blk(C_AG), blk(C_BX), blk(C_BG),
                  full(gmlp_w), full(gmlp_b_rows), full(pool_w), full(pool_scale_row), band, band],
        out_specs=pl.BlockSpec((AB_TM, 2 * BRANCH), lambda i: (i, M_GMLP // 2)),
        out_shape=jax.ShapeDtypeStruct((N_TOK, D_MIX), BF16),
        compiler_params=_params("arbitrary"),
        name="mixer_gmlp_pool",
    )(p, p, p, p, p, gmlp_w, gmlp_b_rows, pool_w, pool_scale_row, *_pool_bands())


ATT_TQ = 1024
V_ROWS = HEAD_DIM_C + 16
NT_DIMS = (((1,), (1,)), ((), ()))


def _lambda(lam_ref, lam_init):
    lq = lam_ref[...]
    a = jnp.sum(lq[0:1] * lq[1:2], axis=-1, keepdims=True)
    b = jnp.sum(lq[2:3] * lq[3:4], axis=-1, keepdims=True)
    return jnp.exp(a) - jnp.exp(b) + lam_init


def _map_masks():
    lane = lax.broadcasted_iota(jnp.int32, (1, HEAD_DIM_C), 1)
    m0 = (lane < QK_HALF).astype(F32)
    return m0, 1.0 - m0


def _scores_t(q, keys):
    return [lax.dot_general(k, q, NT_DIMS, preferred_element_type=F32) for k in keys]


def _softmax_v_t(s, vals_t):
    m = functools.reduce(jnp.maximum, [jnp.max(x, axis=0, keepdims=True) for x in s])
    acc = functools.reduce(
        jnp.add, [jnp.dot(v, jnp.exp2(x - m).astype(BF16), preferred_element_type=F32)
                  for x, v in zip(s, vals_t)])
    return acc[:HEAD_DIM_C] * (1.0 / acc[HEAD_DIM_C:HEAD_DIM_C + 1])


def _with_ones_rows(v_t):
    ones = jnp.ones((V_ROWS - HEAD_DIM_C, v_t.shape[1]), BF16)
    return jnp.concatenate([v_t.astype(BF16), ones], axis=0)


def _diff_attention(q_of, keys_of, vals_t_of, gate_of, store, lam, lam_init, subln):
    masks = _map_masks()
    tasks = [(h, mp) for h in range(N_HEADS_C) for mp in range(2)]
    per_head = {}

    def operands(h):
        if h not in per_head:
            q = q_of(h) * (QK_HALF ** -0.5 * math.log2(math.e))
            per_head[h] = (q, keys_of(h), vals_t_of(h))
        return per_head[h]

    def scores(h, mp):
        q, keys, _ = operands(h)
        return _scores_t((q * masks[mp]).astype(BF16), keys)

    nxt = scores(*tasks[0])
    outs = []
    for i, (h, mp) in enumerate(tasks):
        cur = nxt
        if i + 1 < len(tasks):
            nxt = scores(*tasks[i + 1])
        outs.append(_softmax_v_t(cur, operands(h)[2]))
        if mp == 1:
            o = (outs[0] - lam * outs[1]).T
            outs = []
            o = o * lax.rsqrt(jnp.mean(o * o, axis=-1, keepdims=True) + 1e-5)
            o = o * subln * (1.0 - lam_init)
            store(h, _silu(gate_of(h)) * o)


def _head_cols(h):
    return slice(h * HEAD_DIM_C, (h + 1) * HEAD_DIM_C)


def _attn_ctx_kernel(lam_init, q_ref, k_ref, v_ref, g_ref, lam_ref, sw_ref, *rest):
    o_ref, ko_ref, vo_ref = rest[-3:]
    ko_ref[...] = k_ref[...]
    for h in range(N_HEADS_C):
        vo_ref[:, h, :] = v_ref[:, _head_cols(h)]

    def store(h, y):
        o_ref[:, _head_cols(h)] = y.astype(BF16)

    _diff_attention(
        lambda h: q_ref[:, _head_cols(h)],
        lambda h: [k_ref[:, _head_cols(h)].astype(BF16)],
        lambda h: [_with_ones_rows(v_ref[:, _head_cols(h)].T)],
        lambda h: g_ref[:, _head_cols(h)],
        store, _lambda(lam_ref, lam_init), lam_init, sw_ref[...])


def _mixer_attn_ctx(p, lambda_qk, subln_row, layer, lam_init, ymix, caches):
    blk = lambda c: pl.BlockSpec((SEQ, BRANCH), lambda b: (b, c))
    any_spec = pl.BlockSpec(memory_space=pl.ANY)
    k_spec = pl.BlockSpec((None, None, SEQ, BRANCH), lambda b: (b, layer, 0, 0))
    v_spec = pl.BlockSpec((None, None, SEQ, N_HEADS_C, HEAD_DIM_C), lambda b: (b, layer, 0, 0, 0))
    k_shape = jax.ShapeDtypeStruct((BATCH, DEPTH, SEQ, BRANCH), F32)
    v_shape = jax.ShapeDtypeStruct((BATCH, DEPTH, SEQ, N_HEADS_C, HEAD_DIM_C), F32)
    in_specs = [
        blk(C_Q), blk(C_K), blk(C_V), blk(C_CG),
        pl.BlockSpec((None, 4, QK_HALF), lambda b: (layer, 0, 0)),
        pl.BlockSpec((None, 1, HEAD_DIM_C), lambda b: (layer, 0, 0)),
        any_spec,
    ]
    args = [p, p, p, p, lambda_qk, subln_row, ymix]
    aliases = {6: 0}
    if caches is not None:
        in_specs += [any_spec, any_spec]
        args += list(caches)
        aliases.update({7: 1, 8: 2})
    return pl.pallas_call(
        functools.partial(_attn_ctx_kernel, lam_init),
        grid=(BATCH,),
        in_specs=in_specs,
        out_specs=[pl.BlockSpec((SEQ, BRANCH), lambda b: (b, M_ATTN)), k_spec, v_spec],
        out_shape=[jax.ShapeDtypeStruct((N_TOK, D_MIX), BF16), k_shape, v_shape],
        input_output_aliases=aliases,
        compiler_params=_params("arbitrary"),
        name="mixer_attn_ctx",
    )(*args)


def _rope(x, cos, sin_signed):
    lane = lax.broadcasted_iota(jnp.int32, x.shape, 1)
    first_half = (lane & (ROPE_AXIS_DIM - 1)) < (ROPE_AXIS_DIM // 2)
    half = ROPE_AXIS_DIM // 2
    partner = jnp.where(first_half,
                        pltpu.roll(x, x.shape[1] - half, axis=1),
                        pltpu.roll(x, half, axis=1))
    return x * cos + partner * sin_signed


def _attn_lat_kernel(lam_init, q_ref, k_ref, v_ref, g_ref, ck_ref, cv_ref, cosq_ref, sinq_ref,
                     cosk_ref, sink_ref, lam_ref, sw_ref, wo_ref, ymix_ref, o_ref, wo_bf_ref,
                     kc_ref, kr_ref, vt_ref):
    del ymix_ref
    wo_bf_ref[...] = wo_ref[...].astype(BF16)

    @pl.when(pl.program_id(1) == 0)
    def _():
        kc_ref[...] = ck_ref[...].astype(BF16)
        for h in range(N_HEADS_C):
            cols = slice(h * HEAD_DIM_C, (h + 1) * HEAD_DIM_C)
            kr_ref[:, cols] = _rope(k_ref[:, cols], cosk_ref[...], sink_ref[...]).astype(BF16)
            vt_ref[h, :, 0:PAST_LEN] = _with_ones_rows(cv_ref[:, h, :].T)
            vt_ref[h, :, PAST_LEN:PAST_LEN + DEC_SEQ] = _with_ones_rows(v_ref[:, cols].T)

    def store(h, y):
        o_ref[:, _head_cols(h)] = y.astype(BF16)

    _diff_attention(
        lambda h: _rope(q_ref[:, _head_cols(h)], cosq_ref[...], sinq_ref[...]),
        lambda h: [kc_ref[:, _head_cols(h)], kr_ref[:, _head_cols(h)]],
        lambda h: [vt_ref[h, :, 0:PAST_LEN], vt_ref[h, :, PAST_LEN:PAST_LEN + DEC_SEQ]],
        lambda h: g_ref[:, _head_cols(h)],
        store, _lambda(lam_ref, lam_init), lam_init, sw_ref[...])


def _rope_tables():
    pos = np.arange(DEC_SEQ)
    row = (pos // GRID_W).astype(np.float64)
    col = (pos % GRID_W).astype(np.float64)
    half = ROPE_AXIS_DIM // 2
    inv = ROPE_BASE ** (-np.arange(0, ROPE_AXIS_DIM, 2, dtype=np.float64) / ROPE_AXIS_DIM)
    lane = np.arange(HEAD_DIM_C)
    axis_is_col = (lane // ROPE_AXIS_DIM) % 2 == 1
    idx = lane % ROPE_AXIS_DIM
    ang = np.where(axis_is_col[None, :], col[:, None], row[:, None]) * inv[idx % half][None, :]
    sign = np.where(idx < half, -1.0, 1.0)[None, :]
    return (jnp.asarray(np.cos(ang), F32), jnp.asarray(np.sin(ang) * sign, F32))


def _mixer_attn_lat(p, cache_k4, cache_v, lambda_qk, subln_row, layer, lam_init, w_out, ymix):
    cos_t, sin_t = _rope_tables()
    q_tiles = DEC_SEQ // ATT_TQ
    q0 = N_CTX // ATT_TQ
    b0 = N_CTX // DEC_SEQ
    wo_rows = D_MIX // (DEC_BATCH * q_tiles)
    qblk = lambda c: pl.BlockSpec((ATT_TQ, BRANCH), lambda b, i: (q0 + b * q_tiles + i, c))
    kblk = lambda c: pl.BlockSpec((DEC_SEQ, BRANCH), lambda b, i: (b0 + b, c))
    cblk = pl.BlockSpec((None, None, PAST_LEN, BRANCH), lambda b, i: (b, layer, 0, 0))
    cvblk = pl.BlockSpec((None, None, PAST_LEN, N_HEADS_C, HEAD_DIM_C),
                         lambda b, i: (b, layer, 0, 0, 0))
    return pl.pallas_call(
        functools.partial(_attn_lat_kernel, lam_init),
        grid=(DEC_BATCH, q_tiles),
        in_specs=[
            qblk(C_Q), kblk(C_K), kblk(C_V), qblk(C_CG), cblk, cvblk,
            pl.BlockSpec((ATT_TQ, HEAD_DIM_C), lambda b, i: (i, 0)),
            pl.BlockSpec((ATT_TQ, HEAD_DIM_C), lambda b, i: (i, 0)),
            pl.BlockSpec((DEC_SEQ, HEAD_DIM_C), lambda b, i: (0, 0)),
            pl.BlockSpec((DEC_SEQ, HEAD_DIM_C), lambda b, i: (0, 0)),
            pl.BlockSpec((None, 4, QK_HALF), lambda b, i: (layer, 0, 0)),
            pl.BlockSpec((None, 1, HEAD_DIM_C), lambda b, i: (layer, 0, 0)),
            pl.BlockSpec((None, wo_rows, D_MODEL), lambda b, i: (layer, b * q_tiles + i, 0)),
            pl.BlockSpec(memory_space=pl.ANY),
        ],
        out_specs=[pl.BlockSpec((ATT_TQ, BRANCH), lambda b, i: (q0 + b * q_tiles + i, M_ATTN)),
                   pl.BlockSpec((wo_rows, D_MODEL), lambda b, i: (b * q_tiles + i, 0))],
        out_shape=[jax.ShapeDtypeStruct((N_TOK, D_MIX), BF16),
                   jax.ShapeDtypeStruct((D_MIX, D_MODEL), BF16)],
        input_output_aliases={13: 0},
        scratch_shapes=[pltpu.VMEM((PAST_LEN, BRANCH), BF16), pltpu.VMEM((DEC_SEQ, BRANCH), BF16),
                        pltpu.VMEM((N_HEADS_C, V_ROWS, PAST_LEN + DEC_SEQ), BF16)],
        compiler_params=_params("arbitrary", "arbitrary"),
        name="mixer_attn_lat",
    )(p, p, p, p, cache_k4, cache_v, cos_t, sin_t, cos_t, sin_t, lambda_qk, subln_row, w_out, ymix)


HY_ROWS = 1024
HY_PIECE = 512
HY_STAGGER = 1


def _dft_matrices(seq_len):
    n = 2 * seq_len
    f = np.arange(seq_len, dtype=np.float64)[:, None]
    s = np.arange(seq_len, dtype=np.float64)[None, :]
    theta = 2.0 * np.pi * f * s / n
    alt = np.where(np.arange(seq_len) % 2 == 0, 1.0, -1.0)
    ac = np.cos(theta)
    as_ = -np.sin(theta)
    as_[0, :] = alt
    bc = 2.0 * np.cos(theta.T) / n
    bc[:, 0] = 1.0 / n
    bs = -2.0 * np.sin(theta.T) / n
    bs[:, 0] = alt / n
    fwd = np.concatenate([ac, as_], axis=0)
    inv = np.concatenate([bc, bs], axis=1)
    return jnp.asarray(fwd, F32).astype(BF16), jnp.asarray(inv, F32).astype(BF16)


def _filter_features(seq_len):
    t_idx = np.arange(seq_len, dtype=np.float64)
    t_norm = np.linspace(0.0, 1.0, seq_len)
    bands = np.linspace(1e-4, HY_BANDS - 1, HY_BANDS)
    ang = (2.0 * math.pi * t_idx / seq_len)[:, None] * bands[None, :]
    feats = np.concatenate([t_norm[:, None], np.cos(ang), np.sin(ang)], axis=-1)
    feats = np.pad(feats, ((0, 0), (0, HY_PAD - HY_EMB)))
    deltas = np.abs(np.linspace(math.log(HY_TARGET) / HY_FAST_DECAY,
                                math.log(HY_TARGET) / HY_SLOW_DECAY, BRANCH))
    return (jnp.asarray(feats, F32), jnp.asarray(t_norm[:, None], F32),
            jnp.asarray(deltas[None, :], F32))


def _filter_kernel(seq_len, feats_ref, tn_ref, dl_ref, w1_ref, b1_ref, w2_ref, b2_ref, fr_ref,
                   w3f_ref, w3b_ref, fwd_ref, kr_ref, ki_ref, h_ref):
    sp = lambda x: _split(x, 2)

    @pl.when(pl.program_id(1) == 0)
    def _():
        fr = fr_ref[...]
        h = jnp.sin(fr * (_sdot(sp(feats_ref[...]), sp(w1_ref[...])) + b1_ref[...]))
        h_ref[...] = jnp.sin(fr * (_sdot(sp(h), sp(w2_ref[...])) + b2_ref[...]))

    h = h_ref[...]
    decay = jnp.exp(-tn_ref[...] * dl_ref[...])
    row = lax.broadcasted_iota(jnp.int32, (seq_len, BRANCH), 0)
    fwd = _bdot(h, w3f_ref[...]) * decay
    bwd = jnp.where(row == 0, 0.0, _bdot(h, w3b_ref[...]) * decay)
    norm = (jnp.sum(jnp.abs(fwd), axis=0, keepdims=True)
            + jnp.sum(jnp.abs(bwd), axis=0, keepdims=True))
    fwd = fwd / norm
    bwd = bwd / norm
    even = fwd + bwd
    alt = jnp.where((row & 1) == 0, 1.0, -1.0)
    nyquist = jnp.sum(alt * even, axis=0, keepdims=True)
    kr_ref[...] = jnp.dot(fwd_ref[0:seq_len, :], even.astype(BF16), preferred_element_type=F32)
    ki = jnp.dot(fwd_ref[seq_len:2 * seq_len, :], (fwd - bwd).astype(BF16),
                 preferred_element_type=F32)
    ki_ref[...] = jnp.where(row == 0, nyquist, ki)


def _hyena_spectrum(seq_len, filt, fwd_mat):
    feats, t_norm, deltas = _filter_features(seq_len)
    w1, b1, w2, b2, freq, w3 = filt
    full = lambda a: pl.BlockSpec(a.shape, lambda l, o: (0,) * a.ndim)
    lyr = lambda r, n: pl.BlockSpec((None, r, n), lambda l, o: (l, 0, 0))
    out = pl.BlockSpec((None, None, seq_len, BRANCH), lambda l, o: (l, o, 0, 0))
    return pl.pallas_call(
        functools.partial(_filter_kernel, seq_len),
        grid=(DEPTH, HY_ORDER),
        in_specs=[
            full(feats), full(t_norm), full(deltas),
            lyr(HY_PAD, HY_PAD), lyr(1, HY_PAD), lyr(HY_PAD, HY_PAD), lyr(1, HY_PAD), lyr(1, HY_PAD),
            pl.BlockSpec((None, HY_PAD, BRANCH), lambda l, o: (l, 0, 2 * o)),
            pl.BlockSpec((None, HY_PAD, BRANCH), lambda l, o: (l, 0, 2 * o + 1)),
            full(fwd_mat),
        ],
        out_specs=[out, out],
        out_shape=[jax.ShapeDtypeStruct((DEPTH, HY_ORDER, seq_len, BRANCH), F32)] * 2,
        scratch_shapes=[pltpu.VMEM((seq_len, HY_PAD), F32)],
        compiler_params=_params("arbitrary", "arbitrary"),
        name=f"hyena_spectrum_{seq_len}",
    )(feats, t_norm, deltas, w1, b1, w2, b2, freq, w3, w3, fwd_mat)


def _hyena_chain(seq_len, path, rows, x_refs, cw_refs, cb_refs, kr_ref, ki_ref, hb_ref, fwd_ref,
                 inv_ref, o_ref):
    x1_ref, x2_ref, hv_ref, g_ref = x_refs
    row = lax.broadcasted_iota(jnp.int32, (seq_len, HY_CT), 0)
    first, last = row == 0, row == seq_len - 1

    def short_conv(x_ref, piece):
        x, w = x_ref[path, rows, :], cw_refs[piece][...]
        prev = jnp.where(first, 0.0, pltpu.roll(x, 1, axis=0))
        nxt = jnp.where(last, 0.0, pltpu.roll(x, seq_len - 1, axis=0))
        return prev * w[0:1] + x * w[1:2] + nxt * w[2:3] + cb_refs[piece][...]

    z = short_conv(hv_ref, 2)
    yield
    gate_refs = (x1_ref, x2_ref)
    piece = min(seq_len, HY_PIECE)
    for order in range(HY_ORDER):
        zb = z.astype(BF16)
        y_re, y_im = [], []
        for r in range(0, seq_len, piece):
            zr = jnp.dot(fwd_ref[r:r + piece, :], zb, preferred_element_type=F32)
            yield
            zi = jnp.dot(fwd_ref[seq_len + r:seq_len + r + piece, :], zb,
                         preferred_element_type=F32)
            yield
            kr, kp = kr_ref[order, r:r + piece, :], ki_ref[order, r:r + piece, :]
            ki, kn = kp, kr
            if r == 0:
                dc = lax.broadcasted_iota(jnp.int32, (piece, HY_CT), 0) == 0
                ki = jnp.where(dc, 0.0, kp)
                kn = jnp.where(dc, kp, kr)
            y_re.append((zr * kr - zi * ki).astype(BF16))
            y_im.append((zr * ki + zi * kn).astype(BF16))
            yield
        yf = jnp.concatenate(y_re + y_im, axis=0)
        ys = []
        for r in range(0, seq_len, piece):
            ys.append(jnp.dot(inv_ref[r:r + piece, :], yf, preferred_element_type=F32))
            yield
            if r == 0:
                gate = short_conv(gate_refs[order], order)
        y = ys[0] if len(ys) == 1 else jnp.concatenate(ys, axis=0)
        z = gate * (y + z * hb_ref[order:order + 1, :])
    o_ref[path, rows, :] = (_silu(g_ref[path, rows, :]) * z).astype(BF16)


def _hyena_kernel(x1_ref, x2_ref, hv_ref, g_ref, cw1_ref, cw2_ref, cw3_ref, cb1_ref, cb2_ref,
                  cb3_ref, hb_ref, krc_ref, kic_ref, fwdc_ref, invc_ref, krl_ref, kil_ref,
                  fwdl_ref, invl_ref, ymix_ref, o_ref):
    del ymix_ref
    x_refs = (x1_ref, x2_ref, hv_ref, g_ref)
    conv = ((cw1_ref, cw2_ref, cw3_ref), (cb1_ref, cb2_ref, cb3_ref))
    pending = [(0, _hyena_chain(DEC_SEQ, 1, slice(0, DEC_SEQ), x_refs, *conv, krl_ref, kil_ref,
                                hb_ref, fwdl_ref, invl_ref, o_ref))]
    for s in range(HY_ROWS // SEQ):
        pending.append((HY_STAGGER * s,
                        _hyena_chain(SEQ, 0, slice(s * SEQ, (s + 1) * SEQ), x_refs, *conv, krc_ref,
                                     kic_ref, hb_ref, fwdc_ref, invc_ref, o_ref)))
    running, tick = [], 0
    while pending or running:
        running += [gen for start, gen in pending if start == tick]
        pending = [(start, gen) for start, gen in pending if start > tick]
        for gen in list(running):
            if next(gen, "done") == "done":
                running.remove(gen)
        tick += 1


def _mixer_hyena(p, conv_w, conv_b3, hyena_bias, spec_ctx, mats_ctx, spec_lat, mats_lat, layer,
                 ymix):
    assert N_CTX == N_LAT and HY_ROWS == DEC_SEQ
    n_ct = BRANCH // HY_CT
    blk = lambda pc: pl.BlockSpec((2, HY_ROWS, HY_CT), lambda c, i: (0, i, pc * n_ct + c))
    cw = lambda pc: pl.BlockSpec((None, 3, HY_CT), lambda c, i: (layer, 0, pc * n_ct + c))
    cb = lambda pc: pl.BlockSpec((None, 1, HY_CT), lambda c, i: (layer, 0, pc * n_ct + c))
    spec = lambda a: pl.BlockSpec((None, HY_ORDER, a.shape[2], HY_CT), lambda c, i: (layer, 0, 0, c))
    full = lambda a: pl.BlockSpec(a.shape, lambda c, i: (0,) * a.ndim, pipeline_mode=pl.Buffered(1))
    p3 = p.reshape(2, N_CTX, D_IN)
    out = pl.pallas_call(
        _hyena_kernel,
        grid=(n_ct, N_CTX // HY_ROWS),
        in_specs=[blk(C_DX1), blk(C_DX2), blk(C_DV), blk(C_DG),
                  cw(0), cw(1), cw(2), cb(0), cb(1), cb(2),
                  pl.BlockSpec((None, HY_ORDER, HY_CT), lambda c, i: (layer, 0, c)),
                  spec(spec_ctx[0]), spec(spec_ctx[1]), full(mats_ctx[0]), full(mats_ctx[1]),
                  spec(spec_lat[0]), spec(spec_lat[1]), full(mats_lat[0]), full(mats_lat[1]),
                  pl.BlockSpec(memory_space=pl.ANY)],
        out_specs=pl.BlockSpec((2, HY_ROWS, HY_CT), lambda c, i: (0, i, M_HYENA * n_ct + c)),
        out_shape=jax.ShapeDtypeStruct((2, N_CTX, D_MIX), BF16),
        input_output_aliases={19: 0},
        compiler_params=_params("arbitrary", "arbitrary"),
        name="mixer_hyena",
    )(p3, p3, p3, p3, conv_w, conv_w, conv_w, conv_b3, conv_b3, conv_b3, hyena_bias,
      *spec_ctx, *mats_ctx, *spec_lat, *mats_lat, ymix.reshape(2, N_CTX, D_MIX))
    return out.reshape(N_TOK, D_MIX)


OUT_TM = 512


def _outproj_kernel(alpha, n_x, n_out, ymix_ref, *refs):
    x_refs = refs[:n_x]
    gate_ref, w_ref, b_ref, lng_ref, lnb_ref = refs[n_x:n_x + 5]
    o_refs = refs[n_x + 5:]
    is_ctx = pl.program_id(0) < N_CTX // OUT_TM

    def tile(x_ref, o_ref):
        y = jnp.dot(ymix_ref[...], w_ref[...], preferred_element_type=F32) + b_ref[...]
        r = alpha * x_ref[...] + gate_ref[...] * y
        o_ref[...] = _layer_norm(r) * lng_ref[...] + lnb_ref[...]

    if n_x == 1 and n_out == 1:
        tile(x_refs[0], o_refs[0])
    else:
        pl.when(is_ctx)(lambda: tile(x_refs[0], o_refs[0]))
        pl.when(jnp.logical_not(is_ctx))(lambda: tile(x_refs[-1], o_refs[-1]))


def _out_projection(ymix, xs, mod4, w_out_bf, b_out3, ln_g3, ln_b3, layer, split_out):
    alpha = (2.0 * DEPTH) ** 0.25
    row = lambda i: _cond_row(i, OUT_TM)
    vec = pl.BlockSpec((None, 1, D_MODEL), lambda i: (layer, 0, 0))
    if split_out:
        out_shape = [jax.ShapeDtypeStruct((N_CTX, D_MODEL), F32),
                     jax.ShapeDtypeStruct((N_LAT, D_MODEL), F32)]
    else:
        out_shape = [jax.ShapeDtypeStruct((N_TOK, D_MODEL), F32)]
    return pl.pallas_call(
        functools.partial(_outproj_kernel, alpha, len(xs), len(out_shape)),
        grid=(N_TOK // OUT_TM,),
        in_specs=[pl.BlockSpec((OUT_TM, D_MIX), lambda i: (i, 0))] + _token_specs(xs, OUT_TM) + [
            pl.BlockSpec((None, None, 1, D_MODEL), lambda i: (layer, row(i), 0, 2)),
            pl.BlockSpec((D_MIX, D_MODEL), lambda i: (0, 0), pipeline_mode=pl.Buffered(1)),
            vec, vec, vec,
        ],
        out_specs=_token_specs(out_shape, OUT_TM),
        out_shape=out_shape,
        compiler_params=_params("arbitrary"),
        name="out_projection",
    )(ymix, *xs, mod4, w_out_bf, b_out3, ln_g3, ln_b3)


def kernel(x_prompt, x_sample, cache_k, cache_v, c, c_ctx, w_mod, b_mod, w_in, gmlp_w, gmlp_b,
           pool_w, pool_scale, lambda_qk, subln_w, conv_w, conv_b, filt_w1, filt_b1, filt_w2,
           filt_b2, filt_freq, filt_w3, hyena_bias, w_out, b_out, ln_g, ln_b):
    xs = (x_prompt.reshape(N_CTX, D_MODEL), x_sample.reshape(N_LAT, D_MODEL))
    cond = jnp.concatenate(
        [c_ctx[None, :], c, jnp.zeros((N_COND - 1 - DEC_BATCH, D_MODEL), F32)], axis=0)
    mod4 = _modulation(cond, w_mod, b_mod).reshape(DEPTH, N_COND, 1, 3 * D_MODEL)

    cache_k4 = cache_k.reshape(DEC_BATCH, DEPTH, PAST_LEN, BRANCH)
    gmlp_b_rows = jnp.broadcast_to(gmlp_b[..., None], (DEPTH, N_GROUPS, CHUNK, GROUP_W))
    subln_row = subln_w.reshape(DEPTH, 1, HEAD_DIM_C)
    conv_b3 = conv_b.reshape(DEPTH, 1, 3 * BRANCH)
    b_out3 = b_out.reshape(DEPTH, 1, D_MODEL)
    ln_g3 = ln_g.reshape(DEPTH, 1, D_MODEL)
    ln_b3 = ln_b.reshape(DEPTH, 1, D_MODEL)

    pad_h = HY_PAD - HY_HIDDEN
    filt = (
        jnp.pad(filt_w1, ((0, 0), (0, HY_PAD - HY_EMB), (0, pad_h))),
        jnp.pad(filt_b1, ((0, 0), (0, pad_h))).reshape(DEPTH, 1, HY_PAD),
        jnp.pad(filt_w2, ((0, 0), (0, pad_h), (0, pad_h))),
        jnp.pad(filt_b2, ((0, 0), (0, pad_h))).reshape(DEPTH, 1, HY_PAD),
        jnp.pad(filt_freq, ((0, 0), (0, pad_h))).reshape(DEPTH, 1, HY_PAD),
        jnp.pad(filt_w3, ((0, 0), (0, pad_h), (0, 0))),
    )
    mats_ctx = _dft_matrices(SEQ)
    mats_lat = _dft_matrices(DEC_SEQ)
    spec_ctx = _hyena_spectrum(SEQ, filt, mats_ctx[0])
    spec_lat = _hyena_spectrum(DEC_SEQ, filt, mats_lat[0])

    caches = None
    for layer in range(DEPTH):
        lam_init = 0.8 - 0.6 * math.exp(-0.3 * layer)
        p = _in_projection(xs, mod4, w_in, layer)
        ymix = _mixer_gmlp_pool(p, gmlp_w[layer], gmlp_b_rows[layer], pool_w[layer],
                                pool_scale[layer].reshape(1, BRANCH))
        ymix, new_k, new_v = _mixer_attn_ctx(p, lambda_qk, subln_row, layer, lam_init, ymix, caches)
        caches = (new_k, new_v)
        ymix, w_out_bf = _mixer_attn_lat(p, cache_k4, cache_v, lambda_qk, subln_row, layer,
                                         lam_init, w_out, ymix)
        ymix = _mixer_hyena(p, conv_w, conv_b3, hyena_bias, spec_ctx, mats_ctx, spec_lat, mats_lat,
                            layer, ymix)
        xs = tuple(_out_projection(ymix, xs, mod4, w_out_bf, b_out3, ln_g3, ln_b3, layer,
                                   split_out=layer == DEPTH - 1))

    y_prompt = xs[0].reshape(BATCH, SEQ, D_MODEL)
    y_sample = xs[1].reshape(DEC_BATCH, DEC_SEQ, D_MODEL)
    new_k, new_v = caches
    return (y_prompt, y_sample,
            new_k.reshape(BATCH, DEPTH, SEQ, N_HEADS_C, 2, QK_HALF), new_v)
```

```python
import functools
import math

import numpy as np
import jax
import jax.numpy as jnp
from jax import lax
from jax.experimental import pallas as pl
from jax.experimental.pallas import tpu as pltpu

F32 = jnp.float32
BF16 = jnp.bfloat16

D_MODEL = 2048
BATCH = 16
SEQ = 256
DEPTH = 2
DEC_BATCH = 4
DEC_SEQ = 1024
PAST_LEN = 512
GRID_W = 64
BRANCH = 512
N_GROUPS = 4
GROUP_W = 128
CHUNK = 128
POOL_WINDOWS = (2, 4, 8, 16)
N_HEADS_C = 4
HEAD_DIM_C = 128
QK_HALF = 64
ROPE_AXIS_DIM = 32
ROPE_BASE = 10000.0
HY_BANDS = 16
HY_EMB = 33
HY_HIDDEN = 64
HY_ORDER = 2
HY_FAST_DECAY = 0.3
HY_SLOW_DECAY = 1.5
HY_TARGET = 1e-2
N_IN_PIECES = 13
D_IN = N_IN_PIECES * BRANCH
LN_EPS = 1e-6

N_CTX = BATCH * SEQ
N_LAT = DEC_BATCH * DEC_SEQ
N_TOK = N_CTX + N_LAT
N_COND = 8
LANES = 128
HY_PAD = LANES
HY_CT = 256
VMEM_LIMIT = 56 * 1024 * 1024

(C_AU, C_AV, C_AG, C_BX, C_BG, C_Q, C_K, C_V, C_CG, C_DX1, C_DX2, C_DV, C_DG) = range(13)
D_MIX = 4 * BRANCH
(M_GMLP, M_POOL, M_ATTN, M_HYENA) = range(4)


def _silu(x):
    return x * jax.nn.sigmoid(x)


def _bdot(a, b):
    return jnp.dot(a.astype(BF16), b.astype(BF16), preferred_element_type=F32)


def _split(x, n_terms):
    hi = x.astype(BF16)
    if n_terms == 1:
        return (hi,)
    return (hi, (x - hi.astype(F32)).astype(BF16))


def _sdot(a_terms, b_terms):
    acc = jnp.dot(a_terms[0], b_terms[0], preferred_element_type=F32)
    if len(a_terms) > 1:
        acc = acc + jnp.dot(a_terms[1], b_terms[0], preferred_element_type=F32)
    if len(b_terms) > 1:
        acc = acc + jnp.dot(a_terms[0], b_terms[1], preferred_element_type=F32)
    return acc


def _layer_norm(x):
    mu = jnp.mean(x, axis=-1, keepdims=True)
    xc = x - mu
    var = jnp.mean(xc * xc, axis=-1, keepdims=True)
    return xc * lax.rsqrt(var + LN_EPS)


def _cond_row(tile, rows_per_tile):
    n_ctx_tiles = N_CTX // rows_per_tile
    tiles_per_batch = DEC_SEQ // rows_per_tile
    return jnp.where(tile < n_ctx_tiles, 0, 1 + (tile - n_ctx_tiles) // tiles_per_batch)


def _params(*semantics):
    return pltpu.CompilerParams(dimension_semantics=semantics, vmem_limit_bytes=VMEM_LIMIT)


MOD_TN = 1024


def _mod_kernel(c_ref, w_ref, b_ref, o_ref):
    o_ref[...] = _bdot(_silu(c_ref[...]), w_ref[...]) + b_ref[...]


def _modulation(cond, w_mod, b_mod):
    n = 3 * D_MODEL
    return pl.pallas_call(
        _mod_kernel,
        grid=(DEPTH, n // MOD_TN),
        in_specs=[
            pl.BlockSpec((N_COND, D_MODEL), lambda l, j: (0, 0)),
            pl.BlockSpec((None, D_MODEL, MOD_TN), lambda l, j: (l, 0, j)),
            pl.BlockSpec((None, 1, MOD_TN), lambda l, j: (l, 0, j)),
        ],
        out_specs=pl.BlockSpec((None, N_COND, MOD_TN), lambda l, j: (l, 0, j)),
        out_shape=jax.ShapeDtypeStruct((DEPTH, N_COND, n), F32),
        compiler_params=_params("arbitrary", "arbitrary"),
        name="modulation",
    )(cond, w_mod, b_mod.reshape(DEPTH, 1, n))


IN_TM = 1024
IN_FIRST_TM = 2048
IN_TN = 512
IN_LN_ROWS = 256
IN_CAST_K = 512


def _token_specs(xs, tm, tile_of=lambda i: i):
    n_ctx_tiles = N_CTX // tm
    if len(xs) == 1:
        maps = [lambda i, *_: (tile_of(i), 0)]
    else:
        maps = [lambda i, *_: (jnp.minimum(tile_of(i), n_ctx_tiles - 1), 0),
                lambda i, *_: (jnp.maximum(tile_of(i) - n_ctx_tiles, 0), 0)]
    return [pl.BlockSpec((tm, D_MODEL), m) for m in maps]


def _inproj_kernel(n_x, tm, tile0, cast_w, *refs):
    x_refs = refs[:n_x]
    scale_ref, shift_ref, w_ref = refs[n_x:n_x + 3]
    if cast_w:
        o_ref, wb_ref, h_ref = refs[n_x + 3:]
    else:
        _, o_ref, h_ref = refs[n_x + 3:]

    def weights():
        if not cast_w:
            return w_ref[...]
        w = w_ref[...].astype(BF16)
        wb_ref[...] = w
        return w

    def first_column_step(x_ref):
        w = weights()

        def norm(r):
            rows = slice(r, r + IN_LN_ROWS)
            h = _layer_norm(x_ref[rows, :]) * (1.0 + scale_ref[...]) + shift_ref[...]
            h = h.astype(BF16)
            h_ref[rows, :] = h
            return h

        h = norm(0)
        for r in range(0, tm, IN_LN_ROWS):
            o_ref[r:r + IN_LN_ROWS, :] = jnp.dot(h, w, preferred_element_type=F32)
            if r + IN_LN_ROWS < tm:
                h = norm(r + IN_LN_ROWS)

    first = pl.program_id(1) == 0
    if n_x == 1:
        pl.when(first)(lambda: first_column_step(x_refs[0]))
    else:
        is_ctx = pl.program_id(0) + tile0 < N_CTX // IN_TM
        pl.when(jnp.logical_and(first, is_ctx))(lambda: first_column_step(x_refs[0]))
        pl.when(jnp.logical_and(first, jnp.logical_not(is_ctx)))(
            lambda: first_column_step(x_refs[1]))

    @pl.when(jnp.logical_not(first))
    def _():
        acc = None
        for k in range(0, D_MODEL, IN_CAST_K):
            w = w_ref[k:k + IN_CAST_K, :]
            if cast_w:
                w = w.astype(BF16)
                wb_ref[k:k + IN_CAST_K, :] = w
            part = jnp.dot(h_ref[:, k:k + IN_CAST_K], w, preferred_element_type=F32)
            acc = part if acc is None else acc + part
        o_ref[...] = acc


def _in_projection(xs, mod4, w_in, layer):
    assert IN_FIRST_TM <= N_CTX and IN_FIRST_TM % IN_TM == 0
    n_col = D_IN // IN_TN
    first_tiles = IN_FIRST_TM // IN_TM

    def mod_spec(tile0, piece):
        return pl.BlockSpec((None, None, 1, D_MODEL),
                            lambda i, j: (layer, _cond_row(i + tile0, IN_TM), 0, piece))

    p_shape = jax.ShapeDtypeStruct((N_TOK, D_IN), F32)
    p, w_bf = pl.pallas_call(
        functools.partial(_inproj_kernel, 1, IN_FIRST_TM, 0, True),
        grid=(1, n_col),
        in_specs=[
            pl.BlockSpec((IN_FIRST_TM, D_MODEL), lambda i, j: (0, 0), pipeline_mode=pl.Buffered(1)),
            mod_spec(0, 1), mod_spec(0, 0),
            pl.BlockSpec((None, D_MODEL, IN_TN), lambda i, j: (layer, 0, j)),
        ],
        out_specs=[pl.BlockSpec((IN_FIRST_TM, IN_TN), lambda i, j: (0, j)),
                   pl.BlockSpec((D_MODEL, IN_TN), lambda i, j: (0, j))],
        out_shape=[p_shape, jax.ShapeDtypeStruct((D_MODEL, D_IN), BF16)],
        scratch_shapes=[pltpu.VMEM((IN_FIRST_TM, D_MODEL), BF16)],
        compiler_params=_params("arbitrary", "arbitrary"),
        name="in_projection_first",
    )(xs[0], mod4, mod4, w_in)
    return pl.pallas_call(
        functools.partial(_inproj_kernel, len(xs), IN_TM, first_tiles, False),
        grid=(N_TOK // IN_TM - first_tiles, n_col),
        in_specs=_token_specs(xs, IN_TM, lambda i: i + first_tiles) + [
            mod_spec(first_tiles, 1), mod_spec(first_tiles, 0),
            pl.BlockSpec((D_MODEL, IN_TN), lambda i, j: (0, j)),
            pl.BlockSpec(memory_space=pl.ANY),
        ],
        out_specs=pl.BlockSpec((IN_TM, IN_TN), lambda i, j: (i + first_tiles, j)),
        out_shape=p_shape,
        input_output_aliases={len(xs) + 3: 0},
        scratch_shapes=[pltpu.VMEM((IN_TM, D_MODEL), BF16)],
        compiler_params=_params("arbitrary", "arbitrary"),
        name="in_projection",
    )(*xs, mod4, mod4, w_bf, p)


AB_TM = 1024


def _gmlp_tasks(u_ref, v_ref, g_ref, w_ref, b_ref, o_ref):
    ws = [w_ref[g].astype(BF16) for g in range(N_GROUPS)]

    def chunk(r):
        rows = slice(r, r + CHUNK)
        vn = _layer_norm(v_ref[rows, :]).astype(BF16)
        for g in range(N_GROUPS):
            cols = slice(g * GROUP_W, (g + 1) * GROUP_W)
            mixed = jnp.dot(ws[g], vn[:, cols], preferred_element_type=F32) + b_ref[g]
            o_ref[rows, cols] = (_silu(g_ref[rows, cols]) * u_ref[rows, cols] * mixed).astype(BF16)

    return [functools.partial(chunk, r) for r in range(0, AB_TM, CHUNK)]


POOL_BAND_GROUPS = (2, 3)
POOL_BAND_ROWS = 256
POOL_BAND_HALO = 128


def _window_sum_shifts(p, win, pos, seq_len):
    acc = p
    for d in range(-(win // 2), win // 2):
        if d == 0:
            continue
        shifted = pltpu.roll(p, (-d) % AB_TM, axis=0)
        valid = (pos >= -d) if d < 0 else (pos < seq_len - d)
        acc = acc + jnp.where(valid, shifted, 0.0)
    return acc


def _window_sum_band_tasks(p, band_ref, pieces):
    hi = p.astype(BF16)
    r1 = p - hi.astype(F32)
    mid = r1.astype(BF16)
    lo = (r1 - mid.astype(F32)).astype(BF16)
    terms = jnp.concatenate([hi, mid, lo], axis=1)

    def piece(r):
        k0, k1 = max(0, r - POOL_BAND_HALO), min(AB_TM, r + POOL_BAND_ROWS + POOL_BAND_HALO)
        s = jnp.dot(band_ref[r:r + POOL_BAND_ROWS, k0:k1], terms[k0:k1, :],
                    preferred_element_type=F32)
        pieces.append(s[:, :GROUP_W] + s[:, GROUP_W:2 * GROUP_W] + s[:, 2 * GROUP_W:])

    return [functools.partial(piece, r) for r in range(0, AB_TM, POOL_BAND_ROWS)]


def _pool_finish(g, acc, x_ref, g_ref, w_ref, s_ref, pos, seq_len, o_ref):
    cols = slice(g * GROUP_W, (g + 1) * GROUP_W)
    win = POOL_WINDOWS[g]
    count = jnp.minimum(pos + win // 2, seq_len) - jnp.maximum(pos - win // 2, 0)
    pooled = acc / count.astype(F32)
    y = _bdot(pooled - x_ref[:, cols], w_ref[g])
    o_ref[:, BRANCH + g * GROUP_W:BRANCH + (g + 1) * GROUP_W] = (
        _silu(g_ref[:, cols]) * (y * s_ref[:, cols])).astype(BF16)


def _mixer_ab_kernel(au_ref, av_ref, ag_ref, bx_ref, bg_ref, gw_ref, gb_ref, pw_ref, ps_ref,
                     band_a_ref, band_b_ref, o_ref):
    seq_len = jnp.where(pl.program_id(0) < N_CTX // AB_TM, SEQ, DEC_SEQ)
    pos = lax.broadcasted_iota(jnp.int32, (AB_TM, GROUP_W), 0) & (seq_len - 1)
    pool_refs = (bx_ref, bg_ref, pw_ref, ps_ref, pos, seq_len, o_ref)
    band_sums = {g: [] for g in POOL_BAND_GROUPS}
    mxu_tasks = [t for g, band in zip(POOL_BAND_GROUPS, (band_a_ref, band_b_ref))
                 for t in _window_sum_band_tasks(bx_ref[:, g * GROUP_W:(g + 1) * GROUP_W], band,
                                                 band_sums[g])]
    vpu_tasks = _gmlp_tasks(au_ref, av_ref, ag_ref, gw_ref, gb_ref, o_ref)
    for i in range(max(len(mxu_tasks), len(vpu_tasks))):
        for tasks in (mxu_tasks, vpu_tasks):
            if i < len(tasks):
                tasks[i]()
    for g, win in enumerate(POOL_WINDOWS):
        if g not in POOL_BAND_GROUPS:
            acc = _window_sum_shifts(bx_ref[:, g * GROUP_W:(g + 1) * GROUP_W], win, pos, seq_len)
            _pool_finish(g, acc, *pool_refs)
    for g in POOL_BAND_GROUPS:
        _pool_finish(g, jnp.concatenate(band_sums[g], axis=0), *pool_refs)


def _pool_bands():
    t = np.arange(AB_TM)
    bands = []
    for g in POOL_BAND_GROUPS:
        half = POOL_WINDOWS[g] // 2
        d = t[None, :] - t[:, None]
        near = (d >= -half) & (d < half)
        per_path = [near & ((t[None, :] // L) == (t[:, None] // L)) for L in (SEQ, DEC_SEQ)]
        bands.append(jnp.asarray(np.stack(per_path), F32).astype(BF16))
    return bands


def _mixer_gmlp_pool(p, gmlp_w, gmlp_b_rows, pool_w, pool_scale_row):
    assert M_POOL == M_GMLP + 1
    n_steps = N_TOK // AB_TM
    blk = lambda c: pl.BlockSpec((AB_TM, BRANCH), lambda i: (i, c))
    full = lambda a: pl.BlockSpec(a.shape, lambda i: (0,) * a.ndim)
    band = pl.BlockSpec((None, AB_TM, AB_TM), lambda i: (jnp.where(i < N_CTX // AB_TM, 0, 1), 0, 0))
    return pl.pallas_call(
        _mixer_ab_kernel,
        grid=(n_steps,),
        in_specs=[blk(C_AU), blk(C_AV), blk(C_AG), blk(C_BX), blk(C_BG),
                  full(gmlp_w), full(gmlp_b_rows), full(pool_w), full(pool_scale_row), band, band],
        out_specs=pl.BlockSpec((AB_TM, 2 * BRANCH), lambda i: (i, M_GMLP // 2)),
        out_shape=jax.ShapeDtypeStruct((N_TOK, D_MIX), BF16),
        compiler_params=_params("arbitrary"),
        name="mixer_gmlp_pool",
    )(p, p, p, p, p, gmlp_w, gmlp_b_rows, pool_w, pool_scale_row, *_pool_bands())


ATT_TQ = 1024
CTX_SEQS = 4
V_ROWS = HEAD_DIM_C + 16
NT_DIMS = (((1,), (1,)), ((), ()))


def _lambda(lam_ref, lam_init):
    lq = lam_ref[...]
    a = jnp.sum(lq[0:1] * lq[1:2], axis=-1, keepdims=True)
    b = jnp.sum(lq[2:3] * lq[3:4], axis=-1, keepdims=True)
    return jnp.exp(a) - jnp.exp(b) + lam_init


def _map_masks():
    lane = lax.broadcasted_iota(jnp.int32, (1, HEAD_DIM_C), 1)
    m0 = (lane < QK_HALF).astype(F32)
    return m0, 1.0 - m0


def _scores_t(q, keys):
    return [lax.dot_general(k, q, NT_DIMS, preferred_element_type=F32) for k in keys]


def _softmax_v_t(s, vals_t):
    m = functools.reduce(jnp.maximum, [jnp.max(x, axis=0, keepdims=True) for x in s])
    acc = functools.reduce(
        jnp.add, [jnp.dot(v, jnp.exp2(x - m).astype(BF16), preferred_element_type=F32)
                  for x, v in zip(s, vals_t)])
    return acc[:HEAD_DIM_C] * (1.0 / acc[HEAD_DIM_C:HEAD_DIM_C + 1])


def _with_ones_rows(v_t):
    ones = jnp.ones((V_ROWS - HEAD_DIM_C, v_t.shape[1]), BF16)
    return jnp.concatenate([v_t.astype(BF16), ones], axis=0)


def _diff_attention(n_units, q_of, keys_of, vals_t_of, gate_of, store, lam, lam_init, subln):
    masks = _map_masks()
    tasks = [(h, mp) for h in range(n_units) for mp in range(2)]
    per_head = {}

    def operands(h):
        if h not in per_head:
            q = q_of(h) * (QK_HALF ** -0.5 * math.log2(math.e))
            per_head[h] = (q, keys_of(h), vals_t_of(h))
        return per_head[h]

    def scores(h, mp):
        q, keys, _ = operands(h)
        return _scores_t((q * masks[mp]).astype(BF16), keys)

    nxt = scores(*tasks[0])
    outs = []
    for i, (h, mp) in enumerate(tasks):
        cur = nxt
        if i + 1 < len(tasks):
            nxt = scores(*tasks[i + 1])
        outs.append(_softmax_v_t(cur, operands(h)[2]))
        if mp == 1:
            o = (outs[0] - lam * outs[1]).T
            outs = []
            o = o * lax.rsqrt(jnp.mean(o * o, axis=-1, keepdims=True) + 1e-5)
            o = o * subln * (1.0 - lam_init)
            store(h, _silu(gate_of(h)) * o)


def _head_cols(h):
    return slice(h * HEAD_DIM_C, (h + 1) * HEAD_DIM_C)


def _attn_ctx_kernel(lam_init, q_ref, k_ref, v_ref, g_ref, lam_ref, sw_ref, *rest):
    o_ref, ko_ref, vo_ref = rest[-3:]
    for s in range(CTX_SEQS):
        rows = slice(s * SEQ, (s + 1) * SEQ)
        ko_ref[s] = k_ref[rows, :]
        for h in range(N_HEADS_C):
            vo_ref[s, :, h, :] = v_ref[rows, _head_cols(h)]

    at = lambda ref, u: ref[(u // N_HEADS_C) * SEQ:(u // N_HEADS_C + 1) * SEQ,
                            _head_cols(u % N_HEADS_C)]

    def store(u, y):
        o_ref[(u // N_HEADS_C) * SEQ:(u // N_HEADS_C + 1) * SEQ,
              _head_cols(u % N_HEADS_C)] = y.astype(BF16)

    _diff_attention(
        CTX_SEQS * N_HEADS_C,
        lambda u: at(q_ref, u),
        lambda u: [at(k_ref, u).astype(BF16)],
        lambda u: [_with_ones_rows(at(v_ref, u).T)],
        lambda u: at(g_ref, u),
        store, _lambda(lam_ref, lam_init), lam_init, sw_ref[...])


def _mixer_attn_ctx(p, lambda_qk, subln_row, layer, lam_init, ymix, caches):
    blk = lambda c: pl.BlockSpec((CTX_SEQS * SEQ, BRANCH), lambda b: (b, c))
    any_spec = pl.BlockSpec(memory_space=pl.ANY)
    k_spec = pl.BlockSpec((CTX_SEQS, None, SEQ, BRANCH), lambda b: (b, layer, 0, 0))
    v_spec = pl.BlockSpec((CTX_SEQS, None, SEQ, N_HEADS_C, HEAD_DIM_C),
                          lambda b: (b, layer, 0, 0, 0))
    k_shape = jax.ShapeDtypeStruct((BATCH, DEPTH, SEQ, BRANCH), F32)
    v_shape = jax.ShapeDtypeStruct((BATCH, DEPTH, SEQ, N_HEADS_C, HEAD_DIM_C), F32)
    in_specs = [
        blk(C_Q), blk(C_K), blk(C_V), blk(C_CG),
        pl.BlockSpec((None, 4, QK_HALF), lambda b: (layer, 0, 0)),
        pl.BlockSpec((None, 1, HEAD_DIM_C), lambda b: (layer, 0, 0)),
        any_spec,
    ]
    args = [p, p, p, p, lambda_qk, subln_row, ymix]
    aliases = {6: 0}
    if caches is not None:
        in_specs += [any_spec, any_spec]
        args += list(caches)
        aliases.update({7: 1, 8: 2})
    return pl.pallas_call(
        functools.partial(_attn_ctx_kernel, lam_init),
        grid=(BATCH // CTX_SEQS,),
        in_specs=in_specs,
        out_specs=[pl.BlockSpec((CTX_SEQS * SEQ, BRANCH), lambda b: (b, M_ATTN)), k_spec, v_spec],
        out_shape=[jax.ShapeDtypeStruct((N_TOK, D_MIX), BF16), k_shape, v_shape],
        input_output_aliases=aliases,
        compiler_params=_params("arbitrary"),
        name="mixer_attn_ctx",
    )(*args)


def _rope(x, cos, sin_signed):
    lane = lax.broadcasted_iota(jnp.int32, x.shape, 1)
    first_half = (lane & (ROPE_AXIS_DIM - 1)) < (ROPE_AXIS_DIM // 2)
    half = ROPE_AXIS_DIM // 2
    partner = jnp.where(first_half,
                        pltpu.roll(x, x.shape[1] - half, axis=1),
                        pltpu.roll(x, half, axis=1))
    return x * cos + partner * sin_signed


def _attn_lat_kernel(lam_init, q_ref, k_ref, v_ref, g_ref, ck_ref, cv_ref, cosq_ref, sinq_ref,
                     cosk_ref, sink_ref, lam_ref, sw_ref, wo_ref, ymix_ref, o_ref, wo_bf_ref,
                     kc_ref, kr_ref, vt_ref):
    del ymix_ref
    wo_bf_ref[...] = wo_ref[...].astype(BF16)

    @pl.when(pl.program_id(1) == 0)
    def _():
        kc_ref[...] = ck_ref[...].astype(BF16)
        for h in range(N_HEADS_C):
            cols = slice(h * HEAD_DIM_C, (h + 1) * HEAD_DIM_C)
            kr_ref[:, cols] = _rope(k_ref[:, cols], cosk_ref[...], sink_ref[...]).astype(BF16)
            vt_ref[h, :, 0:PAST_LEN] = _with_ones_rows(cv_ref[:, h, :].T)
            vt_ref[h, :, PAST_LEN:PAST_LEN + DEC_SEQ] = _with_ones_rows(v_ref[:, cols].T)

    def store(h, y):
        o_ref[:, _head_cols(h)] = y.astype(BF16)

    _diff_attention(
        N_HEADS_C,
        lambda h: _rope(q_ref[:, _head_cols(h)], cosq_ref[...], sinq_ref[...]),
        lambda h: [kc_ref[:, _head_cols(h)], kr_ref[:, _head_cols(h)]],
        lambda h: [vt_ref[h, :, 0:PAST_LEN], vt_ref[h, :, PAST_LEN:PAST_LEN + DEC_SEQ]],
        lambda h: g_ref[:, _head_cols(h)],
        store, _lambda(lam_ref, lam_init), lam_init, sw_ref[...])


def _rope_tables():
    pos = np.arange(DEC_SEQ)
    row = (pos // GRID_W).astype(np.float64)
    col = (pos % GRID_W).astype(np.float64)
    half = ROPE_AXIS_DIM // 2
    inv = ROPE_BASE ** (-np.arange(0, ROPE_AXIS_DIM, 2, dtype=np.float64) / ROPE_AXIS_DIM)
    lane = np.arange(HEAD_DIM_C)
    axis_is_col = (lane // ROPE_AXIS_DIM) % 2 == 1
    idx = lane % ROPE_AXIS_DIM
    ang = np.where(axis_is_col[None, :], col[:, None], row[:, None]) * inv[idx % half][None, :]
    sign = np.where(idx < half, -1.0, 1.0)[None, :]
    return (jnp.asarray(np.cos(ang), F32), jnp.asarray(np.sin(ang) * sign, F32))


def _mixer_attn_lat(p, cache_k4, cache_v, lambda_qk, subln_row, layer, lam_init, w_out, ymix):
    cos_t, sin_t = _rope_tables()
    q_tiles = DEC_SEQ // ATT_TQ
    q0 = N_CTX // ATT_TQ
    b0 = N_CTX // DEC_SEQ
    wo_rows = D_MIX // (DEC_BATCH * q_tiles)
    qblk = lambda c: pl.BlockSpec((ATT_TQ, BRANCH), lambda b, i: (q0 + b * q_tiles + i, c))
    kblk = lambda c: pl.BlockSpec((DEC_SEQ, BRANCH), lambda b, i: (b0 + b, c))
    cblk = pl.BlockSpec((None, None, PAST_LEN, BRANCH), lambda b, i: (b, layer, 0, 0))
    cvblk = pl.BlockSpec((None, None, PAST_LEN, N_HEADS_C, HEAD_DIM_C),
                         lambda b, i: (b, layer, 0, 0, 0))
    return pl.pallas_call(
        functools.partial(_attn_lat_kernel, lam_init),
        grid=(DEC_BATCH, q_tiles),
        in_specs=[
            qblk(C_Q), kblk(C_K), kblk(C_V), qblk(C_CG), cblk, cvblk,
            pl.BlockSpec((ATT_TQ, HEAD_DIM_C), lambda b, i: (i, 0)),
            pl.BlockSpec((ATT_TQ, HEAD_DIM_C), lambda b, i: (i, 0)),
            pl.BlockSpec((DEC_SEQ, HEAD_DIM_C), lambda b, i: (0, 0)),
            pl.BlockSpec((DEC_SEQ, HEAD_DIM_C), lambda b, i: (0, 0)),
            pl.BlockSpec((None, 4, QK_HALF), lambda b, i: (layer, 0, 0)),
            pl.BlockSpec((None, 1, HEAD_DIM_C), lambda b, i: (layer, 0, 0)),
            pl.BlockSpec((None, wo_rows, D_MODEL), lambda b, i: (layer, b * q_tiles + i, 0)),
            pl.BlockSpec(memory_space=pl.ANY),
        ],
        out_specs=[pl.BlockSpec((ATT_TQ, BRANCH), lambda b, i: (q0 + b * q_tiles + i, M_ATTN)),
                   pl.BlockSpec((wo_rows, D_MODEL), lambda b, i: (b * q_tiles + i, 0))],
        out_shape=[jax.ShapeDtypeStruct((N_TOK, D_MIX), BF16),
                   jax.ShapeDtypeStruct((D_MIX, D_MODEL), BF16)],
        input_output_aliases={13: 0},
        scratch_shapes=[pltpu.VMEM((PAST_LEN, BRANCH), BF16), pltpu.VMEM((DEC_SEQ, BRANCH), BF16),
                        pltpu.VMEM((N_HEADS_C, V_ROWS, PAST_LEN + DEC_SEQ), BF16)],
        compiler_params=_params("arbitrary", "arbitrary"),
        name="mixer_attn_lat",
    )(p, p, p, p, cache_k4, cache_v, cos_t, sin_t, cos_t, sin_t, lambda_qk, subln_row, w_out, ymix)


HY_ROWS = 1024
HY_PIECE = 512
HY_STAGGER = 1


def _dft_matrices(seq_len):
    n = 2 * seq_len
    f = np.arange(seq_len, dtype=np.float64)[:, None]
    s = np.arange(seq_len, dtype=np.float64)[None, :]
    theta = 2.0 * np.pi * f * s / n
    alt = np.where(np.arange(seq_len) % 2 == 0, 1.0, -1.0)
    ac = np.cos(theta)
    as_ = -np.sin(theta)
    as_[0, :] = alt
    bc = 2.0 * np.cos(theta.T) / n
    bc[:, 0] = 1.0 / n
    bs = -2.0 * np.sin(theta.T) / n
    bs[:, 0] = alt / n
    fwd = np.concatenate([ac, as_], axis=0)
    inv = np.concatenate([bc, bs], axis=1)
    return jnp.asarray(fwd, F32).astype(BF16), jnp.asarray(inv, F32).astype(BF16)


def _filter_features(seq_len):
    t_idx = np.arange(seq_len, dtype=np.float64)
    t_norm = np.linspace(0.0, 1.0, seq_len)
    bands = np.linspace(1e-4, HY_BANDS - 1, HY_BANDS)
    ang = (2.0 * math.pi * t_idx / seq_len)[:, None] * bands[None, :]
    feats = np.concatenate([t_norm[:, None], np.cos(ang), np.sin(ang)], axis=-1)
    feats = np.pad(feats, ((0, 0), (0, HY_PAD - HY_EMB)))
    deltas = np.abs(np.linspace(math.log(HY_TARGET) / HY_FAST_DECAY,
                                math.log(HY_TARGET) / HY_SLOW_DECAY, BRANCH))
    return (jnp.asarray(feats, F32), jnp.asarray(t_norm[:, None], F32),
            jnp.asarray(deltas[None, :], F32))


def _filter_kernel(seq_len, feats_ref, tn_ref, dl_ref, w1_ref, b1_ref, w2_ref, b2_ref, fr_ref,
                   w3f_ref, w3b_ref, fwd_ref, kr_ref, ki_ref, h_ref):
    sp = lambda x: _split(x, 2)

    @pl.when(pl.program_id(1) == 0)
    def _():
        fr = fr_ref[...]
        h = jnp.sin(fr * (_sdot(sp(feats_ref[...]), sp(w1_ref[...])) + b1_ref[...]))
        h_ref[...] = jnp.sin(fr * (_sdot(sp(h), sp(w2_ref[...])) + b2_ref[...]))

    h = h_ref[...]
    decay = jnp.exp(-tn_ref[...] * dl_ref[...])
    row = lax.broadcasted_iota(jnp.int32, (seq_len, BRANCH), 0)
    fwd = _bdot(h, w3f_ref[...]) * decay
    bwd = jnp.where(row == 0, 0.0, _bdot(h, w3b_ref[...]) * decay)
    norm = (jnp.sum(jnp.abs(fwd), axis=0, keepdims=True)
            + jnp.sum(jnp.abs(bwd), axis=0, keepdims=True))
    fwd = fwd / norm
    bwd = bwd / norm
    even = fwd + bwd
    alt = jnp.where((row & 1) == 0, 1.0, -1.0)
    nyquist = jnp.sum(alt * even, axis=0, keepdims=True)
    kr_ref[...] = jnp.dot(fwd_ref[0:seq_len, :], even.astype(BF16), preferred_element_type=F32)
    ki = jnp.dot(fwd_ref[seq_len:2 * seq_len, :], (fwd - bwd).astype(BF16),
                 preferred_element_type=F32)
    ki_ref[...] = jnp.where(row == 0, nyquist, ki)


def _hyena_spectrum(seq_len, filt, fwd_mat):
    feats, t_norm, deltas = _filter_features(seq_len)
    w1, b1, w2, b2, freq, w3 = filt
    full = lambda a: pl.BlockSpec(a.shape, lambda l, o: (0,) * a.ndim)
    lyr = lambda r, n: pl.BlockSpec((None, r, n), lambda l, o: (l, 0, 0))
    out = pl.BlockSpec((None, None, seq_len, BRANCH), lambda l, o: (l, o, 0, 0))
    return pl.pallas_call(
        functools.partial(_filter_kernel, seq_len),
        grid=(DEPTH, HY_ORDER),
        in_specs=[
            full(feats), full(t_norm), full(deltas),
            lyr(HY_PAD, HY_PAD), lyr(1, HY_PAD), lyr(HY_PAD, HY_PAD), lyr(1, HY_PAD), lyr(1, HY_PAD),
            pl.BlockSpec((None, HY_PAD, BRANCH), lambda l, o: (l, 0, 2 * o)),
            pl.BlockSpec((None, HY_PAD, BRANCH), lambda l, o: (l, 0, 2 * o + 1)),
            full(fwd_mat),
        ],
        out_specs=[out, out],
        out_shape=[jax.ShapeDtypeStruct((DEPTH, HY_ORDER, seq_len, BRANCH), F32)] * 2,
        scratch_shapes=[pltpu.VMEM((seq_len, HY_PAD), F32)],
        compiler_params=_params("arbitrary", "arbitrary"),
        name=f"hyena_spectrum_{seq_len}",
    )(feats, t_norm, deltas, w1, b1, w2, b2, freq, w3, w3, fwd_mat)


def _hyena_chain(seq_len, path, rows, x_refs, cw_refs, cb_refs, kr_ref, ki_ref, hb_ref, fwd_ref,
                 inv_ref, o_ref):
    x1_ref, x2_ref, hv_ref, g_ref = x_refs
    row = lax.broadcasted_iota(jnp.int32, (seq_len, HY_CT), 0)
    first, last = row == 0, row == seq_len - 1

    def short_conv(x_ref, piece):
        x, w = x_ref[path, rows, :], cw_refs[piece][...]
        prev = jnp.where(first, 0.0, pltpu.roll(x, 1, axis=0))
        nxt = jnp.where(last, 0.0, pltpu.roll(x, seq_len - 1, axis=0))
        return prev * w[0:1] + x * w[1:2] + nxt * w[2:3] + cb_refs[piece][...]

    z = short_conv(hv_ref, 2)
    yield
    gate_refs = (x1_ref, x2_ref)
    piece = min(seq_len, HY_PIECE)
    for order in range(HY_ORDER):
        zb = z.astype(BF16)
        y_re, y_im = [], []
        for r in range(0, seq_len, piece):
            zr = jnp.dot(fwd_ref[r:r + piece, :], zb, preferred_element_type=F32)
            yield
            zi = jnp.dot(fwd_ref[seq_len + r:seq_len + r + piece, :], zb,
                         preferred_element_type=F32)
            yield
            kr, kp = kr_ref[order, r:r + piece, :], ki_ref[order, r:r + piece, :]
            ki, kn = kp, kr
            if r == 0:
                dc = lax.broadcasted_iota(jnp.int32, (piece, HY_CT), 0) == 0
                ki = jnp.where(dc, 0.0, kp)
                kn = jnp.where(dc, kp, kr)
            y_re.append((zr * kr - zi * ki).astype(BF16))
            y_im.append((zr * ki + zi * kn).astype(BF16))
            yield
        yf = jnp.concatenate(y_re + y_im, axis=0)
        ys = []
        for r in range(0, seq_len, piece):
            ys.append(jnp.dot(inv_ref[r:r + piece, :], yf, preferred_element_type=F32))
            yield
            if r == 0:
                gate = short_conv(gate_refs[order], order)
        y = ys[0] if len(ys) == 1 else jnp.concatenate(ys, axis=0)
        z = gate * (y + z * hb_ref[order:order + 1, :])
    o_ref[path, rows, :] = (_silu(g_ref[path, rows, :]) * z).astype(BF16)


def _hyena_kernel(x1_ref, x2_ref, hv_ref, g_ref, cw1_ref, cw2_ref, cw3_ref, cb1_ref, cb2_ref,
                  cb3_ref, hb_ref, krc_ref, kic_ref, fwdc_ref, invc_ref, krl_ref, kil_ref,
                  fwdl_ref, invl_ref, ymix_ref, o_ref):
    del ymix_ref
    x_refs = (x1_ref, x2_ref, hv_ref, g_ref)
    conv = ((cw1_ref, cw2_ref, cw3_ref), (cb1_ref, cb2_ref, cb3_ref))
    pending = [(0, _hyena_chain(DEC_SEQ, 1, slice(0, DEC_SEQ), x_refs, *conv, krl_ref, kil_ref,
                                hb_ref, fwdl_ref, invl_ref, o_ref))]
    for s in range(HY_ROWS // SEQ):
        pending.append((HY_STAGGER * s,
                        _hyena_chain(SEQ, 0, slice(s * SEQ, (s + 1) * SEQ), x_refs, *conv, krc_ref,
                                     kic_ref, hb_ref, fwdc_ref, invc_ref, o_ref)))
    running, tick = [], 0
    while pending or running:
        running += [gen for start, gen in pending if start == tick]
        pending = [(start, gen) for start, gen in pending if start > tick]
        for gen in list(running):
            if next(gen, "done") == "done":
                running.remove(gen)
        tick += 1


def _mixer_hyena(p, conv_w, conv_b3, hyena_bias, spec_ctx, mats_ctx, spec_lat, mats_lat, layer,
                 ymix):
    assert N_CTX == N_LAT and HY_ROWS == DEC_SEQ
    n_ct = BRANCH // HY_CT
    blk = lambda pc: pl.BlockSpec((2, HY_ROWS, HY_CT), lambda c, i: (0, i, pc * n_ct + c))
    cw = lambda pc: pl.BlockSpec((None, 3, HY_CT), lambda c, i: (layer, 0, pc * n_ct + c))
    cb = lambda pc: pl.BlockSpec((None, 1, HY_CT), lambda c, i: (layer, 0, pc * n_ct + c))
    spec = lambda a: pl.BlockSpec((None, HY_ORDER, a.shape[2], HY_CT), lambda c, i: (layer, 0, 0, c))
    full = lambda a: pl.BlockSpec(a.shape, lambda c, i: (0,) * a.ndim, pipeline_mode=pl.Buffered(1))
    p3 = p.reshape(2, N_CTX, D_IN)
    out = pl.pallas_call(
        _hyena_kernel,
        grid=(n_ct, N_CTX // HY_ROWS),
        in_specs=[blk(C_DX1), blk(C_DX2), blk(C_DV), blk(C_DG),
                  cw(0), cw(1), cw(2), cb(0), cb(1), cb(2),
                  pl.BlockSpec((None, HY_ORDER, HY_CT), lambda c, i: (layer, 0, c)),
                  spec(spec_ctx[0]), spec(spec_ctx[1]), full(mats_ctx[0]), full(mats_ctx[1]),
                  spec(spec_lat[0]), spec(spec_lat[1]), full(mats_lat[0]), full(mats_lat[1]),
                  pl.BlockSpec(memory_space=pl.ANY)],
        out_specs=pl.BlockSpec((2, HY_ROWS, HY_CT), lambda c, i: (0, i, M_HYENA * n_ct + c)),
        out_shape=jax.ShapeDtypeStruct((2, N_CTX, D_MIX), BF16),
        input_output_aliases={19: 0},
        compiler_params=_params("arbitrary", "arbitrary"),
        name="mixer_hyena",
    )(p3, p3, p3, p3, conv_w, conv_w, conv_w, conv_b3, conv_b3, conv_b3, hyena_bias,
      *spec_ctx, *mats_ctx, *spec_lat, *mats_lat, ymix.reshape(2, N_CTX, D_MIX))
    return out.reshape(N_TOK, D_MIX)


OUT_TM = 512


def _outproj_kernel(alpha, n_x, n_out, ymix_ref, *refs):
    x_refs = refs[:n_x]
    gate_ref, w_ref, b_ref, lng_ref, lnb_ref = refs[n_x:n_x + 5]
    o_refs = refs[n_x + 5:]
    is_ctx = pl.program_id(0) < N_CTX // OUT_TM

    def tile(x_ref, o_ref):
        y = jnp.dot(ymix_ref[...], w_ref[...], preferred_element_type=F32) + b_ref[...]
        r = alpha * x_ref[...] + gate_ref[...] * y
        o_ref[...] = _layer_norm(r) * lng_ref[...] + lnb_ref[...]

    if n_x == 1 and n_out == 1:
        tile(x_refs[0], o_refs[0])
    else:
        pl.when(is_ctx)(lambda: tile(x_refs[0], o_refs[0]))
        pl.when(jnp.logical_not(is_ctx))(lambda: tile(x_refs[-1], o_refs[-1]))


def _out_projection(ymix, xs, mod4, w_out_bf, b_out3, ln_g3, ln_b3, layer, split_out):
    alpha = (2.0 * DEPTH) ** 0.25
    row = lambda i: _cond_row(i, OUT_TM)
    vec = pl.BlockSpec((None, 1, D_MODEL), lambda i: (layer, 0, 0))
    if split_out:
        out_shape = [jax.ShapeDtypeStruct((N_CTX, D_MODEL), F32),
                     jax.ShapeDtypeStruct((N_LAT, D_MODEL), F32)]
    else:
        out_shape = [jax.ShapeDtypeStruct((N_TOK, D_MODEL), F32)]
    return pl.pallas_call(
        functools.partial(_outproj_kernel, alpha, len(xs), len(out_shape)),
        grid=(N_TOK // OUT_TM,),
        in_specs=[pl.BlockSpec((OUT_TM, D_MIX), lambda i: (i, 0))] + _token_specs(xs, OUT_TM) + [
            pl.BlockSpec((None, None, 1, D_MODEL), lambda i: (layer, row(i), 0, 2)),
            pl.BlockSpec((D_MIX, D_MODEL), lambda i: (0, 0), pipeline_mode=pl.Buffered(1)),
            vec, vec, vec,
        ],
        out_specs=_token_specs(out_shape, OUT_TM),
        out_shape=out_shape,
        compiler_params=_params("arbitrary"),
        name="out_projection",
    )(ymix, *xs, mod4, w_out_bf, b_out3, ln_g3, ln_b3)


def kernel(x_prompt, x_sample, cache_k, cache_v, c, c_ctx, w_mod, b_mod, w_in, gmlp_w, gmlp_b,
           pool_w, pool_scale, lambda_qk, subln_w, conv_w, conv_b, filt_w1, filt_b1, filt_w2,
           filt_b2, filt_freq, filt_w3, hyena_bias, w_out, b_out, ln_g, ln_b):
    xs = (x_prompt.reshape(N_CTX, D_MODEL), x_sample.reshape(N_LAT, D_MODEL))
    cond = jnp.concatenate(
        [c_ctx[None, :], c, jnp.zeros((N_COND - 1 - DEC_BATCH, D_MODEL), F32)], axis=0)
    mod4 = _modulation(cond, w_mod, b_mod).reshape(DEPTH, N_COND, 1, 3 * D_MODEL)

    cache_k4 = cache_k.reshape(DEC_BATCH, DEPTH, PAST_LEN, BRANCH)
    gmlp_b_rows = jnp.broadcast_to(gmlp_b[..., None], (DEPTH, N_GROUPS, CHUNK, GROUP_W))
    subln_row = subln_w.reshape(DEPTH, 1, HEAD_DIM_C)
    conv_b3 = conv_b.reshape(DEPTH, 1, 3 * BRANCH)
    b_out3 = b_out.reshape(DEPTH, 1, D_MODEL)
    ln_g3 = ln_g.reshape(DEPTH, 1, D_MODEL)
    ln_b3 = ln_b.reshape(DEPTH, 1, D_MODEL)

    pad_h = HY_PAD - HY_HIDDEN
    filt = (
        jnp.pad(filt_w1, ((0, 0), (0, HY_PAD - HY_EMB), (0, pad_h))),
        jnp.pad(filt_b1, ((0, 0), (0, pad_h))).reshape(DEPTH, 1, HY_PAD),
        jnp.pad(filt_w2, ((0, 0), (0, pad_h), (0, pad_h))),
        jnp.pad(filt_b2, ((0, 0), (0, pad_h))).reshape(DEPTH, 1, HY_PAD),
        jnp.pad(filt_freq, ((0, 0), (0, pad_h))).reshape(DEPTH, 1, HY_PAD),
        jnp.pad(filt_w3, ((0, 0), (0, pad_h), (0, 0))),
    )
    mats_ctx = _dft_matrices(SEQ)
    mats_lat = _dft_matrices(DEC_SEQ)
    spec_ctx = _hyena_spectrum(SEQ, filt, mats_ctx[0])
    spec_lat = _hyena_spectrum(DEC_SEQ, filt, mats_lat[0])

    caches = None
    for layer in range(DEPTH):
        lam_init = 0.8 - 0.6 * math.exp(-0.3 * layer)
        p = _in_projection(xs, mod4, w_in, layer)
        ymix = _mixer_gmlp_pool(p, gmlp_w[layer], gmlp_b_rows[layer], pool_w[layer],
                                pool_scale[layer].reshape(1, BRANCH))
        ymix, new_k, new_v = _mixer_attn_ctx(p, lambda_qk, subln_row, layer, lam_init, ymix, caches)
        caches = (new_k, new_v)
        ymix, w_out_bf = _mixer_attn_lat(p, cache_k4, cache_v, lambda_qk, subln_row, layer,
                                         lam_init, w_out, ymix)
        ymix = _mixer_hyena(p, conv_w, conv_b3, hyena_bias, spec_ctx, mats_ctx, spec_lat, mats_lat,
                            layer, ymix)
        xs = tuple(_out_projection(ymix, xs, mod4, w_out_bf, b_out3, ln_g3, ln_b3, layer,
                                   split_out=layer == DEPTH - 1))

    y_prompt = xs[0].reshape(BATCH, SEQ, D_MODEL)
    y_sample = xs[1].reshape(DEC_BATCH, DEC_SEQ, D_MODEL)
    new_k, new_v = caches
    return (y_prompt, y_sample,
            new_k.reshape(BATCH, DEPTH, SEQ, N_HEADS_C, 2, QK_HALF), new_v)
```

```python
import functools
import math

import numpy as np
import jax
import jax.numpy as jnp
from jax import lax
from jax.experimental import pallas as pl
from jax.experimental.pallas import tpu as pltpu

F32 = jnp.float32
BF16 = jnp.bfloat16

D_MODEL = 2048
BATCH = 16
SEQ = 256
DEPTH = 2
DEC_BATCH = 4
DEC_SEQ = 1024
PAST_LEN = 512
GRID_W = 64
BRANCH = 512
N_GROUPS = 4
GROUP_W = 128
CHUNK = 128
POOL_WINDOWS = (2, 4, 8, 16)
N_HEADS_C = 4
HEAD_DIM_C = 128
QK_HALF = 64
ROPE_AXIS_DIM = 32
ROPE_BASE = 10000.0
HY_BANDS = 16
HY_EMB = 33
HY_HIDDEN = 64
HY_ORDER = 2
HY_FAST_DECAY = 0.3
HY_SLOW_DECAY = 1.5
HY_TARGET = 1e-2
N_IN_PIECES = 13
D_IN = N_IN_PIECES * BRANCH
LN_EPS = 1e-6

N_CTX = BATCH * SEQ
N_LAT = DEC_BATCH * DEC_SEQ
N_TOK = N_CTX + N_LAT
N_COND = 8
LANES = 128
HY_PAD = LANES
HY_CT = 256
VMEM_LIMIT = 56 * 1024 * 1024

(C_AU, C_AV, C_AG, C_BX, C_BG, C_Q, C_K, C_V, C_CG, C_DX1, C_DX2, C_DV, C_DG) = range(13)
D_MIX = 4 * BRANCH
(M_GMLP, M_POOL, M_ATTN, M_HYENA) = range(4)


def _silu(x):
    return x * jax.nn.sigmoid(x)


def _bdot(a, b):
    return jnp.dot(a.astype(BF16), b.astype(BF16), preferred_element_type=F32)


def _split(x, n_terms):
    hi = x.astype(BF16)
    if n_terms == 1:
        return (hi,)
    return (hi, (x - hi.astype(F32)).astype(BF16))


def _sdot(a_terms, b_terms):
    acc = jnp.dot(a_terms[0], b_terms[0], preferred_element_type=F32)
    if len(a_terms) > 1:
        acc = acc + jnp.dot(a_terms[1], b_terms[0], preferred_element_type=F32)
    if len(b_terms) > 1:
        acc = acc + jnp.dot(a_terms[0], b_terms[1], preferred_element_type=F32)
    return acc


def _layer_norm(x):
    mu = jnp.mean(x, axis=-1, keepdims=True)
    xc = x - mu
    var = jnp.mean(xc * xc, axis=-1, keepdims=True)
    return xc * lax.rsqrt(var + LN_EPS)


def _cond_row(tile, rows_per_tile):
    n_ctx_tiles = N_CTX // rows_per_tile
    tiles_per_batch = DEC_SEQ // rows_per_tile
    return jnp.where(tile < n_ctx_tiles, 0, 1 + (tile - n_ctx_tiles) // tiles_per_batch)


def _params(*semantics):
    return pltpu.CompilerParams(dimension_semantics=semantics, vmem_limit_bytes=VMEM_LIMIT)


MOD_TN = 1024


def _mod_kernel(c_ref, w_ref, b_ref, o_ref):
    o_ref[...] = _bdot(_silu(c_ref[...]), w_ref[...]) + b_ref[...]


def _modulation(cond, w_mod, b_mod):
    n = 3 * D_MODEL
    return pl.pallas_call(
        _mod_kernel,
        grid=(DEPTH, n // MOD_TN),
        in_specs=[
            pl.BlockSpec((N_COND, D_MODEL), lambda l, j: (0, 0)),
            pl.BlockSpec((None, D_MODEL, MOD_TN), lambda l, j: (l, 0, j)),
            pl.BlockSpec((None, 1, MOD_TN), lambda l, j: (l, 0, j)),
        ],
        out_specs=pl.BlockSpec((None, N_COND, MOD_TN), lambda l, j: (l, 0, j)),
        out_shape=jax.ShapeDtypeStruct((DEPTH, N_COND, n), F32),
        compiler_params=_params("arbitrary", "arbitrary"),
        name="modulation",
    )(cond, w_mod, b_mod.reshape(DEPTH, 1, n))


IN_TM = 1024
IN_FIRST_TM = 2048
IN_TN = 512
IN_LN_ROWS = 256
IN_CAST_K = 512


def _token_specs(xs, tm, tile_of=lambda i: i):
    n_ctx_tiles = N_CTX // tm
    if len(xs) == 1:
        maps = [lambda i, *_: (tile_of(i), 0)]
    else:
        maps = [lambda i, *_: (jnp.minimum(tile_of(i), n_ctx_tiles - 1), 0),
                lambda i, *_: (jnp.maximum(tile_of(i) - n_ctx_tiles, 0), 0)]
    return [pl.BlockSpec((tm, D_MODEL), m) for m in maps]


def _inproj_kernel(n_x, tm, tile0, cast_w, *refs):
    x_refs = refs[:n_x]
    scale_ref, shift_ref, w_ref = refs[n_x:n_x + 3]
    if cast_w:
        o_ref, wb_ref, h_ref = refs[n_x + 3:]
    else:
        _, o_ref, h_ref = refs[n_x + 3:]

    def weights():
        if not cast_w:
            return w_ref[...]
        w = w_ref[...].astype(BF16)
        wb_ref[...] = w
        return w

    def first_column_step(x_ref):
        w = weights()

        def norm(r):
            rows = slice(r, r + IN_LN_ROWS)
            h = _layer_norm(x_ref[rows, :]) * (1.0 + scale_ref[...]) + shift_ref[...]
            h = h.astype(BF16)
            h_ref[rows, :] = h
            return h

        h = norm(0)
        for r in range(0, tm, IN_LN_ROWS):
            o_ref[r:r + IN_LN_ROWS, :] = jnp.dot(h, w, preferred_element_type=F32)
            if r + IN_LN_ROWS < tm:
                h = norm(r + IN_LN_ROWS)

    first = pl.program_id(1) == 0
    if n_x == 1:
        pl.when(first)(lambda: first_column_step(x_refs[0]))
    else:
        is_ctx = pl.program_id(0) + tile0 < N_CTX // IN_TM
        pl.when(jnp.logical_and(first, is_ctx))(lambda: first_column_step(x_refs[0]))
        pl.when(jnp.logical_and(first, jnp.logical_not(is_ctx)))(
            lambda: first_column_step(x_refs[1]))

    @pl.when(jnp.logical_not(first))
    def _():
        acc = None
        for k in range(0, D_MODEL, IN_CAST_K):
            w = w_ref[k:k + IN_CAST_K, :]
            if cast_w:
                w = w.astype(BF16)
                wb_ref[k:k + IN_CAST_K, :] = w
            part = jnp.dot(h_ref[:, k:k + IN_CAST_K], w, preferred_element_type=F32)
            acc = part if acc is None else acc + part
        o_ref[...] = acc


def _in_projection(xs, mod4, w_in, layer):
    assert IN_FIRST_TM <= N_CTX and IN_FIRST_TM % IN_TM == 0
    n_col = D_IN // IN_TN
    first_tiles = IN_FIRST_TM // IN_TM

    def mod_spec(tile0, piece):
        return pl.BlockSpec((None, None, 1, D_MODEL),
                            lambda i, j: (layer, _cond_row(i + tile0, IN_TM), 0, piece))

    p_shape = jax.ShapeDtypeStruct((N_TOK, D_IN), F32)
    p, w_bf = pl.pallas_call(
        functools.partial(_inproj_kernel, 1, IN_FIRST_TM, 0, True),
        grid=(1, n_col),
        in_specs=[
            pl.BlockSpec((IN_FIRST_TM, D_MODEL), lambda i, j: (0, 0), pipeline_mode=pl.Buffered(1)),
            mod_spec(0, 1), mod_spec(0, 0),
            pl.BlockSpec((None, D_MODEL, IN_TN), lambda i, j: (layer, 0, j)),
        ],
        out_specs=[pl.BlockSpec((IN_FIRST_TM, IN_TN), lambda i, j: (0, j)),
                   pl.BlockSpec((D_MODEL, IN_TN), lambda i, j: (0, j))],
        out_shape=[p_shape, jax.ShapeDtypeStruct((D_MODEL, D_IN), BF16)],
        scratch_shapes=[pltpu.VMEM((IN_FIRST_TM, D_MODEL), BF16)],
        compiler_params=_params("arbitrary", "arbitrary"),
        name="in_projection_first",
    )(xs[0], mod4, mod4, w_in)
    return pl.pallas_call(
        functools.partial(_inproj_kernel, len(xs), IN_TM, first_tiles, False),
        grid=(N_TOK // IN_TM - first_tiles, n_col),
        in_specs=_token_specs(xs, IN_TM, lambda i: i + first_tiles) + [
            mod_spec(first_tiles, 1), mod_spec(first_tiles, 0),
            pl.BlockSpec((D_MODEL, IN_TN), lambda i, j: (0, j)),
            pl.BlockSpec(memory_space=pl.ANY),
        ],
        out_specs=pl.BlockSpec((IN_TM, IN_TN), lambda i, j: (i + first_tiles, j)),
        out_shape=p_shape,
        input_output_aliases={len(xs) + 3: 0},
        scratch_shapes=[pltpu.VMEM((IN_TM, D_MODEL), BF16)],
        compiler_params=_params("arbitrary", "arbitrary"),
        name="in_projection",
    )(*xs, mod4, mod4, w_bf, p)


AB_TM = 1024


def _gmlp_tasks(u_ref, v_ref, g_ref, w_ref, b_ref, o_ref):
    ws = [w_ref[g].astype(BF16) for g in range(N_GROUPS)]

    def chunk(r):
        rows = slice(r, r + CHUNK)
        vn = _layer_norm(v_ref[rows, :]).astype(BF16)
        for g in range(N_GROUPS):
            cols = slice(g * GROUP_W, (g + 1) * GROUP_W)
            mixed = jnp.dot(ws[g], vn[:, cols], preferred_element_type=F32) + b_ref[g]
            o_ref[rows, cols] = (_silu(g_ref[rows, cols]) * u_ref[rows, cols] * mixed).astype(BF16)

    return [functools.partial(chunk, r) for r in range(0, AB_TM, CHUNK)]


POOL_BAND_GROUPS = (2, 3)
POOL_BAND_ROWS = 256
POOL_BAND_HALO = 128


def _window_sum_shifts(p, win, pos, seq_len):
    acc = p
    for d in range(-(win // 2), win // 2):
        if d == 0:
            continue
        shifted = pltpu.roll(p, (-d) % AB_TM, axis=0)
        valid = (pos >= -d) if d < 0 else (pos < seq_len - d)
        acc = acc + jnp.where(valid, shifted, 0.0)
    return acc


def _window_sum_band_tasks(p, band_ref, pieces):
    hi = p.astype(BF16)
    r1 = p - hi.astype(F32)
    mid = r1.astype(BF16)
    lo = (r1 - mid.astype(F32)).astype(BF16)
    terms = jnp.concatenate([hi, mid, lo], axis=1)

    def piece(r):
        k0, k1 = max(0, r - POOL_BAND_HALO), min(AB_TM, r + POOL_BAND_ROWS + POOL_BAND_HALO)
        s = jnp.dot(band_ref[r:r + POOL_BAND_ROWS, k0:k1], terms[k0:k1, :],
                    preferred_element_type=F32)
        pieces.append(s[:, :GROUP_W] + s[:, GROUP_W:2 * GROUP_W] + s[:, 2 * GROUP_W:])

    return [functools.partial(piece, r) for r in range(0, AB_TM, POOL_BAND_ROWS)]


def _pool_finish(g, acc, x_ref, g_ref, w_ref, s_ref, pos, seq_len, o_ref):
    cols = slice(g * GROUP_W, (g + 1) * GROUP_W)
    win = POOL_WINDOWS[g]
    count = jnp.minimum(pos + win // 2, seq_len) - jnp.maximum(pos - win // 2, 0)
    pooled = acc / count.astype(F32)
    y = _bdot(pooled - x_ref[:, cols], w_ref[g])
    o_ref[:, BRANCH + g * GROUP_W:BRANCH + (g + 1) * GROUP_W] = (
        _silu(g_ref[:, cols]) * (y * s_ref[:, cols])).astype(BF16)


AB_PIECES = (C_AU, C_AV, C_AG, C_BX, C_BG)
AB_SLOTS = 3


def _mixer_ab_kernel(p_hbm, gw_ref, gb_ref, pw_ref, ps_ref, band_a_ref, band_b_ref, o_ref,
                     buf_ref, sem_ref):
    s = pl.program_id(0)
    n_steps = pl.num_programs(0)

    def copies(step):
        slot = step % AB_SLOTS
        return [pltpu.make_async_copy(
            p_hbm.at[pl.ds(step * AB_TM, AB_TM), pl.ds(c * BRANCH, BRANCH)],
            buf_ref.at[slot, k], sem_ref.at[slot, k]) for k, c in enumerate(AB_PIECES)]

    @pl.when(s == 0)
    def _():
        for step in range(AB_SLOTS - 1):
            for cp in copies(step):
                cp.start()

    @pl.when(s + AB_SLOTS - 1 < n_steps)
    def _():
        for cp in copies(s + AB_SLOTS - 1):
            cp.start()

    for cp in copies(s):
        cp.wait()
    tile = [buf_ref.at[s % AB_SLOTS, k] for k in range(len(AB_PIECES))]
    _mixer_ab_body(*tile, gw_ref, gb_ref, pw_ref, ps_ref, band_a_ref, band_b_ref, o_ref)


def _mixer_ab_body(au_ref, av_ref, ag_ref, bx_ref, bg_ref, gw_ref, gb_ref, pw_ref, ps_ref,
                   band_a_ref, band_b_ref, o_ref):
    seq_len = jnp.where(pl.program_id(0) < N_CTX // AB_TM, SEQ, DEC_SEQ)
    pos = lax.broadcasted_iota(jnp.int32, (AB_TM, GROUP_W), 0) & (seq_len - 1)
    pool_refs = (bx_ref, bg_ref, pw_ref, ps_ref, pos, seq_len, o_ref)
    band_sums = {g: [] for g in POOL_BAND_GROUPS}
    mxu_tasks = [t for g, band in zip(POOL_BAND_GROUPS, (band_a_ref, band_b_ref))
                 for t in _window_sum_band_tasks(bx_ref[:, g * GROUP_W:(g + 1) * GROUP_W], band,
                                                 band_sums[g])]
    vpu_tasks = _gmlp_tasks(au_ref, av_ref, ag_ref, gw_ref, gb_ref, o_ref)
    for i in range(max(len(mxu_tasks), len(vpu_tasks))):
        for tasks in (mxu_tasks, vpu_tasks):
            if i < len(tasks):
                tasks[i]()
    for g, win in enumerate(POOL_WINDOWS):
        if g not in POOL_BAND_GROUPS:
            acc = _window_sum_shifts(bx_ref[:, g * GROUP_W:(g + 1) * GROUP_W], win, pos, seq_len)
            _pool_finish(g, acc, *pool_refs)
    for g in POOL_BAND_GROUPS:
        _pool_finish(g, jnp.concatenate(band_sums[g], axis=0), *pool_refs)


def _pool_bands():
    t = np.arange(AB_TM)
    bands = []
    for g in POOL_BAND_GROUPS:
        half = POOL_WINDOWS[g] // 2
        d = t[None, :] - t[:, None]
        near = (d >= -half) & (d < half)
        per_path = [near & ((t[None, :] // L) == (t[:, None] // L)) for L in (SEQ, DEC_SEQ)]
        bands.append(jnp.asarray(np.stack(per_path), F32).astype(BF16))
    return bands


def _mixer_gmlp_pool(p, gmlp_w, gmlp_b_rows, pool_w, pool_scale_row):
    assert M_POOL == M_GMLP + 1
    n_steps = N_TOK // AB_TM
    assert n_steps >= AB_SLOTS
    full = lambda a: pl.BlockSpec(a.shape, lambda i: (0,) * a.ndim)
    band = pl.BlockSpec((None, AB_TM, AB_TM), lambda i: (jnp.where(i < N_CTX // AB_TM, 0, 1), 0, 0))
    return pl.pallas_call(
        _mixer_ab_kernel,
        grid=(n_steps,),
        in_specs=[pl.BlockSpec(memory_space=pl.ANY),
                  full(gmlp_w), full(gmlp_b_rows), full(pool_w), full(pool_scale_row), band, band],
        out_specs=pl.BlockSpec((AB_TM, 2 * BRANCH), lambda i: (i, M_GMLP // 2)),
        out_shape=jax.ShapeDtypeStruct((N_TOK, D_MIX), BF16),
        scratch_shapes=[pltpu.VMEM((AB_SLOTS, len(AB_PIECES), AB_TM, BRANCH), F32),
                        pltpu.SemaphoreType.DMA((AB_SLOTS, len(AB_PIECES)))],
        compiler_params=_params("arbitrary"),
        name="mixer_gmlp_pool",
    )(p, gmlp_w, gmlp_b_rows, pool_w, pool_scale_row, *_pool_bands())


ATT_TQ = 1024
V_ROWS = HEAD_DIM_C + 16
NT_DIMS = (((1,), (1,)), ((), ()))


def _lambda(lam_ref, lam_init):
    lq = lam_ref[...]
    a = jnp.sum(lq[0:1] * lq[1:2], axis=-1, keepdims=True)
    b = jnp.sum(lq[2:3] * lq[3:4], axis=-1, keepdims=True)
    return jnp.exp(a) - jnp.exp(b) + lam_init


def _map_masks():
    lane = lax.broadcasted_iota(jnp.int32, (1, HEAD_DIM_C), 1)
    m0 = (lane < QK_HALF).astype(F32)
    return m0, 1.0 - m0


def _scores_t(q, keys):
    return [lax.dot_general(k, q, NT_DIMS, preferred_element_type=F32) for k in keys]


def _softmax_v_t(s, vals_t):
    m = functools.reduce(jnp.maximum, [jnp.max(x, axis=0, keepdims=True) for x in s])
    acc = functools.reduce(
        jnp.add, [jnp.dot(v, jnp.exp2(x - m).astype(BF16), preferred_element_type=F32)
                  for x, v in zip(s, vals_t)])
    return acc[:HEAD_DIM_C] * (1.0 / acc[HEAD_DIM_C:HEAD_DIM_C + 1])


def _with_ones_rows(v_t):
    ones = jnp.ones((V_ROWS - HEAD_DIM_C, v_t.shape[1]), BF16)
    return jnp.concatenate([v_t.astype(BF16), ones], axis=0)


def _diff_attention(q_of, keys_of, vals_t_of, gate_of, store, lam, lam_init, subln):
    masks = _map_masks()
    tasks = [(h, mp) for h in range(N_HEADS_C) for mp in range(2)]
    per_head = {}

    def operands(h):
        if h not in per_head:
            q = q_of(h) * (QK_HALF ** -0.5 * math.log2(math.e))
            per_head[h] = (q, keys_of(h), vals_t_of(h))
        return per_head[h]

    def scores(h, mp):
        q, keys, _ = operands(h)
        return _scores_t((q * masks[mp]).astype(BF16), keys)

    nxt = scores(*tasks[0])
    outs = []
    for i, (h, mp) in enumerate(tasks):
        cur = nxt
        if i + 1 < len(tasks):
            nxt = scores(*tasks[i + 1])
        outs.append(_softmax_v_t(cur, operands(h)[2]))
        if mp == 1:
            o = (outs[0] - lam * outs[1]).T
            outs = []
            o = o * lax.rsqrt(jnp.mean(o * o, axis=-1, keepdims=True) + 1e-5)
            o = o * subln * (1.0 - lam_init)
            store(h, _silu(gate_of(h)) * o)


def _head_cols(h):
    return slice(h * HEAD_DIM_C, (h + 1) * HEAD_DIM_C)


def _attn_ctx_kernel(lam_init, q_ref, k_ref, v_ref, g_ref, lam_ref, sw_ref, *rest):
    o_ref, ko_ref, vo_ref = rest[-3:]
    ko_ref[...] = k_ref[...]
    for h in range(N_HEADS_C):
        vo_ref[:, h, :] = v_ref[:, _head_cols(h)]

    def store(h, y):
        o_ref[:, _head_cols(h)] = y.astype(BF16)

    _diff_attention(
        lambda h: q_ref[:, _head_cols(h)],
        lambda h: [k_ref[:, _head_cols(h)].astype(BF16)],
        lambda h: [_with_ones_rows(v_ref[:, _head_cols(h)].T)],
        lambda h: g_ref[:, _head_cols(h)],
        store, _lambda(lam_ref, lam_init), lam_init, sw_ref[...])


def _mixer_attn_ctx(p, lambda_qk, subln_row, layer, lam_init, ymix, caches):
    blk = lambda c: pl.BlockSpec((SEQ, BRANCH), lambda b: (b, c))
    any_spec = pl.BlockSpec(memory_space=pl.ANY)
    k_spec = pl.BlockSpec((None, None, SEQ, BRANCH), lambda b: (b, layer, 0, 0))
    v_spec = pl.BlockSpec((None, None, SEQ, N_HEADS_C, HEAD_DIM_C), lambda b: (b, layer, 0, 0, 0))
    k_shape = jax.ShapeDtypeStruct((BATCH, DEPTH, SEQ, BRANCH), F32)
    v_shape = jax.ShapeDtypeStruct((BATCH, DEPTH, SEQ, N_HEADS_C, HEAD_DIM_C), F32)
    in_specs = [
        blk(C_Q), blk(C_K), blk(C_V), blk(C_CG),
        pl.BlockSpec((None, 4, QK_HALF), lambda b: (layer, 0, 0)),
        pl.BlockSpec((None, 1, HEAD_DIM_C), lambda b: (layer, 0, 0)),
        any_spec,
    ]
    args = [p, p, p, p, lambda_qk, subln_row, ymix]
    aliases = {6: 0}
    if caches is not None:
        in_specs += [any_spec, any_spec]
        args += list(caches)
        aliases.update({7: 1, 8: 2})
    return pl.pallas_call(
        functools.partial(_attn_ctx_kernel, lam_init),
        grid=(BATCH,),
        in_specs=in_specs,
        out_specs=[pl.BlockSpec((SEQ, BRANCH), lambda b: (b, M_ATTN)), k_spec, v_spec],
        out_shape=[jax.ShapeDtypeStruct((N_TOK, D_MIX), BF16), k_shape, v_shape],
        input_output_aliases=aliases,
        compiler_params=_params("arbitrary"),
        name="mixer_attn_ctx",
    )(*args)


def _rope(x, cos, sin_signed):
    lane = lax.broadcasted_iota(jnp.int32, x.shape, 1)
    first_half = (lane & (ROPE_AXIS_DIM - 1)) < (ROPE_AXIS_DIM // 2)
    half = ROPE_AXIS_DIM // 2
    partner = jnp.where(first_half,
                        pltpu.roll(x, x.shape[1] - half, axis=1),
                        pltpu.roll(x, half, axis=1))
    return x * cos + partner * sin_signed


def _attn_lat_kernel(lam_init, q_ref, k_ref, v_ref, g_ref, ck_ref, cv_ref, cosq_ref, sinq_ref,
                     cosk_ref, sink_ref, lam_ref, sw_ref, wo_ref, ymix_ref, o_ref, wo_bf_ref,
                     kc_ref, kr_ref, vt_ref):
    del ymix_ref
    wo_bf_ref[...] = wo_ref[...].astype(BF16)

    @pl.when(pl.program_id(1) == 0)
    def _():
        kc_ref[...] = ck_ref[...].astype(BF16)
        for h in range(N_HEADS_C):
            cols = slice(h * HEAD_DIM_C, (h + 1) * HEAD_DIM_C)
            kr_ref[:, cols] = _rope(k_ref[:, cols], cosk_ref[...], sink_ref[...]).astype(BF16)
            vt_ref[h, :, 0:PAST_LEN] = _with_ones_rows(cv_ref[:, h, :].T)
            vt_ref[h, :, PAST_LEN:PAST_LEN + DEC_SEQ] = _with_ones_rows(v_ref[:, cols].T)

    def store(h, y):
        o_ref[:, _head_cols(h)] = y.astype(BF16)

    _diff_attention(
        lambda h: _rope(q_ref[:, _head_cols(h)], cosq_ref[...], sinq_ref[...]),
        lambda h: [kc_ref[:, _head_cols(h)], kr_ref[:, _head_cols(h)]],
        lambda h: [vt_ref[h, :, 0:PAST_LEN], vt_ref[h, :, PAST_LEN:PAST_LEN + DEC_SEQ]],
        lambda h: g_ref[:, _head_cols(h)],
        store, _lambda(lam_ref, lam_init), lam_init, sw_ref[...])


def _rope_tables():
    pos = np.arange(DEC_SEQ)
    row = (pos // GRID_W).astype(np.float64)
    col = (pos % GRID_W).astype(np.float64)
    half = ROPE_AXIS_DIM // 2
    inv = ROPE_BASE ** (-np.arange(0, ROPE_AXIS_DIM, 2, dtype=np.float64) / ROPE_AXIS_DIM)
    lane = np.arange(HEAD_DIM_C)
    axis_is_col = (lane // ROPE_AXIS_DIM) % 2 == 1
    idx = lane % ROPE_AXIS_DIM
    ang = np.where(axis_is_col[None, :], col[:, None], row[:, None]) * inv[idx % half][None, :]
    sign = np.where(idx < half, -1.0, 1.0)[None, :]
    return (jnp.asarray(np.cos(ang), F32), jnp.asarray(np.sin(ang) * sign, F32))


def _mixer_attn_lat(p, cache_k4, cache_v, lambda_qk, subln_row, layer, lam_init, w_out, ymix):
    cos_t, sin_t = _rope_tables()
    q_tiles = DEC_SEQ // ATT_TQ
    q0 = N_CTX // ATT_TQ
    b0 = N_CTX // DEC_SEQ
    wo_rows = D_MIX // (DEC_BATCH * q_tiles)
    qblk = lambda c: pl.BlockSpec((ATT_TQ, BRANCH), lambda b, i: (q0 + b * q_tiles + i, c))
    kblk = lambda c: pl.BlockSpec((DEC_SEQ, BRANCH), lambda b, i: (b0 + b, c))
    cblk = pl.BlockSpec((None, None, PAST_LEN, BRANCH), lambda b, i: (b, layer, 0, 0))
    cvblk = pl.BlockSpec((None, None, PAST_LEN, N_HEADS_C, HEAD_DIM_C),
                         lambda b, i: (b, layer, 0, 0, 0))
    return pl.pallas_call(
        functools.partial(_attn_lat_kernel, lam_init),
        grid=(DEC_BATCH, q_tiles),
        in_specs=[
            qblk(C_Q), kblk(C_K), kblk(C_V), qblk(C_CG), cblk, cvblk,
            pl.BlockSpec((ATT_TQ, HEAD_DIM_C), lambda b, i: (i, 0)),
            pl.BlockSpec((ATT_TQ, HEAD_DIM_C), lambda b, i: (i, 0)),
            pl.BlockSpec((DEC_SEQ, HEAD_DIM_C), lambda b, i: (0, 0)),
            pl.BlockSpec((DEC_SEQ, HEAD_DIM_C), lambda b, i: (0, 0)),
            pl.BlockSpec((None, 4, QK_HALF), lambda b, i: (layer, 0, 0)),
            pl.BlockSpec((None, 1, HEAD_DIM_C), lambda b, i: (layer, 0, 0)),
            pl.BlockSpec((None, wo_rows, D_MODEL), lambda b, i: (layer, b * q_tiles + i, 0)),
            pl.BlockSpec(memory_space=pl.ANY),
        ],
        out_specs=[pl.BlockSpec((ATT_TQ, BRANCH), lambda b, i: (q0 + b * q_tiles + i, M_ATTN)),
                   pl.BlockSpec((wo_rows, D_MODEL), lambda b, i: (b * q_tiles + i, 0))],
        out_shape=[jax.ShapeDtypeStruct((N_TOK, D_MIX), BF16),
                   jax.ShapeDtypeStruct((D_MIX, D_MODEL), BF16)],
        input_output_aliases={13: 0},
        scratch_shapes=[pltpu.VMEM((PAST_LEN, BRANCH), BF16), pltpu.VMEM((DEC_SEQ, BRANCH), BF16),
                        pltpu.VMEM((N_HEADS_C, V_ROWS, PAST_LEN + DEC_SEQ), BF16)],
        compiler_params=_params("arbitrary", "arbitrary"),
        name="mixer_attn_lat",
    )(p, p, p, p, cache_k4, cache_v, cos_t, sin_t, cos_t, sin_t, lambda_qk, subln_row, w_out, ymix)


HY_ROWS = 1024
HY_PIECE = 512
HY_STAGGER = 1


def _dft_matrices(seq_len):
    n = 2 * seq_len
    f = np.arange(seq_len, dtype=np.float64)[:, None]
    s = np.arange(seq_len, dtype=np.float64)[None, :]
    theta = 2.0 * np.pi * f * s / n
    alt = np.where(np.arange(seq_len) % 2 == 0, 1.0, -1.0)
    ac = np.cos(theta)
    as_ = -np.sin(theta)
    as_[0, :] = alt
    bc = 2.0 * np.cos(theta.T) / n
    bc[:, 0] = 1.0 / n
    bs = -2.0 * np.sin(theta.T) / n
    bs[:, 0] = alt / n
    fwd = np.concatenate([ac, as_], axis=0)
    inv = np.concatenate([bc, bs], axis=1)
    return jnp.asarray(fwd, F32).astype(BF16), jnp.asarray(inv, F32).astype(BF16)


def _filter_features(seq_len):
    t_idx = np.arange(seq_len, dtype=np.float64)
    t_norm = np.linspace(0.0, 1.0, seq_len)
    bands = np.linspace(1e-4, HY_BANDS - 1, HY_BANDS)
    ang = (2.0 * math.pi * t_idx / seq_len)[:, None] * bands[None, :]
    feats = np.concatenate([t_norm[:, None], np.cos(ang), np.sin(ang)], axis=-1)
    feats = np.pad(feats, ((0, 0), (0, HY_PAD - HY_EMB)))
    deltas = np.abs(np.linspace(math.log(HY_TARGET) / HY_FAST_DECAY,
                                math.log(HY_TARGET) / HY_SLOW_DECAY, BRANCH))
    return (jnp.asarray(feats, F32), jnp.asarray(t_norm[:, None], F32),
            jnp.asarray(deltas[None, :], F32))


def _filter_kernel(seq_len, feats_ref, tn_ref, dl_ref, w1_ref, b1_ref, w2_ref, b2_ref, fr_ref,
                   w3f_ref, w3b_ref, fwd_ref, kr_ref, ki_ref, h_ref):
    sp = lambda x: _split(x, 2)

    @pl.when(pl.program_id(1) == 0)
    def _():
        fr = fr_ref[...]
        h = jnp.sin(fr * (_sdot(sp(feats_ref[...]), sp(w1_ref[...])) + b1_ref[...]))
        h_ref[...] = jnp.sin(fr * (_sdot(sp(h), sp(w2_ref[...])) + b2_ref[...]))

    h = h_ref[...]
    decay = jnp.exp(-tn_ref[...] * dl_ref[...])
    row = lax.broadcasted_iota(jnp.int32, (seq_len, BRANCH), 0)
    fwd = _bdot(h, w3f_ref[...]) * decay
    bwd = jnp.where(row == 0, 0.0, _bdot(h, w3b_ref[...]) * decay)
    norm = (jnp.sum(jnp.abs(fwd), axis=0, keepdims=True)
            + jnp.sum(jnp.abs(bwd), axis=0, keepdims=True))
    fwd = fwd / norm
    bwd = bwd / norm
    even = fwd + bwd
    alt = jnp.where((row & 1) == 0, 1.0, -1.0)
    nyquist = jnp.sum(alt * even, axis=0, keepdims=True)
    kr_ref[...] = jnp.dot(fwd_ref[0:seq_len, :], even.astype(BF16), preferred_element_type=F32)
    ki = jnp.dot(fwd_ref[seq_len:2 * seq_len, :], (fwd - bwd).astype(BF16),
                 preferred_element_type=F32)
    ki_ref[...] = jnp.where(row == 0, nyquist, ki)


def _hyena_spectrum(seq_len, filt, fwd_mat):
    feats, t_norm, deltas = _filter_features(seq_len)
    w1, b1, w2, b2, freq, w3 = filt
    full = lambda a: pl.BlockSpec(a.shape, lambda l, o: (0,) * a.ndim)
    lyr = lambda r, n: pl.BlockSpec((None, r, n), lambda l, o: (l, 0, 0))
    out = pl.BlockSpec((None, None, seq_len, BRANCH), lambda l, o: (l, o, 0, 0))
    return pl.pallas_call(
        functools.partial(_filter_kernel, seq_len),
        grid=(DEPTH, HY_ORDER),
        in_specs=[
            full(feats), full(t_norm), full(deltas),
            lyr(HY_PAD, HY_PAD), lyr(1, HY_PAD), lyr(HY_PAD, HY_PAD), lyr(1, HY_PAD), lyr(1, HY_PAD),
            pl.BlockSpec((None, HY_PAD, BRANCH), lambda l, o: (l, 0, 2 * o)),
            pl.BlockSpec((None, HY_PAD, BRANCH), lambda l, o: (l, 0, 2 * o + 1)),
            full(fwd_mat),
        ],
        out_specs=[out, out],
        out_shape=[jax.ShapeDtypeStruct((DEPTH, HY_ORDER, seq_len, BRANCH), F32)] * 2,
        scratch_shapes=[pltpu.VMEM((seq_len, HY_PAD), F32)],
        compiler_params=_params("arbitrary", "arbitrary"),
        name=f"hyena_spectrum_{seq_len}",
    )(feats, t_norm, deltas, w1, b1, w2, b2, freq, w3, w3, fwd_mat)


def _hyena_chain(seq_len, path, rows, x_refs, cw_refs, cb_refs, kr_ref, ki_ref, hb_ref, fwd_ref,
                 inv_ref, o_ref):
    x1_ref, x2_ref, hv_ref, g_ref = x_refs
    row = lax.broadcasted_iota(jnp.int32, (seq_len, HY_CT), 0)
    first, last = row == 0, row == seq_len - 1

    def short_conv(x_ref, piece):
        x, w = x_ref[path, rows, :], cw_refs[piece][...]
        prev = jnp.where(first, 0.0, pltpu.roll(x, 1, axis=0))
        nxt = jnp.where(last, 0.0, pltpu.roll(x, seq_len - 1, axis=0))
        return prev * w[0:1] + x * w[1:2] + nxt * w[2:3] + cb_refs[piece][...]

    z = short_conv(hv_ref, 2)
    yield
    gate_refs = (x1_ref, x2_ref)
    piece = min(seq_len, HY_PIECE)
    for order in range(HY_ORDER):
        zb = z.astype(BF16)
        y_re, y_im = [], []
        for r in range(0, seq_len, piece):
            zr = jnp.dot(fwd_ref[r:r + piece, :], zb, preferred_element_type=F32)
            yield
            zi = jnp.dot(fwd_ref[seq_len + r:seq_len + r + piece, :], zb,
                         preferred_element_type=F32)
            yield
            kr, kp = kr_ref[order, r:r + piece, :], ki_ref[order, r:r + piece, :]
            ki, kn = kp, kr
            if r == 0:
                dc = lax.broadcasted_iota(jnp.int32, (piece, HY_CT), 0) == 0
                ki = jnp.where(dc, 0.0, kp)
                kn = jnp.where(dc, kp, kr)
            y_re.append((zr * kr - zi * ki).astype(BF16))
            y_im.append((zr * ki + zi * kn).astype(BF16))
            yield
        yf = jnp.concatenate(y_re + y_im, axis=0)
        ys = []
        for r in range(0, seq_len, piece):
            ys.append(jnp.dot(inv_ref[r:r + piece, :], yf, preferred_element_type=F32))
            yield
            if r == 0:
                gate = short_conv(gate_refs[order], order)
        y = ys[0] if len(ys) == 1 else jnp.concatenate(ys, axis=0)
        z = gate * (y + z * hb_ref[order:order + 1, :])
    o_ref[path, rows, :] = (_silu(g_ref[path, rows, :]) * z).astype(BF16)


def _hyena_kernel(x1_ref, x2_ref, hv_ref, g_ref, cw1_ref, cw2_ref, cw3_ref, cb1_ref, cb2_ref,
                  cb3_ref, hb_ref, krc_ref, kic_ref, fwdc_ref, invc_ref, krl_ref, kil_ref,
                  fwdl_ref, invl_ref, ymix_ref, o_ref):
    del ymix_ref
    x_refs = (x1_ref, x2_ref, hv_ref, g_ref)
    conv = ((cw1_ref, cw2_ref, cw3_ref), (cb1_ref, cb2_ref, cb3_ref))
    pending = [(0, _hyena_chain(DEC_SEQ, 1, slice(0, DEC_SEQ), x_refs, *conv, krl_ref, kil_ref,
                                hb_ref, fwdl_ref, invl_ref, o_ref))]
    for s in range(HY_ROWS // SEQ):
        pending.append((HY_STAGGER * s,
                        _hyena_chain(SEQ, 0, slice(s * SEQ, (s + 1) * SEQ), x_refs, *conv, krc_ref,
                                     kic_ref, hb_ref, fwdc_ref, invc_ref, o_ref)))
    running, tick = [], 0
    while pending or running:
        running += [gen for start, gen in pending if start == tick]
        pending = [(start, gen) for start, gen in pending if start > tick]
        for gen in list(running):
            if next(gen, "done") == "done":
                running.remove(gen)
        tick += 1


def _mixer_hyena(p, conv_w, conv_b3, hyena_bias, spec_ctx, mats_ctx, spec_lat, mats_lat, layer,
                 ymix):
    assert N_CTX == N_LAT and HY_ROWS == DEC_SEQ
    n_ct = BRANCH // HY_CT
    blk = lambda pc: pl.BlockSpec((2, HY_ROWS, HY_CT), lambda c, i: (0, i, pc * n_ct + c))
    cw = lambda pc: pl.BlockSpec((None, 3, HY_CT), lambda c, i: (layer, 0, pc * n_ct + c))
    cb = lambda pc: pl.BlockSpec((None, 1, HY_CT), lambda c, i: (layer, 0, pc * n_ct + c))
    spec = lambda a: pl.BlockSpec((None, HY_ORDER, a.shape[2], HY_CT), lambda c, i: (layer, 0, 0, c))
    full = lambda a: pl.BlockSpec(a.shape, lambda c, i: (0,) * a.ndim, pipeline_mode=pl.Buffered(1))
    p3 = p.reshape(2, N_CTX, D_IN)
    out = pl.pallas_call(
        _hyena_kernel,
        grid=(n_ct, N_CTX // HY_ROWS),
        in_specs=[blk(C_DX1), blk(C_DX2), blk(C_DV), blk(C_DG),
                  cw(0), cw(1), cw(2), cb(0), cb(1), cb(2),
                  pl.BlockSpec((None, HY_ORDER, HY_CT), lambda c, i: (layer, 0, c)),
                  spec(spec_ctx[0]), spec(spec_ctx[1]), full(mats_ctx[0]), full(mats_ctx[1]),
                  spec(spec_lat[0]), spec(spec_lat[1]), full(mats_lat[0]), full(mats_lat[1]),
                  pl.BlockSpec(memory_space=pl.ANY)],
        out_specs=pl.BlockSpec((2, HY_ROWS, HY_CT), lambda c, i: (0, i, M_HYENA * n_ct + c)),
        out_shape=jax.ShapeDtypeStruct((2, N_CTX, D_MIX), BF16),
        input_output_aliases={19: 0},
        compiler_params=_params("arbitrary", "arbitrary"),
        name="mixer_hyena",
    )(p3, p3, p3, p3, conv_w, conv_w, conv_w, conv_b3, conv_b3, conv_b3, hyena_bias,
      *spec_ctx, *mats_ctx, *spec_lat, *mats_lat, ymix.reshape(2, N_CTX, D_MIX))
    return out.reshape(N_TOK, D_MIX)


OUT_TM = 512


def _outproj_kernel(alpha, n_x, n_out, ymix_ref, *refs):
    x_refs = refs[:n_x]
    gate_ref, w_ref, b_ref, lng_ref, lnb_ref = refs[n_x:n_x + 5]
    o_refs = refs[n_x + 5:]
    is_ctx = pl.program_id(0) < N_CTX // OUT_TM

    def tile(x_ref, o_ref):
        y = jnp.dot(ymix_ref[...], w_ref[...], preferred_element_type=F32) + b_ref[...]
        r = alpha * x_ref[...] + gate_ref[...] * y
        o_ref[...] = _layer_norm(r) * lng_ref[...] + lnb_ref[...]

    if n_x == 1 and n_out == 1:
        tile(x_refs[0], o_refs[0])
    else:
        pl.when(is_ctx)(lambda: tile(x_refs[0], o_refs[0]))
        pl.when(jnp.logical_not(is_ctx))(lambda: tile(x_refs[-1], o_refs[-1]))


def _out_projection(ymix, xs, mod4, w_out_bf, b_out3, ln_g3, ln_b3, layer, split_out):
    alpha = (2.0 * DEPTH) ** 0.25
    row = lambda i: _cond_row(i, OUT_TM)
    vec = pl.BlockSpec((None, 1, D_MODEL), lambda i: (layer, 0, 0))
    if split_out:
        out_shape = [jax.ShapeDtypeStruct((N_CTX, D_MODEL), F32),
                     jax.ShapeDtypeStruct((N_LAT, D_MODEL), F32)]
    else:
        out_shape = [jax.ShapeDtypeStruct((N_TOK, D_MODEL), F32)]
    return pl.pallas_call(
        functools.partial(_outproj_kernel, alpha, len(xs), len(out_shape)),
        grid=(N_TOK // OUT_TM,),
        in_specs=[pl.BlockSpec((OUT_TM, D_MIX), lambda i: (i, 0))] + _token_specs(xs, OUT_TM) + [
            pl.BlockSpec((None, None, 1, D_MODEL), lambda i: (layer, row(i), 0, 2)),
            pl.BlockSpec((D_MIX, D_MODEL), lambda i: (0, 0), pipeline_mode=pl.Buffered(1)),
            vec, vec, vec,
        ],
        out_specs=_token_specs(out_shape, OUT_TM),
        out_shape=out_shape,
        compiler_params=_params("arbitrary"),
        name="out_projection",
    )(ymix, *xs, mod4, w_out_bf, b_out3, ln_g3, ln_b3)


def kernel(x_prompt, x_sample, cache_k, cache_v, c, c_ctx, w_mod, b_mod, w_in, gmlp_w, gmlp_b,
           pool_w, pool_scale, lambda_qk, subln_w, conv_w, conv_b, filt_w1, filt_b1, filt_w2,
           filt_b2, filt_freq, filt_w3, hyena_bias, w_out, b_out, ln_g, ln_b):
    xs = (x_prompt.reshape(N_CTX, D_MODEL), x_sample.reshape(N_LAT, D_MODEL))
    cond = jnp.concatenate(
        [c_ctx[None, :], c, jnp.zeros((N_COND - 1 - DEC_BATCH, D_MODEL), F32)], axis=0)
    mod4 = _modulation(cond, w_mod, b_mod).reshape(DEPTH, N_COND, 1, 3 * D_MODEL)

    cache_k4 = cache_k.reshape(DEC_BATCH, DEPTH, PAST_LEN, BRANCH)
    gmlp_b_rows = jnp.broadcast_to(gmlp_b[..., None], (DEPTH, N_GROUPS, CHUNK, GROUP_W))
    subln_row = subln_w.reshape(DEPTH, 1, HEAD_DIM_C)
    conv_b3 = conv_b.reshape(DEPTH, 1, 3 * BRANCH)
    b_out3 = b_out.reshape(DEPTH, 1, D_MODEL)
    ln_g3 = ln_g.reshape(DEPTH, 1, D_MODEL)
    ln_b3 = ln_b.reshape(DEPTH, 1, D_MODEL)

    pad_h = HY_PAD - HY_HIDDEN
    filt = (
        jnp.pad(filt_w1, ((0, 0), (0, HY_PAD - HY_EMB), (0, pad_h))),
        jnp.pad(filt_b1, ((0, 0), (0, pad_h))).reshape(DEPTH, 1, HY_PAD),
        jnp.pad(filt_w2, ((0, 0), (0, pad_h), (0, pad_h))),
        jnp.pad(filt_b2, ((0, 0), (0, pad_h))).reshape(DEPTH, 1, HY_PAD),
        jnp.pad(filt_freq, ((0, 0), (0, pad_h))).reshape(DEPTH, 1, HY_PAD),
        jnp.pad(filt_w3, ((0, 0), (0, pad_h), (0, 0))),
    )
    mats_ctx = _dft_matrices(SEQ)
    mats_lat = _dft_matrices(DEC_SEQ)
    spec_ctx = _hyena_spectrum(SEQ, filt, mats_ctx[0])
    spec_lat = _hyena_spectrum(DEC_SEQ, filt, mats_lat[0])

    caches = None
    for layer in range(DEPTH):
        lam_init = 0.8 - 0.6 * math.exp(-0.3 * layer)
        p = _in_projection(xs, mod4, w_in, layer)
        ymix = _mixer_gmlp_pool(p, gmlp_w[layer], gmlp_b_rows[layer], pool_w[layer],
                                pool_scale[layer].reshape(1, BRANCH))
        ymix, new_k, new_v = _mixer_attn_ctx(p, lambda_qk, subln_row, layer, lam_init, ymix, caches)
        caches = (new_k, new_v)
        ymix, w_out_bf = _mixer_attn_lat(p, cache_k4, cache_v, lambda_qk, subln_row, layer,
                                         lam_init, w_out, ymix)
        ymix = _mixer_hyena(p, conv_w, conv_b3, hyena_bias, spec_ctx, mats_ctx, spec_lat, mats_lat,
                            layer, ymix)
        xs = tuple(_out_projection(ymix, xs, mod4, w_out_bf, b_out3, ln_g3, ln_b3, layer,
                                   split_out=layer == DEPTH - 1))

    y_prompt = xs[0].reshape(BATCH, SEQ, D_MODEL)
    y_sample = xs[1].reshape(DEC_BATCH, DEC_SEQ, D_MODEL)
    new_k, new_v = caches
    return (y_prompt, y_sample,
            new_k.reshape(BATCH, DEPTH, SEQ, N_HEADS_C, 2, QK_HALF), new_v)
```

```python
import functools
import math

import numpy as np
import jax
import jax.numpy as jnp
from jax import lax
from jax.experimental import pallas as pl
from jax.experimental.pallas import tpu as pltpu

F32 = jnp.float32
BF16 = jnp.bfloat16

D_MODEL = 2048
BATCH = 16
SEQ = 256
DEPTH = 2
DEC_BATCH = 4
DEC_SEQ = 1024
PAST_LEN = 512
GRID_W = 64
BRANCH = 512
N_GROUPS = 4
GROUP_W = 128
CHUNK = 128
POOL_WINDOWS = (2, 4, 8, 16)
N_HEADS_C = 4
HEAD_DIM_C = 128
QK_HALF = 64
ROPE_AXIS_DIM = 32
ROPE_BASE = 10000.0
HY_BANDS = 16
HY_EMB = 33
HY_HIDDEN = 64
HY_ORDER = 2
HY_FAST_DECAY = 0.3
HY_SLOW_DECAY = 1.5
HY_TARGET = 1e-2
N_IN_PIECES = 13
D_IN = N_IN_PIECES * BRANCH
LN_EPS = 1e-6

N_CTX = BATCH * SEQ
N_LAT = DEC_BATCH * DEC_SEQ
N_TOK = N_CTX + N_LAT
N_COND = 8
LANES = 128
HY_PAD = LANES
HY_CT = 256
VMEM_LIMIT = 56 * 1024 * 1024

(C_AU, C_AV, C_AG, C_BX, C_BG, C_Q, C_K, C_V, C_CG, C_DX1, C_DX2, C_DV, C_DG) = range(13)
D_MIX = 4 * BRANCH
(M_GMLP, M_POOL, M_ATTN, M_HYENA) = range(4)


def _silu(x):
    return x * jax.nn.sigmoid(x)


def _bdot(a, b):
    return jnp.dot(a.astype(BF16), b.astype(BF16), preferred_element_type=F32)


def _split(x, n_terms):
    hi = x.astype(BF16)
    if n_terms == 1:
        return (hi,)
    return (hi, (x - hi.astype(F32)).astype(BF16))


def _sdot(a_terms, b_terms):
    acc = jnp.dot(a_terms[0], b_terms[0], preferred_element_type=F32)
    if len(a_terms) > 1:
        acc = acc + jnp.dot(a_terms[1], b_terms[0], preferred_element_type=F32)
    if len(b_terms) > 1:
        acc = acc + jnp.dot(a_terms[0], b_terms[1], preferred_element_type=F32)
    return acc


def _layer_norm(x):
    mu = jnp.mean(x, axis=-1, keepdims=True)
    xc = x - mu
    var = jnp.mean(xc * xc, axis=-1, keepdims=True)
    return xc * lax.rsqrt(var + LN_EPS)


def _cond_row(tile, rows_per_tile):
    n_ctx_tiles = N_CTX // rows_per_tile
    tiles_per_batch = DEC_SEQ // rows_per_tile
    return jnp.where(tile < n_ctx_tiles, 0, 1 + (tile - n_ctx_tiles) // tiles_per_batch)


def _params(*semantics):
    return pltpu.CompilerParams(dimension_semantics=semantics, vmem_limit_bytes=VMEM_LIMIT)


MOD_TN = 1024


def _mod_kernel(c_ref, w_ref, b_ref, o_ref):
    o_ref[...] = _bdot(_silu(c_ref[...]), w_ref[...]) + b_ref[...]


def _modulation(cond, w_mod, b_mod):
    n = 3 * D_MODEL
    return pl.pallas_call(
        _mod_kernel,
        grid=(DEPTH, n // MOD_TN),
        in_specs=[
            pl.BlockSpec((N_COND, D_MODEL), lambda l, j: (0, 0)),
            pl.BlockSpec((None, D_MODEL, MOD_TN), lambda l, j: (l, 0, j)),
            pl.BlockSpec((None, 1, MOD_TN), lambda l, j: (l, 0, j)),
        ],
        out_specs=pl.BlockSpec((None, N_COND, MOD_TN), lambda l, j: (l, 0, j)),
        out_shape=jax.ShapeDtypeStruct((DEPTH, N_COND, n), F32),
        compiler_params=_params("arbitrary", "arbitrary"),
        name="modulation",
    )(cond, w_mod, b_mod.reshape(DEPTH, 1, n))


IN_TM = 1024
IN_FIRST_TM = 2048
IN_TN = 512
IN_LN_ROWS = 256
IN_CAST_K = 512
IN_W_SLOTS = 3


def _token_specs(xs, tm, tile_of=lambda i: i):
    n_ctx_tiles = N_CTX // tm
    if len(xs) == 1:
        maps = [lambda i, *_: (tile_of(i), 0)]
    else:
        maps = [lambda i, *_: (jnp.minimum(tile_of(i), n_ctx_tiles - 1), 0),
                lambda i, *_: (jnp.maximum(tile_of(i) - n_ctx_tiles, 0), 0)]
    return [pl.BlockSpec((tm, D_MODEL), m) for m in maps]


def _inproj_kernel(n_x, tm, tile0, cast_w, *refs):
    x_refs = refs[:n_x]
    scale_ref, shift_ref, w_ref = refs[n_x:n_x + 3]
    if cast_w:
        o_ref, wb_ref, h_ref = refs[n_x + 3:]
    else:
        w_hbm = w_ref
        _, o_ref, h_ref, wbuf_ref, wsem_ref = refs[n_x + 3:]
        n_col = pl.num_programs(1)
        t = pl.program_id(0) * n_col + pl.program_id(1)
        total = pl.num_programs(0) * n_col

        def w_copy(step):
            col = pl.multiple_of((step % n_col) * IN_TN, IN_TN)
            slot = step % IN_W_SLOTS
            return pltpu.make_async_copy(w_hbm.at[:, pl.ds(col, IN_TN)], wbuf_ref.at[slot],
                                         wsem_ref.at[slot])

        @pl.when(t == 0)
        def _():
            for step in range(IN_W_SLOTS - 1):
                w_copy(step).start()

        @pl.when(t + IN_W_SLOTS - 1 < total)
        def _():
            w_copy(t + IN_W_SLOTS - 1).start()

        w_copy(t).wait()
        w_ref = wbuf_ref.at[t % IN_W_SLOTS]

    def weights():
        if not cast_w:
            return w_ref[...]
        w = w_ref[...].astype(BF16)
        wb_ref[...] = w
        return w

    def first_column_step(x_ref):
        w = weights()

        def norm(r):
            rows = slice(r, r + IN_LN_ROWS)
            h = _layer_norm(x_ref[rows, :]) * (1.0 + scale_ref[...]) + shift_ref[...]
            h = h.astype(BF16)
            h_ref[rows, :] = h
            return h

        h = norm(0)
        for r in range(0, tm, IN_LN_ROWS):
            o_ref[r:r + IN_LN_ROWS, :] = jnp.dot(h, w, preferred_element_type=F32)
            if r + IN_LN_ROWS < tm:
                h = norm(r + IN_LN_ROWS)

    first = pl.program_id(1) == 0
    if n_x == 1:
        pl.when(first)(lambda: first_column_step(x_refs[0]))
    else:
        is_ctx = pl.program_id(0) + tile0 < N_CTX // IN_TM
        pl.when(jnp.logical_and(first, is_ctx))(lambda: first_column_step(x_refs[0]))
        pl.when(jnp.logical_and(first, jnp.logical_not(is_ctx)))(
            lambda: first_column_step(x_refs[1]))

    @pl.when(jnp.logical_not(first))
    def _():
        acc = None
        for k in range(0, D_MODEL, IN_CAST_K):
            w = w_ref[k:k + IN_CAST_K, :]
            if cast_w:
                w = w.astype(BF16)
                wb_ref[k:k + IN_CAST_K, :] = w
            part = jnp.dot(h_ref[:, k:k + IN_CAST_K], w, preferred_element_type=F32)
            acc = part if acc is None else acc + part
        o_ref[...] = acc


def _in_projection(xs, mod4, w_in, layer):
    assert IN_FIRST_TM <= N_CTX and IN_FIRST_TM % IN_TM == 0
    n_col = D_IN // IN_TN
    first_tiles = IN_FIRST_TM // IN_TM

    def mod_spec(tile0, piece):
        return pl.BlockSpec((None, None, 1, D_MODEL),
                            lambda i, j: (layer, _cond_row(i + tile0, IN_TM), 0, piece))

    p_shape = jax.ShapeDtypeStruct((N_TOK, D_IN), F32)
    p, w_bf = pl.pallas_call(
        functools.partial(_inproj_kernel, 1, IN_FIRST_TM, 0, True),
        grid=(1, n_col),
        in_specs=[
            pl.BlockSpec((IN_FIRST_TM, D_MODEL), lambda i, j: (0, 0), pipeline_mode=pl.Buffered(1)),
            mod_spec(0, 1), mod_spec(0, 0),
            pl.BlockSpec((None, D_MODEL, IN_TN), lambda i, j: (layer, 0, j)),
        ],
        out_specs=[pl.BlockSpec((IN_FIRST_TM, IN_TN), lambda i, j: (0, j)),
                   pl.BlockSpec((D_MODEL, IN_TN), lambda i, j: (0, j))],
        out_shape=[p_shape, jax.ShapeDtypeStruct((D_MODEL, D_IN), BF16)],
        scratch_shapes=[pltpu.VMEM((IN_FIRST_TM, D_MODEL), BF16)],
        compiler_params=_params("arbitrary", "arbitrary"),
        name="in_projection_first",
    )(xs[0], mod4, mod4, w_in)
    return pl.pallas_call(
        functools.partial(_inproj_kernel, len(xs), IN_TM, first_tiles, False),
        grid=(N_TOK // IN_TM - first_tiles, n_col),
        in_specs=_token_specs(xs, IN_TM, lambda i: i + first_tiles) + [
            mod_spec(first_tiles, 1), mod_spec(first_tiles, 0),
            pl.BlockSpec(memory_space=pl.ANY),
            pl.BlockSpec(memory_space=pl.ANY),
        ],
        out_specs=pl.BlockSpec((IN_TM, IN_TN), lambda i, j: (i + first_tiles, j)),
        out_shape=p_shape,
        input_output_aliases={len(xs) + 3: 0},
        scratch_shapes=[pltpu.VMEM((IN_TM, D_MODEL), BF16),
                        pltpu.VMEM((IN_W_SLOTS, D_MODEL, IN_TN), BF16),
                        pltpu.SemaphoreType.DMA((IN_W_SLOTS,))],
        compiler_params=_params("arbitrary", "arbitrary"),
        name="in_projection",
    )(*xs, mod4, mod4, w_bf, p)


AB_TM = 1024


def _gmlp_tasks(u_ref, v_ref, g_ref, w_ref, b_ref, o_ref):
    ws = [w_ref[g].astype(BF16) for g in range(N_GROUPS)]

    def chunk(r):
        rows = slice(r, r + CHUNK)
        vn = _layer_norm(v_ref[rows, :]).astype(BF16)
        for g in range(N_GROUPS):
            cols = slice(g * GROUP_W, (g + 1) * GROUP_W)
            mixed = jnp.dot(ws[g], vn[:, cols], preferred_element_type=F32) + b_ref[g]
            o_ref[rows, cols] = (_silu(g_ref[rows, cols]) * u_ref[rows, cols] * mixed).astype(BF16)

    return [functools.partial(chunk, r) for r in range(0, AB_TM, CHUNK)]


POOL_BAND_GROUPS = (2, 3)
POOL_BAND_ROWS = 256
POOL_BAND_HALO = 128


def _window_sum_shifts(p, win, pos, seq_len):
    acc = p
    for d in range(-(win // 2), win // 2):
        if d == 0:
            continue
        shifted = pltpu.roll(p, (-d) % AB_TM, axis=0)
        valid = (pos >= -d) if d < 0 else (pos < seq_len - d)
        acc = acc + jnp.where(valid, shifted, 0.0)
    return acc


def _window_sum_band_tasks(p, band_ref, pieces):
    hi = p.astype(BF16)
    r1 = p - hi.astype(F32)
    mid = r1.astype(BF16)
    lo = (r1 - mid.astype(F32)).astype(BF16)
    terms = jnp.concatenate([hi, mid, lo], axis=1)

    def piece(r):
        k0, k1 = max(0, r - POOL_BAND_HALO), min(AB_TM, r + POOL_BAND_ROWS + POOL_BAND_HALO)
        s = jnp.dot(band_ref[r:r + POOL_BAND_ROWS, k0:k1], terms[k0:k1, :],
                    preferred_element_type=F32)
        pieces.append(s[:, :GROUP_W] + s[:, GROUP_W:2 * GROUP_W] + s[:, 2 * GROUP_W:])

    return [functools.partial(piece, r) for r in range(0, AB_TM, POOL_BAND_ROWS)]


def _pool_finish(g, acc, x_ref, g_ref, w_ref, s_ref, pos, seq_len, o_ref):
    cols = slice(g * GROUP_W, (g + 1) * GROUP_W)
    win = POOL_WINDOWS[g]
    count = jnp.minimum(pos + win // 2, seq_len) - jnp.maximum(pos - win // 2, 0)
    pooled = acc / count.astype(F32)
    y = _bdot(pooled - x_ref[:, cols], w_ref[g])
    o_ref[:, BRANCH + g * GROUP_W:BRANCH + (g + 1) * GROUP_W] = (
        _silu(g_ref[:, cols]) * (y * s_ref[:, cols])).astype(BF16)


AB_PIECES = (C_AU, C_AV, C_AG, C_BX, C_BG)
AB_SLOTS = 3


def _mixer_ab_kernel(p_hbm, gw_ref, gb_ref, pw_ref, ps_ref, band_a_ref, band_b_ref, o_ref,
                     buf_ref, sem_ref):
    s = pl.program_id(0)
    n_steps = pl.num_programs(0)

    def copies(step):
        slot = step % AB_SLOTS
        return [pltpu.make_async_copy(
            p_hbm.at[pl.ds(step * AB_TM, AB_TM), pl.ds(c * BRANCH, BRANCH)],
            buf_ref.at[slot, k], sem_ref.at[slot, k]) for k, c in enumerate(AB_PIECES)]

    @pl.when(s == 0)
    def _():
        for step in range(AB_SLOTS - 1):
            for cp in copies(step):
                cp.start()

    @pl.when(s + AB_SLOTS - 1 < n_steps)
    def _():
        for cp in copies(s + AB_SLOTS - 1):
            cp.start()

    for cp in copies(s):
        cp.wait()
    tile = [buf_ref.at[s % AB_SLOTS, k] for k in range(len(AB_PIECES))]
    _mixer_ab_body(*tile, gw_ref, gb_ref, pw_ref, ps_ref, band_a_ref, band_b_ref, o_ref)


def _mixer_ab_body(au_ref, av_ref, ag_ref, bx_ref, bg_ref, gw_ref, gb_ref, pw_ref, ps_ref,
                   band_a_ref, band_b_ref, o_ref):
    seq_len = jnp.where(pl.program_id(0) < N_CTX // AB_TM, SEQ, DEC_SEQ)
    pos = lax.broadcasted_iota(jnp.int32, (AB_TM, GROUP_W), 0) & (seq_len - 1)
    pool_refs = (bx_ref, bg_ref, pw_ref, ps_ref, pos, seq_len, o_ref)
    band_sums = {g: [] for g in POOL_BAND_GROUPS}
    mxu_tasks = [t for g, band in zip(POOL_BAND_GROUPS, (band_a_ref, band_b_ref))
                 for t in _window_sum_band_tasks(bx_ref[:, g * GROUP_W:(g + 1) * GROUP_W], band,
                                                 band_sums[g])]
    vpu_tasks = _gmlp_tasks(au_ref, av_ref, ag_ref, gw_ref, gb_ref, o_ref)
    for i in range(max(len(mxu_tasks), len(vpu_tasks))):
        for tasks in (mxu_tasks, vpu_tasks):
            if i < len(tasks):
                tasks[i]()
    for g, win in enumerate(POOL_WINDOWS):
        if g not in POOL_BAND_GROUPS:
            acc = _window_sum_shifts(bx_ref[:, g * GROUP_W:(g + 1) * GROUP_W], win, pos, seq_len)
            _pool_finish(g, acc, *pool_refs)
    for g in POOL_BAND_GROUPS:
        _pool_finish(g, jnp.concatenate(band_sums[g], axis=0), *pool_refs)


def _pool_bands():
    t = np.arange(AB_TM)
    bands = []
    for g in POOL_BAND_GROUPS:
        half = POOL_WINDOWS[g] // 2
        d = t[None, :] - t[:, None]
        near = (d >= -half) & (d < half)
        per_path = [near & ((t[None, :] // L) == (t[:, None] // L)) for L in (SEQ, DEC_SEQ)]
        bands.append(jnp.asarray(np.stack(per_path), F32).astype(BF16))
    return bands


def _mixer_gmlp_pool(p, gmlp_w, gmlp_b_rows, pool_w, pool_scale_row):
    assert M_POOL == M_GMLP + 1
    n_steps = N_TOK // AB_TM
    assert n_steps >= AB_SLOTS
    full = lambda a: pl.BlockSpec(a.shape, lambda i: (0,) * a.ndim)
    band = pl.BlockSpec((None, AB_TM, AB_TM), lambda i: (jnp.where(i < N_CTX // AB_TM, 0, 1), 0, 0))
    return pl.pallas_call(
        _mixer_ab_kernel,
        grid=(n_steps,),
        in_specs=[pl.BlockSpec(memory_space=pl.ANY),
                  full(gmlp_w), full(gmlp_b_rows), full(pool_w), full(pool_scale_row), band, band],
        out_specs=pl.BlockSpec((AB_TM, 2 * BRANCH), lambda i: (i, M_GMLP // 2)),
        out_shape=jax.ShapeDtypeStruct((N_TOK, D_MIX), BF16),
        scratch_shapes=[pltpu.VMEM((AB_SLOTS, len(AB_PIECES), AB_TM, BRANCH), F32),
                        pltpu.SemaphoreType.DMA((AB_SLOTS, len(AB_PIECES)))],
        compiler_params=_params("arbitrary"),
        name="mixer_gmlp_pool",
    )(p, gmlp_w, gmlp_b_rows, pool_w, pool_scale_row, *_pool_bands())


ATT_TQ = 1024
V_ROWS = HEAD_DIM_C + 16
NT_DIMS = (((1,), (1,)), ((), ()))


def _lambda(lam_ref, lam_init):
    lq = lam_ref[...]
    a = jnp.sum(lq[0:1] * lq[1:2], axis=-1, keepdims=True)
    b = jnp.sum(lq[2:3] * lq[3:4], axis=-1, keepdims=True)
    return jnp.exp(a) - jnp.exp(b) + lam_init


def _map_masks():
    lane = lax.broadcasted_iota(jnp.int32, (1, HEAD_DIM_C), 1)
    m0 = (lane < QK_HALF).astype(F32)
    return m0, 1.0 - m0


def _scores_t(q, keys):
    return [lax.dot_general(k, q, NT_DIMS, preferred_element_type=F32) for k in keys]


def _softmax_v_t(s, vals_t):
    m = functools.reduce(jnp.maximum, [jnp.max(x, axis=0, keepdims=True) for x in s])
    acc = functools.reduce(
        jnp.add, [jnp.dot(v, jnp.exp2(x - m).astype(BF16), preferred_element_type=F32)
                  for x, v in zip(s, vals_t)])
    return acc[:HEAD_DIM_C] * (1.0 / acc[HEAD_DIM_C:HEAD_DIM_C + 1])


def _with_ones_rows(v_t):
    ones = jnp.ones((V_ROWS - HEAD_DIM_C, v_t.shape[1]), BF16)
    return jnp.concatenate([v_t.astype(BF16), ones], axis=0)


def _diff_attention(q_of, keys_of, vals_t_of, gate_of, store, lam, lam_init, subln):
    masks = _map_masks()
    tasks = [(h, mp) for h in range(N_HEADS_C) for mp in range(2)]
    per_head = {}

    def operands(h):
        if h not in per_head:
            q = q_of(h) * (QK_HALF ** -0.5 * math.log2(math.e))
            per_head[h] = (q, keys_of(h), vals_t_of(h))
        return per_head[h]

    def scores(h, mp):
        q, keys, _ = operands(h)
        return _scores_t((q * masks[mp]).astype(BF16), keys)

    nxt = scores(*tasks[0])
    outs = []
    for i, (h, mp) in enumerate(tasks):
        cur = nxt
        if i + 1 < len(tasks):
            nxt = scores(*tasks[i + 1])
        outs.append(_softmax_v_t(cur, operands(h)[2]))
        if mp == 1:
            o = (outs[0] - lam * outs[1]).T
            outs = []
            o = o * lax.rsqrt(jnp.mean(o * o, axis=-1, keepdims=True) + 1e-5)
            o = o * subln * (1.0 - lam_init)
            store(h, _silu(gate_of(h)) * o)


def _head_cols(h):
    return slice(h * HEAD_DIM_C, (h + 1) * HEAD_DIM_C)


def _attn_ctx_kernel(lam_init, q_ref, k_ref, v_ref, g_ref, lam_ref, sw_ref, *rest):
    o_ref, ko_ref, vo_ref = rest[-3:]
    ko_ref[...] = k_ref[...]
    for h in range(N_HEADS_C):
        vo_ref[:, h, :] = v_ref[:, _head_cols(h)]

    def store(h, y):
        o_ref[:, _head_cols(h)] = y.astype(BF16)

    _diff_attention(
        lambda h: q_ref[:, _head_cols(h)],
        lambda h: [k_ref[:, _head_cols(h)].astype(BF16)],
        lambda h: [_with_ones_rows(v_ref[:, _head_cols(h)].T)],
        lambda h: g_ref[:, _head_cols(h)],
        store, _lambda(lam_ref, lam_init), lam_init, sw_ref[...])


def _mixer_attn_ctx(p, lambda_qk, subln_row, layer, lam_init, ymix, caches):
    blk = lambda c: pl.BlockSpec((SEQ, BRANCH), lambda b: (b, c))
    any_spec = pl.BlockSpec(memory_space=pl.ANY)
    k_spec = pl.BlockSpec((None, None, SEQ, BRANCH), lambda b: (b, layer, 0, 0))
    v_spec = pl.BlockSpec((None, None, SEQ, N_HEADS_C, HEAD_DIM_C), lambda b: (b, layer, 0, 0, 0))
    k_shape = jax.ShapeDtypeStruct((BATCH, DEPTH, SEQ, BRANCH), F32)
    v_shape = jax.ShapeDtypeStruct((BATCH, DEPTH, SEQ, N_HEADS_C, HEAD_DIM_C), F32)
    in_specs = [
        blk(C_Q), blk(C_K), blk(C_V), blk(C_CG),
        pl.BlockSpec((None, 4, QK_HALF), lambda b: (layer, 0, 0)),
        pl.BlockSpec((None, 1, HEAD_DIM_C), lambda b: (layer, 0, 0)),
        any_spec,
    ]
    args = [p, p, p, p, lambda_qk, subln_row, ymix]
    aliases = {6: 0}
    if caches is not None:
        in_specs += [any_spec, any_spec]
        args += list(caches)
        aliases.update({7: 1, 8: 2})
    return pl.pallas_call(
        functools.partial(_attn_ctx_kernel, lam_init),
        grid=(BATCH,),
        in_specs=in_specs,
        out_specs=[pl.BlockSpec((SEQ, BRANCH), lambda b: (b, M_ATTN)), k_spec, v_spec],
        out_shape=[jax.ShapeDtypeStruct((N_TOK, D_MIX), BF16), k_shape, v_shape],
        input_output_aliases=aliases,
        compiler_params=_params("arbitrary"),
        name="mixer_attn_ctx",
    )(*args)


def _rope(x, cos, sin_signed):
    lane = lax.broadcasted_iota(jnp.int32, x.shape, 1)
    first_half = (lane & (ROPE_AXIS_DIM - 1)) < (ROPE_AXIS_DIM // 2)
    half = ROPE_AXIS_DIM // 2
    partner = jnp.where(first_half,
                        pltpu.roll(x, x.shape[1] - half, axis=1),
                        pltpu.roll(x, half, axis=1))
    return x * cos + partner * sin_signed


def _attn_lat_kernel(lam_init, q_ref, k_ref, v_ref, g_ref, ck_ref, cv_ref, cosq_ref, sinq_ref,
                     cosk_ref, sink_ref, lam_ref, sw_ref, wo_ref, ymix_ref, o_ref, wo_bf_ref,
                     kc_ref, kr_ref, vt_ref):
    del ymix_ref
    wo_bf_ref[...] = wo_ref[...].astype(BF16)

    @pl.when(pl.program_id(1) == 0)
    def _():
        kc_ref[...] = ck_ref[...].astype(BF16)
        for h in range(N_HEADS_C):
            cols = slice(h * HEAD_DIM_C, (h + 1) * HEAD_DIM_C)
            kr_ref[:, cols] = _rope(k_ref[:, cols], cosk_ref[...], sink_ref[...]).astype(BF16)
            vt_ref[h, :, 0:PAST_LEN] = _with_ones_rows(cv_ref[:, h, :].T)
            vt_ref[h, :, PAST_LEN:PAST_LEN + DEC_SEQ] = _with_ones_rows(v_ref[:, cols].T)

    def store(h, y):
        o_ref[:, _head_cols(h)] = y.astype(BF16)

    _diff_attention(
        lambda h: _rope(q_ref[:, _head_cols(h)], cosq_ref[...], sinq_ref[...]),
        lambda h: [kc_ref[:, _head_cols(h)], kr_ref[:, _head_cols(h)]],
        lambda h: [vt_ref[h, :, 0:PAST_LEN], vt_ref[h, :, PAST_LEN:PAST_LEN + DEC_SEQ]],
        lambda h: g_ref[:, _head_cols(h)],
        store, _lambda(lam_ref, lam_init), lam_init, sw_ref[...])


def _rope_tables():
    pos = np.arange(DEC_SEQ)
    row = (pos // GRID_W).astype(np.float64)
    col = (pos % GRID_W).astype(np.float64)
    half = ROPE_AXIS_DIM // 2
    inv = ROPE_BASE ** (-np.arange(0, ROPE_AXIS_DIM, 2, dtype=np.float64) / ROPE_AXIS_DIM)
    lane = np.arange(HEAD_DIM_C)
    axis_is_col = (lane // ROPE_AXIS_DIM) % 2 == 1
    idx = lane % ROPE_AXIS_DIM
    ang = np.where(axis_is_col[None, :], col[:, None], row[:, None]) * inv[idx % half][None, :]
    sign = np.where(idx < half, -1.0, 1.0)[None, :]
    return (jnp.asarray(np.cos(ang), F32), jnp.asarray(np.sin(ang) * sign, F32))


def _mixer_attn_lat(p, cache_k4, cache_v, lambda_qk, subln_row, layer, lam_init, w_out, ymix):
    cos_t, sin_t = _rope_tables()
    q_tiles = DEC_SEQ // ATT_TQ
    q0 = N_CTX // ATT_TQ
    b0 = N_CTX // DEC_SEQ
    wo_rows = D_MIX // (DEC_BATCH * q_tiles)
    qblk = lambda c: pl.BlockSpec((ATT_TQ, BRANCH), lambda b, i: (q0 + b * q_tiles + i, c))
    kblk = lambda c: pl.BlockSpec((DEC_SEQ, BRANCH), lambda b, i: (b0 + b, c))
    cblk = pl.BlockSpec((None, None, PAST_LEN, BRANCH), lambda b, i: (b, layer, 0, 0))
    cvblk = pl.BlockSpec((None, None, PAST_LEN, N_HEADS_C, HEAD_DIM_C),
                         lambda b, i: (b, layer, 0, 0, 0))
    return pl.pallas_call(
        functools.partial(_attn_lat_kernel, lam_init),
        grid=(DEC_BATCH, q_tiles),
        in_specs=[
            qblk(C_Q), kblk(C_K), kblk(C_V), qblk(C_CG), cblk, cvblk,
            pl.BlockSpec((ATT_TQ, HEAD_DIM_C), lambda b, i: (i, 0)),
            pl.BlockSpec((ATT_TQ, HEAD_DIM_C), lambda b, i: (i, 0)),
            pl.BlockSpec((DEC_SEQ, HEAD_DIM_C), lambda b, i: (0, 0)),
            pl.BlockSpec((DEC_SEQ, HEAD_DIM_C), lambda b, i: (0, 0)),
            pl.BlockSpec((None, 4, QK_HALF), lambda b, i: (layer, 0, 0)),
            pl.BlockSpec((None, 1, HEAD_DIM_C), lambda b, i: (layer, 0, 0)),
            pl.BlockSpec((None, wo_rows, D_MODEL), lambda b, i: (layer, b * q_tiles + i, 0)),
            pl.BlockSpec(memory_space=pl.ANY),
        ],
        out_specs=[pl.BlockSpec((ATT_TQ, BRANCH), lambda b, i: (q0 + b * q_tiles + i, M_ATTN)),
                   pl.BlockSpec((wo_rows, D_MODEL), lambda b, i: (b * q_tiles + i, 0))],
        out_shape=[jax.ShapeDtypeStruct((N_TOK, D_MIX), BF16),
                   jax.ShapeDtypeStruct((D_MIX, D_MODEL), BF16)],
        input_output_aliases={13: 0},
        scratch_shapes=[pltpu.VMEM((PAST_LEN, BRANCH), BF16), pltpu.VMEM((DEC_SEQ, BRANCH), BF16),
                        pltpu.VMEM((N_HEADS_C, V_ROWS, PAST_LEN + DEC_SEQ), BF16)],
        compiler_params=_params("arbitrary", "arbitrary"),
        name="mixer_attn_lat",
    )(p, p, p, p, cache_k4, cache_v, cos_t, sin_t, cos_t, sin_t, lambda_qk, subln_row, w_out, ymix)


HY_ROWS = 1024
HY_PIECE = 512
HY_STAGGER = 1


def _dft_matrices(seq_len):
    n = 2 * seq_len
    f = np.arange(seq_len, dtype=np.float64)[:, None]
    s = np.arange(seq_len, dtype=np.float64)[None, :]
    theta = 2.0 * np.pi * f * s / n
    alt = np.where(np.arange(seq_len) % 2 == 0, 1.0, -1.0)
    ac = np.cos(theta)
    as_ = -np.sin(theta)
    as_[0, :] = alt
    bc = 2.0 * np.cos(theta.T) / n
    bc[:, 0] = 1.0 / n
    bs = -2.0 * np.sin(theta.T) / n
    bs[:, 0] = alt / n
    fwd = np.concatenate([ac, as_], axis=0)
    inv = np.concatenate([bc, bs], axis=1)
    return jnp.asarray(fwd, F32).astype(BF16), jnp.asarray(inv, F32).astype(BF16)


def _filter_features(seq_len):
    t_idx = np.arange(seq_len, dtype=np.float64)
    t_norm = np.linspace(0.0, 1.0, seq_len)
    bands = np.linspace(1e-4, HY_BANDS - 1, HY_BANDS)
    ang = (2.0 * math.pi * t_idx / seq_len)[:, None] * bands[None, :]
    feats = np.concatenate([t_norm[:, None], np.cos(ang), np.sin(ang)], axis=-1)
    feats = np.pad(feats, ((0, 0), (0, HY_PAD - HY_EMB)))
    deltas = np.abs(np.linspace(math.log(HY_TARGET) / HY_FAST_DECAY,
                                math.log(HY_TARGET) / HY_SLOW_DECAY, BRANCH))
    return (jnp.asarray(feats, F32), jnp.asarray(t_norm[:, None], F32),
            jnp.asarray(deltas[None, :], F32))


def _filter_kernel(seq_len, feats_ref, tn_ref, dl_ref, w1_ref, b1_ref, w2_ref, b2_ref, fr_ref,
                   w3f_ref, w3b_ref, fwd_ref, kr_ref, ki_ref, h_ref):
    sp = lambda x: _split(x, 2)

    @pl.when(pl.program_id(1) == 0)
    def _():
        fr = fr_ref[...]
        h = jnp.sin(fr * (_sdot(sp(feats_ref[...]), sp(w1_ref[...])) + b1_ref[...]))
        h_ref[...] = jnp.sin(fr * (_sdot(sp(h), sp(w2_ref[...])) + b2_ref[...]))

    h = h_ref[...]
    decay = jnp.exp(-tn_ref[...] * dl_ref[...])
    row = lax.broadcasted_iota(jnp.int32, (seq_len, BRANCH), 0)
    fwd = _bdot(h, w3f_ref[...]) * decay
    bwd = jnp.where(row == 0, 0.0, _bdot(h, w3b_ref[...]) * decay)
    norm = (jnp.sum(jnp.abs(fwd), axis=0, keepdims=True)
            + jnp.sum(jnp.abs(bwd), axis=0, keepdims=True))
    fwd = fwd / norm
    bwd = bwd / norm
    even = fwd + bwd
    alt = jnp.where((row & 1) == 0, 1.0, -1.0)
    nyquist = jnp.sum(alt * even, axis=0, keepdims=True)
    kr_ref[...] = jnp.dot(fwd_ref[0:seq_len, :], even.astype(BF16), preferred_element_type=F32)
    ki = jnp.dot(fwd_ref[seq_len:2 * seq_len, :], (fwd - bwd).astype(BF16),
                 preferred_element_type=F32)
    ki_ref[...] = jnp.where(row == 0, nyquist, ki)


def _hyena_spectrum(seq_len, filt, fwd_mat):
    feats, t_norm, deltas = _filter_features(seq_len)
    w1, b1, w2, b2, freq, w3 = filt
    full = lambda a: pl.BlockSpec(a.shape, lambda l, o: (0,) * a.ndim)
    lyr = lambda r, n: pl.BlockSpec((None, r, n), lambda l, o: (l, 0, 0))
    out = pl.BlockSpec((None, None, seq_len, BRANCH), lambda l, o: (l, o, 0, 0))
    return pl.pallas_call(
        functools.partial(_filter_kernel, seq_len),
        grid=(DEPTH, HY_ORDER),
        in_specs=[
            full(feats), full(t_norm), full(deltas),
            lyr(HY_PAD, HY_PAD), lyr(1, HY_PAD), lyr(HY_PAD, HY_PAD), lyr(1, HY_PAD), lyr(1, HY_PAD),
            pl.BlockSpec((None, HY_PAD, BRANCH), lambda l, o: (l, 0, 2 * o)),
            pl.BlockSpec((None, HY_PAD, BRANCH), lambda l, o: (l, 0, 2 * o + 1)),
            full(fwd_mat),
        ],
        out_specs=[out, out],
        out_shape=[jax.ShapeDtypeStruct((DEPTH, HY_ORDER, seq_len, BRANCH), F32)] * 2,
        scratch_shapes=[pltpu.VMEM((seq_len, HY_PAD), F32)],
        compiler_params=_params("arbitrary", "arbitrary"),
        name=f"hyena_spectrum_{seq_len}",
    )(feats, t_norm, deltas, w1, b1, w2, b2, freq, w3, w3, fwd_mat)


def _hyena_chain(seq_len, path, rows, x_refs, cw_refs, cb_refs, kr_ref, ki_ref, hb_ref, fwd_ref,
                 inv_ref, o_ref):
    x1_ref, x2_ref, hv_ref, g_ref = x_refs
    row = lax.broadcasted_iota(jnp.int32, (seq_len, HY_CT), 0)
    first, last = row == 0, row == seq_len - 1

    def short_conv(x_ref, piece):
        x, w = x_ref[path, rows, :], cw_refs[piece][...]
        prev = jnp.where(first, 0.0, pltpu.roll(x, 1, axis=0))
        nxt = jnp.where(last, 0.0, pltpu.roll(x, seq_len - 1, axis=0))
        return prev * w[0:1] + x * w[1:2] + nxt * w[2:3] + cb_refs[piece][...]

    z = short_conv(hv_ref, 2)
    yield
    gate_refs = (x1_ref, x2_ref)
    piece = min(seq_len, HY_PIECE)
    for order in range(HY_ORDER):
        zb = z.astype(BF16)
        y_re, y_im = [], []
        for r in range(0, seq_len, piece):
            zr = jnp.dot(fwd_ref[r:r + piece, :], zb, preferred_element_type=F32)
            yield
            zi = jnp.dot(fwd_ref[seq_len + r:seq_len + r + piece, :], zb,
                         preferred_element_type=F32)
            yield
            kr, kp = kr_ref[order, r:r + piece, :], ki_ref[order, r:r + piece, :]
            ki, kn = kp, kr
            if r == 0:
                dc = lax.broadcasted_iota(jnp.int32, (piece, HY_CT), 0) == 0
                ki = jnp.where(dc, 0.0, kp)
                kn = jnp.where(dc, kp, kr)
            y_re.append((zr * kr - zi * ki).astype(BF16))
            y_im.append((zr * ki + zi * kn).astype(BF16))
            yield
        yf = jnp.concatenate(y_re + y_im, axis=0)
        ys = []
        for r in range(0, seq_len, piece):
            ys.append(jnp.dot(inv_ref[r:r + piece, :], yf, preferred_element_type=F32))
            yield
            if r == 0:
                gate = short_conv(gate_refs[order], order)
        y = ys[0] if len(ys) == 1 else jnp.concatenate(ys, axis=0)
        z = gate * (y + z * hb_ref[order:order + 1, :])
    o_ref[path, rows, :] = (_silu(g_ref[path, rows, :]) * z).astype(BF16)


def _hyena_kernel(x1_ref, x2_ref, hv_ref, g_ref, cw1_ref, cw2_ref, cw3_ref, cb1_ref, cb2_ref,
                  cb3_ref, hb_ref, krc_ref, kic_ref, fwdc_ref, invc_ref, krl_ref, kil_ref,
                  fwdl_ref, invl_ref, ymix_ref, o_ref):
    del ymix_ref
    x_refs = (x1_ref, x2_ref, hv_ref, g_ref)
    conv = ((cw1_ref, cw2_ref, cw3_ref), (cb1_ref, cb2_ref, cb3_ref))
    pending = [(0, _hyena_chain(DEC_SEQ, 1, slice(0, DEC_SEQ), x_refs, *conv, krl_ref, kil_ref,
                                hb_ref, fwdl_ref, invl_ref, o_ref))]
    for s in range(HY_ROWS // SEQ):
        pending.append((HY_STAGGER * s,
                        _hyena_chain(SEQ, 0, slice(s * SEQ, (s + 1) * SEQ), x_refs, *conv, krc_ref,
                                     kic_ref, hb_ref, fwdc_ref, invc_ref, o_ref)))
    running, tick = [], 0
    while pending or running:
        running += [gen for start, gen in pending if start == tick]
        pending = [(start, gen) for start, gen in pending if start > tick]
        for gen in list(running):
            if next(gen, "done") == "done":
                running.remove(gen)
        tick += 1


def _mixer_hyena(p, conv_w, conv_b3, hyena_bias, spec_ctx, mats_ctx, spec_lat, mats_lat, layer,
                 ymix):
    assert N_CTX == N_LAT and HY_ROWS == DEC_SEQ
    n_ct = BRANCH // HY_CT
    blk = lambda pc: pl.BlockSpec((2, HY_ROWS, HY_CT), lambda c, i: (0, i, pc * n_ct + c))
    cw = lambda pc: pl.BlockSpec((None, 3, HY_CT), lambda c, i: (layer, 0, pc * n_ct + c))
    cb = lambda pc: pl.BlockSpec((None, 1, HY_CT), lambda c, i: (layer, 0, pc * n_ct + c))
    spec = lambda a: pl.BlockSpec((None, HY_ORDER, a.shape[2], HY_CT), lambda c, i: (layer, 0, 0, c))
    full = lambda a: pl.BlockSpec(a.shape, lambda c, i: (0,) * a.ndim, pipeline_mode=pl.Buffered(1))
    p3 = p.reshape(2, N_CTX, D_IN)
    out = pl.pallas_call(
        _hyena_kernel,
        grid=(n_ct, N_CTX // HY_ROWS),
        in_specs=[blk(C_DX1), blk(C_DX2), blk(C_DV), blk(C_DG),
                  cw(0), cw(1), cw(2), cb(0), cb(1), cb(2),
                  pl.BlockSpec((None, HY_ORDER, HY_CT), lambda c, i: (layer, 0, c)),
                  spec(spec_ctx[0]), spec(spec_ctx[1]), full(mats_ctx[0]), full(mats_ctx[1]),
                  spec(spec_lat[0]), spec(spec_lat[1]), full(mats_lat[0]), full(mats_lat[1]),
                  pl.BlockSpec(memory_space=pl.ANY)],
        out_specs=pl.BlockSpec((2, HY_ROWS, HY_CT), lambda c, i: (0, i, M_HYENA * n_ct + c)),
        out_shape=jax.ShapeDtypeStruct((2, N_CTX, D_MIX), BF16),
        input_output_aliases={19: 0},
        compiler_params=_params("arbitrary", "arbitrary"),
        name="mixer_hyena",
    )(p3, p3, p3, p3, conv_w, conv_w, conv_w, conv_b3, conv_b3, conv_b3, hyena_bias,
      *spec_ctx, *mats_ctx, *spec_lat, *mats_lat, ymix.reshape(2, N_CTX, D_MIX))
    return out.reshape(N_TOK, D_MIX)


OUT_TM = 512


def _outproj_kernel(alpha, n_x, n_out, ymix_ref, *refs):
    x_refs = refs[:n_x]
    gate_ref, w_ref, b_ref, lng_ref, lnb_ref = refs[n_x:n_x + 5]
    o_refs = refs[n_x + 5:]
    is_ctx = pl.program_id(0) < N_CTX // OUT_TM

    def tile(x_ref, o_ref):
        y = jnp.dot(ymix_ref[...], w_ref[...], preferred_element_type=F32) + b_ref[...]
        r = alpha * x_ref[...] + gate_ref[...] * y
        o_ref[...] = _layer_norm(r) * lng_ref[...] + lnb_ref[...]

    if n_x == 1 and n_out == 1:
        tile(x_refs[0], o_refs[0])
    else:
        pl.when(is_ctx)(lambda: tile(x_refs[0], o_refs[0]))
        pl.when(jnp.logical_not(is_ctx))(lambda: tile(x_refs[-1], o_refs[-1]))


def _out_projection(ymix, xs, mod4, w_out_bf, b_out3, ln_g3, ln_b3, layer, split_out):
    alpha = (2.0 * DEPTH) ** 0.25
    row = lambda i: _cond_row(i, OUT_TM)
    vec = pl.BlockSpec((None, 1, D_MODEL), lambda i: (layer, 0, 0))
    if split_out:
        out_shape = [jax.ShapeDtypeStruct((N_CTX, D_MODEL), F32),
                     jax.ShapeDtypeStruct((N_LAT, D_MODEL), F32)]
    else:
        out_shape = [jax.ShapeDtypeStruct((N_TOK, D_MODEL), F32)]
    return pl.pallas_call(
        functools.partial(_outproj_kernel, alpha, len(xs), len(out_shape)),
        grid=(N_TOK // OUT_TM,),
        in_specs=[pl.BlockSpec((OUT_TM, D_MIX), lambda i: (i, 0))] + _token_specs(xs, OUT_TM) + [
            pl.BlockSpec((None, None, 1, D_MODEL), lambda i: (layer, row(i), 0, 2)),
            pl.BlockSpec((D_MIX, D_MODEL), lambda i: (0, 0), pipeline_mode=pl.Buffered(1)),
            vec, vec, vec,
        ],
        out_specs=_token_specs(out_shape, OUT_TM),
        out_shape=out_shape,
        compiler_params=_params("arbitrary"),
        name="out_projection",
    )(ymix, *xs, mod4, w_out_bf, b_out3, ln_g3, ln_b3)


def kernel(x_prompt, x_sample, cache_k, cache_v, c, c_ctx, w_mod, b_mod, w_in, gmlp_w, gmlp_b,
           pool_w, pool_scale, lambda_qk, subln_w, conv_w, conv_b, filt_w1, filt_b1, filt_w2,
           filt_b2, filt_freq, filt_w3, hyena_bias, w_out, b_out, ln_g, ln_b):
    xs = (x_prompt.reshape(N_CTX, D_MODEL), x_sample.reshape(N_LAT, D_MODEL))
    cond = jnp.concatenate(
        [c_ctx[None, :], c, jnp.zeros((N_COND - 1 - DEC_BATCH, D_MODEL), F32)], axis=0)
    mod4 = _modulation(cond, w_mod, b_mod).reshape(DEPTH, N_COND, 1, 3 * D_MODEL)

    cache_k4 = cache_k.reshape(DEC_BATCH, DEPTH, PAST_LEN, BRANCH)
    gmlp_b_rows = jnp.broadcast_to(gmlp_b[..., None], (DEPTH, N_GROUPS, CHUNK, GROUP_W))
    subln_row = subln_w.reshape(DEPTH, 1, HEAD_DIM_C)
    conv_b3 = conv_b.reshape(DEPTH, 1, 3 * BRANCH)
    b_out3 = b_out.reshape(DEPTH, 1, D_MODEL)
    ln_g3 = ln_g.reshape(DEPTH, 1, D_MODEL)
    ln_b3 = ln_b.reshape(DEPTH, 1, D_MODEL)

    pad_h = HY_PAD - HY_HIDDEN
    filt = (
        jnp.pad(filt_w1, ((0, 0), (0, HY_PAD - HY_EMB), (0, pad_h))),
        jnp.pad(filt_b1, ((0, 0), (0, pad_h))).reshape(DEPTH, 1, HY_PAD),
        jnp.pad(filt_w2, ((0, 0), (0, pad_h), (0, pad_h))),
        jnp.pad(filt_b2, ((0, 0), (0, pad_h))).reshape(DEPTH, 1, HY_PAD),
        jnp.pad(filt_freq, ((0, 0), (0, pad_h))).reshape(DEPTH, 1, HY_PAD),
        jnp.pad(filt_w3, ((0, 0), (0, pad_h), (0, 0))),
    )
    mats_ctx = _dft_matrices(SEQ)
    mats_lat = _dft_matrices(DEC_SEQ)
    spec_ctx = _hyena_spectrum(SEQ, filt, mats_ctx[0])
    spec_lat = _hyena_spectrum(DEC_SEQ, filt, mats_lat[0])

    caches = None
    for layer in range(DEPTH):
        lam_init = 0.8 - 0.6 * math.exp(-0.3 * layer)
        p = _in_projection(xs, mod4, w_in, layer)
        ymix = _mixer_gmlp_pool(p, gmlp_w[layer], gmlp_b_rows[layer], pool_w[layer],
                                pool_scale[layer].reshape(1, BRANCH))
        ymix, new_k, new_v = _mixer_attn_ctx(p, lambda_qk, subln_row, layer, lam_init, ymix, caches)
        caches = (new_k, new_v)
        ymix, w_out_bf = _mixer_attn_lat(p, cache_k4, cache_v, lambda_qk, subln_row, layer,
                                         lam_init, w_out, ymix)
        ymix = _mixer_hyena(p, conv_w, conv_b3, hyena_bias, spec_ctx, mats_ctx, spec_lat, mats_lat,
                            layer, ymix)
        xs = tuple(_out_projection(ymix, xs, mod4, w_out_bf, b_out3, ln_g3, ln_b3, layer,
                                   split_out=layer == DEPTH - 1))

    y_prompt = xs[0].reshape(BATCH, SEQ, D_MODEL)
    y_sample = xs[1].reshape(DEC_BATCH, DEC_SEQ, D_MODEL)
    new_k, new_v = caches
    return (y_prompt, y_sample,
            new_k.reshape(BATCH, DEPTH, SEQ, N_HEADS_C, 2, QK_HALF), new_v)
```

```python
import functools
import math

import numpy as np
import jax
import jax.numpy as jnp
from jax import lax
from jax.experimental import pallas as pl
from jax.experimental.pallas import tpu as pltpu

F32 = jnp.float32
BF16 = jnp.bfloat16

D_MODEL = 2048
BATCH = 16
SEQ = 256
DEPTH = 2
DEC_BATCH = 4
DEC_SEQ = 1024
PAST_LEN = 512
GRID_W = 64
BRANCH = 512
N_GROUPS = 4
GROUP_W = 128
CHUNK = 128
POOL_WINDOWS = (2, 4, 8, 16)
N_HEADS_C = 4
HEAD_DIM_C = 128
QK_HALF = 64
ROPE_AXIS_DIM = 32
ROPE_BASE = 10000.0
HY_BANDS = 16
HY_EMB = 33
HY_HIDDEN = 64
HY_ORDER = 2
HY_FAST_DECAY = 0.3
HY_SLOW_DECAY = 1.5
HY_TARGET = 1e-2
N_IN_PIECES = 13
D_IN = N_IN_PIECES * BRANCH
LN_EPS = 1e-6

N_CTX = BATCH * SEQ
N_LAT = DEC_BATCH * DEC_SEQ
N_TOK = N_CTX + N_LAT
N_COND = 8
LANES = 128
HY_PAD = LANES
HY_CT = 256
VMEM_LIMIT = 56 * 1024 * 1024

(C_AU, C_AV, C_AG, C_BX, C_BG, C_Q, C_K, C_V, C_CG, C_DX1, C_DX2, C_DV, C_DG) = range(13)
D_MIX = 4 * BRANCH
(M_GMLP, M_POOL, M_ATTN, M_HYENA) = range(4)


def _silu(x):
    return x * jax.nn.sigmoid(x)


def _bdot(a, b):
    return jnp.dot(a.astype(BF16), b.astype(BF16), preferred_element_type=F32)


def _split(x, n_terms):
    hi = x.astype(BF16)
    if n_terms == 1:
        return (hi,)
    return (hi, (x - hi.astype(F32)).astype(BF16))


def _sdot(a_terms, b_terms):
    acc = jnp.dot(a_terms[0], b_terms[0], preferred_element_type=F32)
    if len(a_terms) > 1:
        acc = acc + jnp.dot(a_terms[1], b_terms[0], preferred_element_type=F32)
    if len(b_terms) > 1:
        acc = acc + jnp.dot(a_terms[0], b_terms[1], preferred_element_type=F32)
    return acc


def _layer_norm(x):
    mu = jnp.mean(x, axis=-1, keepdims=True)
    xc = x - mu
    var = jnp.mean(xc * xc, axis=-1, keepdims=True)
    return xc * lax.rsqrt(var + LN_EPS)


def _cond_row(tile, rows_per_tile):
    n_ctx_tiles = N_CTX // rows_per_tile
    tiles_per_batch = DEC_SEQ // rows_per_tile
    return jnp.where(tile < n_ctx_tiles, 0, 1 + (tile - n_ctx_tiles) // tiles_per_batch)


def _params(*semantics):
    return pltpu.CompilerParams(dimension_semantics=semantics, vmem_limit_bytes=VMEM_LIMIT)


MOD_TN = 1024


def _mod_kernel(c_ref, w_ref, b_ref, o_ref):
    o_ref[...] = _bdot(_silu(c_ref[...]), w_ref[...]) + b_ref[...]


def _modulation(cond, w_mod, b_mod):
    n = 3 * D_MODEL
    return pl.pallas_call(
        _mod_kernel,
        grid=(DEPTH, n // MOD_TN),
        in_specs=[
            pl.BlockSpec((N_COND, D_MODEL), lambda l, j: (0, 0)),
            pl.BlockSpec((None, D_MODEL, MOD_TN), lambda l, j: (l, 0, j)),
            pl.BlockSpec((None, 1, MOD_TN), lambda l, j: (l, 0, j)),
        ],
        out_specs=pl.BlockSpec((None, N_COND, MOD_TN), lambda l, j: (l, 0, j)),
        out_shape=jax.ShapeDtypeStruct((DEPTH, N_COND, n), F32),
        compiler_params=_params("arbitrary", "arbitrary"),
        name="modulation",
    )(cond, w_mod, b_mod.reshape(DEPTH, 1, n))


IN_TM = 1024
IN_FIRST_TM = 2048
IN_TN = 512
IN_LN_ROWS = 256
IN_CAST_K = 512
IN_W_SLOTS = 3


def _token_specs(xs, tm, tile_of=lambda i: i):
    n_ctx_tiles = N_CTX // tm
    if len(xs) == 1:
        maps = [lambda i, *_: (tile_of(i), 0)]
    else:
        maps = [lambda i, *_: (jnp.minimum(tile_of(i), n_ctx_tiles - 1), 0),
                lambda i, *_: (jnp.maximum(tile_of(i) - n_ctx_tiles, 0), 0)]
    return [pl.BlockSpec((tm, D_MODEL), m) for m in maps]


def _inproj_kernel(n_x, tm, tile0, cast_w, *refs):
    x_refs = refs[:n_x]
    scale_ref, shift_ref, w_ref = refs[n_x:n_x + 3]
    if cast_w:
        o_ref, wb_ref, h_ref = refs[n_x + 3:]
    else:
        w_hbm = w_ref
        _, o_ref, h_ref, wbuf_ref, wsem_ref = refs[n_x + 3:]
        n_col = pl.num_programs(1)
        t = pl.program_id(0) * n_col + pl.program_id(1)
        total = pl.num_programs(0) * n_col

        def w_copy(step):
            col = pl.multiple_of((step % n_col) * IN_TN, IN_TN)
            slot = step % IN_W_SLOTS
            return pltpu.make_async_copy(w_hbm.at[:, pl.ds(col, IN_TN)], wbuf_ref.at[slot],
                                         wsem_ref.at[slot])

        @pl.when(t == 0)
        def _():
            for step in range(IN_W_SLOTS - 1):
                w_copy(step).start()

        @pl.when(t + IN_W_SLOTS - 1 < total)
        def _():
            w_copy(t + IN_W_SLOTS - 1).start()

        w_copy(t).wait()
        w_ref = wbuf_ref.at[t % IN_W_SLOTS]

    def weights():
        if not cast_w:
            return w_ref[...]
        w = w_ref[...].astype(BF16)
        wb_ref[...] = w
        return w

    def first_column_step(x_ref):
        w = weights()

        def norm(r):
            rows = slice(r, r + IN_LN_ROWS)
            h = _layer_norm(x_ref[rows, :]) * (1.0 + scale_ref[...]) + shift_ref[...]
            h = h.astype(BF16)
            h_ref[rows, :] = h
            return h

        h = norm(0)
        for r in range(0, tm, IN_LN_ROWS):
            o_ref[r:r + IN_LN_ROWS, :] = jnp.dot(h, w, preferred_element_type=F32)
            if r + IN_LN_ROWS < tm:
                h = norm(r + IN_LN_ROWS)

    first = pl.program_id(1) == 0
    if n_x == 1:
        pl.when(first)(lambda: first_column_step(x_refs[0]))
    else:
        is_ctx = pl.program_id(0) + tile0 < N_CTX // IN_TM
        pl.when(jnp.logical_and(first, is_ctx))(lambda: first_column_step(x_refs[0]))
        pl.when(jnp.logical_and(first, jnp.logical_not(is_ctx)))(
            lambda: first_column_step(x_refs[1]))

    @pl.when(jnp.logical_not(first))
    def _():
        acc = None
        for k in range(0, D_MODEL, IN_CAST_K):
            w = w_ref[k:k + IN_CAST_K, :]
            if cast_w:
                w = w.astype(BF16)
                wb_ref[k:k + IN_CAST_K, :] = w
            part = jnp.dot(h_ref[:, k:k + IN_CAST_K], w, preferred_element_type=F32)
            acc = part if acc is None else acc + part
        o_ref[...] = acc


def _in_projection(xs, mod4, w_in, layer):
    assert IN_FIRST_TM <= N_CTX and IN_FIRST_TM % IN_TM == 0
    n_col = D_IN // IN_TN
    first_tiles = IN_FIRST_TM // IN_TM

    def mod_spec(tile0, piece):
        return pl.BlockSpec((None, None, 1, D_MODEL),
                            lambda i, j: (layer, _cond_row(i + tile0, IN_TM), 0, piece))

    p_shape = jax.ShapeDtypeStruct((N_TOK, D_IN), F32)
    p, w_bf = pl.pallas_call(
        functools.partial(_inproj_kernel, 1, IN_FIRST_TM, 0, True),
        grid=(1, n_col),
        in_specs=[
            pl.BlockSpec((IN_FIRST_TM, D_MODEL), lambda i, j: (0, 0), pipeline_mode=pl.Buffered(1)),
            mod_spec(0, 1), mod_spec(0, 0),
            pl.BlockSpec((None, D_MODEL, IN_TN), lambda i, j: (layer, 0, j)),
        ],
        out_specs=[pl.BlockSpec((IN_FIRST_TM, IN_TN), lambda i, j: (0, j)),
                   pl.BlockSpec((D_MODEL, IN_TN), lambda i, j: (0, j))],
        out_shape=[p_shape, jax.ShapeDtypeStruct((D_MODEL, D_IN), BF16)],
        scratch_shapes=[pltpu.VMEM((IN_FIRST_TM, D_MODEL), BF16)],
        compiler_params=_params("arbitrary", "arbitrary"),
        name="in_projection_first",
    )(xs[0], mod4, mod4, w_in)
    return pl.pallas_call(
        functools.partial(_inproj_kernel, len(xs), IN_TM, first_tiles, False),
        grid=(N_TOK // IN_TM - first_tiles, n_col),
        in_specs=_token_specs(xs, IN_TM, lambda i: i + first_tiles) + [
            mod_spec(first_tiles, 1), mod_spec(first_tiles, 0),
            pl.BlockSpec(memory_space=pl.ANY),
            pl.BlockSpec(memory_space=pl.ANY),
        ],
        out_specs=pl.BlockSpec((IN_TM, IN_TN), lambda i, j: (i + first_tiles, j)),
        out_shape=p_shape,
        input_output_aliases={len(xs) + 3: 0},
        scratch_shapes=[pltpu.VMEM((IN_TM, D_MODEL), BF16),
                        pltpu.VMEM((IN_W_SLOTS, D_MODEL, IN_TN), BF16),
                        pltpu.SemaphoreType.DMA((IN_W_SLOTS,))],
        compiler_params=_params("arbitrary", "arbitrary"),
        name="in_projection",
    )(*xs, mod4, mod4, w_bf, p)


AB_TM = 1024


def _gmlp_tasks(u_ref, v_ref, g_ref, w_ref, b_ref, o_ref):
    ws = [w_ref[g].astype(BF16) for g in range(N_GROUPS)]

    def chunk(r):
        rows = slice(r, r + CHUNK)
        vn = _layer_norm(v_ref[rows, :]).astype(BF16)
        for g in range(N_GROUPS):
            cols = slice(g * GROUP_W, (g + 1) * GROUP_W)
            mixed = jnp.dot(ws[g], vn[:, cols], preferred_element_type=F32) + b_ref[g]
            o_ref[rows, cols] = (_silu(g_ref[rows, cols]) * u_ref[rows, cols] * mixed).astype(BF16)

    return [functools.partial(chunk, r) for r in range(0, AB_TM, CHUNK)]


POOL_BAND_GROUPS = (2, 3)
POOL_BAND_ROWS = 256
POOL_BAND_HALO = 128


def _window_sum_shifts(p, win, pos, seq_len):
    acc = p
    for d in range(-(win // 2), win // 2):
        if d == 0:
            continue
        shifted = pltpu.roll(p, (-d) % AB_TM, axis=0)
        valid = (pos >= -d) if d < 0 else (pos < seq_len - d)
        acc = acc + jnp.where(valid, shifted, 0.0)
    return acc


def _window_sum_band_tasks(p, band_ref, pieces):
    hi = p.astype(BF16)
    r1 = p - hi.astype(F32)
    mid = r1.astype(BF16)
    lo = (r1 - mid.astype(F32)).astype(BF16)
    terms = jnp.concatenate([hi, mid, lo], axis=1)

    def piece(r):
        k0, k1 = max(0, r - POOL_BAND_HALO), min(AB_TM, r + POOL_BAND_ROWS + POOL_BAND_HALO)
        s = jnp.dot(band_ref[r:r + POOL_BAND_ROWS, k0:k1], terms[k0:k1, :],
                    preferred_element_type=F32)
        pieces.append(s[:, :GROUP_W] + s[:, GROUP_W:2 * GROUP_W] + s[:, 2 * GROUP_W:])

    return [functools.partial(piece, r) for r in range(0, AB_TM, POOL_BAND_ROWS)]


def _pool_finish(g, acc, x_ref, g_ref, w_ref, s_ref, pos, seq_len, o_ref):
    cols = slice(g * GROUP_W, (g + 1) * GROUP_W)
    win = POOL_WINDOWS[g]
    count = jnp.minimum(pos + win // 2, seq_len) - jnp.maximum(pos - win // 2, 0)
    pooled = acc / count.astype(F32)
    y = _bdot(pooled - x_ref[:, cols], w_ref[g])
    o_ref[:, BRANCH + g * GROUP_W:BRANCH + (g + 1) * GROUP_W] = (
        _silu(g_ref[:, cols]) * (y * s_ref[:, cols])).astype(BF16)


AB_PIECES = (C_AU, C_AV, C_AG, C_BX, C_BG)
AB_SLOTS = 3


def _mixer_ab_kernel(p_hbm, gw_ref, gb_ref, pw_ref, ps_ref, band_a_ref, band_b_ref, o_ref,
                     buf_ref, sem_ref):
    s = pl.program_id(0)
    n_steps = pl.num_programs(0)

    def copies(step):
        slot = step % AB_SLOTS
        return [pltpu.make_async_copy(
            p_hbm.at[pl.ds(step * AB_TM, AB_TM), pl.ds(c * BRANCH, BRANCH)],
            buf_ref.at[slot, k], sem_ref.at[slot, k]) for k, c in enumerate(AB_PIECES)]

    @pl.when(s == 0)
    def _():
        for step in range(AB_SLOTS - 1):
            for k, cp in enumerate(copies(step)):
                cp.start(priority=k % 2)

    @pl.when(s + AB_SLOTS - 1 < n_steps)
    def _():
        for k, cp in enumerate(copies(s + AB_SLOTS - 1)):
            cp.start(priority=k % 2)

    for cp in copies(s):
        cp.wait()
    tile = [buf_ref.at[s % AB_SLOTS, k] for k in range(len(AB_PIECES))]
    _mixer_ab_body(*tile, gw_ref, gb_ref, pw_ref, ps_ref, band_a_ref, band_b_ref, o_ref)


def _mixer_ab_body(au_ref, av_ref, ag_ref, bx_ref, bg_ref, gw_ref, gb_ref, pw_ref, ps_ref,
                   band_a_ref, band_b_ref, o_ref):
    seq_len = jnp.where(pl.program_id(0) < N_CTX // AB_TM, SEQ, DEC_SEQ)
    pos = lax.broadcasted_iota(jnp.int32, (AB_TM, GROUP_W), 0) & (seq_len - 1)
    pool_refs = (bx_ref, bg_ref, pw_ref, ps_ref, pos, seq_len, o_ref)
    band_sums = {g: [] for g in POOL_BAND_GROUPS}
    mxu_tasks = [t for g, band in zip(POOL_BAND_GROUPS, (band_a_ref, band_b_ref))
                 for t in _window_sum_band_tasks(bx_ref[:, g * GROUP_W:(g + 1) * GROUP_W], band,
                                                 band_sums[g])]
    vpu_tasks = _gmlp_tasks(au_ref, av_ref, ag_ref, gw_ref, gb_ref, o_ref)
    for i in range(max(len(mxu_tasks), len(vpu_tasks))):
        for tasks in (mxu_tasks, vpu_tasks):
            if i < len(tasks):
                tasks[i]()
    for g, win in enumerate(POOL_WINDOWS):
        if g not in POOL_BAND_GROUPS:
            acc = _window_sum_shifts(bx_ref[:, g * GROUP_W:(g + 1) * GROUP_W], win, pos, seq_len)
            _pool_finish(g, acc, *pool_refs)
    for g in POOL_BAND_GROUPS:
        _pool_finish(g, jnp.concatenate(band_sums[g], axis=0), *pool_refs)


def _pool_bands():
    t = np.arange(AB_TM)
    bands = []
    for g in POOL_BAND_GROUPS:
        half = POOL_WINDOWS[g] // 2
        d = t[None, :] - t[:, None]
        near = (d >= -half) & (d < half)
        per_path = [near & ((t[None, :] // L) == (t[:, None] // L)) for L in (SEQ, DEC_SEQ)]
        bands.append(jnp.asarray(np.stack(per_path), F32).astype(BF16))
    return bands


def _mixer_gmlp_pool(p, gmlp_w, gmlp_b_rows, pool_w, pool_scale_row):
    assert M_POOL == M_GMLP + 1
    n_steps = N_TOK // AB_TM
    assert n_steps >= AB_SLOTS
    full = lambda a: pl.BlockSpec(a.shape, lambda i: (0,) * a.ndim)
    band = pl.BlockSpec((None, AB_TM, AB_TM), lambda i: (jnp.where(i < N_CTX // AB_TM, 0, 1), 0, 0))
    return pl.pallas_call(
        _mixer_ab_kernel,
        grid=(n_steps,),
        in_specs=[pl.BlockSpec(memory_space=pl.ANY),
                  full(gmlp_w), full(gmlp_b_rows), full(pool_w), full(pool_scale_row), band, band],
        out_specs=pl.BlockSpec((AB_TM, 2 * BRANCH), lambda i: (i, M_GMLP // 2)),
        out_shape=jax.ShapeDtypeStruct((N_TOK, D_MIX), BF16),
        scratch_shapes=[pltpu.VMEM((AB_SLOTS, len(AB_PIECES), AB_TM, BRANCH), F32),
                        pltpu.SemaphoreType.DMA((AB_SLOTS, len(AB_PIECES)))],
        compiler_params=_params("arbitrary"),
        name="mixer_gmlp_pool",
    )(p, gmlp_w, gmlp_b_rows, pool_w, pool_scale_row, *_pool_bands())


ATT_TQ = 1024
V_ROWS = HEAD_DIM_C + 16
NT_DIMS = (((1,), (1,)), ((), ()))


def _lambda(lam_ref, lam_init):
    lq = lam_ref[...]
    a = jnp.sum(lq[0:1] * lq[1:2], axis=-1, keepdims=True)
    b = jnp.sum(lq[2:3] * lq[3:4], axis=-1, keepdims=True)
    return jnp.exp(a) - jnp.exp(b) + lam_init


def _map_masks():
    lane = lax.broadcasted_iota(jnp.int32, (1, HEAD_DIM_C), 1)
    m0 = (lane < QK_HALF).astype(F32)
    return m0, 1.0 - m0


def _scores_t(q, keys):
    return [lax.dot_general(k, q, NT_DIMS, preferred_element_type=F32) for k in keys]


def _softmax_v_t(s, vals_t):
    m = functools.reduce(jnp.maximum, [jnp.max(x, axis=0, keepdims=True) for x in s])
    acc = functools.reduce(
        jnp.add, [jnp.dot(v, jnp.exp2(x - m).astype(BF16), preferred_element_type=F32)
                  for x, v in zip(s, vals_t)])
    return acc[:HEAD_DIM_C] * (1.0 / acc[HEAD_DIM_C:HEAD_DIM_C + 1])


def _with_ones_rows(v_t):
    ones = jnp.ones((V_ROWS - HEAD_DIM_C, v_t.shape[1]), BF16)
    return jnp.concatenate([v_t.astype(BF16), ones], axis=0)


def _diff_attention(q_of, keys_of, vals_t_of, gate_of, store, lam, lam_init, subln):
    masks = _map_masks()
    tasks = [(h, mp) for h in range(N_HEADS_C) for mp in range(2)]
    per_head = {}

    def operands(h):
        if h not in per_head:
            q = q_of(h) * (QK_HALF ** -0.5 * math.log2(math.e))
            per_head[h] = (q, keys_of(h), vals_t_of(h))
        return per_head[h]

    def scores(h, mp):
        q, keys, _ = operands(h)
        return _scores_t((q * masks[mp]).astype(BF16), keys)

    nxt = scores(*tasks[0])
    outs = []
    for i, (h, mp) in enumerate(tasks):
        cur = nxt
        if i + 1 < len(tasks):
            nxt = scores(*tasks[i + 1])
        outs.append(_softmax_v_t(cur, operands(h)[2]))
        if mp == 1:
            o = (outs[0] - lam * outs[1]).T
            outs = []
            o = o * lax.rsqrt(jnp.mean(o * o, axis=-1, keepdims=True) + 1e-5)
            o = o * subln * (1.0 - lam_init)
            store(h, _silu(gate_of(h)) * o)


def _head_cols(h):
    return slice(h * HEAD_DIM_C, (h + 1) * HEAD_DIM_C)


def _attn_ctx_kernel(lam_init, q_ref, k_ref, v_ref, g_ref, lam_ref, sw_ref, *rest):
    o_ref, ko_ref, vo_ref = rest[-3:]
    ko_ref[...] = k_ref[...]
    for h in range(N_HEADS_C):
        vo_ref[:, h, :] = v_ref[:, _head_cols(h)]

    def store(h, y):
        o_ref[:, _head_cols(h)] = y.astype(BF16)

    _diff_attention(
        lambda h: q_ref[:, _head_cols(h)],
        lambda h: [k_ref[:, _head_cols(h)].astype(BF16)],
        lambda h: [_with_ones_rows(v_ref[:, _head_cols(h)].T)],
        lambda h: g_ref[:, _head_cols(h)],
        store, _lambda(lam_ref, lam_init), lam_init, sw_ref[...])


def _mixer_attn_ctx(p, lambda_qk, subln_row, layer, lam_init, ymix, caches):
    blk = lambda c: pl.BlockSpec((SEQ, BRANCH), lambda b: (b, c))
    any_spec = pl.BlockSpec(memory_space=pl.ANY)
    k_spec = pl.BlockSpec((None, None, SEQ, BRANCH), lambda b: (b, layer, 0, 0))
    v_spec = pl.BlockSpec((None, None, SEQ, N_HEADS_C, HEAD_DIM_C), lambda b: (b, layer, 0, 0, 0))
    k_shape = jax.ShapeDtypeStruct((BATCH, DEPTH, SEQ, BRANCH), F32)
    v_shape = jax.ShapeDtypeStruct((BATCH, DEPTH, SEQ, N_HEADS_C, HEAD_DIM_C), F32)
    in_specs = [
        blk(C_Q), blk(C_K), blk(C_V), blk(C_CG),
        pl.BlockSpec((None, 4, QK_HALF), lambda b: (layer, 0, 0)),
        pl.BlockSpec((None, 1, HEAD_DIM_C), lambda b: (layer, 0, 0)),
        any_spec,
    ]
    args = [p, p, p, p, lambda_qk, subln_row, ymix]
    aliases = {6: 0}
    if caches is not None:
        in_specs += [any_spec, any_spec]
        args += list(caches)
        aliases.update({7: 1, 8: 2})
    return pl.pallas_call(
        functools.partial(_attn_ctx_kernel, lam_init),
        grid=(BATCH,),
        in_specs=in_specs,
        out_specs=[pl.BlockSpec((SEQ, BRANCH), lambda b: (b, M_ATTN)), k_spec, v_spec],
        out_shape=[jax.ShapeDtypeStruct((N_TOK, D_MIX), BF16), k_shape, v_shape],
        input_output_aliases=aliases,
        compiler_params=_params("arbitrary"),
        name="mixer_attn_ctx",
    )(*args)


def _rope(x, cos, sin_signed):
    lane = lax.broadcasted_iota(jnp.int32, x.shape, 1)
    first_half = (lane & (ROPE_AXIS_DIM - 1)) < (ROPE_AXIS_DIM // 2)
    half = ROPE_AXIS_DIM // 2
    partner = jnp.where(first_half,
                        pltpu.roll(x, x.shape[1] - half, axis=1),
                        pltpu.roll(x, half, axis=1))
    return x * cos + partner * sin_signed


def _attn_lat_kernel(lam_init, q_ref, k_ref, v_ref, g_ref, ck_ref, cv_ref, cosq_ref, sinq_ref,
                     cosk_ref, sink_ref, lam_ref, sw_ref, wo_ref, ymix_ref, o_ref, wo_bf_ref,
                     kc_ref, kr_ref, vt_ref):
    del ymix_ref
    wo_bf_ref[...] = wo_ref[...].astype(BF16)

    @pl.when(pl.program_id(1) == 0)
    def _():
        kc_ref[...] = ck_ref[...].astype(BF16)
        for h in range(N_HEADS_C):
            cols = slice(h * HEAD_DIM_C, (h + 1) * HEAD_DIM_C)
            kr_ref[:, cols] = _rope(k_ref[:, cols], cosk_ref[...], sink_ref[...]).astype(BF16)
            vt_ref[h, :, 0:PAST_LEN] = _with_ones_rows(cv_ref[:, h, :].T)
            vt_ref[h, :, PAST_LEN:PAST_LEN + DEC_SEQ] = _with_ones_rows(v_ref[:, cols].T)

    def store(h, y):
        o_ref[:, _head_cols(h)] = y.astype(BF16)

    _diff_attention(
        lambda h: _rope(q_ref[:, _head_cols(h)], cosq_ref[...], sinq_ref[...]),
        lambda h: [kc_ref[:, _head_cols(h)], kr_ref[:, _head_cols(h)]],
        lambda h: [vt_ref[h, :, 0:PAST_LEN], vt_ref[h, :, PAST_LEN:PAST_LEN + DEC_SEQ]],
        lambda h: g_ref[:, _head_cols(h)],
        store, _lambda(lam_ref, lam_init), lam_init, sw_ref[...])


def _rope_tables():
    pos = np.arange(DEC_SEQ)
    row = (pos // GRID_W).astype(np.float64)
    col = (pos % GRID_W).astype(np.float64)
    half = ROPE_AXIS_DIM // 2
    inv = ROPE_BASE ** (-np.arange(0, ROPE_AXIS_DIM, 2, dtype=np.float64) / ROPE_AXIS_DIM)
    lane = np.arange(HEAD_DIM_C)
    axis_is_col = (lane // ROPE_AXIS_DIM) % 2 == 1
    idx = lane % ROPE_AXIS_DIM
    ang = np.where(axis_is_col[None, :], col[:, None], row[:, None]) * inv[idx % half][None, :]
    sign = np.where(idx < half, -1.0, 1.0)[None, :]
    return (jnp.asarray(np.cos(ang), F32), jnp.asarray(np.sin(ang) * sign, F32))


def _mixer_attn_lat(p, cache_k4, cache_v, lambda_qk, subln_row, layer, lam_init, w_out, ymix):
    cos_t, sin_t = _rope_tables()
    q_tiles = DEC_SEQ // ATT_TQ
    q0 = N_CTX // ATT_TQ
    b0 = N_CTX // DEC_SEQ
    wo_rows = D_MIX // (DEC_BATCH * q_tiles)
    qblk = lambda c: pl.BlockSpec((ATT_TQ, BRANCH), lambda b, i: (q0 + b * q_tiles + i, c))
    kblk = lambda c: pl.BlockSpec((DEC_SEQ, BRANCH), lambda b, i: (b0 + b, c))
    cblk = pl.BlockSpec((None, None, PAST_LEN, BRANCH), lambda b, i: (b, layer, 0, 0))
    cvblk = pl.BlockSpec((None, None, PAST_LEN, N_HEADS_C, HEAD_DIM_C),
                         lambda b, i: (b, layer, 0, 0, 0))
    return pl.pallas_call(
        functools.partial(_attn_lat_kernel, lam_init),
        grid=(DEC_BATCH, q_tiles),
        in_specs=[
            qblk(C_Q), kblk(C_K), kblk(C_V), qblk(C_CG), cblk, cvblk,
            pl.BlockSpec((ATT_TQ, HEAD_DIM_C), lambda b, i: (i, 0)),
            pl.BlockSpec((ATT_TQ, HEAD_DIM_C), lambda b, i: (i, 0)),
            pl.BlockSpec((DEC_SEQ, HEAD_DIM_C), lambda b, i: (0, 0)),
            pl.BlockSpec((DEC_SEQ, HEAD_DIM_C), lambda b, i: (0, 0)),
            pl.BlockSpec((None, 4, QK_HALF), lambda b, i: (layer, 0, 0)),
            pl.BlockSpec((None, 1, HEAD_DIM_C), lambda b, i: (layer, 0, 0)),
            pl.BlockSpec((None, wo_rows, D_MODEL), lambda b, i: (layer, b * q_tiles + i, 0)),
            pl.BlockSpec(memory_space=pl.ANY),
        ],
        out_specs=[pl.BlockSpec((ATT_TQ, BRANCH), lambda b, i: (q0 + b * q_tiles + i, M_ATTN)),
                   pl.BlockSpec((wo_rows, D_MODEL), lambda b, i: (b * q_tiles + i, 0))],
        out_shape=[jax.ShapeDtypeStruct((N_TOK, D_MIX), BF16),
                   jax.ShapeDtypeStruct((D_MIX, D_MODEL), BF16)],
        input_output_aliases={13: 0},
        scratch_shapes=[pltpu.VMEM((PAST_LEN, BRANCH), BF16), pltpu.VMEM((DEC_SEQ, BRANCH), BF16),
                        pltpu.VMEM((N_HEADS_C, V_ROWS, PAST_LEN + DEC_SEQ), BF16)],
        compiler_params=_params("arbitrary", "arbitrary"),
        name="mixer_attn_lat",
    )(p, p, p, p, cache_k4, cache_v, cos_t, sin_t, cos_t, sin_t, lambda_qk, subln_row, w_out, ymix)


HY_ROWS = 1024
HY_PIECE = 512
HY_STAGGER = 1


def _dft_matrices(seq_len):
    n = 2 * seq_len
    f = np.arange(seq_len, dtype=np.float64)[:, None]
    s = np.arange(seq_len, dtype=np.float64)[None, :]
    theta = 2.0 * np.pi * f * s / n
    alt = np.where(np.arange(seq_len) % 2 == 0, 1.0, -1.0)
    ac = np.cos(theta)
    as_ = -np.sin(theta)
    as_[0, :] = alt
    bc = 2.0 * np.cos(theta.T) / n
    bc[:, 0] = 1.0 / n
    bs = -2.0 * np.sin(theta.T) / n
    bs[:, 0] = alt / n
    fwd = np.concatenate([ac, as_], axis=0)
    inv = np.concatenate([bc, bs], axis=1)
    return jnp.asarray(fwd, F32).astype(BF16), jnp.asarray(inv, F32).astype(BF16)


def _filter_features(seq_len):
    t_idx = np.arange(seq_len, dtype=np.float64)
    t_norm = np.linspace(0.0, 1.0, seq_len)
    bands = np.linspace(1e-4, HY_BANDS - 1, HY_BANDS)
    ang = (2.0 * math.pi * t_idx / seq_len)[:, None] * bands[None, :]
    feats = np.concatenate([t_norm[:, None], np.cos(ang), np.sin(ang)], axis=-1)
    feats = np.pad(feats, ((0, 0), (0, HY_PAD - HY_EMB)))
    deltas = np.abs(np.linspace(math.log(HY_TARGET) / HY_FAST_DECAY,
                                math.log(HY_TARGET) / HY_SLOW_DECAY, BRANCH))
    return (jnp.asarray(feats, F32), jnp.asarray(t_norm[:, None], F32),
            jnp.asarray(deltas[None, :], F32))


def _filter_kernel(seq_len, feats_ref, tn_ref, dl_ref, w1_ref, b1_ref, w2_ref, b2_ref, fr_ref,
                   w3f_ref, w3b_ref, fwd_ref, kr_ref, ki_ref, h_ref):
    sp = lambda x: _split(x, 2)

    @pl.when(pl.program_id(1) == 0)
    def _():
        fr = fr_ref[...]
        h = jnp.sin(fr * (_sdot(sp(feats_ref[...]), sp(w1_ref[...])) + b1_ref[...]))
        h_ref[...] = jnp.sin(fr * (_sdot(sp(h), sp(w2_ref[...])) + b2_ref[...]))

    h = h_ref[...]
    decay = jnp.exp(-tn_ref[...] * dl_ref[...])
    row = lax.broadcasted_iota(jnp.int32, (seq_len, BRANCH), 0)
    fwd = _bdot(h, w3f_ref[...]) * decay
    bwd = jnp.where(row == 0, 0.0, _bdot(h, w3b_ref[...]) * decay)
    norm = (jnp.sum(jnp.abs(fwd), axis=0, keepdims=True)
            + jnp.sum(jnp.abs(bwd), axis=0, keepdims=True))
    fwd = fwd / norm
    bwd = bwd / norm
    even = fwd + bwd
    alt = jnp.where((row & 1) == 0, 1.0, -1.0)
    nyquist = jnp.sum(alt * even, axis=0, keepdims=True)
    kr_ref[...] = jnp.dot(fwd_ref[0:seq_len, :], even.astype(BF16), preferred_element_type=F32)
    ki = jnp.dot(fwd_ref[seq_len:2 * seq_len, :], (fwd - bwd).astype(BF16),
                 preferred_element_type=F32)
    ki_ref[...] = jnp.where(row == 0, nyquist, ki)


def _hyena_spectrum(seq_len, filt, fwd_mat):
    feats, t_norm, deltas = _filter_features(seq_len)
    w1, b1, w2, b2, freq, w3 = filt
    full = lambda a: pl.BlockSpec(a.shape, lambda l, o: (0,) * a.ndim)
    lyr = lambda r, n: pl.BlockSpec((None, r, n), lambda l, o: (l, 0, 0))
    out = pl.BlockSpec((None, None, seq_len, BRANCH), lambda l, o: (l, o, 0, 0))
    return pl.pallas_call(
        functools.partial(_filter_kernel, seq_len),
        grid=(DEPTH, HY_ORDER),
        in_specs=[
            full(feats), full(t_norm), full(deltas),
            lyr(HY_PAD, HY_PAD), lyr(1, HY_PAD), lyr(HY_PAD, HY_PAD), lyr(1, HY_PAD), lyr(1, HY_PAD),
            pl.BlockSpec((None, HY_PAD, BRANCH), lambda l, o: (l, 0, 2 * o)),
            pl.BlockSpec((None, HY_PAD, BRANCH), lambda l, o: (l, 0, 2 * o + 1)),
            full(fwd_mat),
        ],
        out_specs=[out, out],
        out_shape=[jax.ShapeDtypeStruct((DEPTH, HY_ORDER, seq_len, BRANCH), F32)] * 2,
        scratch_shapes=[pltpu.VMEM((seq_len, HY_PAD), F32)],
        compiler_params=_params("arbitrary", "arbitrary"),
        name=f"hyena_spectrum_{seq_len}",
    )(feats, t_norm, deltas, w1, b1, w2, b2, freq, w3, w3, fwd_mat)


def _hyena_chain(seq_len, path, rows, x_refs, cw_refs, cb_refs, kr_ref, ki_ref, hb_ref, fwd_ref,
                 inv_ref, o_ref):
    x1_ref, x2_ref, hv_ref, g_ref = x_refs
    row = lax.broadcasted_iota(jnp.int32, (seq_len, HY_CT), 0)
    first, last = row == 0, row == seq_len - 1

    def short_conv(x_ref, piece):
        x, w = x_ref[path, rows, :], cw_refs[piece][...]
        prev = jnp.where(first, 0.0, pltpu.roll(x, 1, axis=0))
        nxt = jnp.where(last, 0.0, pltpu.roll(x, seq_len - 1, axis=0))
        return prev * w[0:1] + x * w[1:2] + nxt * w[2:3] + cb_refs[piece][...]

    z = short_conv(hv_ref, 2)
    yield
    gate_refs = (x1_ref, x2_ref)
    piece = min(seq_len, HY_PIECE)
    for order in range(HY_ORDER):
        zb = z.astype(BF16)
        y_re, y_im = [], []
        for r in range(0, seq_len, piece):
            zr = jnp.dot(fwd_ref[r:r + piece, :], zb, preferred_element_type=F32)
            yield
            zi = jnp.dot(fwd_ref[seq_len + r:seq_len + r + piece, :], zb,
                         preferred_element_type=F32)
            yield
            kr, kp = kr_ref[order, r:r + piece, :], ki_ref[order, r:r + piece, :]
            ki, kn = kp, kr
            if r == 0:
                dc = lax.broadcasted_iota(jnp.int32, (piece, HY_CT), 0) == 0
                ki = jnp.where(dc, 0.0, kp)
                kn = jnp.where(dc, kp, kr)
            y_re.append((zr * kr - zi * ki).astype(BF16))
            y_im.append((zr * ki + zi * kn).astype(BF16))
            yield
        yf = jnp.concatenate(y_re + y_im, axis=0)
        ys = []
        for r in range(0, seq_len, piece):
            ys.append(jnp.dot(inv_ref[r:r + piece, :], yf, preferred_element_type=F32))
            yield
            if r == 0:
                gate = short_conv(gate_refs[order], order)
        y = ys[0] if len(ys) == 1 else jnp.concatenate(ys, axis=0)
        z = gate * (y + z * hb_ref[order:order + 1, :])
    o_ref[path, rows, :] = (_silu(g_ref[path, rows, :]) * z).astype(BF16)


def _hyena_kernel(x1_ref, x2_ref, hv_ref, g_ref, cw1_ref, cw2_ref, cw3_ref, cb1_ref, cb2_ref,
                  cb3_ref, hb_ref, krc_ref, kic_ref, fwdc_ref, invc_ref, krl_ref, kil_ref,
                  fwdl_ref, invl_ref, ymix_ref, o_ref):
    del ymix_ref
    x_refs = (x1_ref, x2_ref, hv_ref, g_ref)
    conv = ((cw1_ref, cw2_ref, cw3_ref), (cb1_ref, cb2_ref, cb3_ref))
    pending = [(0, _hyena_chain(DEC_SEQ, 1, slice(0, DEC_SEQ), x_refs, *conv, krl_ref, kil_ref,
                                hb_ref, fwdl_ref, invl_ref, o_ref))]
    for s in range(HY_ROWS // SEQ):
        pending.append((HY_STAGGER * s,
                        _hyena_chain(SEQ, 0, slice(s * SEQ, (s + 1) * SEQ), x_refs, *conv, krc_ref,
                                     kic_ref, hb_ref, fwdc_ref, invc_ref, o_ref)))
    running, tick = [], 0
    while pending or running:
        running += [gen for start, gen in pending if start == tick]
        pending = [(start, gen) for start, gen in pending if start > tick]
        for gen in list(running):
            if next(gen, "done") == "done":
                running.remove(gen)
        tick += 1


def _mixer_hyena(p, conv_w, conv_b3, hyena_bias, spec_ctx, mats_ctx, spec_lat, mats_lat, layer,
                 ymix):
    assert N_CTX == N_LAT and HY_ROWS == DEC_SEQ
    n_ct = BRANCH // HY_CT
    blk = lambda pc: pl.BlockSpec((2, HY_ROWS, HY_CT), lambda c, i: (0, i, pc * n_ct + c))
    cw = lambda pc: pl.BlockSpec((None, 3, HY_CT), lambda c, i: (layer, 0, pc * n_ct + c))
    cb = lambda pc: pl.BlockSpec((None, 1, HY_CT), lambda c, i: (layer, 0, pc * n_ct + c))
    spec = lambda a: pl.BlockSpec((None, HY_ORDER, a.shape[2], HY_CT), lambda c, i: (layer, 0, 0, c))
    full = lambda a: pl.BlockSpec(a.shape, lambda c, i: (0,) * a.ndim, pipeline_mode=pl.Buffered(1))
    p3 = p.reshape(2, N_CTX, D_IN)
    out = pl.pallas_call(
        _hyena_kernel,
        grid=(n_ct, N_CTX // HY_ROWS),
        in_specs=[blk(C_DX1), blk(C_DX2), blk(C_DV), blk(C_DG),
                  cw(0), cw(1), cw(2), cb(0), cb(1), cb(2),
                  pl.BlockSpec((None, HY_ORDER, HY_CT), lambda c, i: (layer, 0, c)),
                  spec(spec_ctx[0]), spec(spec_ctx[1]), full(mats_ctx[0]), full(mats_ctx[1]),
                  spec(spec_lat[0]), spec(spec_lat[1]), full(mats_lat[0]), full(mats_lat[1]),
                  pl.BlockSpec(memory_space=pl.ANY)],
        out_specs=pl.BlockSpec((2, HY_ROWS, HY_CT), lambda c, i: (0, i, M_HYENA * n_ct + c)),
        out_shape=jax.ShapeDtypeStruct((2, N_CTX, D_MIX), BF16),
        input_output_aliases={19: 0},
        compiler_params=_params("arbitrary", "arbitrary"),
        name="mixer_hyena",
    )(p3, p3, p3, p3, conv_w, conv_w, conv_w, conv_b3, conv_b3, conv_b3, hyena_bias,
      *spec_ctx, *mats_ctx, *spec_lat, *mats_lat, ymix.reshape(2, N_CTX, D_MIX))
    return out.reshape(N_TOK, D_MIX)


OUT_TM = 512


def _outproj_kernel(alpha, n_x, n_out, ymix_ref, *refs):
    x_refs = refs[:n_x]
    gate_ref, w_ref, b_ref, lng_ref, lnb_ref = refs[n_x:n_x + 5]
    o_refs = refs[n_x + 5:]
    is_ctx = pl.program_id(0) < N_CTX // OUT_TM

    def tile(x_ref, o_ref):
        y = jnp.dot(ymix_ref[...], w_ref[...], preferred_element_type=F32) + b_ref[...]
        r = alpha * x_ref[...] + gate_ref[...] * y
        o_ref[...] = _layer_norm(r) * lng_ref[...] + lnb_ref[...]

    if n_x == 1 and n_out == 1:
        tile(x_refs[0], o_refs[0])
    else:
        pl.when(is_ctx)(lambda: tile(x_refs[0], o_refs[0]))
        pl.when(jnp.logical_not(is_ctx))(lambda: tile(x_refs[-1], o_refs[-1]))


def _out_projection(ymix, xs, mod4, w_out_bf, b_out3, ln_g3, ln_b3, layer, split_out):
    alpha = (2.0 * DEPTH) ** 0.25
    row = lambda i: _cond_row(i, OUT_TM)
    vec = pl.BlockSpec((None, 1, D_MODEL), lambda i: (layer, 0, 0))
    if split_out:
        out_shape = [jax.ShapeDtypeStruct((N_CTX, D_MODEL), F32),
                     jax.ShapeDtypeStruct((N_LAT, D_MODEL), F32)]
    else:
        out_shape = [jax.ShapeDtypeStruct((N_TOK, D_MODEL), F32)]
    return pl.pallas_call(
        functools.partial(_outproj_kernel, alpha, len(xs), len(out_shape)),
        grid=(N_TOK // OUT_TM,),
        in_specs=[pl.BlockSpec((OUT_TM, D_MIX), lambda i: (i, 0))] + _token_specs(xs, OUT_TM) + [
            pl.BlockSpec((None, None, 1, D_MODEL), lambda i: (layer, row(i), 0, 2)),
            pl.BlockSpec((D_MIX, D_MODEL), lambda i: (0, 0), pipeline_mode=pl.Buffered(1)),
            vec, vec, vec,
        ],
        out_specs=_token_specs(out_shape, OUT_TM),
        out_shape=out_shape,
        compiler_params=_params("arbitrary"),
        name="out_projection",
    )(ymix, *xs, mod4, w_out_bf, b_out3, ln_g3, ln_b3)


def kernel(x_prompt, x_sample, cache_k, cache_v, c, c_ctx, w_mod, b_mod, w_in, gmlp_w, gmlp_b,
           pool_w, pool_scale, lambda_qk, subln_w, conv_w, conv_b, filt_w1, filt_b1, filt_w2,
           filt_b2, filt_freq, filt_w3, hyena_bias, w_out, b_out, ln_g, ln_b):
    xs = (x_prompt.reshape(N_CTX, D_MODEL), x_sample.reshape(N_LAT, D_MODEL))
    cond = jnp.concatenate(
        [c_ctx[None, :], c, jnp.zeros((N_COND - 1 - DEC_BATCH, D_MODEL), F32)], axis=0)
    mod4 = _modulation(cond, w_mod, b_mod).reshape(DEPTH, N_COND, 1, 3 * D_MODEL)

    cache_k4 = cache_k.reshape(DEC_BATCH, DEPTH, PAST_LEN, BRANCH)
    gmlp_b_rows = jnp.broadcast_to(gmlp_b[..., None], (DEPTH, N_GROUPS, CHUNK, GROUP_W))
    subln_row = subln_w.reshape(DEPTH, 1, HEAD_DIM_C)
    conv_b3 = conv_b.reshape(DEPTH, 1, 3 * BRANCH)
    b_out3 = b_out.reshape(DEPTH, 1, D_MODEL)
    ln_g3 = ln_g.reshape(DEPTH, 1, D_MODEL)
    ln_b3 = ln_b.reshape(DEPTH, 1, D_MODEL)

    pad_h = HY_PAD - HY_HIDDEN
    filt = (
        jnp.pad(filt_w1, ((0, 0), (0, HY_PAD - HY_EMB), (0, pad_h))),
        jnp.pad(filt_b1, ((0, 0), (0, pad_h))).reshape(DEPTH, 1, HY_PAD),
        jnp.pad(filt_w2, ((0, 0), (0, pad_h), (0, pad_h))),
        jnp.pad(filt_b2, ((0, 0), (0, pad_h))).reshape(DEPTH, 1, HY_PAD),
        jnp.pad(filt_freq, ((0, 0), (0, pad_h))).reshape(DEPTH, 1, HY_PAD),
        jnp.pad(filt_w3, ((0, 0), (0, pad_h), (0, 0))),
    )
    mats_ctx = _dft_matrices(SEQ)
    mats_lat = _dft_matrices(DEC_SEQ)
    spec_ctx = _hyena_spectrum(SEQ, filt, mats_ctx[0])
    spec_lat = _hyena_spectrum(DEC_SEQ, filt, mats_lat[0])

    caches = None
    for layer in range(DEPTH):
        lam_init = 0.8 - 0.6 * math.exp(-0.3 * layer)
        p = _in_projection(xs, mod4, w_in, layer)
        ymix = _mixer_gmlp_pool(p, gmlp_w[layer], gmlp_b_rows[layer], pool_w[layer],
                                pool_scale[layer].reshape(1, BRANCH))
        ymix, new_k, new_v = _mixer_attn_ctx(p, lambda_qk, subln_row, layer, lam_init, ymix, caches)
        caches = (new_k, new_v)
        ymix, w_out_bf = _mixer_attn_lat(p, cache_k4, cache_v, lambda_qk, subln_row, layer,
                                         lam_init, w_out, ymix)
        ymix = _mixer_hyena(p, conv_w, conv_b3, hyena_bias, spec_ctx, mats_ctx, spec_lat, mats_lat,
                            layer, ymix)
        xs = tuple(_out_projection(ymix, xs, mod4, w_out_bf, b_out3, ln_g3, ln_b3, layer,
                                   split_out=layer == DEPTH - 1))

    y_prompt = xs[0].reshape(BATCH, SEQ, D_MODEL)
    y_sample = xs[1].reshape(DEC_BATCH, DEC_SEQ, D_MODEL)
    new_k, new_v = caches
    return (y_prompt, y_sample,
            new_k.reshape(BATCH, DEPTH, SEQ, N_HEADS_C, 2, QK_HALF), new_v)
```
